```python
import math
import jax, jax.numpy as jnp
from jax import lax
import numpy as np

D_MODEL = 1024
BATCH = 8
SEQ = 2048
DEPTH = 4
DEC_BATCH = 128
DEC_SEQ = 4
PAST_LEN = 16384
PAGE_SIZE = 128

N_MIXERS = 3
N_A = len([i for i in range(DEPTH) if i % N_MIXERS == 0])
N_B = len([i for i in range(DEPTH) if i % N_MIXERS == 1])
N_C = len([i for i in range(DEPTH) if i % N_MIXERS == 2])

DN_ALPHA = (2.0 * DEPTH) ** 0.25
DN_BETA = (8.0 * DEPTH) ** -0.25
LN_EPS = 1e-5
CONV_W = 4

LRU_W = D_MODEL
LRU_BLOCKS = 16
LRU_BS = LRU_W // LRU_BLOCKS
LRU_C = 8.0

RW_W = D_MODEL
RW_N = 64
RW_H = RW_W // RW_N
RW_LORA_W = 64
RW_LORA_A = 64
RW_GN_EPS = 64e-5
RW_NORM_EPS = 1e-12

GDN_HK = 4
GDN_HV = 8
GDN_DK = 128
GDN_DV = 128
GDN_KEY_W = GDN_HK * GDN_DK
GDN_VAL_W = GDN_HV * GDN_DV
GDN_CONV_CH = 2 * GDN_KEY_W + GDN_VAL_W
GDN_IN_W = GDN_CONV_CH + GDN_VAL_W + 2 * GDN_HV
GDN_CHUNK = 64
GDN_EPS = 1e-6

kernel_name = 'hybrid_rglru_rwkv7_gdn_decode_step'


def layer_norm(x, g, b):
    xf = x.astype(jnp.float32)
    mu = jnp.mean(xf, -1, keepdims=True)
    var = jnp.mean(jnp.square(xf - mu), -1, keepdims=True)
    return ((xf - mu) * lax.rsqrt(var + LN_EPS) * g + b).astype(x.dtype)


def l2norm(z, eps):
    return z * lax.rsqrt(jnp.sum(z * z, -1, keepdims=True) + eps)


def causal_dwconv(x, buf, w, bias):
    T = x.shape[1]
    xp = jnp.concatenate([buf.astype(x.dtype), x], axis=1)
    y = xp[:, 0:T] * w[0]
    for j in range(1, CONV_W):
        y = y + xp[:, j:j + T] * w[j]
    if bias is not None:
        y = y + bias
    return y, xp[:, T:]


def _linear_combine(left, right):
    a1, b1 = left
    a2, b2 = right
    return a1 * a2, a2 * b1 + b2


def rglru_mixer(x, conv_buf, h0, w_in, conv_w, conv_b, wa, ba, wx, bx, lam, w_out):
    B, T, _ = x.shape
    f32 = jnp.float32
    u = x @ w_in
    xb, gate = u[..., :LRU_W], u[..., LRU_W:]
    xc, new_buf = causal_dwconv(xb, conv_buf, conv_w, conv_b)
    xh = xc.reshape(B, T, LRU_BLOCKS, LRU_BS)
    r = jax.nn.sigmoid((jnp.einsum('btni,nij->btnj', xh, wa).reshape(B, T, LRU_W) + ba).astype(f32))
    i = jax.nn.sigmoid((jnp.einsum('btni,nij->btnj', xh, wx).reshape(B, T, LRU_W) + bx).astype(f32))
    log_a = -LRU_C * r * jax.nn.softplus(-lam.astype(f32))
    a = jnp.exp(log_a)
    b = jnp.sqrt(-jnp.expm1(2.0 * log_a)) * i * xc.astype(f32)
    b = b.at[:, 0].add(a[:, 0] * h0.astype(f32))
    _, h = lax.associative_scan(_linear_combine, (a, b), axis=1)
    y = (h.astype(x.dtype) * jax.nn.silu(gate)) @ w_out
    return y, new_buf, h[:, -1]


def rwkv7_mixer(x, shift_buf, S0, mu, w_in, w0, w1, w2, a0, a1, a2, k_k, k_a, r_k, gn_g, gn_b, w_out):
    B, T, _ = x.shape
    f32 = jnp.float32
    x_prev = jnp.concatenate([shift_buf[:, None, :].astype(x.dtype), x[:, :-1]], axis=1)
    xx = x_prev - x
    xm = x[None] + xx[None] * mu[:, None, None, :]
    proj = jnp.einsum('nbtd,ndw->nbtw', xm[:4], w_in)
    r, k, v, gate = proj[0], proj[1], proj[2], proj[3]
    w_raw = (w0 + jnp.tanh(xm[4] @ w1) @ w2).astype(f32)
    log_w = -jnp.exp(-jax.nn.softplus(-w_raw) - 0.5)
    a = jax.nn.sigmoid((a0 + (xm[5] @ a1) @ a2).astype(f32))
    hd = lambda z: z.astype(f32).reshape(B, T, RW_H, RW_N)
    kk = l2norm(hd(k * k_k), RW_NORM_EPS)
    kh = hd(k.astype(f32) * (1.0 + (a - 1.0) * k_a))
    rh, vh, ah, wh = hd(r), hd(v), hd(a), jnp.exp(hd(log_w))

    def step(S, inp):
        r_t, w_t, k_t, v_t, a_t, b_t = inp
        sa = jnp.einsum('bhij,bhj->bhi', S, a_t)
        S = S * w_t[:, :, None, :] + sa[..., None] * b_t[:, :, None, :] + v_t[..., None] * k_t[:, :, None, :]
        return S, jnp.einsum('bhij,bhj->bhi', S, r_t)

    seq = tuple(jnp.moveaxis(z, 1, 0) for z in (rh, wh, kh, vh, -kk, kk * ah))
    S_T, ys = lax.scan(step, S0.astype(f32), seq)
    y = jnp.moveaxis(ys, 0, 1)
    m = jnp.mean(y, -1, keepdims=True)
    var = jnp.mean(jnp.square(y - m), -1, keepdims=True)
    yn = (y - m) * lax.rsqrt(var + RW_GN_EPS) * gn_g.reshape(RW_H, RW_N) + gn_b.reshape(RW_H, RW_N)
    bonus = jnp.sum(rh * kh * r_k.reshape(RW_H, RW_N), -1, keepdims=True) * vh
    o = (yn + bonus).reshape(B, T, RW_W).astype(x.dtype)
    return (o * jax.nn.silu(gate)) @ w_out, x[:, -1], S_T


def chunk_gated_delta(q, k, v, g, beta, S0):
    B, T, H, _ = q.shape
    C = min(GDN_CHUNK, T)
    n = -(-T // C)
    pad = n * C - T

    def blocks(z):
        z = jnp.pad(z, [(0, 0), (0, pad)] + [(0, 0)] * (z.ndim - 2))
        z = z.reshape((B, n, C) + z.shape[2:])
        return jnp.swapaxes(jnp.swapaxes(z, 2, 3), 0, 1)

    q, k, v, g, beta = blocks(q), blocks(k), blocks(v), blocks(g), blocks(beta)
    gc = jnp.cumsum(g, axis=-1)
    diff = gc[..., :, None] - gc[..., None, :]
    idx = jnp.arange(C)
    strict = idx[:, None] > idx[None, :]
    incl = idx[:, None] >= idx[None, :]
    kb = k * beta[..., None]
    L = jnp.where(strict, jnp.einsum('nbhid,nbhjd->nbhij', kb, k) * jnp.exp(jnp.where(strict, diff, 0.0)), 0.0)
    rhs = jnp.concatenate([v * beta[..., None], kb * jnp.exp(gc)[..., None]], axis=-1)
    sol = lax.linalg.triangular_solve(L, rhs, left_side=True, lower=True, unit_diagonal=True)
    U, Wd = sol[..., :GDN_DV], sol[..., GDN_DV:]
    A = jnp.where(incl, jnp.einsum('nbhid,nbhjd->nbhij', q, k) * jnp.exp(jnp.where(incl, diff, 0.0)), 0.0)
    qg = q * jnp.exp(gc)[..., None]
    kd = k * jnp.exp(gc[..., -1:] - gc)[..., None]
    g_last = jnp.exp(gc[..., -1])

    def step(S, inp):
        U_i, W_i, qg_i, kd_i, A_i, gl_i = inp
        v_new = U_i - jnp.einsum('bhck,bhkv->bhcv', W_i, S)
        o = jnp.einsum('bhck,bhkv->bhcv', qg_i, S) + jnp.einsum('bhij,bhjv->bhiv', A_i, v_new)
        S = S * gl_i[..., None, None] + jnp.einsum('bhck,bhcv->bhkv', kd_i, v_new)
        return S, o

    S_T, o = lax.scan(step, S0, (U, Wd, qg, kd, A, g_last))
    o = jnp.swapaxes(jnp.swapaxes(o, 0, 1), 2, 3).reshape(B, n * C, H, GDN_DV)[:, :T]
    return o, S_T


def gdn_mixer(x, conv_buf, S0, w_in, conv_w, a_log, dt_bias, norm_g, w_out):
    B, T, _ = x.shape
    f32 = jnp.float32
    u = x @ w_in
    o1 = GDN_CONV_CH
    o2 = o1 + GDN_VAL_W
    o3 = o2 + GDN_HV
    qkv, new_buf = causal_dwconv(u[..., :o1], conv_buf, conv_w, None)
    qkv = jax.nn.silu(qkv).astype(f32)
    q = qkv[..., :GDN_KEY_W].reshape(B, T, GDN_HK, GDN_DK)
    k = qkv[..., GDN_KEY_W:2 * GDN_KEY_W].reshape(B, T, GDN_HK, GDN_DK)
    v = qkv[..., 2 * GDN_KEY_W:].reshape(B, T, GDN_HV, GDN_DV)
    rep = GDN_HV // GDN_HK
    q = jnp.repeat(l2norm(q, GDN_EPS) * (GDN_DK ** -0.5), rep, axis=2)
    k = jnp.repeat(l2norm(k, GDN_EPS), rep, axis=2)
    beta = jax.nn.sigmoid(u[..., o2:o3].astype(f32))
    g = -jnp.exp(a_log.astype(f32)) * jax.nn.softplus(u[..., o3:].astype(f32) + dt_bias)
    o, S_T = chunk_gated_delta(q, k, v, g, beta, S0.astype(f32))
    o = o * lax.rsqrt(jnp.mean(o * o, -1, keepdims=True) + GDN_EPS) * norm_g
    z = u[..., o1:o2].astype(f32).reshape(B, T, GDN_HV, GDN_DV)
    o = (o * jax.nn.silu(z)).reshape(B, T, GDN_VAL_W).astype(x.dtype)
    return o @ w_out, new_buf, S_T


def trunk(x, st, w):
    (lru_conv, lru_h, rw_shift, rw_S, gdn_conv, gdn_S) = st
    (ln_g, ln_b, lru_w_in, lru_conv_w, lru_conv_b, lru_wa, lru_ba, lru_wx, lru_bx, lru_lambda, lru_w_out,
     rw_mu, rw_w_in, rw_w0, rw_w1, rw_w2, rw_a0, rw_a1, rw_a2, rw_k_k, rw_k_a, rw_r_k, rw_gn_g, rw_gn_b,
     rw_w_out, gdn_w_in, gdn_conv_w, gdn_a_log, gdn_dt_bias, gdn_norm_g, gdn_w_out) = w
    new = ([], [], [], [], [], [])
    ia = ib = ic = 0
    for layer in range(DEPTH):
        kind = layer % N_MIXERS
        if kind == 0:
            f, c, h = rglru_mixer(x, lru_conv[ia], lru_h[ia], lru_w_in[ia], lru_conv_w[ia], lru_conv_b[ia],
                                  lru_wa[ia], lru_ba[ia], lru_wx[ia], lru_bx[ia], lru_lambda[ia], lru_w_out[ia])
            new[0].append(c)
            new[1].append(h)
            ia += 1
        elif kind == 1:
            f, s, S = rwkv7_mixer(x, rw_shift[ib], rw_S[ib], rw_mu[ib], rw_w_in[ib], rw_w0[ib], rw_w1[ib],
                                  rw_w2[ib], rw_a0[ib], rw_a1[ib], rw_a2[ib], rw_k_k[ib], rw_k_a[ib],
                                  rw_r_k[ib], rw_gn_g[ib], rw_gn_b[ib], rw_w_out[ib])
            new[2].append(s)
            new[3].append(S)
            ib += 1
        else:
            f, c, S = gdn_mixer(x, gdn_conv[ic], gdn_S[ic], gdn_w_in[ic], gdn_conv_w[ic], gdn_a_log[ic],
                                gdn_dt_bias[ic], gdn_norm_g[ic], gdn_w_out[ic])
            new[4].append(c)
            new[5].append(S)
            ic += 1
        x = layer_norm(DN_ALPHA * x + f, ln_g[layer], ln_b[layer])
    dt = x.dtype
    return x, tuple(jnp.stack(s).astype(dt) for s in new)


def setup_inputs(seed: int = 0) -> dict:
    key = jax.random.key(seed)
    ks = iter(jax.random.split(key, 64))
    f32 = jnp.float32
    nrm = lambda shape, scale: jax.random.normal(next(ks), shape, f32) * scale
    uni = lambda shape, lo, hi: jax.random.uniform(next(ks), shape, f32, minval=lo, maxval=hi)
    D = D_MODEL
    d = {}
    d['x_prompt'] = nrm((BATCH, SEQ, D), 1.0)
    d['x_sample'] = nrm((DEC_BATCH, DEC_SEQ, D), 1.0)
    d['state_lru_conv'] = nrm((N_A, DEC_BATCH, CONV_W - 1, LRU_W), 1.0)
    d['state_lru_h'] = nrm((N_A, DEC_BATCH, LRU_W), 0.5)
    d['state_rwkv_shift'] = nrm((N_B, DEC_BATCH, D), 1.0)
    d['state_rwkv_wkv'] = nrm((N_B, DEC_BATCH, RW_H, RW_N, RW_N), 0.1)
    d['state_gdn_conv'] = nrm((N_C, DEC_BATCH, CONV_W - 1, GDN_CONV_CH), 1.0)
    d['state_gdn_S'] = nrm((N_C, DEC_BATCH, GDN_HV, GDN_DK, GDN_DV), 0.1)
    d['ln_g'] = 1.0 + nrm((DEPTH, D), 0.02)
    d['ln_b'] = nrm((DEPTH, D), 0.02)
    d['lru_w_in'] = nrm((N_A, D, 2 * LRU_W), D ** -0.5)
    d['lru_conv_w'] = nrm((N_A, CONV_W, LRU_W), CONV_W ** -0.5)
    d['lru_conv_b'] = nrm((N_A, LRU_W), 0.01)
    d['lru_wa'] = nrm((N_A, LRU_BLOCKS, LRU_BS, LRU_BS), LRU_BS ** -0.5)
    d['lru_ba'] = nrm((N_A, LRU_W), 0.01)
    d['lru_wx'] = nrm((N_A, LRU_BLOCKS, LRU_BS, LRU_BS), LRU_BS ** -0.5)
    d['lru_bx'] = nrm((N_A, LRU_W), 0.01)
    s = uni((N_A, LRU_W), 0.9, 0.999) ** (1.0 / LRU_C)
    d['lru_lambda'] = jnp.log(s) - jnp.log1p(-s)
    d['lru_w_out'] = nrm((N_A, LRU_W, D), LRU_W ** -0.5 * DN_BETA)
    d['rw_mu'] = uni((N_B, 6, D), 0.0, 1.0)
    d['rw_w_in'] = nrm((N_B, 4, D, RW_W), D ** -0.5)
    d['rw_w0'] = jnp.linspace(-6.0, -1.0, RW_W, dtype=f32)[None] + nrm((N_B, RW_W), 0.1)
    d['rw_w1'] = nrm((N_B, D, RW_LORA_W), D ** -0.5)
    d['rw_w2'] = nrm((N_B, RW_LORA_W, RW_W), 0.1 * RW_LORA_W ** -0.5)
    d['rw_a0'] = nrm((N_B, RW_W), 0.1)
    d['rw_a1'] = nrm((N_B, D, RW_LORA_A), D ** -0.5)
    d['rw_a2'] = nrm((N_B, RW_LORA_A, RW_W), RW_LORA_A ** -0.5)
    d['rw_k_k'] = 0.85 + nrm((N_B, RW_W), 0.02)
    d['rw_k_a'] = 1.0 + nrm((N_B, RW_W), 0.02)
    d['rw_r_k'] = nrm((N_B, RW_W), 0.1)
    d['rw_gn_g'] = 1.0 + nrm((N_B, RW_W), 0.02)
    d['rw_gn_b'] = nrm((N_B, RW_W), 0.02)
    d['rw_w_out'] = nrm((N_B, RW_W, D), RW_W ** -0.5 * DN_BETA)
    d['gdn_w_in'] = nrm((N_C, D, GDN_IN_W), D ** -0.5)
    d['gdn_conv_w'] = nrm((N_C, CONV_W, GDN_CONV_CH), CONV_W ** -0.5)
    d['gdn_a_log'] = jnp.log(uni((N_C, GDN_HV), 1.0, 16.0))
    dt = jnp.exp(uni((N_C, GDN_HV), math.log(1e-3), math.log(0.1)))
    d['gdn_dt_bias'] = dt + jnp.log(-jnp.expm1(-dt))
    d['gdn_norm_g'] = 1.0 + nrm((N_C, GDN_DV), 0.02)
    d['gdn_w_out'] = nrm((N_C, GDN_VAL_W, D), GDN_VAL_W ** -0.5 * DN_BETA)
    return d


def reference(x_prompt, x_sample, state_lru_conv, state_lru_h, state_rwkv_shift, state_rwkv_wkv,
              state_gdn_conv, state_gdn_S, ln_g, ln_b, lru_w_in, lru_conv_w, lru_conv_b, lru_wa, lru_ba,
              lru_wx, lru_bx, lru_lambda, lru_w_out, rw_mu, rw_w_in, rw_w0, rw_w1, rw_w2, rw_a0, rw_a1,
              rw_a2, rw_k_k, rw_k_a, rw_r_k, rw_gn_g, rw_gn_b, rw_w_out, gdn_w_in, gdn_conv_w, gdn_a_log,
              gdn_dt_bias, gdn_norm_g, gdn_w_out):
    w = (ln_g, ln_b, lru_w_in, lru_conv_w, lru_conv_b, lru_wa, lru_ba, lru_wx, lru_bx, lru_lambda, lru_w_out,
         rw_mu, rw_w_in, rw_w0, rw_w1, rw_w2, rw_a0, rw_a1, rw_a2, rw_k_k, rw_k_a, rw_r_k, rw_gn_g, rw_gn_b,
         rw_w_out, gdn_w_in, gdn_conv_w, gdn_a_log, gdn_dt_bias, gdn_norm_g, gdn_w_out)
    bp = x_prompt.shape[0]
    dt = x_prompt.dtype
    zero_state = (jnp.zeros((N_A, bp, CONV_W - 1, LRU_W), dt),
                  jnp.zeros((N_A, bp, LRU_W), dt),
                  jnp.zeros((N_B, bp, D_MODEL), dt),
                  jnp.zeros((N_B, bp, RW_H, RW_N, RW_N), dt),
                  jnp.zeros((N_C, bp, CONV_W - 1, GDN_CONV_CH), dt),
                  jnp.zeros((N_C, bp, GDN_HV, GDN_DK, GDN_DV), dt))
    y_prompt, sp = trunk(x_prompt, zero_state, w)
    y_sample, ss = trunk(x_sample, (state_lru_conv, state_lru_h, state_rwkv_shift, state_rwkv_wkv,
                                    state_gdn_conv, state_gdn_S), w)
    return (y_prompt, y_sample, sp[0], ss[0], sp[1], ss[1], sp[2], ss[2], sp[3], ss[3], sp[4], ss[4], sp[5], ss[5])
```

```python
import functools
import math

import jax
import jax.numpy as jnp
from jax import lax
from jax.experimental import pallas as pl
from jax.experimental.pallas import tpu as pltpu

F32 = jnp.float32
BF = jnp.bfloat16

D_MODEL = 1024
DEPTH = 4
N_MIXERS = 3
DN_ALPHA = (2.0 * DEPTH) ** 0.25
LN_EPS = 1e-5
CONV_W = 4

LRU_W = D_MODEL
LRU_BLOCKS = 16
LRU_BS = LRU_W // LRU_BLOCKS
LRU_C = 8.0

RW_W = D_MODEL
RW_N = 64
RW_H = RW_W // RW_N
RW_GN_EPS = 64e-5
RW_NORM_EPS = 1e-12

GDN_HK = 4
GDN_HV = 8
GDN_DK = 128
GDN_DV = 128
GDN_KEY_W = GDN_HK * GDN_DK
GDN_VAL_W = GDN_HV * GDN_DV
GDN_CONV_CH = 2 * GDN_KEY_W + GDN_VAL_W
GDN_CHUNK = 64
GDN_EPS = 1e-6

LANES = 128
SUBLANES = 8
VMEM_LIMIT = 56 * 1024 * 1024
ROW_BLOCK = 256
RW_CHUNK = 64

_NN = (((1,), (0,)), ((), ()))
_NT = (((1,), (1,)), ((), ()))
_TN = (((0,), (0,)), ((), ()))


def _mm(a, b):
    return jnp.dot(a.astype(BF), b.astype(BF), preferred_element_type=F32)


def _split2(x):
    hi = x.astype(BF)
    lo = (x - hi.astype(F32)).astype(BF)
    return hi, lo


def _dot3(a, b, dims=_NN):
    ah, al = _split2(a)
    bh, bl = _split2(b)
    d = lambda p, q: lax.dot_general(p, q, dims, preferred_element_type=F32)
    return d(ah, bh) + (d(ah, bl) + d(al, bh))


def _dot_exact_lhs(a01, b):
    a = a01.astype(BF)
    b1 = b.astype(BF)
    r1 = b - b1.astype(F32)
    b2 = r1.astype(BF)
    b3 = (r1 - b2.astype(F32)).astype(BF)
    d = lambda q: jnp.dot(a, q, preferred_element_type=F32)
    return d(b1) + (d(b2) + d(b3))


def _layer_norm(z, g, b):
    mu = jnp.mean(z, axis=-1, keepdims=True)
    zc = z - mu
    var = jnp.mean(zc * zc, axis=-1, keepdims=True)
    return zc * lax.rsqrt(var + LN_EPS) * g + b


def _silu(x):
    return x * jax.nn.sigmoid(x)


def _cparams(n_axes):
    return pltpu.CompilerParams(dimension_semantics=("arbitrary",) * n_axes, vmem_limit_bytes=VMEM_LIMIT)


def _const_spec(shape):
    nd = len(shape)
    return pl.BlockSpec(shape, lambda *_: (0,) * nd)


def _halo_rows(n_steps, s):
    rows = n_steps * s
    return rows if rows % SUBLANES == 0 else SUBLANES


def _lru_kernel(x_ref, halo_ref, h0_ref, win_ref, cw_ref, cb_ref, wg_ref, bg_ref, lam_ref, wout_ref, lng_ref,
                lnb_ref, out_ref, tail_ref, hT_ref, xscr, ascr, bscr, hscr, *, s, R, H, Hs):
    c = pl.program_id(1)
    C = LRU_W
    tt = R // s

    @pl.when(c == 0)
    def _init():
        xscr[0:H, :] = halo_ref[...]
        hscr[...] = h0_ref[...]
        ascr[0:Hs, :] = jnp.ones((Hs, C), F32)
        bscr[0:Hs, :] = jnp.zeros((Hs, C), F32)

    x = x_ref[...]
    u = _mm(x, win_ref[...])
    xb = u[:, :C]
    gate = u[:, C:]

    xscr[H:H + R, :] = xb
    cw = cw_ref[...]
    xc = xb * cw[3:4, :] + cb_ref[...]
    for d in (1, 2, 3):
        xc = xc + xscr[pl.ds(H - d * s, R), :] * cw[3 - d:4 - d, :]
    tail = xscr[pl.ds(R, H), :]
    xscr[0:H, :] = tail
    tail_ref[...] = tail

    xcb = xc.astype(BF)
    ra, ix = [], []
    for g in range(C // 256):
        gt = jnp.dot(xcb[:, 256 * g:256 * (g + 1)], wg_ref[g], preferred_element_type=F32)
        ra.append(gt[:, :256])
        ix.append(gt[:, 256:])
    bg = bg_ref[...]
    r = jax.nn.sigmoid(jnp.concatenate(ra, axis=1) + bg[0:1, :])
    i = jax.nn.sigmoid(jnp.concatenate(ix, axis=1) + bg[1:2, :])
    log_a = (-LRU_C) * r * jax.nn.softplus(-lam_ref[...])
    a = jnp.exp(log_a)
    th = jnp.tanh(log_a)
    b = jnp.sqrt(-2.0 * th / (1.0 - th)) * i * xc

    rows = lax.broadcasted_iota(jnp.int32, (R, 1), 0)
    hprev = hscr[...]
    if s > 1:
        hprev = jnp.concatenate([hprev] * tt, axis=0)
    b = b + jnp.where(rows < s, a * hprev, 0.0)
    k = 1
    while k < tt:
        ascr[Hs:Hs + R, :] = a
        bscr[Hs:Hs + R, :] = b
        a_s = ascr[pl.ds(Hs - k * s, R), :]
        b_s = bscr[pl.ds(Hs - k * s, R), :]
        b = a * b_s + b
        a = a * a_s
        k *= 2
    h = b
    bscr[Hs:Hs + R, :] = h
    hlast = bscr[pl.ds(Hs + R - s, s), :]
    hscr[...] = hlast
    hT_ref[...] = hlast

    y = _mm(h * _silu(gate), wout_ref[...])
    out_ref[...] = _layer_norm(DN_ALPHA * x + y, lng_ref[...], lnb_ref[...])


def _lru_layer(x3, halo, h0, win, cw, cb, wg, bg, lam, wout, lng, lnb, *, s, R):
    NB, TT, D = x3.shape
    C = LRU_W
    H = halo.shape[1]
    tt = R // s
    Hs = max(SUBLANES, (tt // 2) * s)
    kern = functools.partial(_lru_kernel, s=s, R=R, H=H, Hs=Hs)
    return pl.pallas_call(
        kern,
        grid=(NB, TT // R),
        in_specs=[
            pl.BlockSpec((None, R, D), lambda i, c: (i, c, 0)),
            pl.BlockSpec((None, H, C), lambda i, c: (i, 0, 0)),
            pl.BlockSpec((None, s, C), lambda i, c: (i, 0, 0)),
            _const_spec(win.shape), _const_spec(cw.shape), _const_spec(cb.shape), _const_spec(wg.shape),
            _const_spec(bg.shape), _const_spec(lam.shape), _const_spec(wout.shape), _const_spec(lng.shape),
            _const_spec(lnb.shape),
        ],
        out_specs=[
            pl.BlockSpec((None, R, D), lambda i, c: (i, c, 0)),
            pl.BlockSpec((None, H, C), lambda i, c: (i, 0, 0)),
            pl.BlockSpec((None, s, C), lambda i, c: (i, 0, 0)),
        ],
        out_shape=[
            jax.ShapeDtypeStruct((NB, TT, D), F32),
            jax.ShapeDtypeStruct((NB, H, C), F32),
            jax.ShapeDtypeStruct((NB, s, C), F32),
        ],
        scratch_shapes=[
            pltpu.VMEM((H + R, C), F32),
            pltpu.VMEM((Hs + R, C), F32),
            pltpu.VMEM((Hs + R, C), F32),
            pltpu.VMEM((s, C), F32),
        ],
        compiler_params=_cparams(2),
        name="lru_layer",
    )(x3, halo, h0, win, cw, cb, wg, bg, lam, wout, lng, lnb)


def _lru_gate_weights(wa, wx):
    def bd(w):
        w4 = w.reshape(4, 4, LRU_BS, LRU_BS)
        eye = jnp.eye(4, dtype=w.dtype)
        return jnp.einsum("gaij,ab->gaibj", w4, eye).reshape(4, 256, 256)
    return jnp.concatenate([bd(wa), bd(wx)], axis=2).astype(BF)


def _post_kernel(x_ref, o_ref, g_ref, w_ref, lng_ref, lnb_ref, out_ref):
    nh = o_ref.shape[0]
    o = jnp.concatenate([o_ref[p] for p in range(nh)], axis=1)
    g = jnp.concatenate([g_ref[p] for p in range(nh)], axis=1)
    y = _mm(o * _silu(g), w_ref[...])
    out_ref[...] = _layer_norm(DN_ALPHA * x_ref[...] + y, lng_ref[...], lnb_ref[...])


def _post_layer(x3, o4, g4, w, lng, lnb, *, R):
    NB, TT, D = x3.shape
    nh = o4.shape[1]
    return pl.pallas_call(
        _post_kernel,
        grid=(NB, TT // R),
        in_specs=[
            pl.BlockSpec((None, R, D), lambda i, c: (i, c, 0)),
            pl.BlockSpec((None, nh, R, LANES), lambda i, c: (i, 0, c, 0)),
            pl.BlockSpec((None, nh, R, LANES), lambda i, c: (i, 0, c, 0)),
            _const_spec(w.shape), _const_spec(lng.shape), _const_spec(lnb.shape),
        ],
        out_specs=pl.BlockSpec((None, R, D), lambda i, c: (i, c, 0)),
        out_shape=jax.ShapeDtypeStruct((NB, TT, D), F32),
        compiler_params=_cparams(2),
        name="post_layer",
    )(x3, o4, g4, w, lng, lnb)


def _rwkv_pre_kernel(x_ref, halo_ref, mu_ref, win_ref, w0_ref, w1_ref, w2_ref, a0_ref, a1_ref, a2_ref,
                     r_ref, k_ref, v_ref, g_ref, lw_ref, a_ref, xscr, *, s, R, H):
    c = pl.program_id(1)

    @pl.when(c == 0)
    def _init():
        xscr[0:H, :] = halo_ref[...]

    x = x_ref[...]
    xscr[H:H + R, :] = x
    xprev = xscr[pl.ds(H - s, R), :]
    xscr[0:H, :] = xscr[pl.ds(R, H), :]
    xx = xprev - x
    mu = mu_ref[...]
    xm = lambda n: x + xx * mu[n:n + 1, :]

    def put(ref, val):
        for p in range(RW_W // LANES):
            ref[p] = val[:, LANES * p:LANES * (p + 1)]

    put(r_ref, _mm(xm(0), win_ref[0]))
    put(k_ref, _mm(xm(1), win_ref[1]))
    put(v_ref, _mm(xm(2), win_ref[2]))
    put(g_ref, _mm(xm(3), win_ref[3]))
    w_raw = w0_ref[...] + _mm(jnp.tanh(_mm(xm(4), w1_ref[...])), w2_ref[...])
    put(lw_ref, (-math.exp(-0.5)) * jax.nn.sigmoid(w_raw))
    put(a_ref, jax.nn.sigmoid(a0_ref[...] + _mm(_mm(xm(5), a1_ref[...]), a2_ref[...])))


def _rwkv_pre(x3, halo, mu, win, w0, w1, w2, a0, a1, a2, *, s, R):
    NB, TT, D = x3.shape
    H = halo.shape[1]
    nh = RW_W // LANES
    kern = functools.partial(_rwkv_pre_kernel, s=s, R=R, H=H)
    ospec = pl.BlockSpec((None, nh, R, LANES), lambda i, c: (i, 0, c, 0))
    oshape = jax.ShapeDtypeStruct((NB, nh, TT, LANES), F32)
    return pl.pallas_call(
        kern,
        grid=(NB, TT // R),
        in_specs=[
            pl.BlockSpec((None, R, D), lambda i, c: (i, c, 0)),
            pl.BlockSpec((None, H, D), lambda i, c: (i, 0, 0)),
            _const_spec(mu.shape), _const_spec(win.shape), _const_spec(w0.shape), _const_spec(w1.shape),
            _const_spec(w2.shape), _const_spec(a0.shape), _const_spec(a1.shape), _const_spec(a2.shape),
        ],
        out_specs=[ospec] * 6,
        out_shape=[oshape] * 6,
        scratch_shapes=[pltpu.VMEM((H + R, D), F32)],
        compiler_params=_cparams(2),
        name="rwkv_pre",
    )(x3, halo, mu, win, w0, w1, w2, a0, a1, a2)


def _seg_sum(x):
    lane = lax.broadcasted_iota(jnp.int32, x.shape, 1)
    lo = lane < RW_N
    s0 = jnp.sum(jnp.where(lo, x, 0.0), axis=-1, keepdims=True)
    s1 = jnp.sum(jnp.where(lo, 0.0, x), axis=-1, keepdims=True)
    return jnp.where(lo, s0, s1)


def _stack2(x):
    lane = lax.broadcasted_iota(jnp.int32, x.shape, 1)
    lo = lane < RW_N
    return jnp.concatenate([jnp.where(lo, x, 0.0), jnp.where(lo, 0.0, x)], axis=0)


def _unit_lower_inverse(L, n):
    m = L.shape[0]
    eye = (lax.broadcasted_iota(jnp.int32, (m, m), 0) == lax.broadcasted_iota(jnp.int32, (m, m), 1)).astype(F32)
    inv = eye + L
    Lp = L
    span = 2
    while span < n:
        Lp = _dot3(Lp, Lp)
        inv = inv + _dot3(inv, Lp)
        span *= 2
    return inv


def _rwkv_rec_kernel(r_ref, k_ref, v_ref, lw_ref, a_ref, s0_ref, kk_ref, ka_ref, rk_ref, gg_ref, gb_ref,
                     o_ref, sT_ref, sscr, *, C, nchunk):
    tb = pl.program_id(1)
    npair = r_ref.shape[0]
    C2 = 2 * C
    lane_sq = lax.broadcasted_iota(jnp.int32, (LANES, LANES), 1)
    row_sq = lax.broadcasted_iota(jnp.int32, (LANES, LANES), 0)
    same_head = (lane_sq < RW_N) == (row_sq < RW_N)

    @pl.when(tb == 0)
    def _init():
        for p in range(npair):
            s2 = s0_ref[p]
            sscr[p] = jnp.where(same_head, jnp.concatenate([s2, s2], axis=1), 0.0)

    ti = lax.broadcasted_iota(jnp.int32, (C2, C2), 0)
    si = lax.broadcasted_iota(jnp.int32, (C2, C2), 1)
    blk = (ti < C) == (si < C)
    strict = blk & (si < ti)
    incl = blk & (si <= ti)
    tri = (lax.broadcasted_iota(jnp.int32, (C, C), 1) <= lax.broadcasted_iota(jnp.int32, (C, C), 0)).astype(F32)

    def chunk(idx, carry):
        p = idx // nchunk
        ci = idx % nchunk
        rs = pl.ds(pl.multiple_of(ci * C, C), C)
        r = r_ref[p, rs, :]
        k = k_ref[p, rs, :]
        v = v_ref[p, rs, :]
        lw = lw_ref[p, rs, :]
        a = a_ref[p, rs, :]
        k_k = kk_ref[p]
        k_a = ka_ref[p]
        S = sscr[p]

        kn = k * k_k
        kk = kn * lax.rsqrt(_seg_sum(kn * kn) + RW_NORM_EPS)
        kh = k * (1.0 + (a - 1.0) * k_a)
        avec = -kk
        bvec = kk * a

        cum = _dot_exact_lhs(tri, lw)
        cum_last = cum[C - 1:C, :]
        e_neg = jnp.exp(-cum)
        e_dec = jnp.exp(cum_last - cum)
        At = avec * jnp.exp(cum - lw)
        Rt = r * jnp.exp(cum)
        Bt = bvec * e_neg
        Kt = kh * e_neg

        At2, Rt2, Bt2, Kt2, V2 = _stack2(At), _stack2(Rt), _stack2(Bt), _stack2(Kt), _stack2(v)
        Lab = jnp.where(strict, _dot3(At2, Bt2, _NT), 0.0)
        Lak = jnp.where(strict, _dot3(At2, Kt2, _NT), 0.0)
        Mrb = jnp.where(incl, _dot3(Rt2, Bt2, _NT), 0.0)
        Mrk = jnp.where(incl, _dot3(Rt2, Kt2, _NT), 0.0)
        Tinv = _unit_lower_inverse(Lab, C)

        U0 = _dot3(At, S, _NT)
        Y0 = _dot3(Rt, S, _NT)
        U2 = _dot3(Tinv, _stack2(U0) + _dot3(Lak, V2))
        Y2 = _dot3(Mrb, U2) + _dot3(Mrk, V2)
        y = Y0 + Y2[:C, :] + Y2[C:, :]
        sscr[p] = S * jnp.exp(cum_last) + _dot3(U2, _stack2(bvec * e_dec), _TN) + _dot3(V2, _stack2(kh * e_dec), _TN)

        m = _seg_sum(y) * (1.0 / RW_N)
        yc = y - m
        var = _seg_sum(yc * yc) * (1.0 / RW_N)
        yn = yc * lax.rsqrt(var + RW_GN_EPS) * gg_ref[p] + gb_ref[p]
        bonus = _seg_sum(r * kh * rk_ref[p]) * v
        o_ref[p, rs, :] = yn + bonus
        return carry

    lax.fori_loop(0, npair * nchunk, chunk, 0)

    lane_h = lax.broadcasted_iota(jnp.int32, (LANES, RW_N), 0) < RW_N
    for p in range(npair):
        S = sscr[p]
        sT_ref[p] = jnp.where(lane_h, S[:, :RW_N], S[:, RW_N:])


def _rwkv_rec(r4, k4, v4, lw4, a4, s0, k_k, k_a, r_k, gn_g, gn_b, *, C, Rt):
    B, npair, T, _ = r4.shape
    nchunk = Rt // C
    kern = functools.partial(_rwkv_rec_kernel, C=C, nchunk=nchunk)
    tspec = pl.BlockSpec((None, npair, Rt, LANES), lambda b, t: (b, 0, t, 0))
    sspec = pl.BlockSpec((None, npair, LANES, RW_N), lambda b, t: (b, 0, 0, 0))
    wspec = _const_spec((npair, 1, LANES))
    return pl.pallas_call(
        kern,
        grid=(B, T // Rt),
        in_specs=[tspec] * 5 + [sspec] + [wspec] * 5,
        out_specs=[tspec, sspec],
        out_shape=[jax.ShapeDtypeStruct(r4.shape, F32), jax.ShapeDtypeStruct(s0.shape, F32)],
        scratch_shapes=[pltpu.VMEM((npair, LANES, LANES), F32)],
        compiler_params=_cparams(2),
        name="rwkv_rec",
    )(r4, k4, v4, lw4, a4, s0, k_k, k_a, r_k, gn_g, gn_b)


def _gdn_pre_kernel(x_ref, halo_ref, win_ref, cw_ref, alog_ref, dtb_ref,
                    q_ref, k_ref, v_ref, z_ref, gc_ref, beta_ref, tail_ref, xscr, gscr, *, s, R, H, Hg, C):
    c = pl.program_id(1)
    CH = GDN_CONV_CH
    tt = R // s

    @pl.when(c == 0)
    def _init():
        xscr[0:H, :] = halo_ref[...]
        gscr[0:Hg, :] = jnp.zeros((Hg, LANES), F32)

    x = x_ref[...]
    u = _mm(x, win_ref[...])
    xb = u[:, :CH]
    xscr[H:H + R, :] = xb
    cw = cw_ref[...]
    y = xb * cw[3:4, :]
    for d in (1, 2, 3):
        y = y + xscr[pl.ds(H - d * s, R), :] * cw[3 - d:4 - d, :]
    tail = xscr[pl.ds(R, H), :]
    xscr[0:H, :] = tail
    tail_ref[...] = tail
    qkv = _silu(y)

    def l2n(z, scale):
        return z * (lax.rsqrt(jnp.sum(z * z, axis=-1, keepdims=True) + GDN_EPS) * scale)

    for h in range(GDN_HK):
        q_ref[h] = l2n(qkv[:, LANES * h:LANES * (h + 1)], GDN_DK ** -0.5)
        k_ref[h] = l2n(qkv[:, GDN_KEY_W + LANES * h:GDN_KEY_W + LANES * (h + 1)], 1.0)
    for h in range(GDN_HV):
        v_ref[h] = qkv[:, 2 * GDN_KEY_W + LANES * h:2 * GDN_KEY_W + LANES * (h + 1)]
        z_ref[h] = u[:, CH + LANES * h:CH + LANES * (h + 1)]

    bg = u[:, CH + GDN_VAL_W:CH + GDN_VAL_W + LANES]
    beta = jax.nn.sigmoid(bg)
    g = -jnp.exp(alog_ref[...]) * jax.nn.softplus(bg + dtb_ref[...])
    t_in = (lax.broadcasted_iota(jnp.int32, (R, 1), 0) // s) % C
    k = 1
    while k < min(C, tt):
        gscr[Hg:Hg + R, :] = g
        g = g + jnp.where(t_in >= k, gscr[pl.ds(Hg - k * s, R), :], 0.0)
        k *= 2
    for h in range(GDN_HV):
        beta_ref[h] = jnp.broadcast_to(beta[:, h:h + 1], (R, LANES))
        gc_ref[h] = jnp.broadcast_to(g[:, GDN_HV + h:GDN_HV + h + 1], (R, LANES))


def _gdn_pre(x3, halo, win, cw, alog, dtb, *, s, R, C):
    NB, TT, D = x3.shape
    H = halo.shape[1]
    tt = R // s
    Hg = max(SUBLANES, (min(C, tt) // 2) * s)
    kern = functools.partial(_gdn_pre_kernel, s=s, R=R, H=H, Hg=Hg, C=C)

    def ospec(nh):
        return pl.BlockSpec((None, nh, R, LANES), lambda i, c: (i, 0, c, 0))

    def oshape(nh):
        return jax.ShapeDtypeStruct((NB, nh, TT, LANES), F32)

    return pl.pallas_call(
        kern,
        grid=(NB, TT // R),
        in_specs=[
            pl.BlockSpec((None, R, D), lambda i, c: (i, c, 0)),
            pl.BlockSpec((None, H, GDN_CONV_CH), lambda i, c: (i, 0, 0)),
            _const_spec(win.shape), _const_spec(cw.shape), _const_spec(alog.shape), _const_spec(dtb.shape),
        ],
        out_specs=[ospec(GDN_HK), ospec(GDN_HK), ospec(GDN_HV), ospec(GDN_HV), ospec(GDN_HV), ospec(GDN_HV),
                   pl.BlockSpec((None, H, GDN_CONV_CH), lambda i, c: (i, 0, 0))],
        out_shape=[oshape(GDN_HK), oshape(GDN_HK), oshape(GDN_HV), oshape(GDN_HV), oshape(GDN_HV), oshape(GDN_HV),
                   jax.ShapeDtypeStruct((NB, H, GDN_CONV_CH), F32)],
        scratch_shapes=[pltpu.VMEM((H + R, GDN_CONV_CH), F32), pltpu.VMEM((Hg + R, LANES), F32)],
        compiler_params=_cparams(2),
        name="gdn_pre",
    )(x3, halo, win, cw, alog, dtb)


def _gdn_rec_kernel(q_ref, k_ref, v_ref, gc_ref, beta_ref, s0_ref, ng_ref, o_ref, sT_ref, sscr, *, C, nchunk):
    tb = pl.program_id(1)
    rep = GDN_HV // GDN_HK

    @pl.when(tb == 0)
    def _init():
        sscr[...] = s0_ref[...]

    ti = lax.broadcasted_iota(jnp.int32, (C, C), 0)
    si = lax.broadcasted_iota(jnp.int32, (C, C), 1)
    strict = si < ti
    incl = si <= ti

    def chunk(idx, carry):
        h = idx // nchunk
        ci = idx % nchunk
        rs = pl.ds(pl.multiple_of(ci * C, C), C)
        q = q_ref[h // rep, rs, :]
        k = k_ref[h // rep, rs, :]
        v = v_ref[h, rs, :]
        gc = gc_ref[h, rs, :]
        beta = beta_ref[h, rs, :]
        S = sscr[h]

        gcol = gc[:, :C]
        grow = gc.T[:C, :]
        diff = gcol - grow
        kb = k * beta
        L = jnp.where(strict, _dot3(kb, k, _NT) * jnp.exp(jnp.where(strict, diff, 0.0)), 0.0)
        A = jnp.where(incl, _dot3(q, k, _NT) * jnp.exp(jnp.where(incl, diff, 0.0)), 0.0)
        Tinv = _unit_lower_inverse(-L, C)
        eg = jnp.exp(gc)
        U = _dot3(Tinv, v * beta)
        Wd = _dot3(Tinv, kb * eg)
        g_last = gc[C - 1:C, :]
        v_new = U - _dot3(Wd, S)
        o = _dot3(q * eg, S) + _dot3(A, v_new)
        sscr[h] = S * jnp.exp(g_last) + _dot3(k * jnp.exp(g_last - gc), v_new, _TN)
        o = o * lax.rsqrt(jnp.mean(o * o, axis=-1, keepdims=True) + GDN_EPS) * ng_ref[...]
        o_ref[h, rs, :] = o
        return carry

    lax.fori_loop(0, GDN_HV * nchunk, chunk, 0)
    sT_ref[...] = sscr[...]


def _gdn_rec(q4, k4, v4, gc4, beta4, s0, norm_g, *, C, Rt):
    B, _, T, _ = v4.shape
    nchunk = Rt // C
    kern = functools.partial(_gdn_rec_kernel, C=C, nchunk=nchunk)

    def tspec(nh):
        return pl.BlockSpec((None, nh, Rt, LANES), lambda b, t: (b, 0, t, 0))

    sspec = pl.BlockSpec((None, GDN_HV, GDN_DK, GDN_DV), lambda b, t: (b, 0, 0, 0))
    return pl.pallas_call(
        kern,
        grid=(B, T // Rt),
        in_specs=[tspec(GDN_HK), tspec(GDN_HK), tspec(GDN_HV), tspec(GDN_HV), tspec(GDN_HV), sspec,
                  _const_spec(norm_g.shape)],
        out_specs=[tspec(GDN_HV), sspec],
        out_shape=[jax.ShapeDtypeStruct(v4.shape, F32), jax.ShapeDtypeStruct(s0.shape, F32)],
        scratch_shapes=[pltpu.VMEM((GDN_HV, GDN_DK, GDN_DV), F32)],
        compiler_params=_cparams(2),
        name="gdn_rec",
    )(q4, k4, v4, gc4, beta4, s0, norm_g)


class _Group:
    def __init__(self, B, T, time_major):
        self.B, self.T, self.time_major = B, T, time_major
        if time_major:
            self.s, self.NB, self.TT, self.R = B, 1, T * B, T * B
        else:
            self.s, self.NB, self.TT, self.R = 1, B, T, min(ROW_BLOCK, T)

    def to_rows(self, x):
        if self.time_major:
            return jnp.swapaxes(x, 0, 1).reshape(1, self.TT, x.shape[-1])
        return x

    def from_rows(self, x3):
        if self.time_major:
            return jnp.swapaxes(x3.reshape(self.T, self.B, x3.shape[-1]), 0, 1)
        return x3

    def halo(self, st, n_steps):
        if self.time_major:
            return jnp.swapaxes(st, 0, 1).reshape(1, n_steps * self.B, st.shape[-1])
        H = _halo_rows(n_steps, 1)
        return jnp.pad(st, ((0, 0), (H - n_steps, 0), (0, 0)))

    def unhalo(self, tail, n_steps):
        if self.time_major:
            return jnp.swapaxes(tail.reshape(n_steps, self.B, tail.shape[-1]), 0, 1)
        return tail[:, tail.shape[1] - n_steps:]

    def vec(self, st):
        return st[None] if self.time_major else st[:, None, :]

    def unvec(self, v):
        return v[0] if self.time_major else v[:, 0, :]

    def heads_to_batch(self, a4, Tpad):
        if not self.time_major:
            return a4
        nh = a4.shape[1]
        a = a4.reshape(nh, self.T, self.B, LANES).transpose(2, 0, 1, 3)
        return jnp.pad(a, ((0, 0), (0, 0), (0, Tpad - self.T), (0, 0)))

    def heads_from_batch(self, a4):
        if not self.time_major:
            return a4
        nh = a4.shape[1]
        return a4[:, :, :self.T].transpose(1, 2, 0, 3).reshape(1, nh, self.TT, LANES)


def _lru_apply(g, x3, conv_st, h_st, p, lng, lnb):
    out, tail, hT = _lru_layer(x3, g.halo(conv_st, CONV_W - 1), g.vec(h_st), p["win"], p["cw"], p["cb"], p["wg"],
                               p["bg"], p["lam"], p["wout"], lng, lnb, s=g.s, R=g.R)
    return out, g.unhalo(tail, CONV_W - 1), g.unvec(hT)


def _rwkv_apply(g, x3, shift_st, wkv_st, p, lng, lnb):
    B, T = g.B, g.T
    r4, k4, v4, g4, lw4, a4 = _rwkv_pre(x3, g.halo(shift_st[:, None, :], 1), p["mu"], p["win"], p["w0"], p["w1"],
                                        p["w2"], p["a0"], p["a1"], p["a2"], s=g.s, R=g.R)
    if g.time_major:
        C = Tp = -(-T // SUBLANES) * SUBLANES
        Rt = Tp
    else:
        C = min(RW_CHUNK, T)
        Tp = T
        Rt = min(ROW_BLOCK, T)
    tb = lambda a: g.heads_to_batch(a, Tp)
    s0 = wkv_st.reshape(B, RW_H // 2, 2 * RW_N, RW_N)
    o4, sT = _rwkv_rec(tb(r4), tb(k4), tb(v4), tb(lw4), tb(a4), s0, p["k_k"], p["k_a"], p["r_k"], p["gn_g"],
                       p["gn_b"], C=C, Rt=Rt)
    out = _post_layer(x3, g.heads_from_batch(o4), g4, p["wout"], lng, lnb, R=g.R)
    new_shift = g.from_rows(x3)[:, -1]
    return out, new_shift, sT.reshape(B, RW_H, RW_N, RW_N)


def _gdn_apply(g, x3, conv_st, S_st, p, lng, lnb):
    B, T = g.B, g.T
    if g.time_major:
        C = T
        Cp = Tp = -(-T // SUBLANES) * SUBLANES
        Rt = Tp
    else:
        C = Cp = min(GDN_CHUNK, T)
        Tp = T
        Rt = min(ROW_BLOCK, T)
    q4, k4, v4, z4, gc4, beta4, tail = _gdn_pre(x3, g.halo(conv_st, CONV_W - 1), p["win"], p["cw"], p["alog"],
                                                p["dtb"], s=g.s, R=g.R, C=C)
    gcb = g.heads_to_batch(gc4, Tp)
    if Tp != T:
        gcb = gcb.at[:, :, T:, :].set(jnp.broadcast_to(gcb[:, :, T - 1:T, :], gcb[:, :, T:, :].shape))
    tb = lambda a: g.heads_to_batch(a, Tp)
    o4, sT = _gdn_rec(tb(q4), tb(k4), tb(v4), gcb, tb(beta4), S_st, p["norm_g"], C=Cp, Rt=Rt)
    out = _post_layer(x3, g.heads_from_batch(o4), z4, p["wout"], lng, lnb, R=g.R)
    return out, g.unhalo(tail, CONV_W - 1), sT


def _trunk(g, x, st, params, ln_g, ln_b):
    lru_conv, lru_h, rw_shift, rw_S, gdn_conv, gdn_S = st
    new = ([], [], [], [], [], [])
    x3 = g.to_rows(x)
    ia = ib = ic = 0
    for layer in range(DEPTH):
        kind = layer % N_MIXERS
        lng, lnb = ln_g[layer][None, :], ln_b[layer][None, :]
        if kind == 0:
            x3, c, h = _lru_apply(g, x3, lru_conv[ia], lru_h[ia], params["lru"][ia], lng, lnb)
            new[0].append(c)
            new[1].append(h)
            ia += 1
        elif kind == 1:
            x3, sh, S = _rwkv_apply(g, x3, rw_shift[ib], rw_S[ib], params["rwkv"][ib], lng, lnb)
            new[2].append(sh)
            new[3].append(S)
            ib += 1
        else:
            x3, c, S = _gdn_apply(g, x3, gdn_conv[ic], gdn_S[ic], params["gdn"][ic], lng, lnb)
            new[4].append(c)
            new[5].append(S)
            ic += 1
    return g.from_rows(x3), tuple(jnp.stack(s) for s in new)


def _prep_params(lru_w_in, lru_conv_w, lru_conv_b, lru_wa, lru_ba, lru_wx, lru_bx, lru_lambda, lru_w_out, rw_mu,
                 rw_w_in, rw_w0, rw_w1, rw_w2, rw_a0, rw_a1, rw_a2, rw_k_k, rw_k_a, rw_r_k, rw_gn_g, rw_gn_b,
                 rw_w_out, gdn_w_in, gdn_conv_w, gdn_a_log, gdn_dt_bias, gdn_norm_g, gdn_w_out):
    row = lambda v: v[None, :]
    lru = []
    for n in range(lru_w_in.shape[0]):
        lru.append(dict(win=lru_w_in[n].astype(BF), cw=lru_conv_w[n], cb=row(lru_conv_b[n]),
                        wg=_lru_gate_weights(lru_wa[n], lru_wx[n]), bg=jnp.stack([lru_ba[n], lru_bx[n]]),
                        lam=row(lru_lambda[n]), wout=lru_w_out[n].astype(BF)))
    pairw = lambda v: v.reshape(RW_W // LANES, 1, LANES)
    rwkv = []
    for n in range(rw_w_in.shape[0]):
        rwkv.append(dict(mu=rw_mu[n], win=rw_w_in[n].astype(BF), w0=row(rw_w0[n]), w1=rw_w1[n].astype(BF),
                         w2=rw_w2[n].astype(BF), a0=row(rw_a0[n]), a1=rw_a1[n].astype(BF), a2=rw_a2[n].astype(BF),
                         k_k=pairw(rw_k_k[n]), k_a=pairw(rw_k_a[n]), r_k=pairw(rw_r_k[n]), gn_g=pairw(rw_gn_g[n]),
                         gn_b=pairw(rw_gn_b[n]), wout=rw_w_out[n].astype(BF)))
    gdn = []
    for n in range(gdn_w_in.shape[0]):
        w = gdn_w_in[n]
        o2 = GDN_CONV_CH + GDN_VAL_W
        wpad = jnp.pad(w[:, o2:], ((0, 0), (0, LANES - 2 * GDN_HV)))
        lanes = lambda v: jnp.pad(v, (GDN_HV, LANES - 2 * GDN_HV))[None, :]
        gdn.append(dict(win=jnp.concatenate([w[:, :o2], wpad], axis=1).astype(BF), cw=gdn_conv_w[n],
                        alog=lanes(gdn_a_log[n]), dtb=lanes(gdn_dt_bias[n]), norm_g=row(gdn_norm_g[n]),
                        wout=gdn_w_out[n].astype(BF)))
    return dict(lru=lru, rwkv=rwkv, gdn=gdn)


def kernel(x_prompt, x_sample, state_lru_conv, state_lru_h, state_rwkv_shift, state_rwkv_wkv, state_gdn_conv, state_gdn_S, ln_g, ln_b, lru_w_in, lru_conv_w, lru_conv_b, lru_wa, lru_ba, lru_wx, lru_bx, lru_lambda, lru_w_out, rw_mu, rw_w_in, rw_w0, rw_w1, rw_w2, rw_a0, rw_a1, rw_a2, rw_k_k, rw_k_a, rw_r_k, rw_gn_g, rw_gn_b, rw_w_out, gdn_w_in, gdn_conv_w, gdn_a_log, gdn_dt_bias, gdn_norm_g, gdn_w_out):
    params = _prep_params(lru_w_in, lru_conv_w, lru_conv_b, lru_wa, lru_ba, lru_wx, lru_bx, lru_lambda, lru_w_out,
                          rw_mu, rw_w_in, rw_w0, rw_w1, rw_w2, rw_a0, rw_a1, rw_a2, rw_k_k, rw_k_a, rw_r_k, rw_gn_g,
                          rw_gn_b, rw_w_out, gdn_w_in, gdn_conv_w, gdn_a_log, gdn_dt_bias, gdn_norm_g, gdn_w_out)
    bp, tp, _ = x_prompt.shape
    bs, ts, _ = x_sample.shape
    n_a, n_b, n_c = state_lru_conv.shape[0], state_rwkv_shift.shape[0], state_gdn_conv.shape[0]
    zero_state = (jnp.zeros((n_a, bp, CONV_W - 1, LRU_W), F32),
                  jnp.zeros((n_a, bp, LRU_W), F32),
                  jnp.zeros((n_b, bp, D_MODEL), F32),
                  jnp.zeros((n_b, bp, RW_H, RW_N, RW_N), F32),
                  jnp.zeros((n_c, bp, CONV_W - 1, GDN_CONV_CH), F32),
                  jnp.zeros((n_c, bp, GDN_HV, GDN_DK, GDN_DV), F32))
    y_prompt, sp = _trunk(_Group(bp, tp, False), x_prompt, zero_state, params, ln_g, ln_b)
    y_sample, ss = _trunk(_Group(bs, ts, True), x_sample,
                          (state_lru_conv, state_lru_h, state_rwkv_shift, state_rwkv_wkv, state_gdn_conv,
                           state_gdn_S), params, ln_g, ln_b)
    return (y_prompt, y_sample, sp[0], ss[0], sp[1], ss[1], sp[2], ss[2], sp[3], ss[3], sp[4], ss[4], sp[5], ss[5])
```

```python
import functools
import math

import jax
import jax.numpy as jnp
from jax import lax
from jax.experimental import pallas as pl
from jax.experimental.pallas import tpu as pltpu

F32 = jnp.float32
BF = jnp.bfloat16

D_MODEL = 1024
DEPTH = 4
N_MIXERS = 3
DN_ALPHA = (2.0 * DEPTH) ** 0.25
LN_EPS = 1e-5
CONV_W = 4

LRU_W = D_MODEL
LRU_BLOCKS = 16
LRU_BS = LRU_W // LRU_BLOCKS
LRU_C = 8.0

RW_W = D_MODEL
RW_N = 64
RW_H = RW_W // RW_N
RW_GN_EPS = 64e-5
RW_NORM_EPS = 1e-12

GDN_HK = 4
GDN_HV = 8
GDN_DK = 128
GDN_DV = 128
GDN_KEY_W = GDN_HK * GDN_DK
GDN_VAL_W = GDN_HV * GDN_DV
GDN_CONV_CH = 2 * GDN_KEY_W + GDN_VAL_W
GDN_CHUNK = 64
GDN_EPS = 1e-6

LANES = 128
SUBLANES = 8
VMEM_LIMIT = 56 * 1024 * 1024
ROW_BLOCK = 256
RW_CHUNK = 64

_NN = (((1,), (0,)), ((), ()))
_NT = (((1,), (1,)), ((), ()))
_TN = (((0,), (0,)), ((), ()))


def _mm(a, b):
    return jnp.dot(a.astype(BF), b.astype(BF), preferred_element_type=F32)


def _dot(a, b, dims=_NN):
    return lax.dot_general(a.astype(BF), b.astype(BF), dims, preferred_element_type=F32)


def _dot_exact_lhs(a01, b):
    a = a01.astype(BF)
    b1 = b.astype(BF)
    r1 = b - b1.astype(F32)
    b2 = r1.astype(BF)
    b3 = (r1 - b2.astype(F32)).astype(BF)
    d = lambda q: jnp.dot(a, q, preferred_element_type=F32)
    return d(b1) + (d(b2) + d(b3))


def _layer_norm(z, g, b):
    mu = jnp.mean(z, axis=-1, keepdims=True)
    zc = z - mu
    var = jnp.mean(zc * zc, axis=-1, keepdims=True)
    return zc * lax.rsqrt(var + LN_EPS) * g + b


def _silu(x):
    return x * jax.nn.sigmoid(x)


def _cparams(n_axes):
    return pltpu.CompilerParams(dimension_semantics=("arbitrary",) * n_axes, vmem_limit_bytes=VMEM_LIMIT)


def _const_spec(shape):
    nd = len(shape)
    return pl.BlockSpec(shape, lambda *_: (0,) * nd)


def _halo_rows(n_steps, s):
    rows = n_steps * s
    return rows if rows % SUBLANES == 0 else SUBLANES


def _lru_kernel(x_ref, halo_ref, h0_ref, win_ref, cw_ref, cb_ref, wg_ref, bg_ref, lam_ref, wout_ref, lng_ref,
                lnb_ref, out_ref, tail_ref, hT_ref, xscr, ascr, bscr, hscr, *, s, R, H, Hs):
    c = pl.program_id(1)
    C = LRU_W
    tt = R // s

    @pl.when(c == 0)
    def _init():
        xscr[0:H, :] = halo_ref[...]
        hscr[...] = h0_ref[...]
        ascr[0:Hs, :] = jnp.ones((Hs, C), F32)
        bscr[0:Hs, :] = jnp.zeros((Hs, C), F32)

    x = x_ref[...]
    u = _mm(x, win_ref[...])
    xb = u[:, :C]
    gate = u[:, C:]

    xscr[H:H + R, :] = xb
    cw = cw_ref[...]
    xc = xb * cw[3:4, :] + cb_ref[...]
    for d in (1, 2, 3):
        xc = xc + xscr[pl.ds(H - d * s, R), :] * cw[3 - d:4 - d, :]
    tail = xscr[pl.ds(R, H), :]
    xscr[0:H, :] = tail
    tail_ref[...] = tail

    xcb = xc.astype(BF)
    ra, ix = [], []
    for g in range(C // 256):
        gt = jnp.dot(xcb[:, 256 * g:256 * (g + 1)], wg_ref[g], preferred_element_type=F32)
        ra.append(gt[:, :256])
        ix.append(gt[:, 256:])
    bg = bg_ref[...]
    r = jax.nn.sigmoid(jnp.concatenate(ra, axis=1) + bg[0:1, :])
    i = jax.nn.sigmoid(jnp.concatenate(ix, axis=1) + bg[1:2, :])
    log_a = (-LRU_C) * r * jax.nn.softplus(-lam_ref[...])
    a = jnp.exp(log_a)
    th = jnp.tanh(log_a)
    b = jnp.sqrt(-2.0 * th / (1.0 - th)) * i * xc

    rows = lax.broadcasted_iota(jnp.int32, (R, 1), 0)
    hprev = hscr[...]
    if s > 1:
        hprev = jnp.concatenate([hprev] * tt, axis=0)
    b = b + jnp.where(rows < s, a * hprev, 0.0)
    k = 1
    while k < tt:
        ascr[Hs:Hs + R, :] = a
        bscr[Hs:Hs + R, :] = b
        a_s = ascr[pl.ds(Hs - k * s, R), :]
        b_s = bscr[pl.ds(Hs - k * s, R), :]
        b = a * b_s + b
        a = a * a_s
        k *= 2
    h = b
    bscr[Hs:Hs + R, :] = h
    hlast = bscr[pl.ds(Hs + R - s, s), :]
    hscr[...] = hlast
    hT_ref[...] = hlast

    y = _mm(h * _silu(gate), wout_ref[...])
    out_ref[...] = _layer_norm(DN_ALPHA * x + y, lng_ref[...], lnb_ref[...])


def _lru_layer(x3, halo, h0, win, cw, cb, wg, bg, lam, wout, lng, lnb, *, s, R):
    NB, TT, D = x3.shape
    C = LRU_W
    H = halo.shape[1]
    tt = R // s
    Hs = max(SUBLANES, (tt // 2) * s)
    kern = functools.partial(_lru_kernel, s=s, R=R, H=H, Hs=Hs)
    return pl.pallas_call(
        kern,
        grid=(NB, TT // R),
        in_specs=[
            pl.BlockSpec((None, R, D), lambda i, c: (i, c, 0)),
            pl.BlockSpec((None, H, C), lambda i, c: (i, 0, 0)),
            pl.BlockSpec((None, s, C), lambda i, c: (i, 0, 0)),
            _const_spec(win.shape), _const_spec(cw.shape), _const_spec(cb.shape), _const_spec(wg.shape),
            _const_spec(bg.shape), _const_spec(lam.shape), _const_spec(wout.shape), _const_spec(lng.shape),
            _const_spec(lnb.shape),
        ],
        out_specs=[
            pl.BlockSpec((None, R, D), lambda i, c: (i, c, 0)),
            pl.BlockSpec((None, H, C), lambda i, c: (i, 0, 0)),
            pl.BlockSpec((None, s, C), lambda i, c: (i, 0, 0)),
        ],
        out_shape=[
            jax.ShapeDtypeStruct((NB, TT, D), F32),
            jax.ShapeDtypeStruct((NB, H, C), F32),
            jax.ShapeDtypeStruct((NB, s, C), F32),
        ],
        scratch_shapes=[
            pltpu.VMEM((H + R, C), F32),
            pltpu.VMEM((Hs + R, C), F32),
            pltpu.VMEM((Hs + R, C), F32),
            pltpu.VMEM((s, C), F32),
        ],
        compiler_params=_cparams(2),
        name="lru_layer",
    )(x3, halo, h0, win, cw, cb, wg, bg, lam, wout, lng, lnb)


def _lru_gate_weights(wa, wx):
    def bd(w):
        w4 = w.reshape(4, 4, LRU_BS, LRU_BS)
        eye = jnp.eye(4, dtype=w.dtype)
        return jnp.einsum("gaij,ab->gaibj", w4, eye).reshape(4, 256, 256)
    return jnp.concatenate([bd(wa), bd(wx)], axis=2).astype(BF)


def _post_kernel(x_ref, o_ref, g_ref, w_ref, lng_ref, lnb_ref, out_ref):
    nh = o_ref.shape[0]
    o = jnp.concatenate([o_ref[p] for p in range(nh)], axis=1)
    g = jnp.concatenate([g_ref[p] for p in range(nh)], axis=1)
    y = _mm(o * _silu(g), w_ref[...])
    out_ref[...] = _layer_norm(DN_ALPHA * x_ref[...] + y, lng_ref[...], lnb_ref[...])


def _post_layer(x3, o4, g4, w, lng, lnb, *, R):
    NB, TT, D = x3.shape
    nh = o4.shape[1]
    return pl.pallas_call(
        _post_kernel,
        grid=(NB, TT // R),
        in_specs=[
            pl.BlockSpec((None, R, D), lambda i, c: (i, c, 0)),
            pl.BlockSpec((None, nh, R, LANES), lambda i, c: (i, 0, c, 0)),
            pl.BlockSpec((None, nh, R, LANES), lambda i, c: (i, 0, c, 0)),
            _const_spec(w.shape), _const_spec(lng.shape), _const_spec(lnb.shape),
        ],
        out_specs=pl.BlockSpec((None, R, D), lambda i, c: (i, c, 0)),
        out_shape=jax.ShapeDtypeStruct((NB, TT, D), F32),
        compiler_params=_cparams(2),
        name="post_layer",
    )(x3, o4, g4, w, lng, lnb)


def _rwkv_pre_kernel(x_ref, halo_ref, mu_ref, win_ref, w0_ref, w1_ref, w2_ref, a0_ref, a1_ref, a2_ref,
                     r_ref, k_ref, v_ref, g_ref, lw_ref, a_ref, xscr, *, s, R, H):
    c = pl.program_id(1)

    @pl.when(c == 0)
    def _init():
        xscr[0:H, :] = halo_ref[...]

    x = x_ref[...]
    xscr[H:H + R, :] = x
    xprev = xscr[pl.ds(H - s, R), :]
    xscr[0:H, :] = xscr[pl.ds(R, H), :]
    xx = xprev - x
    mu = mu_ref[...]
    xm = lambda n: x + xx * mu[n:n + 1, :]

    def put(ref, val):
        for p in range(RW_W // LANES):
            ref[p] = val[:, LANES * p:LANES * (p + 1)]

    put(r_ref, _mm(xm(0), win_ref[0]))
    put(k_ref, _mm(xm(1), win_ref[1]))
    put(v_ref, _mm(xm(2), win_ref[2]))
    put(g_ref, _mm(xm(3), win_ref[3]))
    w_raw = w0_ref[...] + _mm(jnp.tanh(_mm(xm(4), w1_ref[...])), w2_ref[...])
    put(lw_ref, (-math.exp(-0.5)) * jax.nn.sigmoid(w_raw))
    put(a_ref, jax.nn.sigmoid(a0_ref[...] + _mm(_mm(xm(5), a1_ref[...]), a2_ref[...])))


def _rwkv_pre(x3, halo, mu, win, w0, w1, w2, a0, a1, a2, *, s, R):
    NB, TT, D = x3.shape
    H = halo.shape[1]
    nh = RW_W // LANES
    kern = functools.partial(_rwkv_pre_kernel, s=s, R=R, H=H)
    ospec = pl.BlockSpec((None, nh, R, LANES), lambda i, c: (i, 0, c, 0))
    oshape = jax.ShapeDtypeStruct((NB, nh, TT, LANES), F32)
    return pl.pallas_call(
        kern,
        grid=(NB, TT // R),
        in_specs=[
            pl.BlockSpec((None, R, D), lambda i, c: (i, c, 0)),
            pl.BlockSpec((None, H, D), lambda i, c: (i, 0, 0)),
            _const_spec(mu.shape), _const_spec(win.shape), _const_spec(w0.shape), _const_spec(w1.shape),
            _const_spec(w2.shape), _const_spec(a0.shape), _const_spec(a1.shape), _const_spec(a2.shape),
        ],
        out_specs=[ospec] * 6,
        out_shape=[oshape] * 6,
        scratch_shapes=[pltpu.VMEM((H + R, D), F32)],
        compiler_params=_cparams(2),
        name="rwkv_pre",
    )(x3, halo, mu, win, w0, w1, w2, a0, a1, a2)


def _seg_sum(x):
    lane = lax.broadcasted_iota(jnp.int32, x.shape, 1)
    lo = lane < RW_N
    s0 = jnp.sum(jnp.where(lo, x, 0.0), axis=-1, keepdims=True)
    s1 = jnp.sum(jnp.where(lo, 0.0, x), axis=-1, keepdims=True)
    return jnp.where(lo, s0, s1)


def _stack2(x):
    lane = lax.broadcasted_iota(jnp.int32, x.shape, 1)
    lo = lane < RW_N
    return jnp.concatenate([jnp.where(lo, x, 0.0), jnp.where(lo, 0.0, x)], axis=0)


def _each(f, *lists):
    return [f(*t) for t in zip(*lists)]


def _unit_lower_inverse(Ls, n):
    m = Ls[0].shape[0]
    eye = (lax.broadcasted_iota(jnp.int32, (m, m), 0) == lax.broadcasted_iota(jnp.int32, (m, m), 1)).astype(F32)
    invs = [eye + L for L in Ls]
    Lps = Ls
    span = 2
    while span < n:
        Lps = _each(lambda Lp: _dot(Lp, Lp), Lps)
        invs = _each(lambda inv, Lp: inv + _dot(inv, Lp), invs, Lps)
        span *= 2
    return invs


def _rwkv_rec_kernel(r_ref, k_ref, v_ref, lw_ref, a_ref, s0_ref, kk_ref, ka_ref, rk_ref, gg_ref, gb_ref,
                     o_ref, sT_ref, sscr, *, C, nchunk):
    tb = pl.program_id(1)
    npair = r_ref.shape[0]
    C2 = 2 * C
    lane_sq = lax.broadcasted_iota(jnp.int32, (LANES, LANES), 1)
    row_sq = lax.broadcasted_iota(jnp.int32, (LANES, LANES), 0)
    same_head = (lane_sq < RW_N) == (row_sq < RW_N)

    @pl.when(tb == 0)
    def _init():
        for p in range(npair):
            s2 = s0_ref[p]
            sscr[p] = jnp.where(same_head, jnp.concatenate([s2, s2], axis=1), 0.0)

    ti = lax.broadcasted_iota(jnp.int32, (C2, C2), 0)
    si = lax.broadcasted_iota(jnp.int32, (C2, C2), 1)
    blk = (ti < C) == (si < C)
    strict = blk & (si < ti)
    incl = blk & (si <= ti)
    tri = (lax.broadcasted_iota(jnp.int32, (C, C), 1) <= lax.broadcasted_iota(jnp.int32, (C, C), 0)).astype(F32)

    P = range(npair)

    def chunk(ci, carry):
        rs = pl.ds(pl.multiple_of(ci * C, C), C)
        r = [r_ref[p, rs, :] for p in P]
        k = [k_ref[p, rs, :] for p in P]
        v = [v_ref[p, rs, :] for p in P]
        lw = [lw_ref[p, rs, :] for p in P]
        a = [a_ref[p, rs, :] for p in P]
        S = [sscr[p] for p in P]

        kn = [k[p] * kk_ref[p] for p in P]
        kk = _each(lambda z: z * lax.rsqrt(_seg_sum(z * z) + RW_NORM_EPS), kn)
        kh = [k[p] * (1.0 + (a[p] - 1.0) * ka_ref[p]) for p in P]
        bvec = _each(lambda x, y: x * y, kk, a)

        cum = _each(lambda x: _dot_exact_lhs(tri, x), lw)
        cum_last = [c_[C - 1:C, :] for c_ in cum]
        e_neg = _each(lambda c_: jnp.exp(-c_), cum)
        e_dec = _each(lambda cl, c_: jnp.exp(cl - c_), cum_last, cum)
        At = _each(lambda kk_, c_, lw_: -kk_ * jnp.exp(c_ - lw_), kk, cum, lw)
        Rt = _each(lambda r_, c_: r_ * jnp.exp(c_), r, cum)

        AR2 = _each(lambda x, y: jnp.concatenate([_stack2(x), _stack2(y)], axis=0), At, Rt)
        BK2 = _each(lambda b_, kh_, e: jnp.concatenate([_stack2(b_ * e), _stack2(kh_ * e)], axis=0), bvec, kh, e_neg)
        V2 = _each(_stack2, v)
        sc = _each(lambda x, y: _dot(x, y, _NT), AR2, BK2)
        Lab = [jnp.where(strict, z[:C2, :C2], 0.0) for z in sc]
        Lak = [jnp.where(strict, z[:C2, C2:], 0.0) for z in sc]
        Mrbk = [jnp.concatenate([jnp.where(incl, z[C2:, :C2], 0.0), jnp.where(incl, z[C2:, C2:], 0.0)], axis=1)
                for z in sc]
        UY0 = _each(lambda x, y, s_: _dot(jnp.concatenate([x, y], axis=0), s_, _NT), At, Rt, S)
        LV = _each(_dot, Lak, V2)
        Tinv = _unit_lower_inverse(Lab, C)
        U2 = _each(lambda t, u, lv: _dot(t, _stack2(u[:C, :]) + lv), Tinv, UY0, LV)
        UV2 = _each(lambda x, y: jnp.concatenate([x, y], axis=0), U2, V2)
        Y2 = _each(_dot, Mrbk, UV2)
        y = _each(lambda u, y2: u[C:, :] + y2[:C, :] + y2[C:, :], UY0, Y2)
        dec2 = _each(lambda b_, kh_, e: jnp.concatenate([_stack2(b_ * e), _stack2(kh_ * e)], axis=0), bvec, kh, e_dec)
        Snew = _each(lambda s_, cl, uv, d2: s_ * jnp.exp(cl) + _dot(uv, d2, _TN), S, cum_last, UV2, dec2)
        for p in P:
            sscr[p] = Snew[p]

        for p in P:
            m = _seg_sum(y[p]) * (1.0 / RW_N)
            yc = y[p] - m
            var = _seg_sum(yc * yc) * (1.0 / RW_N)
            yn = yc * lax.rsqrt(var + RW_GN_EPS) * gg_ref[p] + gb_ref[p]
            bonus = _seg_sum(r[p] * kh[p] * rk_ref[p]) * v[p]
            o_ref[p, rs, :] = yn + bonus
        return carry

    lax.fori_loop(0, nchunk, chunk, 0)

    lane_h = lax.broadcasted_iota(jnp.int32, (LANES, RW_N), 0) < RW_N
    for p in range(npair):
        S = sscr[p]
        sT_ref[p] = jnp.where(lane_h, S[:, :RW_N], S[:, RW_N:])


def _rwkv_rec(r4, k4, v4, lw4, a4, s0, k_k, k_a, r_k, gn_g, gn_b, *, C, Rt):
    B, npair, T, _ = r4.shape
    nchunk = Rt // C
    kern = functools.partial(_rwkv_rec_kernel, C=C, nchunk=nchunk)
    tspec = pl.BlockSpec((None, npair, Rt, LANES), lambda b, t: (b, 0, t, 0))
    sspec = pl.BlockSpec((None, npair, LANES, RW_N), lambda b, t: (b, 0, 0, 0))
    wspec = _const_spec((npair, 1, LANES))
    return pl.pallas_call(
        kern,
        grid=(B, T // Rt),
        in_specs=[tspec] * 5 + [sspec] + [wspec] * 5,
        out_specs=[tspec, sspec],
        out_shape=[jax.ShapeDtypeStruct(r4.shape, F32), jax.ShapeDtypeStruct(s0.shape, F32)],
        scratch_shapes=[pltpu.VMEM((npair, LANES, LANES), F32)],
        compiler_params=_cparams(2),
        name="rwkv_rec",
    )(r4, k4, v4, lw4, a4, s0, k_k, k_a, r_k, gn_g, gn_b)


def _gdn_pre_kernel(x_ref, halo_ref, win_ref, cw_ref, alog_ref, dtb_ref,
                    q_ref, k_ref, v_ref, z_ref, gc_ref, beta_ref, tail_ref, xscr, gscr, *, s, R, H, Hg, C):
    c = pl.program_id(1)
    CH = GDN_CONV_CH
    tt = R // s

    @pl.when(c == 0)
    def _init():
        xscr[0:H, :] = halo_ref[...]
        gscr[0:Hg, :] = jnp.zeros((Hg, LANES), F32)

    x = x_ref[...]
    u = _mm(x, win_ref[...])
    xb = u[:, :CH]
    xscr[H:H + R, :] = xb
    cw = cw_ref[...]
    y = xb * cw[3:4, :]
    for d in (1, 2, 3):
        y = y + xscr[pl.ds(H - d * s, R), :] * cw[3 - d:4 - d, :]
    tail = xscr[pl.ds(R, H), :]
    xscr[0:H, :] = tail
    tail_ref[...] = tail
    qkv = _silu(y)

    def l2n(z, scale):
        return z * (lax.rsqrt(jnp.sum(z * z, axis=-1, keepdims=True) + GDN_EPS) * scale)

    for h in range(GDN_HK):
        q_ref[h] = l2n(qkv[:, LANES * h:LANES * (h + 1)], GDN_DK ** -0.5)
        k_ref[h] = l2n(qkv[:, GDN_KEY_W + LANES * h:GDN_KEY_W + LANES * (h + 1)], 1.0)
    for h in range(GDN_HV):
        v_ref[h] = qkv[:, 2 * GDN_KEY_W + LANES * h:2 * GDN_KEY_W + LANES * (h + 1)]
        z_ref[h] = u[:, CH + LANES * h:CH + LANES * (h + 1)]

    bg = u[:, CH + GDN_VAL_W:CH + GDN_VAL_W + LANES]
    beta = jax.nn.sigmoid(bg)
    g = -jnp.exp(alog_ref[...]) * jax.nn.softplus(bg + dtb_ref[...])
    t_in = (lax.broadcasted_iota(jnp.int32, (R, 1), 0) // s) % C
    k = 1
    while k < min(C, tt):
        gscr[Hg:Hg + R, :] = g
        g = g + jnp.where(t_in >= k, gscr[pl.ds(Hg - k * s, R), :], 0.0)
        k *= 2
    for h in range(GDN_HV):
        beta_ref[h] = jnp.broadcast_to(beta[:, h:h + 1], (R, LANES))
        gc_ref[h] = jnp.broadcast_to(g[:, GDN_HV + h:GDN_HV + h + 1], (R, LANES))


def _gdn_pre(x3, halo, win, cw, alog, dtb, *, s, R, C):
    NB, TT, D = x3.shape
    H = halo.shape[1]
    tt = R // s
    Hg = max(SUBLANES, (min(C, tt) // 2) * s)
    kern = functools.partial(_gdn_pre_kernel, s=s, R=R, H=H, Hg=Hg, C=C)

    def ospec(nh):
        return pl.BlockSpec((None, nh, R, LANES), lambda i, c: (i, 0, c, 0))

    def oshape(nh):
        return jax.ShapeDtypeStruct((NB, nh, TT, LANES), F32)

    return pl.pallas_call(
        kern,
        grid=(NB, TT // R),
        in_specs=[
            pl.BlockSpec((None, R, D), lambda i, c: (i, c, 0)),
            pl.BlockSpec((None, H, GDN_CONV_CH), lambda i, c: (i, 0, 0)),
            _const_spec(win.shape), _const_spec(cw.shape), _const_spec(alog.shape), _const_spec(dtb.shape),
        ],
        out_specs=[ospec(GDN_HK), ospec(GDN_HK), ospec(GDN_HV), ospec(GDN_HV), ospec(GDN_HV), ospec(GDN_HV),
                   pl.BlockSpec((None, H, GDN_CONV_CH), lambda i, c: (i, 0, 0))],
        out_shape=[oshape(GDN_HK), oshape(GDN_HK), oshape(GDN_HV), oshape(GDN_HV), oshape(GDN_HV), oshape(GDN_HV),
                   jax.ShapeDtypeStruct((NB, H, GDN_CONV_CH), F32)],
        scratch_shapes=[pltpu.VMEM((H + R, GDN_CONV_CH), F32), pltpu.VMEM((Hg + R, LANES), F32)],
        compiler_params=_cparams(2),
        name="gdn_pre",
    )(x3, halo, win, cw, alog, dtb)


def _gdn_rec_kernel(q_ref, k_ref, v_ref, gc_ref, beta_ref, s0_ref, ng_ref, o_ref, sT_ref, sscr, *, C, nchunk):
    tb = pl.program_id(1)
    rep = GDN_HV // GDN_HK

    @pl.when(tb == 0)
    def _init():
        sscr[...] = s0_ref[...]

    ti = lax.broadcasted_iota(jnp.int32, (C, C), 0)
    si = lax.broadcasted_iota(jnp.int32, (C, C), 1)
    strict = si < ti
    incl = si <= ti

    HV = range(GDN_HV)

    def chunk(ci, carry):
        rs = pl.ds(pl.multiple_of(ci * C, C), C)
        q = [q_ref[h // rep, rs, :] for h in HV]
        k = [k_ref[h // rep, rs, :] for h in HV]
        v = [v_ref[h, rs, :] for h in HV]
        gc = [gc_ref[h, rs, :] for h in HV]
        beta = [beta_ref[h, rs, :] for h in HV]
        S = [sscr[h] for h in HV]

        diff = _each(lambda g_: g_[:, :C] - g_.T[:C, :], gc)
        kb = _each(lambda x, y: x * y, k, beta)
        sc = _each(lambda kb_, q_, k_: _dot(jnp.concatenate([kb_, q_], axis=0), k_, _NT), kb, q, k)
        L = _each(lambda z, d: jnp.where(strict, z[:C, :] * jnp.exp(jnp.where(strict, d, 0.0)), 0.0), sc, diff)
        A = _each(lambda z, d: jnp.where(incl, z[C:, :] * jnp.exp(jnp.where(incl, d, 0.0)), 0.0), sc, diff)
        Tinv = _unit_lower_inverse([-l_ for l_ in L], C)
        eg = _each(jnp.exp, gc)
        UW = _each(lambda t, v_, b_, kb_, e: _dot(t, jnp.concatenate([v_ * b_, kb_ * e], axis=1)),
                   Tinv, v, beta, kb, eg)
        WQ = _each(lambda uw, q_, e, s_: _dot(jnp.concatenate([uw[:, GDN_DV:], q_ * e], axis=0), s_),
                   UW, q, eg, S)
        v_new = _each(lambda uw, wq: uw[:, :GDN_DV] - wq[:C, :], UW, WQ)
        o = _each(lambda wq, a_, vn: wq[C:, :] + _dot(a_, vn), WQ, A, v_new)
        g_last = [g_[C - 1:C, :] for g_ in gc]
        Snew = _each(lambda s_, gl, k_, g_, vn: s_ * jnp.exp(gl) + _dot(k_ * jnp.exp(gl - g_), vn, _TN),
                     S, g_last, k, gc, v_new)
        for h in HV:
            sscr[h] = Snew[h]
            oh = o[h]
            o_ref[h, rs, :] = oh * lax.rsqrt(jnp.mean(oh * oh, axis=-1, keepdims=True) + GDN_EPS) * ng_ref[...]
        return carry

    lax.fori_loop(0, nchunk, chunk, 0)
    sT_ref[...] = sscr[...]


def _gdn_rec(q4, k4, v4, gc4, beta4, s0, norm_g, *, C, Rt):
    B, _, T, _ = v4.shape
    nchunk = Rt // C
    kern = functools.partial(_gdn_rec_kernel, C=C, nchunk=nchunk)

    def tspec(nh):
        return pl.BlockSpec((None, nh, Rt, LANES), lambda b, t: (b, 0, t, 0))

    sspec = pl.BlockSpec((None, GDN_HV, GDN_DK, GDN_DV), lambda b, t: (b, 0, 0, 0))
    return pl.pallas_call(
        kern,
        grid=(B, T // Rt),
        in_specs=[tspec(GDN_HK), tspec(GDN_HK), tspec(GDN_HV), tspec(GDN_HV), tspec(GDN_HV), sspec,
                  _const_spec(norm_g.shape)],
        out_specs=[tspec(GDN_HV), sspec],
        out_shape=[jax.ShapeDtypeStruct(v4.shape, F32), jax.ShapeDtypeStruct(s0.shape, F32)],
        scratch_shapes=[pltpu.VMEM((GDN_HV, GDN_DK, GDN_DV), F32)],
        compiler_params=_cparams(2),
        name="gdn_rec",
    )(q4, k4, v4, gc4, beta4, s0, norm_g)


class _Group:
    def __init__(self, B, T, time_major):
        self.B, self.T, self.time_major = B, T, time_major
        if time_major:
            self.s, self.NB, self.TT, self.R = B, 1, T * B, T * B
        else:
            self.s, self.NB, self.TT, self.R = 1, B, T, min(ROW_BLOCK, T)

    def to_rows(self, x):
        if self.time_major:
            return jnp.swapaxes(x, 0, 1).reshape(1, self.TT, x.shape[-1])
        return x

    def from_rows(self, x3):
        if self.time_major:
            return jnp.swapaxes(x3.reshape(self.T, self.B, x3.shape[-1]), 0, 1)
        return x3

    def halo(self, st, n_steps):
        if self.time_major:
            return jnp.swapaxes(st, 0, 1).reshape(1, n_steps * self.B, st.shape[-1])
        H = _halo_rows(n_steps, 1)
        return jnp.pad(st, ((0, 0), (H - n_steps, 0), (0, 0)))

    def unhalo(self, tail, n_steps):
        if self.time_major:
            return jnp.swapaxes(tail.reshape(n_steps, self.B, tail.shape[-1]), 0, 1)
        return tail[:, tail.shape[1] - n_steps:]

    def vec(self, st):
        return st[None] if self.time_major else st[:, None, :]

    def unvec(self, v):
        return v[0] if self.time_major else v[:, 0, :]

    def heads_to_batch(self, a4, Tpad):
        if not self.time_major:
            return a4
        nh = a4.shape[1]
        a = a4.reshape(nh, self.T, self.B, LANES).transpose(2, 0, 1, 3)
        return jnp.pad(a, ((0, 0), (0, 0), (0, Tpad - self.T), (0, 0)))

    def heads_from_batch(self, a4):
        if not self.time_major:
            return a4
        nh = a4.shape[1]
        return a4[:, :, :self.T].transpose(1, 2, 0, 3).reshape(1, nh, self.TT, LANES)


def _lru_apply(g, x3, conv_st, h_st, p, lng, lnb):
    out, tail, hT = _lru_layer(x3, g.halo(conv_st, CONV_W - 1), g.vec(h_st), p["win"], p["cw"], p["cb"], p["wg"],
                               p["bg"], p["lam"], p["wout"], lng, lnb, s=g.s, R=g.R)
    return out, g.unhalo(tail, CONV_W - 1), g.unvec(hT)


def _rwkv_apply(g, x3, shift_st, wkv_st, p, lng, lnb):
    B, T = g.B, g.T
    r4, k4, v4, g4, lw4, a4 = _rwkv_pre(x3, g.halo(shift_st[:, None, :], 1), p["mu"], p["win"], p["w0"], p["w1"],
                                        p["w2"], p["a0"], p["a1"], p["a2"], s=g.s, R=g.R)
    if g.time_major:
        C = Tp = -(-T // SUBLANES) * SUBLANES
        Rt = Tp
    else:
        C = min(RW_CHUNK, T)
        Tp = T
        Rt = min(ROW_BLOCK, T)
    tb = lambda a: g.heads_to_batch(a, Tp)
    s0 = wkv_st.reshape(B, RW_H // 2, 2 * RW_N, RW_N)
    o4, sT = _rwkv_rec(tb(r4), tb(k4), tb(v4), tb(lw4), tb(a4), s0, p["k_k"], p["k_a"], p["r_k"], p["gn_g"],
                       p["gn_b"], C=C, Rt=Rt)
    out = _post_layer(x3, g.heads_from_batch(o4), g4, p["wout"], lng, lnb, R=g.R)
    new_shift = g.from_rows(x3)[:, -1]
    return out, new_shift, sT.reshape(B, RW_H, RW_N, RW_N)


def _gdn_apply(g, x3, conv_st, S_st, p, lng, lnb):
    B, T = g.B, g.T
    if g.time_major:
        C = T
        Cp = Tp = -(-T // SUBLANES) * SUBLANES
        Rt = Tp
    else:
        C = Cp = min(GDN_CHUNK, T)
        Tp = T
        Rt = min(ROW_BLOCK, T)
    q4, k4, v4, z4, gc4, beta4, tail = _gdn_pre(x3, g.halo(conv_st, CONV_W - 1), p["win"], p["cw"], p["alog"],
                                                p["dtb"], s=g.s, R=g.R, C=C)
    gcb = g.heads_to_batch(gc4, Tp)
    if Tp != T:
        gcb = gcb.at[:, :, T:, :].set(jnp.broadcast_to(gcb[:, :, T - 1:T, :], gcb[:, :, T:, :].shape))
    tb = lambda a: g.heads_to_batch(a, Tp)
    o4, sT = _gdn_rec(tb(q4), tb(k4), tb(v4), gcb, tb(beta4), S_st, p["norm_g"], C=Cp, Rt=Rt)
    out = _post_layer(x3, g.heads_from_batch(o4), z4, p["wout"], lng, lnb, R=g.R)
    return out, g.unhalo(tail, CONV_W - 1), sT


def _trunk(g, x, st, params, ln_g, ln_b):
    lru_conv, lru_h, rw_shift, rw_S, gdn_conv, gdn_S = st
    new = ([], [], [], [], [], [])
    x3 = g.to_rows(x)
    ia = ib = ic = 0
    for layer in range(DEPTH):
        kind = layer % N_MIXERS
        lng, lnb = ln_g[layer][None, :], ln_b[layer][None, :]
        if kind == 0:
            x3, c, h = _lru_apply(g, x3, lru_conv[ia], lru_h[ia], params["lru"][ia], lng, lnb)
            new[0].append(c)
            new[1].append(h)
            ia += 1
        elif kind == 1:
            x3, sh, S = _rwkv_apply(g, x3, rw_shift[ib], rw_S[ib], params["rwkv"][ib], lng, lnb)
            new[2].append(sh)
            new[3].append(S)
            ib += 1
        else:
            x3, c, S = _gdn_apply(g, x3, gdn_conv[ic], gdn_S[ic], params["gdn"][ic], lng, lnb)
            new[4].append(c)
            new[5].append(S)
            ic += 1
    return g.from_rows(x3), tuple(jnp.stack(s) for s in new)


def _prep_params(lru_w_in, lru_conv_w, lru_conv_b, lru_wa, lru_ba, lru_wx, lru_bx, lru_lambda, lru_w_out, rw_mu,
                 rw_w_in, rw_w0, rw_w1, rw_w2, rw_a0, rw_a1, rw_a2, rw_k_k, rw_k_a, rw_r_k, rw_gn_g, rw_gn_b,
                 rw_w_out, gdn_w_in, gdn_conv_w, gdn_a_log, gdn_dt_bias, gdn_norm_g, gdn_w_out):
    row = lambda v: v[None, :]
    lru = []
    for n in range(lru_w_in.shape[0]):
        lru.append(dict(win=lru_w_in[n].astype(BF), cw=lru_conv_w[n], cb=row(lru_conv_b[n]),
                        wg=_lru_gate_weights(lru_wa[n], lru_wx[n]), bg=jnp.stack([lru_ba[n], lru_bx[n]]),
                        lam=row(lru_lambda[n]), wout=lru_w_out[n].astype(BF)))
    pairw = lambda v: v.reshape(RW_W // LANES, 1, LANES)
    rwkv = []
    for n in range(rw_w_in.shape[0]):
        rwkv.append(dict(mu=rw_mu[n], win=rw_w_in[n].astype(BF), w0=row(rw_w0[n]), w1=rw_w1[n].astype(BF),
                         w2=rw_w2[n].astype(BF), a0=row(rw_a0[n]), a1=rw_a1[n].astype(BF), a2=rw_a2[n].astype(BF),
                         k_k=pairw(rw_k_k[n]), k_a=pairw(rw_k_a[n]), r_k=pairw(rw_r_k[n]), gn_g=pairw(rw_gn_g[n]),
                         gn_b=pairw(rw_gn_b[n]), wout=rw_w_out[n].astype(BF)))
    gdn = []
    for n in range(gdn_w_in.shape[0]):
        w = gdn_w_in[n]
        o2 = GDN_CONV_CH + GDN_VAL_W
        wpad = jnp.pad(w[:, o2:], ((0, 0), (0, LANES - 2 * GDN_HV)))
        lanes = lambda v: jnp.pad(v, (GDN_HV, LANES - 2 * GDN_HV))[None, :]
        gdn.append(dict(win=jnp.concatenate([w[:, :o2], wpad], axis=1).astype(BF), cw=gdn_conv_w[n],
                        alog=lanes(gdn_a_log[n]), dtb=lanes(gdn_dt_bias[n]), norm_g=row(gdn_norm_g[n]),
                        wout=gdn_w_out[n].astype(BF)))
    return dict(lru=lru, rwkv=rwkv, gdn=gdn)


def kernel(x_prompt, x_sample, state_lru_conv, state_lru_h, state_rwkv_shift, state_rwkv_wkv, state_gdn_conv, state_gdn_S, ln_g, ln_b, lru_w_in, lru_conv_w, lru_conv_b, lru_wa, lru_ba, lru_wx, lru_bx, lru_lambda, lru_w_out, rw_mu, rw_w_in, rw_w0, rw_w1, rw_w2, rw_a0, rw_a1, rw_a2, rw_k_k, rw_k_a, rw_r_k, rw_gn_g, rw_gn_b, rw_w_out, gdn_w_in, gdn_conv_w, gdn_a_log, gdn_dt_bias, gdn_norm_g, gdn_w_out):
    params = _prep_params(lru_w_in, lru_conv_w, lru_conv_b, lru_wa, lru_ba, lru_wx, lru_bx, lru_lambda, lru_w_out,
                          rw_mu, rw_w_in, rw_w0, rw_w1, rw_w2, rw_a0, rw_a1, rw_a2, rw_k_k, rw_k_a, rw_r_k, rw_gn_g,
                          rw_gn_b, rw_w_out, gdn_w_in, gdn_conv_w, gdn_a_log, gdn_dt_bias, gdn_norm_g, gdn_w_out)
    bp, tp, _ = x_prompt.shape
    bs, ts, _ = x_sample.shape
    n_a, n_b, n_c = state_lru_conv.shape[0], state_rwkv_shift.shape[0], state_gdn_conv.shape[0]
    zero_state = (jnp.zeros((n_a, bp, CONV_W - 1, LRU_W), F32),
                  jnp.zeros((n_a, bp, LRU_W), F32),
                  jnp.zeros((n_b, bp, D_MODEL), F32),
                  jnp.zeros((n_b, bp, RW_H, RW_N, RW_N), F32),
                  jnp.zeros((n_c, bp, CONV_W - 1, GDN_CONV_CH), F32),
                  jnp.zeros((n_c, bp, GDN_HV, GDN_DK, GDN_DV), F32))
    y_prompt, sp = _trunk(_Group(bp, tp, False), x_prompt, zero_state, params, ln_g, ln_b)
    y_sample, ss = _trunk(_Group(bs, ts, True), x_sample,
                          (state_lru_conv, state_lru_h, state_rwkv_shift, state_rwkv_wkv, state_gdn_conv,
                           state_gdn_S), params, ln_g, ln_b)
    return (y_prompt, y_sample, sp[0], ss[0], sp[1], ss[1], sp[2], ss[2], sp[3], ss[3], sp[4], ss[4], sp[5], ss[5])
```

```python
import functools
import math

import jax
import jax.numpy as jnp
from jax import lax
from jax.experimental import pallas as pl
from jax.experimental.pallas import tpu as pltpu

F32 = jnp.float32
BF = jnp.bfloat16

D_MODEL = 1024
DEPTH = 4
N_MIXERS = 3
DN_ALPHA = (2.0 * DEPTH) ** 0.25
LN_EPS = 1e-5
CONV_W = 4

LRU_W = D_MODEL
LRU_BLOCKS = 16
LRU_BS = LRU_W // LRU_BLOCKS
LRU_C = 8.0

RW_W = D_MODEL
RW_N = 64
RW_H = RW_W // RW_N
RW_GN_EPS = 64e-5
RW_NORM_EPS = 1e-12

GDN_HK = 4
GDN_HV = 8
GDN_DK = 128
GDN_DV = 128
GDN_KEY_W = GDN_HK * GDN_DK
GDN_VAL_W = GDN_HV * GDN_DV
GDN_CONV_CH = 2 * GDN_KEY_W + GDN_VAL_W
GDN_CHUNK = 64
GDN_EPS = 1e-6

LANES = 128
SUBLANES = 8
VMEM_LIMIT = 56 * 1024 * 1024
ROW_BLOCK = 256
SAMPLE_SEQ_BLOCK = 8
SAMPLE_SEQ_GROUP = 4
RW_CHUNK = 64

_NN = (((1,), (0,)), ((), ()))
_NT = (((1,), (1,)), ((), ()))
_TN = (((0,), (0,)), ((), ()))


def _mm(a, b):
    return jnp.dot(a.astype(BF), b.astype(BF), preferred_element_type=F32)


def _dot(a, b, dims=_NN):
    return lax.dot_general(a.astype(BF), b.astype(BF), dims, preferred_element_type=F32)


def _dot_exact_lhs(a01, b):
    a = a01.astype(BF)
    b1 = b.astype(BF)
    r1 = b - b1.astype(F32)
    b2 = r1.astype(BF)
    b3 = (r1 - b2.astype(F32)).astype(BF)
    d = lambda q: jnp.dot(a, q, preferred_element_type=F32)
    return d(b1) + (d(b2) + d(b3))


def _layer_norm(z, g, b):
    mu = jnp.mean(z, axis=-1, keepdims=True)
    zc = z - mu
    var = jnp.mean(zc * zc, axis=-1, keepdims=True)
    return zc * lax.rsqrt(var + LN_EPS) * g + b


def _silu(x):
    return x * jax.nn.sigmoid(x)


def _cparams(n_axes):
    return pltpu.CompilerParams(dimension_semantics=("arbitrary",) * n_axes, vmem_limit_bytes=VMEM_LIMIT)


def _const_spec(shape):
    nd = len(shape)
    return pl.BlockSpec(shape, lambda *_: (0,) * nd)


def _halo_rows(n_steps, s):
    rows = n_steps * s
    return rows if rows % SUBLANES == 0 else SUBLANES


def _lru_kernel(x_ref, halo_ref, h0_ref, win_ref, cw_ref, cb_ref, wg_ref, bg_ref, lam_ref, wout_ref, lng_ref,
                lnb_ref, out_ref, tail_ref, hT_ref, xscr, ascr, bscr, hscr, *, s, R, H, Hs):
    c = pl.program_id(1)
    C = LRU_W
    tt = R // s

    @pl.when(c == 0)
    def _init():
        xscr[0:H, :] = halo_ref[...]
        hscr[...] = h0_ref[...]
        ascr[0:Hs, :] = jnp.ones((Hs, C), F32)
        bscr[0:Hs, :] = jnp.zeros((Hs, C), F32)

    x = x_ref[...]
    u = _mm(x, win_ref[...])
    xb = u[:, :C]
    gate = u[:, C:]

    xscr[H:H + R, :] = xb
    cw = cw_ref[...]
    xc = xb * cw[3:4, :] + cb_ref[...]
    for d in (1, 2, 3):
        xc = xc + xscr[pl.ds(H - d * s, R), :] * cw[3 - d:4 - d, :]
    tail = xscr[pl.ds(R, H), :]
    xscr[0:H, :] = tail
    tail_ref[...] = tail

    xcb = xc.astype(BF)
    ra, ix = [], []
    for g in range(C // 256):
        gt = jnp.dot(xcb[:, 256 * g:256 * (g + 1)], wg_ref[g], preferred_element_type=F32)
        ra.append(gt[:, :256])
        ix.append(gt[:, 256:])
    bg = bg_ref[...]
    r = jax.nn.sigmoid(jnp.concatenate(ra, axis=1) + bg[0:1, :])
    i = jax.nn.sigmoid(jnp.concatenate(ix, axis=1) + bg[1:2, :])
    log_a = (-LRU_C) * r * jax.nn.softplus(-lam_ref[...])
    a = jnp.exp(log_a)
    th = jnp.tanh(log_a)
    b = jnp.sqrt(-2.0 * th / (1.0 - th)) * i * xc

    rows = lax.broadcasted_iota(jnp.int32, (R, 1), 0)
    hprev = hscr[...]
    if s > 1:
        hprev = jnp.concatenate([hprev] * tt, axis=0)
    b = b + jnp.where(rows < s, a * hprev, 0.0)
    k = 1
    while k < tt:
        ascr[Hs:Hs + R, :] = a
        bscr[Hs:Hs + R, :] = b
        a_s = ascr[pl.ds(Hs - k * s, R), :]
        b_s = bscr[pl.ds(Hs - k * s, R), :]
        b = a * b_s + b
        a = a * a_s
        k *= 2
    h = b
    bscr[Hs:Hs + R, :] = h
    hlast = bscr[pl.ds(Hs + R - s, s), :]
    hscr[...] = hlast
    hT_ref[...] = hlast

    y = _mm(h * _silu(gate), wout_ref[...])
    out_ref[...] = _layer_norm(DN_ALPHA * x + y, lng_ref[...], lnb_ref[...])


def _lru_layer(x3, halo, h0, win, cw, cb, wg, bg, lam, wout, lng, lnb, *, s, R):
    NB, TT, D = x3.shape
    C = LRU_W
    H = halo.shape[1]
    tt = R // s
    Hs = max(SUBLANES, (tt // 2) * s)
    kern = functools.partial(_lru_kernel, s=s, R=R, H=H, Hs=Hs)
    return pl.pallas_call(
        kern,
        grid=(NB, TT // R),
        in_specs=[
            pl.BlockSpec((None, R, D), lambda i, c: (i, c, 0)),
            pl.BlockSpec((None, H, C), lambda i, c: (i, 0, 0)),
            pl.BlockSpec((None, s, C), lambda i, c: (i, 0, 0)),
            _const_spec(win.shape), _const_spec(cw.shape), _const_spec(cb.shape), _const_spec(wg.shape),
            _const_spec(bg.shape), _const_spec(lam.shape), _const_spec(wout.shape), _const_spec(lng.shape),
            _const_spec(lnb.shape),
        ],
        out_specs=[
            pl.BlockSpec((None, R, D), lambda i, c: (i, c, 0)),
            pl.BlockSpec((None, H, C), lambda i, c: (i, 0, 0)),
            pl.BlockSpec((None, s, C), lambda i, c: (i, 0, 0)),
        ],
        out_shape=[
            jax.ShapeDtypeStruct((NB, TT, D), F32),
            jax.ShapeDtypeStruct((NB, H, C), F32),
            jax.ShapeDtypeStruct((NB, s, C), F32),
        ],
        scratch_shapes=[
            pltpu.VMEM((H + R, C), F32),
            pltpu.VMEM((Hs + R, C), F32),
            pltpu.VMEM((Hs + R, C), F32),
            pltpu.VMEM((s, C), F32),
        ],
        compiler_params=_cparams(2),
        name="lru_layer",
    )(x3, halo, h0, win, cw, cb, wg, bg, lam, wout, lng, lnb)


def _lru_gate_weights(wa, wx):
    def bd(w):
        w4 = w.reshape(4, 4, LRU_BS, LRU_BS)
        eye = jnp.eye(4, dtype=w.dtype)
        return jnp.einsum("gaij,ab->gaibj", w4, eye).reshape(4, 256, 256)
    return jnp.concatenate([bd(wa), bd(wx)], axis=2).astype(BF)


def _gated_out_ln(oscr, g_ref, x_ref, w_ref, lng_ref, lnb_ref, out_ref):
    bb, nh, Rt, _ = g_ref.shape
    rows = bb * Rt
    g = jnp.concatenate([g_ref[:, p, :, :].reshape(rows, LANES) for p in range(nh)], axis=1)
    y = _mm(oscr[...] * _silu(g), w_ref[...])
    x = x_ref[...].reshape(rows, D_MODEL)
    out_ref[...] = _layer_norm(DN_ALPHA * x + y, lng_ref[...], lnb_ref[...]).reshape(bb, Rt, D_MODEL)


def _rwkv_pre_kernel(x_ref, halo_ref, mu_ref, win_ref, w0_ref, w1_ref, w2_ref, a0_ref, a1_ref, a2_ref,
                     r_ref, k_ref, v_ref, g_ref, lw_ref, a_ref, xscr, *, s, R, H):
    c = pl.program_id(1)

    @pl.when(c == 0)
    def _init():
        xscr[0:H, :] = halo_ref[...]

    x = x_ref[...]
    xscr[H:H + R, :] = x
    xprev = xscr[pl.ds(H - s, R), :]
    xscr[0:H, :] = xscr[pl.ds(R, H), :]
    xx = xprev - x
    mu = mu_ref[...]
    xm = lambda n: x + xx * mu[n:n + 1, :]

    def put(ref, val):
        for p in range(RW_W // LANES):
            ref[p] = val[:, LANES * p:LANES * (p + 1)]

    put(r_ref, _mm(xm(0), win_ref[0]))
    put(k_ref, _mm(xm(1), win_ref[1]))
    put(v_ref, _mm(xm(2), win_ref[2]))
    put(g_ref, _mm(xm(3), win_ref[3]))
    w_raw = w0_ref[...] + _mm(jnp.tanh(_mm(xm(4), w1_ref[...])), w2_ref[...])
    put(lw_ref, (-math.exp(-0.5)) * jax.nn.sigmoid(w_raw))
    put(a_ref, jax.nn.sigmoid(a0_ref[...] + _mm(_mm(xm(5), a1_ref[...]), a2_ref[...])))


def _rwkv_pre(x3, halo, mu, win, w0, w1, w2, a0, a1, a2, *, s, R):
    NB, TT, D = x3.shape
    H = halo.shape[1]
    nh = RW_W // LANES
    kern = functools.partial(_rwkv_pre_kernel, s=s, R=R, H=H)
    ospec = pl.BlockSpec((None, nh, R, LANES), lambda i, c: (i, 0, c, 0))
    oshape = jax.ShapeDtypeStruct((NB, nh, TT, LANES), F32)
    return pl.pallas_call(
        kern,
        grid=(NB, TT // R),
        in_specs=[
            pl.BlockSpec((None, R, D), lambda i, c: (i, c, 0)),
            pl.BlockSpec((None, H, D), lambda i, c: (i, 0, 0)),
            _const_spec(mu.shape), _const_spec(win.shape), _const_spec(w0.shape), _const_spec(w1.shape),
            _const_spec(w2.shape), _const_spec(a0.shape), _const_spec(a1.shape), _const_spec(a2.shape),
        ],
        out_specs=[ospec] * 6,
        out_shape=[oshape] * 6,
        scratch_shapes=[pltpu.VMEM((H + R, D), F32)],
        compiler_params=_cparams(2),
        name="rwkv_pre",
    )(x3, halo, mu, win, w0, w1, w2, a0, a1, a2)


def _seg_sum(x):
    lane = lax.broadcasted_iota(jnp.int32, x.shape, 1)
    lo = lane < RW_N
    s0 = jnp.sum(jnp.where(lo, x, 0.0), axis=-1, keepdims=True)
    s1 = jnp.sum(jnp.where(lo, 0.0, x), axis=-1, keepdims=True)
    return jnp.where(lo, s0, s1)


def _stack2(x):
    lane = lax.broadcasted_iota(jnp.int32, x.shape, 1)
    lo = lane < RW_N
    return jnp.concatenate([jnp.where(lo, x, 0.0), jnp.where(lo, 0.0, x)], axis=0)


def _each(f, *lists):
    return [f(*t) for t in zip(*lists)]


def _unit_lower_inverse(Ls, n):
    m = Ls[0].shape[0]
    eye = (lax.broadcasted_iota(jnp.int32, (m, m), 0) == lax.broadcasted_iota(jnp.int32, (m, m), 1)).astype(F32)
    invs = [eye + L for L in Ls]
    Lps = Ls
    span = 2
    while span < n:
        Lps = _each(lambda Lp: _dot(Lp, Lp), Lps)
        invs = _each(lambda inv, Lp: inv + _dot(inv, Lp), invs, Lps)
        span *= 2
    return invs


def _rwkv_rec_kernel(r_ref, k_ref, v_ref, lw_ref, a_ref, g_ref, x_ref, s0_ref, kk_ref, ka_ref, rk_ref, gg_ref,
                     gb_ref, wout_ref, lng_ref, lnb_ref, out_ref, sT_ref, sscr, oscr, *, C, gb):
    tb = pl.program_id(1)
    bb, npair, Rt, _ = r_ref.shape
    nchunk = Rt // C
    C2 = 2 * C
    lane_sq = lax.broadcasted_iota(jnp.int32, (LANES, LANES), 1)
    row_sq = lax.broadcasted_iota(jnp.int32, (LANES, LANES), 0)
    same_head = (lane_sq < RW_N) == (row_sq < RW_N)

    @pl.when(tb == 0)
    def _init():
        def init_b(b, carry):
            for p in range(npair):
                s2 = s0_ref[b, p]
                sscr[b * npair + p] = jnp.where(same_head, jnp.concatenate([s2, s2], axis=1), 0.0)
            return carry
        lax.fori_loop(0, bb, init_b, 0)

    ti = lax.broadcasted_iota(jnp.int32, (C2, C2), 0)
    si = lax.broadcasted_iota(jnp.int32, (C2, C2), 1)
    blk = (ti < C) == (si < C)
    strict = blk & (si < ti)
    incl = blk & (si <= ti)
    tri = (lax.broadcasted_iota(jnp.int32, (C, C), 1) <= lax.broadcasted_iota(jnp.int32, (C, C), 0)).astype(F32)

    CH = [(j, p) for j in range(gb) for p in range(npair)]

    def chunk(idx, carry):
        b0 = (idx // nchunk) * gb
        ci = idx % nchunk
        rs = pl.ds(pl.multiple_of(ci * C, C), C)
        r = [r_ref[b0 + j, p, rs, :] for j, p in CH]
        k = [k_ref[b0 + j, p, rs, :] for j, p in CH]
        v = [v_ref[b0 + j, p, rs, :] for j, p in CH]
        lw = [lw_ref[b0 + j, p, rs, :] for j, p in CH]
        a = [a_ref[b0 + j, p, rs, :] for j, p in CH]
        S = [sscr[(b0 + j) * npair + p] for j, p in CH]

        kn = [k_ * kk_ref[p] for k_, (j, p) in zip(k, CH)]
        kk = _each(lambda z: z * lax.rsqrt(_seg_sum(z * z) + RW_NORM_EPS), kn)
        kh = [k_ * (1.0 + (a_ - 1.0) * ka_ref[p]) for k_, a_, (j, p) in zip(k, a, CH)]
        bvec = _each(lambda x, y: x * y, kk, a)

        cum = _each(lambda x: _dot_exact_lhs(tri, x), lw)
        cum_last = [c_[C - 1:C, :] for c_ in cum]
        e_neg = _each(lambda c_: jnp.exp(-c_), cum)
        e_dec = _each(lambda cl, c_: jnp.exp(cl - c_), cum_last, cum)
        At = _each(lambda kk_, c_, lw_: -kk_ * jnp.exp(c_ - lw_), kk, cum, lw)
        Rt_ = _each(lambda r_, c_: r_ * jnp.exp(c_), r, cum)

        AR2 = _each(lambda x, y: jnp.concatenate([_stack2(x), _stack2(y)], axis=0), At, Rt_)
        BK2 = _each(lambda b_, kh_, e: jnp.concatenate([_stack2(b_ * e), _stack2(kh_ * e)], axis=0), bvec, kh, e_neg)
        V2 = _each(_stack2, v)
        sc = _each(lambda x, y: _dot(x, y, _NT), AR2, BK2)
        Lab = [jnp.where(strict, z[:C2, :C2], 0.0) for z in sc]
        Lak = [jnp.where(strict, z[:C2, C2:], 0.0) for z in sc]
        Mrbk = [jnp.concatenate([jnp.where(incl, z[C2:, :C2], 0.0), jnp.where(incl, z[C2:, C2:], 0.0)], axis=1)
                for z in sc]
        UY0 = _each(lambda x, y, s_: _dot(jnp.concatenate([x, y], axis=0), s_, _NT), At, Rt_, S)
        LV = _each(_dot, Lak, V2)
        Tinv = _unit_lower_inverse(Lab, C)
        U2 = _each(lambda t, u, lv: _dot(t, _stack2(u[:C, :]) + lv), Tinv, UY0, LV)
        UV2 = _each(lambda x, y: jnp.concatenate([x, y], axis=0), U2, V2)
        Y2 = _each(_dot, Mrbk, UV2)
        y = _each(lambda u, y2: u[C:, :] + y2[:C, :] + y2[C:, :], UY0, Y2)
        dec2 = _each(lambda b_, kh_, e: jnp.concatenate([_stack2(b_ * e), _stack2(kh_ * e)], axis=0), bvec, kh, e_dec)
        Snew = _each(lambda s_, cl, uv, d2: s_ * jnp.exp(cl) + _dot(uv, d2, _TN), S, cum_last, UV2, dec2)

        for n, (j, p) in enumerate(CH):
            sscr[(b0 + j) * npair + p] = Snew[n]
            m = _seg_sum(y[n]) * (1.0 / RW_N)
            yc = y[n] - m
            var = _seg_sum(yc * yc) * (1.0 / RW_N)
            yn = yc * lax.rsqrt(var + RW_GN_EPS) * gg_ref[p] + gb_ref[p]
            bonus = _seg_sum(r[n] * kh[n] * rk_ref[p]) * v[n]
            orow = pl.ds(pl.multiple_of((b0 + j) * Rt + ci * C, C), C)
            oscr[orow, LANES * p:LANES * (p + 1)] = yn + bonus
        return carry

    lax.fori_loop(0, (bb // gb) * nchunk, chunk, 0)

    row_h = lax.broadcasted_iota(jnp.int32, (LANES, RW_N), 0) < RW_N

    def fin_b(b, carry):
        for p in range(npair):
            S = sscr[b * npair + p]
            sT_ref[b, p] = jnp.where(row_h, S[:, :RW_N], S[:, RW_N:])
        return carry
    lax.fori_loop(0, bb, fin_b, 0)
    _gated_out_ln(oscr, g_ref, x_ref, wout_ref, lng_ref, lnb_ref, out_ref)


def _rwkv_rec(r4, k4, v4, lw4, a4, g4, x, s0, k_k, k_a, r_k, gn_g, gn_b, wout, lng, lnb, *, C, Rt, bb, gb):
    B, npair, T, _ = r4.shape
    D = x.shape[-1]
    kern = functools.partial(_rwkv_rec_kernel, C=C, gb=gb)
    tspec = pl.BlockSpec((bb, npair, Rt, LANES), lambda b, t: (b, 0, t, 0))
    xspec = pl.BlockSpec((bb, Rt, D), lambda b, t: (b, t, 0))
    sspec = pl.BlockSpec((bb, npair, LANES, RW_N), lambda b, t: (b, 0, 0, 0))
    wspec = _const_spec((npair, 1, LANES))
    return pl.pallas_call(
        kern,
        grid=(B // bb, T // Rt),
        in_specs=[tspec] * 6 + [xspec, sspec] + [wspec] * 5 + [_const_spec(wout.shape), _const_spec(lng.shape),
                                                              _const_spec(lnb.shape)],
        out_specs=[xspec, sspec],
        out_shape=[jax.ShapeDtypeStruct(x.shape, F32), jax.ShapeDtypeStruct(s0.shape, F32)],
        scratch_shapes=[pltpu.VMEM((bb * npair, LANES, LANES), F32), pltpu.VMEM((bb * Rt, RW_W), F32)],
        compiler_params=_cparams(2),
        name="rwkv_rec",
    )(r4, k4, v4, lw4, a4, g4, x, s0, k_k, k_a, r_k, gn_g, gn_b, wout, lng, lnb)


def _gdn_pre_kernel(x_ref, halo_ref, win_ref, cw_ref, alog_ref, dtb_ref,
                    q_ref, k_ref, v_ref, z_ref, bg_ref, tail_ref, xscr, gscr, *, s, R, H, Hg, C):
    c = pl.program_id(1)
    CH = GDN_CONV_CH
    tt = R // s

    @pl.when(c == 0)
    def _init():
        xscr[0:H, :] = halo_ref[...]
        gscr[0:Hg, :] = jnp.zeros((Hg, LANES), F32)

    x = x_ref[...]
    u = _mm(x, win_ref[...])
    xb = u[:, :CH]
    xscr[H:H + R, :] = xb
    cw = cw_ref[...]
    y = xb * cw[3:4, :]
    for d in (1, 2, 3):
        y = y + xscr[pl.ds(H - d * s, R), :] * cw[3 - d:4 - d, :]
    tail = xscr[pl.ds(R, H), :]
    xscr[0:H, :] = tail
    tail_ref[...] = tail
    qkv = _silu(y)

    def l2n(z, scale):
        return z * (lax.rsqrt(jnp.sum(z * z, axis=-1, keepdims=True) + GDN_EPS) * scale)

    for h in range(GDN_HK):
        q_ref[h] = l2n(qkv[:, LANES * h:LANES * (h + 1)], GDN_DK ** -0.5)
        k_ref[h] = l2n(qkv[:, GDN_KEY_W + LANES * h:GDN_KEY_W + LANES * (h + 1)], 1.0)
    for h in range(GDN_HV):
        v_ref[h] = qkv[:, 2 * GDN_KEY_W + LANES * h:2 * GDN_KEY_W + LANES * (h + 1)]
        z_ref[h] = u[:, CH + LANES * h:CH + LANES * (h + 1)]

    bg = u[:, CH + GDN_VAL_W:CH + GDN_VAL_W + LANES]
    beta = jax.nn.sigmoid(bg)
    g = -jnp.exp(alog_ref[...]) * jax.nn.softplus(bg + dtb_ref[...])
    t_in = (lax.broadcasted_iota(jnp.int32, (R, 1), 0) // s) % C
    k = 1
    while k < min(C, tt):
        gscr[Hg:Hg + R, :] = g
        g = g + jnp.where(t_in >= k, gscr[pl.ds(Hg - k * s, R), :], 0.0)
        k *= 2
    lane = lax.broadcasted_iota(jnp.int32, (R, LANES), 1)
    bg_ref[...] = jnp.where(lane < GDN_HV, beta, g)


def _gdn_pre(x3, halo, win, cw, alog, dtb, *, s, R, C):
    NB, TT, D = x3.shape
    H = halo.shape[1]
    tt = R // s
    Hg = max(SUBLANES, (min(C, tt) // 2) * s)
    kern = functools.partial(_gdn_pre_kernel, s=s, R=R, H=H, Hg=Hg, C=C)

    def ospec(nh):
        return pl.BlockSpec((None, nh, R, LANES), lambda i, c: (i, 0, c, 0))

    def oshape(nh):
        return jax.ShapeDtypeStruct((NB, nh, TT, LANES), F32)

    return pl.pallas_call(
        kern,
        grid=(NB, TT // R),
        in_specs=[
            pl.BlockSpec((None, R, D), lambda i, c: (i, c, 0)),
            pl.BlockSpec((None, H, GDN_CONV_CH), lambda i, c: (i, 0, 0)),
            _const_spec(win.shape), _const_spec(cw.shape), _const_spec(alog.shape), _const_spec(dtb.shape),
        ],
        out_specs=[ospec(GDN_HK), ospec(GDN_HK), ospec(GDN_HV), ospec(GDN_HV),
                   pl.BlockSpec((None, R, LANES), lambda i, c: (i, c, 0)),
                   pl.BlockSpec((None, H, GDN_CONV_CH), lambda i, c: (i, 0, 0))],
        out_shape=[oshape(GDN_HK), oshape(GDN_HK), oshape(GDN_HV), oshape(GDN_HV),
                   jax.ShapeDtypeStruct((NB, TT, LANES), F32),
                   jax.ShapeDtypeStruct((NB, H, GDN_CONV_CH), F32)],
        scratch_shapes=[pltpu.VMEM((H + R, GDN_CONV_CH), F32), pltpu.VMEM((Hg + R, LANES), F32)],
        compiler_params=_cparams(2),
        name="gdn_pre",
    )(x3, halo, win, cw, alog, dtb)


def _gdn_rec_kernel(q_ref, k_ref, v_ref, z_ref, bg_ref, x_ref, s0_ref, ng_ref, wout_ref, lng_ref, lnb_ref,
                    out_ref, sT_ref, sscr, oscr, *, C, gb):
    tb = pl.program_id(1)
    bb, nhv, Rt, _ = v_ref.shape
    nchunk = Rt // C
    rep = GDN_HV // GDN_HK

    @pl.when(tb == 0)
    def _init():
        sscr[...] = s0_ref[...].reshape(sscr.shape)

    ti = lax.broadcasted_iota(jnp.int32, (C, C), 0)
    si = lax.broadcasted_iota(jnp.int32, (C, C), 1)
    strict = si < ti
    incl = si <= ti

    CH = [(j, h) for j in range(gb) for h in range(nhv)]

    def chunk(idx, carry):
        b0 = (idx // nchunk) * gb
        ci = idx % nchunk
        rs = pl.ds(pl.multiple_of(ci * C, C), C)
        q = [q_ref[b0 + j, h // rep, rs, :] for j, h in CH]
        k = [k_ref[b0 + j, h // rep, rs, :] for j, h in CH]
        v = [v_ref[b0 + j, h, rs, :] for j, h in CH]
        bgs = [bg_ref[b0 + j, rs, :] for j in range(gb)]
        beta = [jnp.broadcast_to(bgs[j][:, h:h + 1], (C, LANES)) for j, h in CH]
        gc = [jnp.broadcast_to(bgs[j][:, GDN_HV + h:GDN_HV + h + 1], (C, LANES)) for j, h in CH]
        S = [sscr[(b0 + j) * nhv + h] for j, h in CH]

        diff = _each(lambda g_: g_[:, :C] - g_.T[:C, :], gc)
        kb = _each(lambda x, y: x * y, k, beta)
        sc = _each(lambda kb_, q_, k_: _dot(jnp.concatenate([kb_, q_], axis=0), k_, _NT), kb, q, k)
        L = _each(lambda z, d: jnp.where(strict, z[:C, :] * jnp.exp(jnp.where(strict, d, 0.0)), 0.0), sc, diff)
        A = _each(lambda z, d: jnp.where(incl, z[C:, :] * jnp.exp(jnp.where(incl, d, 0.0)), 0.0), sc, diff)
        Tinv = _unit_lower_inverse([-l_ for l_ in L], C)
        eg = _each(jnp.exp, gc)
        UW = _each(lambda t, v_, b_, kb_, e: _dot(t, jnp.concatenate([v_ * b_, kb_ * e], axis=1)),
                   Tinv, v, beta, kb, eg)
        WQ = _each(lambda uw, q_, e, s_: _dot(jnp.concatenate([uw[:, GDN_DV:], q_ * e], axis=0), s_),
                   UW, q, eg, S)
        v_new = _each(lambda uw, wq: uw[:, :GDN_DV] - wq[:C, :], UW, WQ)
        o = _each(lambda wq, a_, vn: wq[C:, :] + _dot(a_, vn), WQ, A, v_new)
        g_last = [g_[C - 1:C, :] for g_ in gc]
        Snew = _each(lambda s_, gl, k_, g_, vn: s_ * jnp.exp(gl) + _dot(k_ * jnp.exp(gl - g_), vn, _TN),
                     S, g_last, k, gc, v_new)
        for n, (j, h) in enumerate(CH):
            sscr[(b0 + j) * nhv + h] = Snew[n]
            oh = o[n]
            orow = pl.ds(pl.multiple_of((b0 + j) * Rt + ci * C, C), C)
            oscr[orow, LANES * h:LANES * (h + 1)] = (
                oh * lax.rsqrt(jnp.mean(oh * oh, axis=-1, keepdims=True) + GDN_EPS) * ng_ref[...])
        return carry

    lax.fori_loop(0, (bb // gb) * nchunk, chunk, 0)
    sT_ref[...] = sscr[...].reshape(sT_ref.shape)
    _gated_out_ln(oscr, z_ref, x_ref, wout_ref, lng_ref, lnb_ref, out_ref)


def _gdn_rec(q4, k4, v4, z4, bg, x, s0, norm_g, wout, lng, lnb, *, C, Rt, bb, gb):
    B, _, T, _ = v4.shape
    D = x.shape[-1]
    kern = functools.partial(_gdn_rec_kernel, C=C, gb=gb)

    def tspec(nh):
        return pl.BlockSpec((bb, nh, Rt, LANES), lambda b, t: (b, 0, t, 0))

    xspec = pl.BlockSpec((bb, Rt, D), lambda b, t: (b, t, 0))
    sspec = pl.BlockSpec((bb, GDN_HV, GDN_DK, GDN_DV), lambda b, t: (b, 0, 0, 0))
    return pl.pallas_call(
        kern,
        grid=(B // bb, T // Rt),
        in_specs=[tspec(GDN_HK), tspec(GDN_HK), tspec(GDN_HV), tspec(GDN_HV),
                  pl.BlockSpec((bb, Rt, LANES), lambda b, t: (b, t, 0)), xspec, sspec,
                  _const_spec(norm_g.shape), _const_spec(wout.shape), _const_spec(lng.shape), _const_spec(lnb.shape)],
        out_specs=[xspec, sspec],
        out_shape=[jax.ShapeDtypeStruct(x.shape, F32), jax.ShapeDtypeStruct(s0.shape, F32)],
        scratch_shapes=[pltpu.VMEM((bb * GDN_HV, GDN_DK, GDN_DV), F32), pltpu.VMEM((bb * Rt, GDN_VAL_W), F32)],
        compiler_params=_cparams(2),
        name="gdn_rec",
    )(q4, k4, v4, z4, bg, x, s0, norm_g, wout, lng, lnb)


class _Group:
    def __init__(self, B, T, time_major):
        self.B, self.T, self.time_major = B, T, time_major
        if time_major:
            self.s, self.NB, self.TT, self.R = B, 1, T * B, T * B
        else:
            self.s, self.NB, self.TT, self.R = 1, B, T, min(ROW_BLOCK, T)

    def to_rows(self, x):
        if self.time_major:
            return jnp.swapaxes(x, 0, 1).reshape(1, self.TT, x.shape[-1])
        return x

    def from_rows(self, x3):
        if self.time_major:
            return jnp.swapaxes(x3.reshape(self.T, self.B, x3.shape[-1]), 0, 1)
        return x3

    def halo(self, st, n_steps):
        if self.time_major:
            return jnp.swapaxes(st, 0, 1).reshape(1, n_steps * self.B, st.shape[-1])
        H = _halo_rows(n_steps, 1)
        return jnp.pad(st, ((0, 0), (H - n_steps, 0), (0, 0)))

    def unhalo(self, tail, n_steps):
        if self.time_major:
            return jnp.swapaxes(tail.reshape(n_steps, self.B, tail.shape[-1]), 0, 1)
        return tail[:, tail.shape[1] - n_steps:]

    def vec(self, st):
        return st[None] if self.time_major else st[:, None, :]

    def unvec(self, v):
        return v[0] if self.time_major else v[:, 0, :]

    def heads_to_batch(self, a4, Tpad):
        if not self.time_major:
            return a4
        nh = a4.shape[1]
        a = a4.reshape(nh, self.T, self.B, LANES).transpose(2, 0, 1, 3)
        return jnp.pad(a, ((0, 0), (0, 0), (0, Tpad - self.T), (0, 0)))

    def rows_to_batch(self, a3, Tpad):
        if not self.time_major:
            return a3
        return jnp.pad(self.from_rows(a3), ((0, 0), (0, Tpad - self.T), (0, 0)))

    def rows_from_batch(self, a3):
        if not self.time_major:
            return a3
        return self.to_rows(a3[:, :self.T])

    def rec_tiling(self, chunk):
        if self.time_major:
            Tp = -(-self.T // SUBLANES) * SUBLANES
            return Tp, Tp, Tp, min(self.B, SAMPLE_SEQ_BLOCK), min(self.B, SAMPLE_SEQ_GROUP)
        return min(chunk, self.T), self.T, min(ROW_BLOCK, self.T), 1, 1


def _lru_apply(g, x3, conv_st, h_st, p, lng, lnb):
    out, tail, hT = _lru_layer(x3, g.halo(conv_st, CONV_W - 1), g.vec(h_st), p["win"], p["cw"], p["cb"], p["wg"],
                               p["bg"], p["lam"], p["wout"], lng, lnb, s=g.s, R=g.R)
    return out, g.unhalo(tail, CONV_W - 1), g.unvec(hT)


def _rwkv_apply(g, x3, shift_st, wkv_st, p, lng, lnb):
    B = g.B
    pre = _rwkv_pre(x3, g.halo(shift_st[:, None, :], 1), p["mu"], p["win"], p["w0"], p["w1"], p["w2"], p["a0"],
                    p["a1"], p["a2"], s=g.s, R=g.R)
    C, Tp, Rt, bb, gb = g.rec_tiling(RW_CHUNK)
    r4, k4, v4, g4, lw4, a4 = [g.heads_to_batch(a, Tp) for a in pre]
    s0 = wkv_st.reshape(B, RW_H // 2, 2 * RW_N, RW_N)
    out, sT = _rwkv_rec(r4, k4, v4, lw4, a4, g4, g.rows_to_batch(x3, Tp), s0, p["k_k"], p["k_a"], p["r_k"],
                        p["gn_g"], p["gn_b"], p["wout"], lng, lnb, C=C, Rt=Rt, bb=bb, gb=gb)
    new_shift = g.from_rows(x3)[:, -1]
    return g.rows_from_batch(out), new_shift, sT.reshape(B, RW_H, RW_N, RW_N)


def _gdn_apply(g, x3, conv_st, S_st, p, lng, lnb):
    T = g.T
    C, Tp, Rt, bb, gb = g.rec_tiling(GDN_CHUNK)
    q4, k4, v4, z4, bg, tail = _gdn_pre(x3, g.halo(conv_st, CONV_W - 1), p["win"], p["cw"], p["alog"], p["dtb"],
                                        s=g.s, R=g.R, C=min(GDN_CHUNK, T))
    bgb = g.rows_to_batch(bg, Tp)
    if Tp != T:
        hold = jnp.broadcast_to(bgb[:, T - 1:T, GDN_HV:], (g.B, Tp - T, LANES - GDN_HV))
        bgb = bgb.at[:, T:, GDN_HV:].set(hold)
    tb = lambda a: g.heads_to_batch(a, Tp)
    out, sT = _gdn_rec(tb(q4), tb(k4), tb(v4), tb(z4), bgb, g.rows_to_batch(x3, Tp), S_st, p["norm_g"], p["wout"],
                       lng, lnb, C=C, Rt=Rt, bb=bb, gb=gb)
    return g.rows_from_batch(out), g.unhalo(tail, CONV_W - 1), sT


def _trunk(g, x, st, params, ln_g, ln_b):
    lru_conv, lru_h, rw_shift, rw_S, gdn_conv, gdn_S = st
    new = ([], [], [], [], [], [])
    x3 = g.to_rows(x)
    ia = ib = ic = 0
    for layer in range(DEPTH):
        kind = layer % N_MIXERS
        lng, lnb = ln_g[layer][None, :], ln_b[layer][None, :]
        if kind == 0:
            x3, c, h = _lru_apply(g, x3, lru_conv[ia], lru_h[ia], params["lru"][ia], lng, lnb)
            new[0].append(c)
            new[1].append(h)
            ia += 1
        elif kind == 1:
            x3, sh, S = _rwkv_apply(g, x3, rw_shift[ib], rw_S[ib], params["rwkv"][ib], lng, lnb)
            new[2].append(sh)
            new[3].append(S)
            ib += 1
        else:
            x3, c, S = _gdn_apply(g, x3, gdn_conv[ic], gdn_S[ic], params["gdn"][ic], lng, lnb)
            new[4].append(c)
            new[5].append(S)
            ic += 1
    return g.from_rows(x3), tuple(s[0][None] if len(s) == 1 else jnp.stack(s) for s in new)


def _prep_params(lru_w_in, lru_conv_w, lru_conv_b, lru_wa, lru_ba, lru_wx, lru_bx, lru_lambda, lru_w_out, rw_mu,
                 rw_w_in, rw_w0, rw_w1, rw_w2, rw_a0, rw_a1, rw_a2, rw_k_k, rw_k_a, rw_r_k, rw_gn_g, rw_gn_b,
                 rw_w_out, gdn_w_in, gdn_conv_w, gdn_a_log, gdn_dt_bias, gdn_norm_g, gdn_w_out):
    row = lambda v: v[None, :]
    lru = []
    for n in range(lru_w_in.shape[0]):
        lru.append(dict(win=lru_w_in[n].astype(BF), cw=lru_conv_w[n], cb=row(lru_conv_b[n]),
                        wg=_lru_gate_weights(lru_wa[n], lru_wx[n]), bg=jnp.stack([lru_ba[n], lru_bx[n]]),
                        lam=row(lru_lambda[n]), wout=lru_w_out[n].astype(BF)))
    pairw = lambda v: v.reshape(RW_W // LANES, 1, LANES)
    rwkv = []
    for n in range(rw_w_in.shape[0]):
        rwkv.append(dict(mu=rw_mu[n], win=rw_w_in[n].astype(BF), w0=row(rw_w0[n]), w1=rw_w1[n].astype(BF),
                         w2=rw_w2[n].astype(BF), a0=row(rw_a0[n]), a1=rw_a1[n].astype(BF), a2=rw_a2[n].astype(BF),
                         k_k=pairw(rw_k_k[n]), k_a=pairw(rw_k_a[n]), r_k=pairw(rw_r_k[n]), gn_g=pairw(rw_gn_g[n]),
                         gn_b=pairw(rw_gn_b[n]), wout=rw_w_out[n].astype(BF)))
    gdn = []
    for n in range(gdn_w_in.shape[0]):
        w = gdn_w_in[n]
        o2 = GDN_CONV_CH + GDN_VAL_W
        wpad = jnp.pad(w[:, o2:], ((0, 0), (0, LANES - 2 * GDN_HV)))
        lanes = lambda v: jnp.pad(v, (GDN_HV, LANES - 2 * GDN_HV))[None, :]
        gdn.append(dict(win=jnp.concatenate([w[:, :o2], wpad], axis=1).astype(BF), cw=gdn_conv_w[n],
                        alog=lanes(gdn_a_log[n]), dtb=lanes(gdn_dt_bias[n]), norm_g=row(gdn_norm_g[n]),
                        wout=gdn_w_out[n].astype(BF)))
    return dict(lru=lru, rwkv=rwkv, gdn=gdn)


def kernel(x_prompt, x_sample, state_lru_conv, state_lru_h, state_rwkv_shift, state_rwkv_wkv, state_gdn_conv, state_gdn_S, ln_g, ln_b, lru_w_in, lru_conv_w, lru_conv_b, lru_wa, lru_ba, lru_wx, lru_bx, lru_lambda, lru_w_out, rw_mu, rw_w_in, rw_w0, rw_w1, rw_w2, rw_a0, rw_a1, rw_a2, rw_k_k, rw_k_a, rw_r_k, rw_gn_g, rw_gn_b, rw_w_out, gdn_w_in, gdn_conv_w, gdn_a_log, gdn_dt_bias, gdn_norm_g, gdn_w_out):
    params = _prep_params(lru_w_in, lru_conv_w, lru_conv_b, lru_wa, lru_ba, lru_wx, lru_bx, lru_lambda, lru_w_out,
                          rw_mu, rw_w_in, rw_w0, rw_w1, rw_w2, rw_a0, rw_a1, rw_a2, rw_k_k, rw_k_a, rw_r_k, rw_gn_g,
                          rw_gn_b, rw_w_out, gdn_w_in, gdn_conv_w, gdn_a_log, gdn_dt_bias, gdn_norm_g, gdn_w_out)
    bp, tp, _ = x_prompt.shape
    bs, ts, _ = x_sample.shape
    n_a, n_b, n_c = state_lru_conv.shape[0], state_rwkv_shift.shape[0], state_gdn_conv.shape[0]
    zero_state = (jnp.zeros((n_a, bp, CONV_W - 1, LRU_W), F32),
                  jnp.zeros((n_a, bp, LRU_W), F32),
                  jnp.zeros((n_b, bp, D_MODEL), F32),
                  jnp.zeros((n_b, bp, RW_H, RW_N, RW_N), F32),
                  jnp.zeros((n_c, bp, CONV_W - 1, GDN_CONV_CH), F32),
                  jnp.zeros((n_c, bp, GDN_HV, GDN_DK, GDN_DV), F32))
    y_prompt, sp = _trunk(_Group(bp, tp, False), x_prompt, zero_state, params, ln_g, ln_b)
    y_sample, ss = _trunk(_Group(bs, ts, True), x_sample,
                          (state_lru_conv, state_lru_h, state_rwkv_shift, state_rwkv_wkv, state_gdn_conv,
                           state_gdn_S), params, ln_g, ln_b)
    return (y_prompt, y_sample, sp[0], ss[0], sp[1], ss[1], sp[2], ss[2], sp[3], ss[3], sp[4], ss[4], sp[5], ss[5])
```

```python
import functools
import math

import jax
import jax.numpy as jnp
from jax import lax
from jax.experimental import pallas as pl
from jax.experimental.pallas import tpu as pltpu

F32 = jnp.float32
BF = jnp.bfloat16

D_MODEL = 1024
DEPTH = 4
N_MIXERS = 3
DN_ALPHA = (2.0 * DEPTH) ** 0.25
LN_EPS = 1e-5
CONV_W = 4

LRU_W = D_MODEL
LRU_BLOCKS = 16
LRU_BS = LRU_W // LRU_BLOCKS
LRU_C = 8.0

RW_W = D_MODEL
RW_N = 64
RW_H = RW_W // RW_N
RW_GN_EPS = 64e-5
RW_NORM_EPS = 1e-12

GDN_HK = 4
GDN_HV = 8
GDN_DK = 128
GDN_DV = 128
GDN_KEY_W = GDN_HK * GDN_DK
GDN_VAL_W = GDN_HV * GDN_DV
GDN_CONV_CH = 2 * GDN_KEY_W + GDN_VAL_W
GDN_CHUNK = 64
GDN_EPS = 1e-6

LANES = 128
SUBLANES = 8
VMEM_LIMIT = 56 * 1024 * 1024
ROW_BLOCK = 256
SAMPLE_SEQ_BLOCK = 8
SAMPLE_SEQ_GROUP = 4
RW_CHUNK = 64

_NN = (((1,), (0,)), ((), ()))
_NT = (((1,), (1,)), ((), ()))
_TN = (((0,), (0,)), ((), ()))


def _mm(a, b):
    return jnp.dot(a.astype(BF), b.astype(BF), preferred_element_type=F32)


def _dot(a, b, dims=_NN):
    return lax.dot_general(a.astype(BF), b.astype(BF), dims, preferred_element_type=F32)


def _dot_exact_lhs(a01, b):
    a = a01.astype(BF)
    b1 = b.astype(BF)
    r1 = b - b1.astype(F32)
    b2 = r1.astype(BF)
    b3 = (r1 - b2.astype(F32)).astype(BF)
    d = lambda q: jnp.dot(a, q, preferred_element_type=F32)
    return d(b1) + (d(b2) + d(b3))


def _layer_norm(z, g, b):
    mu = jnp.mean(z, axis=-1, keepdims=True)
    zc = z - mu
    var = jnp.mean(zc * zc, axis=-1, keepdims=True)
    return zc * lax.rsqrt(var + LN_EPS) * g + b


def _silu(x):
    return x * jax.nn.sigmoid(x)


def _cparams(n_axes):
    return pltpu.CompilerParams(dimension_semantics=("arbitrary",) * n_axes, vmem_limit_bytes=VMEM_LIMIT)


def _const_spec(shape):
    nd = len(shape)
    return pl.BlockSpec(shape, lambda *_: (0,) * nd)


def _halo_rows(n_steps, s):
    rows = n_steps * s
    return rows if rows % SUBLANES == 0 else SUBLANES


def _lru_kernel(x_ref, halo_ref, h0_ref, win_ref, cw_ref, cb_ref, wg_ref, bg_ref, lam_ref, wout_ref, lng_ref,
                lnb_ref, out_ref, tail_ref, hT_ref, xscr, ascr, bscr, hscr, *, s, R, H, Hs):
    c = pl.program_id(1)
    C = LRU_W
    tt = R // s

    @pl.when(c == 0)
    def _init():
        xscr[0:H, :] = halo_ref[...]
        hscr[...] = h0_ref[...]
        ascr[0:Hs, :] = jnp.ones((Hs, C), F32)
        bscr[0:Hs, :] = jnp.zeros((Hs, C), F32)

    x = x_ref[...]
    u = _mm(x, win_ref[...])
    xb = u[:, :C]
    gate = u[:, C:]

    xscr[H:H + R, :] = xb
    cw = cw_ref[...]
    xc = xb * cw[3:4, :] + cb_ref[...]
    for d in (1, 2, 3):
        xc = xc + xscr[pl.ds(H - d * s, R), :] * cw[3 - d:4 - d, :]
    tail = xscr[pl.ds(R, H), :]
    xscr[0:H, :] = tail
    tail_ref[...] = tail

    xcb = xc.astype(BF)
    ra, ix = [], []
    for g in range(C // 256):
        gt = jnp.dot(xcb[:, 256 * g:256 * (g + 1)], wg_ref[g], preferred_element_type=F32)
        ra.append(gt[:, :256])
        ix.append(gt[:, 256:])
    bg = bg_ref[...]
    r = jax.nn.sigmoid(jnp.concatenate(ra, axis=1) + bg[0:1, :])
    i = jax.nn.sigmoid(jnp.concatenate(ix, axis=1) + bg[1:2, :])
    log_a = (-LRU_C) * r * jax.nn.softplus(-lam_ref[...])
    a = jnp.exp(log_a)
    th = jnp.tanh(log_a)
    b = jnp.sqrt(-2.0 * th / (1.0 - th)) * i * xc

    rows = lax.broadcasted_iota(jnp.int32, (R, 1), 0)
    hprev = hscr[...]
    if s > 1:
        hprev = jnp.concatenate([hprev] * tt, axis=0)
    b = b + jnp.where(rows < s, a * hprev, 0.0)
    k = 1
    while k < tt:
        ascr[Hs:Hs + R, :] = a
        bscr[Hs:Hs + R, :] = b
        a_s = ascr[pl.ds(Hs - k * s, R), :]
        b_s = bscr[pl.ds(Hs - k * s, R), :]
        b = a * b_s + b
        a = a * a_s
        k *= 2
    h = b
    bscr[Hs:Hs + R, :] = h
    hlast = bscr[pl.ds(Hs + R - s, s), :]
    hscr[...] = hlast
    hT_ref[...] = hlast

    y = _mm(h * _silu(gate), wout_ref[...])
    out_ref[...] = _layer_norm(DN_ALPHA * x + y, lng_ref[...], lnb_ref[...])


def _lru_layer(x3, halo, h0, win, cw, cb, wg, bg, lam, wout, lng, lnb, *, s, R):
    NB, TT, D = x3.shape
    C = LRU_W
    H = halo.shape[1]
    tt = R // s
    Hs = max(SUBLANES, (tt // 2) * s)
    kern = functools.partial(_lru_kernel, s=s, R=R, H=H, Hs=Hs)
    return pl.pallas_call(
        kern,
        grid=(NB, TT // R),
        in_specs=[
            pl.BlockSpec((None, R, D), lambda i, c: (i, c, 0)),
            pl.BlockSpec((None, H, C), lambda i, c: (i, 0, 0)),
            pl.BlockSpec((None, s, C), lambda i, c: (i, 0, 0)),
            _const_spec(win.shape), _const_spec(cw.shape), _const_spec(cb.shape), _const_spec(wg.shape),
            _const_spec(bg.shape), _const_spec(lam.shape), _const_spec(wout.shape), _const_spec(lng.shape),
            _const_spec(lnb.shape),
        ],
        out_specs=[
            pl.BlockSpec((None, R, D), lambda i, c: (i, c, 0)),
            pl.BlockSpec((None, H, C), lambda i, c: (i, 0, 0)),
            pl.BlockSpec((None, s, C), lambda i, c: (i, 0, 0)),
        ],
        out_shape=[
            jax.ShapeDtypeStruct((NB, TT, D), F32),
            jax.ShapeDtypeStruct((NB, H, C), F32),
            jax.ShapeDtypeStruct((NB, s, C), F32),
        ],
        scratch_shapes=[
            pltpu.VMEM((H + R, C), F32),
            pltpu.VMEM((Hs + R, C), F32),
            pltpu.VMEM((Hs + R, C), F32),
            pltpu.VMEM((s, C), F32),
        ],
        compiler_params=_cparams(2),
        name="lru_layer",
    )(x3, halo, h0, win, cw, cb, wg, bg, lam, wout, lng, lnb)


def _lru_gate_weights(wa, wx):
    def bd(w):
        w4 = w.reshape(4, 4, LRU_BS, LRU_BS)
        eye = jnp.eye(4, dtype=w.dtype)
        return jnp.einsum("gaij,ab->gaibj", w4, eye).reshape(4, 256, 256)
    return jnp.concatenate([bd(wa), bd(wx)], axis=2).astype(BF)


def _gated_out_ln(oscr, g_ref, x_ref, w_ref, lng_ref, lnb_ref, out_ref):
    bb, nh, Rt, _ = g_ref.shape
    rows = bb * Rt
    g = jnp.concatenate([g_ref[:, p, :, :].reshape(rows, LANES) for p in range(nh)], axis=1)
    y = _mm(oscr[...] * _silu(g), w_ref[...])
    x = x_ref[...].reshape(rows, D_MODEL)
    out_ref[...] = _layer_norm(DN_ALPHA * x + y, lng_ref[...], lnb_ref[...]).reshape(bb, Rt, D_MODEL)


def _rwkv_pre_kernel(x_ref, halo_ref, mu_ref, win_ref, w0_ref, w1_ref, w2_ref, a0_ref, a1_ref, a2_ref,
                     r_ref, k_ref, v_ref, g_ref, lw_ref, a_ref, xscr, *, s, R, H):
    c = pl.program_id(1)

    @pl.when(c == 0)
    def _init():
        xscr[0:H, :] = halo_ref[...]

    x = x_ref[...]
    xscr[H:H + R, :] = x
    xprev = xscr[pl.ds(H - s, R), :]
    xscr[0:H, :] = xscr[pl.ds(R, H), :]
    xx = xprev - x
    mu = mu_ref[...]
    xm = lambda n: x + xx * mu[n:n + 1, :]

    def put(ref, val):
        for p in range(RW_W // LANES):
            ref[p] = val[:, LANES * p:LANES * (p + 1)]

    put(r_ref, _mm(xm(0), win_ref[0]))
    put(k_ref, _mm(xm(1), win_ref[1]))
    put(v_ref, _mm(xm(2), win_ref[2]))
    put(g_ref, _mm(xm(3), win_ref[3]))
    w_raw = w0_ref[...] + _mm(jnp.tanh(_mm(xm(4), w1_ref[...])), w2_ref[...])
    put(lw_ref, (-math.exp(-0.5)) * jax.nn.sigmoid(w_raw))
    put(a_ref, jax.nn.sigmoid(a0_ref[...] + _mm(_mm(xm(5), a1_ref[...]), a2_ref[...])))


def _rwkv_pre(x3, halo, mu, win, w0, w1, w2, a0, a1, a2, *, s, R):
    NB, TT, D = x3.shape
    H = halo.shape[1]
    nh = RW_W // LANES
    kern = functools.partial(_rwkv_pre_kernel, s=s, R=R, H=H)
    ospec = pl.BlockSpec((None, nh, R, LANES), lambda i, c: (i, 0, c, 0))
    oshape = jax.ShapeDtypeStruct((NB, nh, TT, LANES), F32)
    return pl.pallas_call(
        kern,
        grid=(NB, TT // R),
        in_specs=[
            pl.BlockSpec((None, R, D), lambda i, c: (i, c, 0)),
            pl.BlockSpec((None, H, D), lambda i, c: (i, 0, 0)),
            _const_spec(mu.shape), _const_spec(win.shape), _const_spec(w0.shape), _const_spec(w1.shape),
            _const_spec(w2.shape), _const_spec(a0.shape), _const_spec(a1.shape), _const_spec(a2.shape),
        ],
        out_specs=[ospec] * 6,
        out_shape=[oshape] * 6,
        scratch_shapes=[pltpu.VMEM((H + R, D), F32)],
        compiler_params=_cparams(2),
        name="rwkv_pre",
    )(x3, halo, mu, win, w0, w1, w2, a0, a1, a2)


def _seg_sum(x):
    lane = lax.broadcasted_iota(jnp.int32, x.shape, 1)
    lo = lane < RW_N
    s0 = jnp.sum(jnp.where(lo, x, 0.0), axis=-1, keepdims=True)
    s1 = jnp.sum(jnp.where(lo, 0.0, x), axis=-1, keepdims=True)
    return jnp.where(lo, s0, s1)


def _stack2(x):
    lane = lax.broadcasted_iota(jnp.int32, x.shape, 1)
    lo = lane < x.shape[1] // 2
    return jnp.concatenate([jnp.where(lo, x, 0.0), jnp.where(lo, 0.0, x)], axis=0)


def _each(f, *lists):
    return [f(*t) for t in zip(*lists)]


def _packed_masks(C):
    t = lax.broadcasted_iota(jnp.int32, (C, 2 * C), 0)
    s = lax.broadcasted_iota(jnp.int32, (C, 2 * C), 1) % C
    return s < t, s <= t, s == t


def _unit_lower_inverse(Ls, eye):
    n = Ls[0].shape[0]
    invs = [jnp.where(eye, 1.0, L) for L in Ls]
    Lps = Ls
    span = 2
    while span < n:
        Lps = _each(lambda Lp: _dot(Lp, _stack2(Lp)), Lps)
        yield
        invs = _each(lambda inv, Lp: inv + _dot(inv, _stack2(Lp)), invs, Lps)
        yield
        span *= 2
    return invs


def _run_pipelined(part_a, part_b, batches):
    ctx = {}
    prev = None
    for batch in batches:
        gens = [part_a(batch, ctx)] + ([part_b(prev, ctx)] if prev is not None else [])
        while gens:
            for g in list(gens):
                try:
                    next(g)
                except StopIteration:
                    gens.remove(g)
        prev = batch
    for _ in part_b(prev, ctx):
        pass


def _rec_batches(bb, nchunk, gb):
    if nchunk > 1:
        assert bb == 1
        return tuple(tuple((0, c) for c in range(c0, min(c0 + 2, nchunk))) for c0 in range(0, nchunk, 2))
    return tuple(tuple((b, 0) for b in range(b0, b0 + gb)) for b0 in range(0, bb, gb))


def _rwkv_rec_kernel(r_ref, k_ref, v_ref, lw_ref, a_ref, g_ref, x_ref, s0_ref, kk_ref, ka_ref, rk_ref, gg_ref,
                     gb_ref, wout_ref, lng_ref, lnb_ref, out_ref, sT_ref, sscr, oscr, *, C, batches):
    tb = pl.program_id(1)
    bb, npair, Rt, _ = r_ref.shape
    C2 = 2 * C
    lane_sq = lax.broadcasted_iota(jnp.int32, (LANES, LANES), 1)
    row_sq = lax.broadcasted_iota(jnp.int32, (LANES, LANES), 0)
    same_head = (lane_sq < RW_N) == (row_sq < RW_N)

    @pl.when(tb == 0)
    def _init():
        def init_b(b, carry):
            for p in range(npair):
                s2 = s0_ref[b, p]
                sscr[b * npair + p] = jnp.where(same_head, jnp.concatenate([s2, s2], axis=1), 0.0)
            return carry
        lax.fori_loop(0, bb, init_b, 0)

    strict, incl, eye = _packed_masks(C)
    tri = (lax.broadcasted_iota(jnp.int32, (C, C), 1) <= lax.broadcasted_iota(jnp.int32, (C, C), 0)).astype(F32)

    def part_a(items, ctx):
        CH = [(b, p, ci) for b, ci in items for p in range(npair)]
        rs = lambda ci: pl.ds(ci * C, C)
        r = [r_ref[b, p, rs(ci), :] for b, p, ci in CH]
        k = [k_ref[b, p, rs(ci), :] for b, p, ci in CH]
        v = [v_ref[b, p, rs(ci), :] for b, p, ci in CH]
        lw = [lw_ref[b, p, rs(ci), :] for b, p, ci in CH]
        a = [a_ref[b, p, rs(ci), :] for b, p, ci in CH]
        kn = [k_ * kk_ref[p] for k_, (b, p, ci) in zip(k, CH)]
        kk = _each(lambda z: z * lax.rsqrt(_seg_sum(z * z) + RW_NORM_EPS), kn)
        kh = [k_ * (1.0 + (a_ - 1.0) * ka_ref[p]) for k_, a_, (b, p, ci) in zip(k, a, CH)]
        bvec = _each(lambda x, y: x * y, kk, a)
        cum = _each(lambda x: _dot_exact_lhs(tri, x), lw)
        yield
        cum_last = [c_[C - 1:C, :] for c_ in cum]
        e_neg = _each(lambda c_: jnp.exp(-c_), cum)
        e_dec = _each(lambda cl, c_: jnp.exp(cl - c_), cum_last, cum)
        At = _each(lambda kk_, c_, lw_: -kk_ * jnp.exp(c_ - lw_), kk, cum, lw)
        Rt_ = _each(lambda r_, c_: r_ * jnp.exp(c_), r, cum)
        AR = _each(lambda x, y: jnp.concatenate([x, y], axis=0), At, Rt_)
        BKs = _each(lambda b_, kh_, e: jnp.concatenate([_stack2(b_ * e), _stack2(kh_ * e)], axis=0), bvec, kh, e_neg)
        V2 = _each(_stack2, v)
        sc = _each(lambda x, y: _dot(x, y, _NT), AR, BKs)
        yield
        Lab = [jnp.where(strict, z[:C, :C2], 0.0) for z in sc]
        Lak = [jnp.where(strict, z[:C, C2:], 0.0) for z in sc]
        Mrbk = [jnp.concatenate([jnp.where(incl, z[C:, :C2], 0.0), jnp.where(incl, z[C:, C2:], 0.0)], axis=1)
                for z in sc]
        LV = _each(_dot, Lak, V2)
        Tinv = yield from _unit_lower_inverse(Lab, eye)
        for n, key in enumerate(CH):
            ctx[key] = dict(AR=AR[n], V2=V2[n], LV=LV[n], Tinv=Tinv[n], Mrbk=Mrbk[n], v=v[n], r=r[n], kh=kh[n],
                            sdec=jnp.exp(cum_last[n]), bkdec=jnp.concatenate([bvec[n] * e_dec[n], kh[n] * e_dec[n]], 0))

    def part_b(items, ctx):
        for ci in sorted({ci for _, ci in items}):
            CH = [(b, p, ci) for b, c_ in items if c_ == ci for p in range(npair)]
            X = [ctx.pop(key) for key in CH]
            S = [sscr[b * npair + p] for b, p, _ in CH]
            UY0 = _each(lambda x, s_: _dot(x["AR"], s_, _NT), X, S)
            yield
            U = _each(lambda x, u: _dot(x["Tinv"], _stack2(u[:C, :] + x["LV"])), X, UY0)
            yield
            y = _each(lambda x, u0, u: u0[C:, :] + _dot(x["Mrbk"], jnp.concatenate([_stack2(u), x["V2"]], axis=0)),
                      X, UY0, U)
            Snew = _each(lambda x, s_, u: s_ * x["sdec"] + jnp.where(
                same_head, _dot(jnp.concatenate([u, x["v"]], axis=0), x["bkdec"], _TN), 0.0), X, S, U)
            yield
            for n, (b, p, _) in enumerate(CH):
                sscr[b * npair + p] = Snew[n]
                m = _seg_sum(y[n]) * (1.0 / RW_N)
                yc = y[n] - m
                var = _seg_sum(yc * yc) * (1.0 / RW_N)
                yn = yc * lax.rsqrt(var + RW_GN_EPS) * gg_ref[p] + gb_ref[p]
                bonus = _seg_sum(X[n]["r"] * X[n]["kh"] * rk_ref[p]) * X[n]["v"]
                oscr[b * Rt + ci * C:b * Rt + (ci + 1) * C, LANES * p:LANES * (p + 1)] = yn + bonus
            yield

    _run_pipelined(part_a, part_b, batches)

    row_h = lax.broadcasted_iota(jnp.int32, (LANES, RW_N), 0) < RW_N

    def fin_b(b, carry):
        for p in range(npair):
            S = sscr[b * npair + p]
            sT_ref[b, p] = jnp.where(row_h, S[:, :RW_N], S[:, RW_N:])
        return carry
    lax.fori_loop(0, bb, fin_b, 0)
    _gated_out_ln(oscr, g_ref, x_ref, wout_ref, lng_ref, lnb_ref, out_ref)


def _rwkv_rec(r4, k4, v4, lw4, a4, g4, x, s0, k_k, k_a, r_k, gn_g, gn_b, wout, lng, lnb, *, C, Rt, bb, gb):
    B, npair, T, _ = r4.shape
    D = x.shape[-1]
    kern = functools.partial(_rwkv_rec_kernel, C=C, batches=_rec_batches(bb, Rt // C, gb))
    tspec = pl.BlockSpec((bb, npair, Rt, LANES), lambda b, t: (b, 0, t, 0))
    xspec = pl.BlockSpec((bb, Rt, D), lambda b, t: (b, t, 0))
    sspec = pl.BlockSpec((bb, npair, LANES, RW_N), lambda b, t: (b, 0, 0, 0))
    wspec = _const_spec((npair, 1, LANES))
    return pl.pallas_call(
        kern,
        grid=(B // bb, T // Rt),
        in_specs=[tspec] * 6 + [xspec, sspec] + [wspec] * 5 + [_const_spec(wout.shape), _const_spec(lng.shape),
                                                              _const_spec(lnb.shape)],
        out_specs=[xspec, sspec],
        out_shape=[jax.ShapeDtypeStruct(x.shape, F32), jax.ShapeDtypeStruct(s0.shape, F32)],
        scratch_shapes=[pltpu.VMEM((bb * npair, LANES, LANES), F32), pltpu.VMEM((bb * Rt, RW_W), F32)],
        compiler_params=_cparams(2),
        name="rwkv_rec",
    )(r4, k4, v4, lw4, a4, g4, x, s0, k_k, k_a, r_k, gn_g, gn_b, wout, lng, lnb)


def _gdn_pre_kernel(x_ref, halo_ref, win_ref, cw_ref, alog_ref, dtb_ref,
                    q_ref, k_ref, v_ref, z_ref, bg_ref, tail_ref, xscr, gscr, *, s, R, H, Hg, C):
    c = pl.program_id(1)
    CH = GDN_CONV_CH
    tt = R // s

    @pl.when(c == 0)
    def _init():
        xscr[0:H, :] = halo_ref[...]
        gscr[0:Hg, :] = jnp.zeros((Hg, LANES), F32)

    x = x_ref[...]
    u = _mm(x, win_ref[...])
    xb = u[:, :CH]
    xscr[H:H + R, :] = xb
    cw = cw_ref[...]
    y = xb * cw[3:4, :]
    for d in (1, 2, 3):
        y = y + xscr[pl.ds(H - d * s, R), :] * cw[3 - d:4 - d, :]
    tail = xscr[pl.ds(R, H), :]
    xscr[0:H, :] = tail
    tail_ref[...] = tail
    qkv = _silu(y)

    def l2n(z, scale):
        return z * (lax.rsqrt(jnp.sum(z * z, axis=-1, keepdims=True) + GDN_EPS) * scale)

    for h in range(GDN_HK):
        q_ref[h] = l2n(qkv[:, LANES * h:LANES * (h + 1)], GDN_DK ** -0.5)
        k_ref[h] = l2n(qkv[:, GDN_KEY_W + LANES * h:GDN_KEY_W + LANES * (h + 1)], 1.0)
    for h in range(GDN_HV):
        v_ref[h] = qkv[:, 2 * GDN_KEY_W + LANES * h:2 * GDN_KEY_W + LANES * (h + 1)]
        z_ref[h] = u[:, CH + LANES * h:CH + LANES * (h + 1)]

    bg = u[:, CH + GDN_VAL_W:CH + GDN_VAL_W + LANES]
    beta = jax.nn.sigmoid(bg)
    g = -jnp.exp(alog_ref[...]) * jax.nn.softplus(bg + dtb_ref[...])
    t_in = (lax.broadcasted_iota(jnp.int32, (R, 1), 0) // s) % C
    k = 1
    while k < min(C, tt):
        gscr[Hg:Hg + R, :] = g
        g = g + jnp.where(t_in >= k, gscr[pl.ds(Hg - k * s, R), :], 0.0)
        k *= 2
    lane = lax.broadcasted_iota(jnp.int32, (R, LANES), 1)
    bg_ref[...] = jnp.where(lane < GDN_HV, beta, g)


def _gdn_pre(x3, halo, win, cw, alog, dtb, *, s, R, C):
    NB, TT, D = x3.shape
    H = halo.shape[1]
    tt = R // s
    Hg = max(SUBLANES, (min(C, tt) // 2) * s)
    kern = functools.partial(_gdn_pre_kernel, s=s, R=R, H=H, Hg=Hg, C=C)

    def ospec(nh):
        return pl.BlockSpec((None, nh, R, LANES), lambda i, c: (i, 0, c, 0))

    def oshape(nh):
        return jax.ShapeDtypeStruct((NB, nh, TT, LANES), F32)

    return pl.pallas_call(
        kern,
        grid=(NB, TT // R),
        in_specs=[
            pl.BlockSpec((None, R, D), lambda i, c: (i, c, 0)),
            pl.BlockSpec((None, H, GDN_CONV_CH), lambda i, c: (i, 0, 0)),
            _const_spec(win.shape), _const_spec(cw.shape), _const_spec(alog.shape), _const_spec(dtb.shape),
        ],
        out_specs=[ospec(GDN_HK), ospec(GDN_HK), ospec(GDN_HV), ospec(GDN_HV),
                   pl.BlockSpec((None, R, LANES), lambda i, c: (i, c, 0)),
                   pl.BlockSpec((None, H, GDN_CONV_CH), lambda i, c: (i, 0, 0))],
        out_shape=[oshape(GDN_HK), oshape(GDN_HK), oshape(GDN_HV), oshape(GDN_HV),
                   jax.ShapeDtypeStruct((NB, TT, LANES), F32),
                   jax.ShapeDtypeStruct((NB, H, GDN_CONV_CH), F32)],
        scratch_shapes=[pltpu.VMEM((H + R, GDN_CONV_CH), F32), pltpu.VMEM((Hg + R, LANES), F32)],
        compiler_params=_cparams(2),
        name="gdn_pre",
    )(x3, halo, win, cw, alog, dtb)


def _gdn_rec_kernel(q_ref, k_ref, v_ref, z_ref, bg_ref, x_ref, s0_ref, ng_ref, wout_ref, lng_ref, lnb_ref,
                    out_ref, sT_ref, sscr, oscr, *, C, batches):
    tb = pl.program_id(1)
    bb, nhv, Rt, _ = v_ref.shape
    rep = GDN_HV // GDN_HK

    @pl.when(tb == 0)
    def _init():
        sscr[...] = s0_ref[...].reshape(sscr.shape)

    strict, incl, eye = _packed_masks(C)
    lo = lax.broadcasted_iota(jnp.int32, (C, 2 * C), 1) < C

    def part_a(items, ctx):
        CH = [(b, h, ci) for b, ci in items for h in range(nhv)]
        KH = [(b, m, ci) for b, ci in items for m in range(GDN_HK)]
        rs = lambda ci: pl.ds(ci * C, C)
        bgs = {(b, ci): bg_ref[b, rs(ci), :] for b, ci in items}
        bgT = {key: x.T for key, x in bgs.items()}

        kq = [jnp.concatenate([k_ref[b, m, rs(ci), :], q_ref[b, m, rs(ci), :]], axis=0) for b, m, ci in KH]
        sc = _each(lambda x: _dot(x, jnp.concatenate([x[:C, :], x[:C, :]], axis=0), _NT), kq)
        yield

        def col2(x, lane):
            return jnp.where(lo, jnp.broadcast_to(x[:, lane:lane + 1], (C, 2 * C)),
                             jnp.broadcast_to(x[:, lane + 1:lane + 2], (C, 2 * C)))

        bcol = [col2(bgs[b, ci], rep * m) for b, m, ci in KH]
        gcol = [col2(bgs[b, ci], GDN_HV + rep * m) for b, m, ci in KH]
        grow = [jnp.concatenate([bgT[b, ci][GDN_HV + rep * m:GDN_HV + rep * m + 1, :],
                                 bgT[b, ci][GDN_HV + rep * m + 1:GDN_HV + rep * m + 2, :]], axis=1) for b, m, ci in KH]
        diff = _each(lambda c_, r_: c_ - r_, gcol, grow)
        Lp = _each(lambda z, bc, d: jnp.where(strict, z[:C, :] * bc * jnp.exp(jnp.where(strict, d, 0.0)), 0.0),
                   sc, bcol, diff)
        Ap = _each(lambda z, d: jnp.where(incl, z[C:, :] * jnp.exp(jnp.where(incl, d, 0.0)), 0.0), sc, diff)
        Tinv_p = yield from _unit_lower_inverse([-l_ for l_ in Lp], eye)

        half = lambda xs, n, h: xs[n // rep][:, (h % rep) * C:(h % rep + 1) * C]
        Tinv = [half(Tinv_p, n, h) for n, (b, h, ci) in enumerate(CH)]
        A = [half(Ap, n, h) for n, (b, h, ci) in enumerate(CH)]
        k = [kq[n // rep][:C, :] for n in range(len(CH))]
        q = [kq[n // rep][C:, :] for n in range(len(CH))]
        v = [v_ref[b, h, rs(ci), :] for b, h, ci in CH]
        beta = [jnp.broadcast_to(bgs[b, ci][:, h:h + 1], (C, LANES)) for b, h, ci in CH]
        gc = [jnp.broadcast_to(bgs[b, ci][:, GDN_HV + h:GDN_HV + h + 1], (C, LANES)) for b, h, ci in CH]
        kb = _each(lambda x, y: x * y, k, beta)
        eg = _each(jnp.exp, gc)
        UW = _each(lambda t, v_, b_, kb_, e: _dot(t, jnp.concatenate([v_ * b_, kb_ * e], axis=1)),
                   Tinv, v, beta, kb, eg)
        yield
        for n, key in enumerate(CH):
            g_last = gc[n][C - 1:C, :]
            ctx[key] = dict(U=UW[n][:, :GDN_DV], WQl=jnp.concatenate([UW[n][:, GDN_DV:], q[n] * eg[n]], axis=0),
                            A=A[n], kdec=k[n] * jnp.exp(g_last - gc[n]), sdec=jnp.exp(g_last))

    def part_b(items, ctx):
        for ci in sorted({ci for _, ci in items}):
            CH = [(b, h, ci) for b, c_ in items if c_ == ci for h in range(nhv)]
            X = [ctx.pop(key) for key in CH]
            S = [sscr[b * nhv + h] for b, h, _ in CH]
            WQ = _each(lambda x, s_: _dot(x["WQl"], s_), X, S)
            yield
            v_new = _each(lambda x, wq: x["U"] - wq[:C, :], X, WQ)
            o = _each(lambda x, wq, vn: wq[C:, :] + _dot(x["A"], vn), X, WQ, v_new)
            Snew = _each(lambda x, s_, vn: s_ * x["sdec"] + _dot(x["kdec"], vn, _TN), X, S, v_new)
            yield
            for n, (b, h, _) in enumerate(CH):
                sscr[b * nhv + h] = Snew[n]
                oh = o[n]
                oscr[b * Rt + ci * C:b * Rt + (ci + 1) * C, LANES * h:LANES * (h + 1)] = (
                    oh * lax.rsqrt(jnp.mean(oh * oh, axis=-1, keepdims=True) + GDN_EPS) * ng_ref[...])
            yield

    _run_pipelined(part_a, part_b, batches)
    sT_ref[...] = sscr[...].reshape(sT_ref.shape)
    _gated_out_ln(oscr, z_ref, x_ref, wout_ref, lng_ref, lnb_ref, out_ref)


def _gdn_rec(q4, k4, v4, z4, bg, x, s0, norm_g, wout, lng, lnb, *, C, Rt, bb, gb):
    B, _, T, _ = v4.shape
    D = x.shape[-1]
    kern = functools.partial(_gdn_rec_kernel, C=C, batches=_rec_batches(bb, Rt // C, gb))

    def tspec(nh):
        return pl.BlockSpec((bb, nh, Rt, LANES), lambda b, t: (b, 0, t, 0))

    xspec = pl.BlockSpec((bb, Rt, D), lambda b, t: (b, t, 0))
    sspec = pl.BlockSpec((bb, GDN_HV, GDN_DK, GDN_DV), lambda b, t: (b, 0, 0, 0))
    return pl.pallas_call(
        kern,
        grid=(B // bb, T // Rt),
        in_specs=[tspec(GDN_HK), tspec(GDN_HK), tspec(GDN_HV), tspec(GDN_HV),
                  pl.BlockSpec((bb, Rt, LANES), lambda b, t: (b, t, 0)), xspec, sspec,
                  _const_spec(norm_g.shape), _const_spec(wout.shape), _const_spec(lng.shape), _const_spec(lnb.shape)],
        out_specs=[xspec, sspec],
        out_shape=[jax.ShapeDtypeStruct(x.shape, F32), jax.ShapeDtypeStruct(s0.shape, F32)],
        scratch_shapes=[pltpu.VMEM((bb * GDN_HV, GDN_DK, GDN_DV), F32), pltpu.VMEM((bb * Rt, GDN_VAL_W), F32)],
        compiler_params=_cparams(2),
        name="gdn_rec",
    )(q4, k4, v4, z4, bg, x, s0, norm_g, wout, lng, lnb)


class _Group:
    def __init__(self, B, T, time_major):
        self.B, self.T, self.time_major = B, T, time_major
        if time_major:
            self.s, self.NB, self.TT, self.R = B, 1, T * B, T * B
        else:
            self.s, self.NB, self.TT, self.R = 1, B, T, min(ROW_BLOCK, T)

    def to_rows(self, x):
        if self.time_major:
            return jnp.swapaxes(x, 0, 1).reshape(1, self.TT, x.shape[-1])
        return x

    def from_rows(self, x3):
        if self.time_major:
            return jnp.swapaxes(x3.reshape(self.T, self.B, x3.shape[-1]), 0, 1)
        return x3

    def halo(self, st, n_steps):
        if self.time_major:
            return jnp.swapaxes(st, 0, 1).reshape(1, n_steps * self.B, st.shape[-1])
        H = _halo_rows(n_steps, 1)
        return jnp.pad(st, ((0, 0), (H - n_steps, 0), (0, 0)))

    def unhalo(self, tail, n_steps):
        if self.time_major:
            return jnp.swapaxes(tail.reshape(n_steps, self.B, tail.shape[-1]), 0, 1)
        return tail[:, tail.shape[1] - n_steps:]

    def vec(self, st):
        return st[None] if self.time_major else st[:, None, :]

    def unvec(self, v):
        return v[0] if self.time_major else v[:, 0, :]

    def heads_to_batch(self, a4, Tpad):
        if not self.time_major:
            return a4
        nh = a4.shape[1]
        a = a4.reshape(nh, self.T, self.B, LANES).transpose(2, 0, 1, 3)
        return jnp.pad(a, ((0, 0), (0, 0), (0, Tpad - self.T), (0, 0)))

    def rows_to_batch(self, a3, Tpad):
        if not self.time_major:
            return a3
        return jnp.pad(self.from_rows(a3), ((0, 0), (0, Tpad - self.T), (0, 0)))

    def rows_from_batch(self, a3):
        if not self.time_major:
            return a3
        return self.to_rows(a3[:, :self.T])

    def rec_tiling(self, chunk):
        if self.time_major:
            Tp = -(-self.T // SUBLANES) * SUBLANES
            return Tp, Tp, Tp, min(self.B, SAMPLE_SEQ_BLOCK), min(self.B, SAMPLE_SEQ_GROUP)
        return min(chunk, self.T), self.T, min(ROW_BLOCK, self.T), 1, 1


def _lru_apply(g, x3, conv_st, h_st, p, lng, lnb):
    out, tail, hT = _lru_layer(x3, g.halo(conv_st, CONV_W - 1), g.vec(h_st), p["win"], p["cw"], p["cb"], p["wg"],
                               p["bg"], p["lam"], p["wout"], lng, lnb, s=g.s, R=g.R)
    return out, g.unhalo(tail, CONV_W - 1), g.unvec(hT)


def _rwkv_apply(g, x3, shift_st, wkv_st, p, lng, lnb):
    B = g.B
    pre = _rwkv_pre(x3, g.halo(shift_st[:, None, :], 1), p["mu"], p["win"], p["w0"], p["w1"], p["w2"], p["a0"],
                    p["a1"], p["a2"], s=g.s, R=g.R)
    C, Tp, Rt, bb, gb = g.rec_tiling(RW_CHUNK)
    r4, k4, v4, g4, lw4, a4 = [g.heads_to_batch(a, Tp) for a in pre]
    s0 = wkv_st.reshape(B, RW_H // 2, 2 * RW_N, RW_N)
    out, sT = _rwkv_rec(r4, k4, v4, lw4, a4, g4, g.rows_to_batch(x3, Tp), s0, p["k_k"], p["k_a"], p["r_k"],
                        p["gn_g"], p["gn_b"], p["wout"], lng, lnb, C=C, Rt=Rt, bb=bb, gb=gb)
    new_shift = g.from_rows(x3)[:, -1]
    return g.rows_from_batch(out), new_shift, sT.reshape(B, RW_H, RW_N, RW_N)


def _gdn_apply(g, x3, conv_st, S_st, p, lng, lnb):
    T = g.T
    C, Tp, Rt, bb, gb = g.rec_tiling(GDN_CHUNK)
    q4, k4, v4, z4, bg, tail = _gdn_pre(x3, g.halo(conv_st, CONV_W - 1), p["win"], p["cw"], p["alog"], p["dtb"],
                                        s=g.s, R=g.R, C=min(GDN_CHUNK, T))
    bgb = g.rows_to_batch(bg, T)
    if Tp != T:
        held = jnp.where(jnp.arange(LANES) < GDN_HV, 0.0, bgb[:, T - 1:T, :])
        bgb = jnp.concatenate([bgb, jnp.broadcast_to(held, (g.B, Tp - T, LANES))], axis=1)
    tb = lambda a: g.heads_to_batch(a, Tp)
    out, sT = _gdn_rec(tb(q4), tb(k4), tb(v4), tb(z4), bgb, g.rows_to_batch(x3, Tp), S_st, p["norm_g"], p["wout"],
                       lng, lnb, C=C, Rt=Rt, bb=bb, gb=gb)
    return g.rows_from_batch(out), g.unhalo(tail, CONV_W - 1), sT


def _trunk(g, x, st, params, ln_g, ln_b):
    lru_conv, lru_h, rw_shift, rw_S, gdn_conv, gdn_S = st
    new = ([], [], [], [], [], [])
    x3 = g.to_rows(x)
    ia = ib = ic = 0
    for layer in range(DEPTH):
        kind = layer % N_MIXERS
        lng, lnb = ln_g[layer][None, :], ln_b[layer][None, :]
        if kind == 0:
            x3, c, h = _lru_apply(g, x3, lru_conv[ia], lru_h[ia], params["lru"][ia], lng, lnb)
            new[0].append(c)
            new[1].append(h)
            ia += 1
        elif kind == 1:
            x3, sh, S = _rwkv_apply(g, x3, rw_shift[ib], rw_S[ib], params["rwkv"][ib], lng, lnb)
            new[2].append(sh)
            new[3].append(S)
            ib += 1
        else:
            x3, c, S = _gdn_apply(g, x3, gdn_conv[ic], gdn_S[ic], params["gdn"][ic], lng, lnb)
            new[4].append(c)
            new[5].append(S)
            ic += 1
    return g.from_rows(x3), tuple(s[0][None] if len(s) == 1 else jnp.stack(s) for s in new)


def _prep_params(lru_w_in, lru_conv_w, lru_conv_b, lru_wa, lru_ba, lru_wx, lru_bx, lru_lambda, lru_w_out, rw_mu,
                 rw_w_in, rw_w0, rw_w1, rw_w2, rw_a0, rw_a1, rw_a2, rw_k_k, rw_k_a, rw_r_k, rw_gn_g, rw_gn_b,
                 rw_w_out, gdn_w_in, gdn_conv_w, gdn_a_log, gdn_dt_bias, gdn_norm_g, gdn_w_out):
    row = lambda v: v[None, :]
    lru = []
    for n in range(lru_w_in.shape[0]):
        lru.append(dict(win=lru_w_in[n].astype(BF), cw=lru_conv_w[n], cb=row(lru_conv_b[n]),
                        wg=_lru_gate_weights(lru_wa[n], lru_wx[n]), bg=jnp.stack([lru_ba[n], lru_bx[n]]),
                        lam=row(lru_lambda[n]), wout=lru_w_out[n].astype(BF)))
    pairw = lambda v: v.reshape(RW_W // LANES, 1, LANES)
    rwkv = []
    for n in range(rw_w_in.shape[0]):
        rwkv.append(dict(mu=rw_mu[n], win=rw_w_in[n].astype(BF), w0=row(rw_w0[n]), w1=rw_w1[n].astype(BF),
                         w2=rw_w2[n].astype(BF), a0=row(rw_a0[n]), a1=rw_a1[n].astype(BF), a2=rw_a2[n].astype(BF),
                         k_k=pairw(rw_k_k[n]), k_a=pairw(rw_k_a[n]), r_k=pairw(rw_r_k[n]), gn_g=pairw(rw_gn_g[n]),
                         gn_b=pairw(rw_gn_b[n]), wout=rw_w_out[n].astype(BF)))
    gdn = []
    for n in range(gdn_w_in.shape[0]):
        w = gdn_w_in[n]
        o2 = GDN_CONV_CH + GDN_VAL_W
        wpad = jnp.pad(w[:, o2:], ((0, 0), (0, LANES - 2 * GDN_HV)))
        lanes = lambda v: jnp.pad(v, (GDN_HV, LANES - 2 * GDN_HV))[None, :]
        gdn.append(dict(win=jnp.concatenate([w[:, :o2], wpad], axis=1).astype(BF), cw=gdn_conv_w[n],
                        alog=lanes(gdn_a_log[n]), dtb=lanes(gdn_dt_bias[n]), norm_g=row(gdn_norm_g[n]),
                        wout=gdn_w_out[n].astype(BF)))
    return dict(lru=lru, rwkv=rwkv, gdn=gdn)


def kernel(x_prompt, x_sample, state_lru_conv, state_lru_h, state_rwkv_shift, state_rwkv_wkv, state_gdn_conv, state_gdn_S, ln_g, ln_b, lru_w_in, lru_conv_w, lru_conv_b, lru_wa, lru_ba, lru_wx, lru_bx, lru_lambda, lru_w_out, rw_mu, rw_w_in, rw_w0, rw_w1, rw_w2, rw_a0, rw_a1, rw_a2, rw_k_k, rw_k_a, rw_r_k, rw_gn_g, rw_gn_b, rw_w_out, gdn_w_in, gdn_conv_w, gdn_a_log, gdn_dt_bias, gdn_norm_g, gdn_w_out):
    params = _prep_params(lru_w_in, lru_conv_w, lru_conv_b, lru_wa, lru_ba, lru_wx, lru_bx, lru_lambda, lru_w_out,
                          rw_mu, rw_w_in, rw_w0, rw_w1, rw_w2, rw_a0, rw_a1, rw_a2, rw_k_k, rw_k_a, rw_r_k, rw_gn_g,
                          rw_gn_b, rw_w_out, gdn_w_in, gdn_conv_w, gdn_a_log, gdn_dt_bias, gdn_norm_g, gdn_w_out)
    bp, tp, _ = x_prompt.shape
    bs, ts, _ = x_sample.shape
    n_a, n_b, n_c = state_lru_conv.shape[0], state_rwkv_shift.shape[0], state_gdn_conv.shape[0]
    zero_state = (jnp.zeros((n_a, bp, CONV_W - 1, LRU_W), F32),
                  jnp.zeros((n_a, bp, LRU_W), F32),
                  jnp.zeros((n_b, bp, D_MODEL), F32),
                  jnp.zeros((n_b, bp, RW_H, RW_N, RW_N), F32),
                  jnp.zeros((n_c, bp, CONV_W - 1, GDN_CONV_CH), F32),
                  jnp.zeros((n_c, bp, GDN_HV, GDN_DK, GDN_DV), F32))
    y_prompt, sp = _trunk(_Group(bp, tp, False), x_prompt, zero_state, params, ln_g, ln_b)
    y_sample, ss = _trunk(_Group(bs, ts, True), x_sample,
                          (state_lru_conv, state_lru_h, state_rwkv_shift, state_rwkv_wkv, state_gdn_conv,
                           state_gdn_S), params, ln_g, ln_b)
    return (y_prompt, y_sample, sp[0], ss[0], sp[1], ss[1], sp[2], ss[2], sp[3], ss[3], sp[4], ss[4], sp[5], ss[5])
```

```python
import functools
import math

import jax
import jax.numpy as jnp
from jax import lax
from jax.experimental import pallas as pl
from jax.experimental.pallas import tpu as pltpu

F32 = jnp.float32
BF = jnp.bfloat16

D_MODEL = 1024
DEPTH = 4
N_MIXERS = 3
DN_ALPHA = (2.0 * DEPTH) ** 0.25
LN_EPS = 1e-5
CONV_W = 4

LRU_W = D_MODEL
LRU_BLOCKS = 16
LRU_BS = LRU_W // LRU_BLOCKS
LRU_C = 8.0

RW_W = D_MODEL
RW_N = 64
RW_H = RW_W // RW_N
RW_GN_EPS = 64e-5
RW_NORM_EPS = 1e-12

GDN_HK = 4
GDN_HV = 8
GDN_DK = 128
GDN_DV = 128
GDN_KEY_W = GDN_HK * GDN_DK
GDN_VAL_W = GDN_HV * GDN_DV
GDN_CONV_CH = 2 * GDN_KEY_W + GDN_VAL_W
GDN_CHUNK = 64
GDN_EPS = 1e-6

LANES = 128
SUBLANES = 8
VMEM_LIMIT = 56 * 1024 * 1024
ROW_BLOCK = 256
SAMPLE_SEQ_BLOCK = 8
SAMPLE_SEQ_GROUP = 4
RW_CHUNK = 64

_NN = (((1,), (0,)), ((), ()))
_NT = (((1,), (1,)), ((), ()))
_TN = (((0,), (0,)), ((), ()))


def _mm(a, b):
    return jnp.dot(a.astype(BF), b.astype(BF), preferred_element_type=F32)


def _dot(a, b, dims=_NN):
    return lax.dot_general(a.astype(BF), b.astype(BF), dims, preferred_element_type=F32)


def _dot_exact_lhs(a01, b):
    a = a01.astype(BF)
    b1 = b.astype(BF)
    r1 = b - b1.astype(F32)
    b2 = r1.astype(BF)
    b3 = (r1 - b2.astype(F32)).astype(BF)
    d = lambda q: jnp.dot(a, q, preferred_element_type=F32)
    return d(b1) + (d(b2) + d(b3))


def _layer_norm(z, g, b):
    mu = jnp.mean(z, axis=-1, keepdims=True)
    zc = z - mu
    var = jnp.mean(zc * zc, axis=-1, keepdims=True)
    return zc * lax.rsqrt(var + LN_EPS) * g + b


def _silu(x):
    return x * jax.nn.sigmoid(x)


def _cparams(n_axes):
    return pltpu.CompilerParams(dimension_semantics=("arbitrary",) * n_axes, vmem_limit_bytes=VMEM_LIMIT)


def _const_spec(shape):
    nd = len(shape)
    return pl.BlockSpec(shape, lambda *_: (0,) * nd)


def _halo_rows(n_steps, s):
    rows = n_steps * s
    return rows if rows % SUBLANES == 0 else SUBLANES


def _lru_kernel(x_ref, halo_ref, h0_ref, win_ref, cw_ref, cb_ref, wg_ref, bg_ref, lam_ref, wout_ref, lng_ref,
                lnb_ref, out_ref, tail_ref, hT_ref, xscr, ascr, bscr, hscr, *, s, R, H, Hs, swap):
    c = pl.program_id(1)
    C = LRU_W
    tt = R // s

    @pl.when(c == 0)
    def _init():
        xscr[0:H, :] = halo_ref[...]
        hscr[...] = h0_ref[...]
        ascr[0:Hs, :] = jnp.ones((Hs, C), F32)
        bscr[0:Hs, :] = jnp.zeros((Hs, C), F32)

    x = x_ref[...]
    if swap:
        x = jnp.swapaxes(x, 0, 1).reshape(R, D_MODEL)
    u = _mm(x, win_ref[...])
    xb = u[:, :C]
    gate = u[:, C:]

    xscr[H:H + R, :] = xb
    cw = cw_ref[...]
    xc = xb * cw[3:4, :] + cb_ref[...]
    for d in (1, 2, 3):
        xc = xc + xscr[pl.ds(H - d * s, R), :] * cw[3 - d:4 - d, :]
    tail = xscr[pl.ds(R, H), :]
    xscr[0:H, :] = tail
    tail_ref[...] = tail

    xcb = xc.astype(BF)
    ra, ix = [], []
    for g in range(C // 256):
        gt = jnp.dot(xcb[:, 256 * g:256 * (g + 1)], wg_ref[g], preferred_element_type=F32)
        ra.append(gt[:, :256])
        ix.append(gt[:, 256:])
    bg = bg_ref[...]
    r = jax.nn.sigmoid(jnp.concatenate(ra, axis=1) + bg[0:1, :])
    i = jax.nn.sigmoid(jnp.concatenate(ix, axis=1) + bg[1:2, :])
    log_a = (-LRU_C) * r * jax.nn.softplus(-lam_ref[...])
    a = jnp.exp(log_a)
    th = jnp.tanh(log_a)
    b = jnp.sqrt(-2.0 * th / (1.0 - th)) * i * xc

    rows = lax.broadcasted_iota(jnp.int32, (R, 1), 0)
    hprev = hscr[...]
    if s > 1:
        hprev = jnp.concatenate([hprev] * tt, axis=0)
    b = b + jnp.where(rows < s, a * hprev, 0.0)
    k = 1
    while k < tt:
        ascr[Hs:Hs + R, :] = a
        bscr[Hs:Hs + R, :] = b
        a_s = ascr[pl.ds(Hs - k * s, R), :]
        b_s = bscr[pl.ds(Hs - k * s, R), :]
        b = a * b_s + b
        a = a * a_s
        k *= 2
    h = b
    bscr[Hs:Hs + R, :] = h
    hlast = bscr[pl.ds(Hs + R - s, s), :]
    hscr[...] = hlast
    hT_ref[...] = hlast

    y = _mm(h * _silu(gate), wout_ref[...])
    out = _layer_norm(DN_ALPHA * x + y, lng_ref[...], lnb_ref[...])
    if swap:
        out = jnp.swapaxes(out.reshape(tt, s, D_MODEL), 0, 1)
    out_ref[...] = out


def _lru_layer(x3, halo, h0, win, cw, cb, wg, bg, lam, wout, lng, lnb, *, s, R, swap):
    C = LRU_W
    D = x3.shape[-1]
    H = halo.shape[1]
    tt = R // s
    Hs = max(SUBLANES, (tt // 2) * s)
    if swap:
        NB, TT = 1, x3.shape[1] * s
        xspec = pl.BlockSpec((s, tt, D), lambda i, c: (0, c, 0))
    else:
        NB, TT = x3.shape[:2]
        xspec = pl.BlockSpec((None, R, D), lambda i, c: (i, c, 0))
    kern = functools.partial(_lru_kernel, s=s, R=R, H=H, Hs=Hs, swap=swap)
    return pl.pallas_call(
        kern,
        grid=(NB, TT // R),
        in_specs=[
            xspec,
            pl.BlockSpec((None, H, C), lambda i, c: (i, 0, 0)),
            pl.BlockSpec((None, s, C), lambda i, c: (i, 0, 0)),
            _const_spec(win.shape), _const_spec(cw.shape), _const_spec(cb.shape), _const_spec(wg.shape),
            _const_spec(bg.shape), _const_spec(lam.shape), _const_spec(wout.shape), _const_spec(lng.shape),
            _const_spec(lnb.shape),
        ],
        out_specs=[
            xspec,
            pl.BlockSpec((None, H, C), lambda i, c: (i, 0, 0)),
            pl.BlockSpec((None, s, C), lambda i, c: (i, 0, 0)),
        ],
        out_shape=[
            jax.ShapeDtypeStruct(x3.shape, F32),
            jax.ShapeDtypeStruct((NB, H, C), F32),
            jax.ShapeDtypeStruct((NB, s, C), F32),
        ],
        scratch_shapes=[
            pltpu.VMEM((H + R, C), F32),
            pltpu.VMEM((Hs + R, C), F32),
            pltpu.VMEM((Hs + R, C), F32),
            pltpu.VMEM((s, C), F32),
        ],
        compiler_params=_cparams(2),
        name="lru_layer",
    )(x3, halo, h0, win, cw, cb, wg, bg, lam, wout, lng, lnb)


def _lru_gate_weights(wa, wx):
    def bd(w):
        w4 = w.reshape(4, 4, LRU_BS, LRU_BS)
        eye = jnp.eye(4, dtype=w.dtype)
        return jnp.einsum("gaij,ab->gaibj", w4, eye).reshape(4, 256, 256)
    return jnp.concatenate([bd(wa), bd(wx)], axis=2).astype(BF)


def _gated_out_ln(oscr, g_ref, x_ref, w_ref, lng_ref, lnb_ref, out_ref):
    bb, nh, Rt, _ = g_ref.shape
    rows = bb * Rt
    g = jnp.concatenate([g_ref[:, p, :, :].reshape(rows, LANES) for p in range(nh)], axis=1)
    y = _mm(oscr[...] * _silu(g), w_ref[...])
    x = x_ref[...].reshape(rows, D_MODEL)
    out_ref[...] = _layer_norm(DN_ALPHA * x + y, lng_ref[...], lnb_ref[...]).reshape(bb, Rt, D_MODEL)


def _rwkv_pre_kernel(x_ref, halo_ref, mu_ref, win_ref, w0_ref, w1_ref, w2_ref, a0_ref, a1_ref, a2_ref,
                     r_ref, k_ref, v_ref, g_ref, lw_ref, a_ref, xscr, *, s, R, H):
    c = pl.program_id(1)

    @pl.when(c == 0)
    def _init():
        xscr[0:H, :] = halo_ref[...]

    x = x_ref[...]
    xscr[H:H + R, :] = x
    xprev = xscr[pl.ds(H - s, R), :]
    xscr[0:H, :] = xscr[pl.ds(R, H), :]
    xx = xprev - x
    mu = mu_ref[...]
    xm = lambda n: x + xx * mu[n:n + 1, :]

    def put(ref, val):
        for p in range(RW_W // LANES):
            ref[p] = val[:, LANES * p:LANES * (p + 1)]

    put(r_ref, _mm(xm(0), win_ref[0]))
    put(k_ref, _mm(xm(1), win_ref[1]))
    put(v_ref, _mm(xm(2), win_ref[2]))
    put(g_ref, _mm(xm(3), win_ref[3]))
    w_raw = w0_ref[...] + _mm(jnp.tanh(_mm(xm(4), w1_ref[...])), w2_ref[...])
    put(lw_ref, (-math.exp(-0.5)) * jax.nn.sigmoid(w_raw))
    put(a_ref, jax.nn.sigmoid(a0_ref[...] + _mm(_mm(xm(5), a1_ref[...]), a2_ref[...])))


def _rwkv_pre(x3, halo, mu, win, w0, w1, w2, a0, a1, a2, *, s, R):
    NB, TT, D = x3.shape
    H = halo.shape[1]
    nh = RW_W // LANES
    kern = functools.partial(_rwkv_pre_kernel, s=s, R=R, H=H)
    ospec = pl.BlockSpec((None, nh, R, LANES), lambda i, c: (i, 0, c, 0))
    oshape = jax.ShapeDtypeStruct((NB, nh, TT, LANES), F32)
    return pl.pallas_call(
        kern,
        grid=(NB, TT // R),
        in_specs=[
            pl.BlockSpec((None, R, D), lambda i, c: (i, c, 0)),
            pl.BlockSpec((None, H, D), lambda i, c: (i, 0, 0)),
            _const_spec(mu.shape), _const_spec(win.shape), _const_spec(w0.shape), _const_spec(w1.shape),
            _const_spec(w2.shape), _const_spec(a0.shape), _const_spec(a1.shape), _const_spec(a2.shape),
        ],
        out_specs=[ospec] * 6,
        out_shape=[oshape] * 6,
        scratch_shapes=[pltpu.VMEM((H + R, D), F32)],
        compiler_params=_cparams(2),
        name="rwkv_pre",
    )(x3, halo, mu, win, w0, w1, w2, a0, a1, a2)


def _seg_sum(x):
    lane = lax.broadcasted_iota(jnp.int32, x.shape, 1)
    lo = lane < RW_N
    s0 = jnp.sum(jnp.where(lo, x, 0.0), axis=-1, keepdims=True)
    s1 = jnp.sum(jnp.where(lo, 0.0, x), axis=-1, keepdims=True)
    return jnp.where(lo, s0, s1)


def _stack2(x):
    lane = lax.broadcasted_iota(jnp.int32, x.shape, 1)
    lo = lane < x.shape[1] // 2
    return jnp.concatenate([jnp.where(lo, x, 0.0), jnp.where(lo, 0.0, x)], axis=0)


def _each(f, *lists):
    return [f(*t) for t in zip(*lists)]


def _packed_masks(C):
    t = lax.broadcasted_iota(jnp.int32, (C, 2 * C), 0)
    s = lax.broadcasted_iota(jnp.int32, (C, 2 * C), 1) % C
    return s < t, s <= t, s == t


def _unit_lower_inverse(Ls, eye):
    n = Ls[0].shape[0]
    invs = [jnp.where(eye, 1.0, L) for L in Ls]
    Lps = Ls
    span = 2
    while span < n:
        Lps = _each(lambda Lp: _dot(Lp, _stack2(Lp)), Lps)
        yield
        invs = _each(lambda inv, Lp: inv + _dot(inv, _stack2(Lp)), invs, Lps)
        yield
        span *= 2
    return invs


def _run_pipelined(part_a, part_b, batches):
    ctx = {}
    prev = None
    for batch in batches:
        gens = [part_a(batch, ctx)] + ([part_b(prev, ctx)] if prev is not None else [])
        while gens:
            for g in list(gens):
                try:
                    next(g)
                except StopIteration:
                    gens.remove(g)
        prev = batch
    for _ in part_b(prev, ctx):
        pass


def _rec_batches(bb, nchunk, gb):
    if nchunk > 1:
        assert bb == 1
        return tuple(tuple((0, c) for c in range(c0, min(c0 + 2, nchunk))) for c0 in range(0, nchunk, 2))
    return tuple(tuple((b, 0) for b in range(b0, b0 + gb)) for b0 in range(0, bb, gb))


def _rwkv_rec_kernel(r_ref, k_ref, v_ref, lw_ref, a_ref, g_ref, x_ref, s0_ref, kk_ref, ka_ref, rk_ref, gg_ref,
                     gb_ref, wout_ref, lng_ref, lnb_ref, out_ref, sT_ref, sscr, oscr, *, C, batches):
    tb = pl.program_id(1)
    bb, npair, Rt, _ = r_ref.shape
    C2 = 2 * C
    lane_sq = lax.broadcasted_iota(jnp.int32, (LANES, LANES), 1)
    row_sq = lax.broadcasted_iota(jnp.int32, (LANES, LANES), 0)
    same_head = (lane_sq < RW_N) == (row_sq < RW_N)

    @pl.when(tb == 0)
    def _init():
        def init_b(b, carry):
            for p in range(npair):
                s2 = s0_ref[b, p]
                sscr[b * npair + p] = jnp.where(same_head, jnp.concatenate([s2, s2], axis=1), 0.0)
            return carry
        lax.fori_loop(0, bb, init_b, 0)

    strict, incl, eye = _packed_masks(C)
    tri = (lax.broadcasted_iota(jnp.int32, (C, C), 1) <= lax.broadcasted_iota(jnp.int32, (C, C), 0)).astype(F32)

    def part_a(items, ctx):
        CH = [(b, p, ci) for b, ci in items for p in range(npair)]
        rs = lambda ci: pl.ds(ci * C, C)
        r = [r_ref[b, p, rs(ci), :] for b, p, ci in CH]
        k = [k_ref[b, p, rs(ci), :] for b, p, ci in CH]
        v = [v_ref[b, p, rs(ci), :] for b, p, ci in CH]
        lw = [lw_ref[b, p, rs(ci), :] for b, p, ci in CH]
        a = [a_ref[b, p, rs(ci), :] for b, p, ci in CH]
        kn = [k_ * kk_ref[p] for k_, (b, p, ci) in zip(k, CH)]
        kk = _each(lambda z: z * lax.rsqrt(_seg_sum(z * z) + RW_NORM_EPS), kn)
        kh = [k_ * (1.0 + (a_ - 1.0) * ka_ref[p]) for k_, a_, (b, p, ci) in zip(k, a, CH)]
        bvec = _each(lambda x, y: x * y, kk, a)
        cum = _each(lambda x: _dot_exact_lhs(tri, x), lw)
        yield
        cum_last = [c_[C - 1:C, :] for c_ in cum]
        e_neg = _each(lambda c_: jnp.exp(-c_), cum)
        e_dec = _each(lambda cl, c_: jnp.exp(cl - c_), cum_last, cum)
        At = _each(lambda kk_, c_, lw_: -kk_ * jnp.exp(c_ - lw_), kk, cum, lw)
        Rt_ = _each(lambda r_, c_: r_ * jnp.exp(c_), r, cum)
        AR = _each(lambda x, y: jnp.concatenate([x, y], axis=0), At, Rt_)
        BKs = _each(lambda b_, kh_, e: jnp.concatenate([_stack2(b_ * e), _stack2(kh_ * e)], axis=0), bvec, kh, e_neg)
        V2 = _each(_stack2, v)
        sc = _each(lambda x, y: _dot(x, y, _NT), AR, BKs)
        yield
        Lab = [jnp.where(strict, z[:C, :C2], 0.0) for z in sc]
        Lak = [jnp.where(strict, z[:C, C2:], 0.0) for z in sc]
        Mrbk = [jnp.concatenate([jnp.where(incl, z[C:, :C2], 0.0), jnp.where(incl, z[C:, C2:], 0.0)], axis=1)
                for z in sc]
        LV = _each(_dot, Lak, V2)
        Tinv = yield from _unit_lower_inverse(Lab, eye)
        for n, key in enumerate(CH):
            ctx[key] = dict(AR=AR[n], V2=V2[n], LV=LV[n], Tinv=Tinv[n], Mrbk=Mrbk[n], v=v[n], r=r[n], kh=kh[n],
                            sdec=jnp.exp(cum_last[n]), bkdec=jnp.concatenate([bvec[n] * e_dec[n], kh[n] * e_dec[n]], 0))

    def part_b(items, ctx):
        for ci in sorted({ci for _, ci in items}):
            CH = [(b, p, ci) for b, c_ in items if c_ == ci for p in range(npair)]
            X = [ctx.pop(key) for key in CH]
            S = [sscr[b * npair + p] for b, p, _ in CH]
            UY0 = _each(lambda x, s_: _dot(x["AR"], s_, _NT), X, S)
            yield
            U = _each(lambda x, u: _dot(x["Tinv"], _stack2(u[:C, :] + x["LV"])), X, UY0)
            yield
            y = _each(lambda x, u0, u: u0[C:, :] + _dot(x["Mrbk"], jnp.concatenate([_stack2(u), x["V2"]], axis=0)),
                      X, UY0, U)
            Snew = _each(lambda x, s_, u: s_ * x["sdec"] + jnp.where(
                same_head, _dot(jnp.concatenate([u, x["v"]], axis=0), x["bkdec"], _TN), 0.0), X, S, U)
            yield
            for n, (b, p, _) in enumerate(CH):
                sscr[b * npair + p] = Snew[n]
                m = _seg_sum(y[n]) * (1.0 / RW_N)
                yc = y[n] - m
                var = _seg_sum(yc * yc) * (1.0 / RW_N)
                yn = yc * lax.rsqrt(var + RW_GN_EPS) * gg_ref[p] + gb_ref[p]
                bonus = _seg_sum(X[n]["r"] * X[n]["kh"] * rk_ref[p]) * X[n]["v"]
                oscr[b * Rt + ci * C:b * Rt + (ci + 1) * C, LANES * p:LANES * (p + 1)] = yn + bonus
            yield

    _run_pipelined(part_a, part_b, batches)

    row_h = lax.broadcasted_iota(jnp.int32, (LANES, RW_N), 0) < RW_N

    def fin_b(b, carry):
        for p in range(npair):
            S = sscr[b * npair + p]
            sT_ref[b, p] = jnp.where(row_h, S[:, :RW_N], S[:, RW_N:])
        return carry
    lax.fori_loop(0, bb, fin_b, 0)
    _gated_out_ln(oscr, g_ref, x_ref, wout_ref, lng_ref, lnb_ref, out_ref)


def _rwkv_rec(r4, k4, v4, lw4, a4, g4, x, s0, k_k, k_a, r_k, gn_g, gn_b, wout, lng, lnb, *, C, Rt, bb, gb):
    B, npair, T, _ = r4.shape
    D = x.shape[-1]
    kern = functools.partial(_rwkv_rec_kernel, C=C, batches=_rec_batches(bb, Rt // C, gb))
    tspec = pl.BlockSpec((bb, npair, Rt, LANES), lambda b, t: (b, 0, t, 0))
    xspec = pl.BlockSpec((bb, Rt, D), lambda b, t: (b, t, 0))
    sspec = pl.BlockSpec((bb, npair, LANES, RW_N), lambda b, t: (b, 0, 0, 0))
    wspec = _const_spec((npair, 1, LANES))
    return pl.pallas_call(
        kern,
        grid=(B // bb, T // Rt),
        in_specs=[tspec] * 6 + [xspec, sspec] + [wspec] * 5 + [_const_spec(wout.shape), _const_spec(lng.shape),
                                                              _const_spec(lnb.shape)],
        out_specs=[xspec, sspec],
        out_shape=[jax.ShapeDtypeStruct(x.shape, F32), jax.ShapeDtypeStruct(s0.shape, F32)],
        scratch_shapes=[pltpu.VMEM((bb * npair, LANES, LANES), F32), pltpu.VMEM((bb * Rt, RW_W), F32)],
        compiler_params=_cparams(2),
        name="rwkv_rec",
    )(r4, k4, v4, lw4, a4, g4, x, s0, k_k, k_a, r_k, gn_g, gn_b, wout, lng, lnb)


def _gdn_pre_kernel(x_ref, halo_ref, win_ref, cw_ref, alog_ref, dtb_ref,
                    q_ref, k_ref, v_ref, z_ref, bg_ref, tail_ref, xscr, gscr, *, s, R, H, Hg, C):
    c = pl.program_id(1)
    CH = GDN_CONV_CH
    tt = R // s

    @pl.when(c == 0)
    def _init():
        xscr[0:H, :] = halo_ref[...]
        gscr[0:Hg, :] = jnp.zeros((Hg, LANES), F32)

    x = x_ref[...]
    u = _mm(x, win_ref[...])
    xb = u[:, :CH]
    xscr[H:H + R, :] = xb
    cw = cw_ref[...]
    y = xb * cw[3:4, :]
    for d in (1, 2, 3):
        y = y + xscr[pl.ds(H - d * s, R), :] * cw[3 - d:4 - d, :]
    tail = xscr[pl.ds(R, H), :]
    xscr[0:H, :] = tail
    tail_ref[...] = tail
    qkv = _silu(y)

    def l2n(z, scale):
        return z * (lax.rsqrt(jnp.sum(z * z, axis=-1, keepdims=True) + GDN_EPS) * scale)

    for h in range(GDN_HK):
        q_ref[h] = l2n(qkv[:, LANES * h:LANES * (h + 1)], GDN_DK ** -0.5)
        k_ref[h] = l2n(qkv[:, GDN_KEY_W + LANES * h:GDN_KEY_W + LANES * (h + 1)], 1.0)
    for h in range(GDN_HV):
        v_ref[h] = qkv[:, 2 * GDN_KEY_W + LANES * h:2 * GDN_KEY_W + LANES * (h + 1)]
        z_ref[h] = u[:, CH + LANES * h:CH + LANES * (h + 1)]

    bg = u[:, CH + GDN_VAL_W:CH + GDN_VAL_W + LANES]
    beta = jax.nn.sigmoid(bg)
    g = -jnp.exp(alog_ref[...]) * jax.nn.softplus(bg + dtb_ref[...])
    t_in = (lax.broadcasted_iota(jnp.int32, (R, 1), 0) // s) % C
    k = 1
    while k < min(C, tt):
        gscr[Hg:Hg + R, :] = g
        g = g + jnp.where(t_in >= k, gscr[pl.ds(Hg - k * s, R), :], 0.0)
        k *= 2
    lane = lax.broadcasted_iota(jnp.int32, (R, LANES), 1)
    bg_ref[...] = jnp.where(lane < GDN_HV, beta, g)


def _gdn_pre(x3, halo, win, cw, alog, dtb, *, s, R, C):
    NB, TT, D = x3.shape
    H = halo.shape[1]
    tt = R // s
    Hg = max(SUBLANES, (min(C, tt) // 2) * s)
    kern = functools.partial(_gdn_pre_kernel, s=s, R=R, H=H, Hg=Hg, C=C)

    def ospec(nh):
        return pl.BlockSpec((None, nh, R, LANES), lambda i, c: (i, 0, c, 0))

    def oshape(nh):
        return jax.ShapeDtypeStruct((NB, nh, TT, LANES), F32)

    return pl.pallas_call(
        kern,
        grid=(NB, TT // R),
        in_specs=[
            pl.BlockSpec((None, R, D), lambda i, c: (i, c, 0)),
            pl.BlockSpec((None, H, GDN_CONV_CH), lambda i, c: (i, 0, 0)),
            _const_spec(win.shape), _const_spec(cw.shape), _const_spec(alog.shape), _const_spec(dtb.shape),
        ],
        out_specs=[ospec(GDN_HK), ospec(GDN_HK), ospec(GDN_HV), ospec(GDN_HV),
                   pl.BlockSpec((None, R, LANES), lambda i, c: (i, c, 0)),
                   pl.BlockSpec((None, H, GDN_CONV_CH), lambda i, c: (i, 0, 0))],
        out_shape=[oshape(GDN_HK), oshape(GDN_HK), oshape(GDN_HV), oshape(GDN_HV),
                   jax.ShapeDtypeStruct((NB, TT, LANES), F32),
                   jax.ShapeDtypeStruct((NB, H, GDN_CONV_CH), F32)],
        scratch_shapes=[pltpu.VMEM((H + R, GDN_CONV_CH), F32), pltpu.VMEM((Hg + R, LANES), F32)],
        compiler_params=_cparams(2),
        name="gdn_pre",
    )(x3, halo, win, cw, alog, dtb)


def _gdn_rec_kernel(q_ref, k_ref, v_ref, z_ref, bg_ref, x_ref, s0_ref, ng_ref, wout_ref, lng_ref, lnb_ref,
                    out_ref, sT_ref, sscr, oscr, *, C, batches):
    tb = pl.program_id(1)
    bb, nhv, Rt, _ = v_ref.shape
    rep = GDN_HV // GDN_HK

    @pl.when(tb == 0)
    def _init():
        sscr[...] = s0_ref[...].reshape(sscr.shape)

    strict, incl, eye = _packed_masks(C)
    lo = lax.broadcasted_iota(jnp.int32, (C, 2 * C), 1) < C

    def part_a(items, ctx):
        CH = [(b, h, ci) for b, ci in items for h in range(nhv)]
        KH = [(b, m, ci) for b, ci in items for m in range(GDN_HK)]
        rs = lambda ci: pl.ds(ci * C, C)
        bgs = {(b, ci): bg_ref[b, rs(ci), :] for b, ci in items}
        bgT = {key: x.T for key, x in bgs.items()}

        kq = [jnp.concatenate([k_ref[b, m, rs(ci), :], q_ref[b, m, rs(ci), :]], axis=0) for b, m, ci in KH]
        sc = _each(lambda x: _dot(x, jnp.concatenate([x[:C, :], x[:C, :]], axis=0), _NT), kq)
        yield

        def col2(x, lane):
            return jnp.where(lo, jnp.broadcast_to(x[:, lane:lane + 1], (C, 2 * C)),
                             jnp.broadcast_to(x[:, lane + 1:lane + 2], (C, 2 * C)))

        bcol = [col2(bgs[b, ci], rep * m) for b, m, ci in KH]
        gcol = [col2(bgs[b, ci], GDN_HV + rep * m) for b, m, ci in KH]
        grow = [jnp.concatenate([bgT[b, ci][GDN_HV + rep * m:GDN_HV + rep * m + 1, :],
                                 bgT[b, ci][GDN_HV + rep * m + 1:GDN_HV + rep * m + 2, :]], axis=1) for b, m, ci in KH]
        diff = _each(lambda c_, r_: c_ - r_, gcol, grow)
        Lp = _each(lambda z, bc, d: jnp.where(strict, z[:C, :] * bc * jnp.exp(jnp.where(strict, d, 0.0)), 0.0),
                   sc, bcol, diff)
        Ap = _each(lambda z, d: jnp.where(incl, z[C:, :] * jnp.exp(jnp.where(incl, d, 0.0)), 0.0), sc, diff)
        Tinv_p = yield from _unit_lower_inverse([-l_ for l_ in Lp], eye)

        half = lambda xs, n, h: xs[n // rep][:, (h % rep) * C:(h % rep + 1) * C]
        Tinv = [half(Tinv_p, n, h) for n, (b, h, ci) in enumerate(CH)]
        A = [half(Ap, n, h) for n, (b, h, ci) in enumerate(CH)]
        k = [kq[n // rep][:C, :] for n in range(len(CH))]
        q = [kq[n // rep][C:, :] for n in range(len(CH))]
        v = [v_ref[b, h, rs(ci), :] for b, h, ci in CH]
        beta = [jnp.broadcast_to(bgs[b, ci][:, h:h + 1], (C, LANES)) for b, h, ci in CH]
        gc = [jnp.broadcast_to(bgs[b, ci][:, GDN_HV + h:GDN_HV + h + 1], (C, LANES)) for b, h, ci in CH]
        kb = _each(lambda x, y: x * y, k, beta)
        eg = _each(jnp.exp, gc)
        UW = _each(lambda t, v_, b_, kb_, e: _dot(t, jnp.concatenate([v_ * b_, kb_ * e], axis=1)),
                   Tinv, v, beta, kb, eg)
        yield
        for n, key in enumerate(CH):
            g_last = gc[n][C - 1:C, :]
            ctx[key] = dict(U=UW[n][:, :GDN_DV], WQl=jnp.concatenate([UW[n][:, GDN_DV:], q[n] * eg[n]], axis=0),
                            A=A[n], kdec=k[n] * jnp.exp(g_last - gc[n]), sdec=jnp.exp(g_last))

    def part_b(items, ctx):
        for ci in sorted({ci for _, ci in items}):
            CH = [(b, h, ci) for b, c_ in items if c_ == ci for h in range(nhv)]
            X = [ctx.pop(key) for key in CH]
            S = [sscr[b * nhv + h] for b, h, _ in CH]
            WQ = _each(lambda x, s_: _dot(x["WQl"], s_), X, S)
            yield
            v_new = _each(lambda x, wq: x["U"] - wq[:C, :], X, WQ)
            o = _each(lambda x, wq, vn: wq[C:, :] + _dot(x["A"], vn), X, WQ, v_new)
            Snew = _each(lambda x, s_, vn: s_ * x["sdec"] + _dot(x["kdec"], vn, _TN), X, S, v_new)
            yield
            for n, (b, h, _) in enumerate(CH):
                sscr[b * nhv + h] = Snew[n]
                oh = o[n]
                oscr[b * Rt + ci * C:b * Rt + (ci + 1) * C, LANES * h:LANES * (h + 1)] = (
                    oh * lax.rsqrt(jnp.mean(oh * oh, axis=-1, keepdims=True) + GDN_EPS) * ng_ref[...])
            yield

    _run_pipelined(part_a, part_b, batches)
    sT_ref[...] = sscr[...].reshape(sT_ref.shape)
    _gated_out_ln(oscr, z_ref, x_ref, wout_ref, lng_ref, lnb_ref, out_ref)


def _gdn_rec(q4, k4, v4, z4, bg, x, s0, norm_g, wout, lng, lnb, *, C, Rt, bb, gb):
    B, _, T, _ = v4.shape
    D = x.shape[-1]
    kern = functools.partial(_gdn_rec_kernel, C=C, batches=_rec_batches(bb, Rt // C, gb))

    def tspec(nh):
        return pl.BlockSpec((bb, nh, Rt, LANES), lambda b, t: (b, 0, t, 0))

    xspec = pl.BlockSpec((bb, Rt, D), lambda b, t: (b, t, 0))
    sspec = pl.BlockSpec((bb, GDN_HV, GDN_DK, GDN_DV), lambda b, t: (b, 0, 0, 0))
    return pl.pallas_call(
        kern,
        grid=(B // bb, T // Rt),
        in_specs=[tspec(GDN_HK), tspec(GDN_HK), tspec(GDN_HV), tspec(GDN_HV),
                  pl.BlockSpec((bb, Rt, LANES), lambda b, t: (b, t, 0)), xspec, sspec,
                  _const_spec(norm_g.shape), _const_spec(wout.shape), _const_spec(lng.shape), _const_spec(lnb.shape)],
        out_specs=[xspec, sspec],
        out_shape=[jax.ShapeDtypeStruct(x.shape, F32), jax.ShapeDtypeStruct(s0.shape, F32)],
        scratch_shapes=[pltpu.VMEM((bb * GDN_HV, GDN_DK, GDN_DV), F32), pltpu.VMEM((bb * Rt, GDN_VAL_W), F32)],
        compiler_params=_cparams(2),
        name="gdn_rec",
    )(q4, k4, v4, z4, bg, x, s0, norm_g, wout, lng, lnb)


class _Group:
    def __init__(self, B, T, time_major):
        self.B, self.T, self.time_major = B, T, time_major
        if time_major:
            self.s, self.NB, self.TT, self.R = B, 1, T * B, T * B
        else:
            self.s, self.NB, self.TT, self.R = 1, B, T, min(ROW_BLOCK, T)

    def to_rows(self, x):
        if self.time_major:
            return jnp.swapaxes(x, 0, 1).reshape(1, self.TT, x.shape[-1])
        return x

    def from_rows(self, x3):
        if self.time_major:
            return jnp.swapaxes(x3.reshape(self.T, self.B, x3.shape[-1]), 0, 1)
        return x3

    def halo(self, st, n_steps):
        if self.time_major:
            return jnp.swapaxes(st, 0, 1).reshape(1, n_steps * self.B, st.shape[-1])
        H = _halo_rows(n_steps, 1)
        return jnp.pad(st, ((0, 0), (H - n_steps, 0), (0, 0)))

    def unhalo(self, tail, n_steps):
        if self.time_major:
            return jnp.swapaxes(tail.reshape(n_steps, self.B, tail.shape[-1]), 0, 1)
        return tail[:, tail.shape[1] - n_steps:]

    def vec(self, st):
        return st[None] if self.time_major else st[:, None, :]

    def unvec(self, v):
        return v[0] if self.time_major else v[:, 0, :]

    def heads_to_batch(self, a4, Tpad):
        if not self.time_major:
            return a4
        nh = a4.shape[1]
        a = a4.reshape(nh, self.T, self.B, LANES).transpose(2, 0, 1, 3)
        return jnp.pad(a, ((0, 0), (0, 0), (0, Tpad - self.T), (0, 0)))

    def rows_to_batch(self, a3, Tpad):
        if not self.time_major:
            return a3
        return jnp.pad(self.from_rows(a3), ((0, 0), (0, Tpad - self.T), (0, 0)))

    def rows_from_batch(self, a3):
        if not self.time_major:
            return a3
        return self.to_rows(a3[:, :self.T])

    def rec_tiling(self, chunk):
        if self.time_major:
            Tp = -(-self.T // SUBLANES) * SUBLANES
            return Tp, Tp, Tp, min(self.B, SAMPLE_SEQ_BLOCK), min(self.B, SAMPLE_SEQ_GROUP)
        return min(chunk, self.T), self.T, min(ROW_BLOCK, self.T), 1, 1


def _lru_apply(g, x3, conv_st, h_st, p, lng, lnb):
    swap = not g.time_major and g.B % SUBLANES == 0
    lay = _Group(g.B, g.T, True) if swap else g
    out, tail, hT = _lru_layer(x3, lay.halo(conv_st, CONV_W - 1), lay.vec(h_st), p["win"], p["cw"], p["cb"],
                               p["wg"], p["bg"], p["lam"], p["wout"], lng, lnb, s=lay.s,
                               R=min(ROW_BLOCK, g.T * g.B) if swap else g.R, swap=swap)
    return out, lay.unhalo(tail, CONV_W - 1), lay.unvec(hT)


def _rwkv_apply(g, x3, shift_st, wkv_st, p, lng, lnb):
    B = g.B
    pre = _rwkv_pre(x3, g.halo(shift_st[:, None, :], 1), p["mu"], p["win"], p["w0"], p["w1"], p["w2"], p["a0"],
                    p["a1"], p["a2"], s=g.s, R=g.R)
    C, Tp, Rt, bb, gb = g.rec_tiling(RW_CHUNK)
    r4, k4, v4, g4, lw4, a4 = [g.heads_to_batch(a, Tp) for a in pre]
    s0 = wkv_st.reshape(B, RW_H // 2, 2 * RW_N, RW_N)
    out, sT = _rwkv_rec(r4, k4, v4, lw4, a4, g4, g.rows_to_batch(x3, Tp), s0, p["k_k"], p["k_a"], p["r_k"],
                        p["gn_g"], p["gn_b"], p["wout"], lng, lnb, C=C, Rt=Rt, bb=bb, gb=gb)
    new_shift = g.from_rows(x3)[:, -1]
    return g.rows_from_batch(out), new_shift, sT.reshape(B, RW_H, RW_N, RW_N)


def _gdn_apply(g, x3, conv_st, S_st, p, lng, lnb):
    T = g.T
    C, Tp, Rt, bb, gb = g.rec_tiling(GDN_CHUNK)
    q4, k4, v4, z4, bg, tail = _gdn_pre(x3, g.halo(conv_st, CONV_W - 1), p["win"], p["cw"], p["alog"], p["dtb"],
                                        s=g.s, R=g.R, C=min(GDN_CHUNK, T))
    bgb = g.rows_to_batch(bg, T)
    if Tp != T:
        held = jnp.where(jnp.arange(LANES) < GDN_HV, 0.0, bgb[:, T - 1:T, :])
        bgb = jnp.concatenate([bgb, jnp.broadcast_to(held, (g.B, Tp - T, LANES))], axis=1)
    tb = lambda a: g.heads_to_batch(a, Tp)
    out, sT = _gdn_rec(tb(q4), tb(k4), tb(v4), tb(z4), bgb, g.rows_to_batch(x3, Tp), S_st, p["norm_g"], p["wout"],
                       lng, lnb, C=C, Rt=Rt, bb=bb, gb=gb)
    return g.rows_from_batch(out), g.unhalo(tail, CONV_W - 1), sT


def _trunk(g, x, st, params, ln_g, ln_b):
    lru_conv, lru_h, rw_shift, rw_S, gdn_conv, gdn_S = st
    new = ([], [], [], [], [], [])
    x3 = g.to_rows(x)
    ia = ib = ic = 0
    for layer in range(DEPTH):
        kind = layer % N_MIXERS
        lng, lnb = ln_g[layer][None, :], ln_b[layer][None, :]
        if kind == 0:
            x3, c, h = _lru_apply(g, x3, lru_conv[ia], lru_h[ia], params["lru"][ia], lng, lnb)
            new[0].append(c)
            new[1].append(h)
            ia += 1
        elif kind == 1:
            x3, sh, S = _rwkv_apply(g, x3, rw_shift[ib], rw_S[ib], params["rwkv"][ib], lng, lnb)
            new[2].append(sh)
            new[3].append(S)
            ib += 1
        else:
            x3, c, S = _gdn_apply(g, x3, gdn_conv[ic], gdn_S[ic], params["gdn"][ic], lng, lnb)
            new[4].append(c)
            new[5].append(S)
            ic += 1
    return g.from_rows(x3), tuple(s[0][None] if len(s) == 1 else jnp.stack(s) for s in new)


def _prep_params(lru_w_in, lru_conv_w, lru_conv_b, lru_wa, lru_ba, lru_wx, lru_bx, lru_lambda, lru_w_out, rw_mu,
                 rw_w_in, rw_w0, rw_w1, rw_w2, rw_a0, rw_a1, rw_a2, rw_k_k, rw_k_a, rw_r_k, rw_gn_g, rw_gn_b,
                 rw_w_out, gdn_w_in, gdn_conv_w, gdn_a_log, gdn_dt_bias, gdn_norm_g, gdn_w_out):
    row = lambda v: v[None, :]
    lru = []
    for n in range(lru_w_in.shape[0]):
        lru.append(dict(win=lru_w_in[n].astype(BF), cw=lru_conv_w[n], cb=row(lru_conv_b[n]),
                        wg=_lru_gate_weights(lru_wa[n], lru_wx[n]), bg=jnp.stack([lru_ba[n], lru_bx[n]]),
                        lam=row(lru_lambda[n]), wout=lru_w_out[n].astype(BF)))
    pairw = lambda v: v.reshape(RW_W // LANES, 1, LANES)
    rwkv = []
    for n in range(rw_w_in.shape[0]):
        rwkv.append(dict(mu=rw_mu[n], win=rw_w_in[n].astype(BF), w0=row(rw_w0[n]), w1=rw_w1[n].astype(BF),
                         w2=rw_w2[n].astype(BF), a0=row(rw_a0[n]), a1=rw_a1[n].astype(BF), a2=rw_a2[n].astype(BF),
                         k_k=pairw(rw_k_k[n]), k_a=pairw(rw_k_a[n]), r_k=pairw(rw_r_k[n]), gn_g=pairw(rw_gn_g[n]),
                         gn_b=pairw(rw_gn_b[n]), wout=rw_w_out[n].astype(BF)))
    gdn = []
    for n in range(gdn_w_in.shape[0]):
        w = gdn_w_in[n]
        o2 = GDN_CONV_CH + GDN_VAL_W
        wpad = jnp.pad(w[:, o2:], ((0, 0), (0, LANES - 2 * GDN_HV)))
        lanes = lambda v: jnp.pad(v, (GDN_HV, LANES - 2 * GDN_HV))[None, :]
        gdn.append(dict(win=jnp.concatenate([w[:, :o2], wpad], axis=1).astype(BF), cw=gdn_conv_w[n],
                        alog=lanes(gdn_a_log[n]), dtb=lanes(gdn_dt_bias[n]), norm_g=row(gdn_norm_g[n]),
                        wout=gdn_w_out[n].astype(BF)))
    return dict(lru=lru, rwkv=rwkv, gdn=gdn)


def kernel(x_prompt, x_sample, state_lru_conv, state_lru_h, state_rwkv_shift, state_rwkv_wkv, state_gdn_conv, state_gdn_S, ln_g, ln_b, lru_w_in, lru_conv_w, lru_conv_b, lru_wa, lru_ba, lru_wx, lru_bx, lru_lambda, lru_w_out, rw_mu, rw_w_in, rw_w0, rw_w1, rw_w2, rw_a0, rw_a1, rw_a2, rw_k_k, rw_k_a, rw_r_k, rw_gn_g, rw_gn_b, rw_w_out, gdn_w_in, gdn_conv_w, gdn_a_log, gdn_dt_bias, gdn_norm_g, gdn_w_out):
    params = _prep_params(lru_w_in, lru_conv_w, lru_conv_b, lru_wa, lru_ba, lru_wx, lru_bx, lru_lambda, lru_w_out,
                          rw_mu, rw_w_in, rw_w0, rw_w1, rw_w2, rw_a0, rw_a1, rw_a2, rw_k_k, rw_k_a, rw_r_k, rw_gn_g,
                          rw_gn_b, rw_w_out, gdn_w_in, gdn_conv_w, gdn_a_log, gdn_dt_bias, gdn_norm_g, gdn_w_out)
    bp, tp, _ = x_prompt.shape
    bs, ts, _ = x_sample.shape
    n_a, n_b, n_c = state_lru_conv.shape[0], state_rwkv_shift.shape[0], state_gdn_conv.shape[0]
    zero_state = (jnp.zeros((n_a, bp, CONV_W - 1, LRU_W), F32),
                  jnp.zeros((n_a, bp, LRU_W), F32),
                  jnp.zeros((n_b, bp, D_MODEL), F32),
                  jnp.zeros((n_b, bp, RW_H, RW_N, RW_N), F32),
                  jnp.zeros((n_c, bp, CONV_W - 1, GDN_CONV_CH), F32),
                  jnp.zeros((n_c, bp, GDN_HV, GDN_DK, GDN_DV), F32))
    y_prompt, sp = _trunk(_Group(bp, tp, False), x_prompt, zero_state, params, ln_g, ln_b)
    y_sample, ss = _trunk(_Group(bs, ts, True), x_sample,
                          (state_lru_conv, state_lru_h, state_rwkv_shift, state_rwkv_wkv, state_gdn_conv,
                           state_gdn_S), params, ln_g, ln_b)
    return (y_prompt, y_sample, sp[0], ss[0], sp[1], ss[1], sp[2], ss[2], sp[3], ss[3], sp[4], ss[4], sp[5], ss[5])
```

```python
import functools
import math

import jax
import jax.numpy as jnp
from jax import lax
from jax.experimental import pallas as pl
from jax.experimental.pallas import tpu as pltpu

F32 = jnp.float32
BF = jnp.bfloat16

D_MODEL = 1024
DEPTH = 4
N_MIXERS = 3
DN_ALPHA = (2.0 * DEPTH) ** 0.25
LN_EPS = 1e-5
CONV_W = 4

LRU_W = D_MODEL
LRU_BLOCKS = 16
LRU_BS = LRU_W // LRU_BLOCKS
LRU_C = 8.0

RW_W = D_MODEL
RW_N = 64
RW_H = RW_W // RW_N
RW_GN_EPS = 64e-5
RW_NORM_EPS = 1e-12

GDN_HK = 4
GDN_HV = 8
GDN_DK = 128
GDN_DV = 128
GDN_KEY_W = GDN_HK * GDN_DK
GDN_VAL_W = GDN_HV * GDN_DV
GDN_CONV_CH = 2 * GDN_KEY_W + GDN_VAL_W
GDN_CHUNK = 64
GDN_EPS = 1e-6

LANES = 128
SUBLANES = 8
VMEM_LIMIT = 56 * 1024 * 1024
ROW_BLOCK = 256
SAMPLE_SEQ_BLOCK = 8
SAMPLE_SEQ_GROUP = 4
RW_CHUNK = 64

_NN = (((1,), (0,)), ((), ()))
_NT = (((1,), (1,)), ((), ()))
_TN = (((0,), (0,)), ((), ()))


def _mm(a, b):
    return jnp.dot(a.astype(BF), b.astype(BF), preferred_element_type=F32)


def _dot(a, b, dims=_NN):
    return lax.dot_general(a.astype(BF), b.astype(BF), dims, preferred_element_type=F32)


def _dot_exact_lhs(a01, b):
    a = a01.astype(BF)
    b1 = b.astype(BF)
    r1 = b - b1.astype(F32)
    b2 = r1.astype(BF)
    b3 = (r1 - b2.astype(F32)).astype(BF)
    d = lambda q: jnp.dot(a, q, preferred_element_type=F32)
    return d(b1) + (d(b2) + d(b3))


def _layer_norm(z, g, b):
    mu = jnp.mean(z, axis=-1, keepdims=True)
    zc = z - mu
    var = jnp.mean(zc * zc, axis=-1, keepdims=True)
    return zc * lax.rsqrt(var + LN_EPS) * g + b


def _silu(x):
    return x * jax.nn.sigmoid(x)


def _cparams(n_axes):
    return pltpu.CompilerParams(dimension_semantics=("arbitrary",) * n_axes, vmem_limit_bytes=VMEM_LIMIT)


def _const_spec(shape):
    nd = len(shape)
    return pl.BlockSpec(shape, lambda *_: (0,) * nd)


def _halo_rows(n_steps, s):
    rows = n_steps * s
    return rows if rows % SUBLANES == 0 else SUBLANES


def _lru_kernel(x_ref, halo_ref, h0_ref, win_ref, cw_ref, cb_ref, wg_ref, bg_ref, lam_ref, wout_ref, lng_ref,
                lnb_ref, out_ref, tail_ref, hT_ref, xscr, ascr, bscr, hscr, *, s, R, H, Hs, swap):
    c = pl.program_id(1)
    C = LRU_W
    tt = R // s

    @pl.when(c == 0)
    def _init():
        xscr[0:H, :] = halo_ref[...]
        hscr[...] = h0_ref[...]
        ascr[0:Hs, :] = jnp.ones((Hs, C), F32)
        bscr[0:Hs, :] = jnp.zeros((Hs, C), F32)

    x = x_ref[...]
    if swap:
        x = jnp.swapaxes(x, 0, 1).reshape(R, D_MODEL)
    u = _mm(x, win_ref[...])
    xb = u[:, :C]
    gate = u[:, C:]

    xscr[H:H + R, :] = xb
    xc = xb * cw_ref[3] + cb_ref[...]
    for d in (1, 2, 3):
        xc = xc + xscr[pl.ds(H - d * s, R), :] * cw_ref[3 - d]
    tail = xscr[pl.ds(R, H), :]
    xscr[0:H, :] = tail
    tail_ref[...] = tail

    xcb = xc.astype(BF)
    ra, ix = [], []
    for g in range(C // 256):
        gt = jnp.dot(xcb[:, 256 * g:256 * (g + 1)], wg_ref[g], preferred_element_type=F32)
        ra.append(gt[:, :256])
        ix.append(gt[:, 256:])
    r = jax.nn.sigmoid(jnp.concatenate(ra, axis=1) + bg_ref[0])
    i = jax.nn.sigmoid(jnp.concatenate(ix, axis=1) + bg_ref[1])
    log_a = (-LRU_C) * r * jax.nn.softplus(-lam_ref[...])
    a = jnp.exp(log_a)
    th = jnp.tanh(log_a)
    b = jnp.sqrt(-2.0 * th / (1.0 - th)) * i * xc

    rows = lax.broadcasted_iota(jnp.int32, (R, 1), 0)
    hprev = hscr[...]
    if s > 1:
        hprev = jnp.concatenate([hprev] * tt, axis=0)
    b = b + jnp.where(rows < s, a * hprev, 0.0)
    k = 1
    while k < tt:
        ascr[Hs:Hs + R, :] = a
        bscr[Hs:Hs + R, :] = b
        a_s = ascr[pl.ds(Hs - k * s, R), :]
        b_s = bscr[pl.ds(Hs - k * s, R), :]
        b = a * b_s + b
        a = a * a_s
        k *= 2
    h = b
    bscr[Hs:Hs + R, :] = h
    hlast = bscr[pl.ds(Hs + R - s, s), :]
    hscr[...] = hlast
    hT_ref[...] = hlast

    y = _mm(h * _silu(gate), wout_ref[...])
    out = _layer_norm(DN_ALPHA * x + y, lng_ref[...], lnb_ref[...])
    if swap:
        out = jnp.swapaxes(out.reshape(tt, s, D_MODEL), 0, 1)
    out_ref[...] = out


def _lru_layer(x3, halo, h0, win, cw, cb, wg, bg, lam, wout, lng, lnb, *, s, R, swap):
    C = LRU_W
    D = x3.shape[-1]
    H = halo.shape[1]
    tt = R // s
    Hs = max(SUBLANES, (tt // 2) * s)
    if swap:
        NB, TT = 1, x3.shape[1] * s
        xspec = pl.BlockSpec((s, tt, D), lambda i, c: (0, c, 0))
    else:
        NB, TT = x3.shape[:2]
        xspec = pl.BlockSpec((None, R, D), lambda i, c: (i, c, 0))
    kern = functools.partial(_lru_kernel, s=s, R=R, H=H, Hs=Hs, swap=swap)
    return pl.pallas_call(
        kern,
        grid=(NB, TT // R),
        in_specs=[
            xspec,
            pl.BlockSpec((None, H, C), lambda i, c: (i, 0, 0)),
            pl.BlockSpec((None, s, C), lambda i, c: (i, 0, 0)),
            _const_spec(win.shape), _const_spec(cw.shape), _const_spec(cb.shape), _const_spec(wg.shape),
            _const_spec(bg.shape), _const_spec(lam.shape), _const_spec(wout.shape), _const_spec(lng.shape),
            _const_spec(lnb.shape),
        ],
        out_specs=[
            xspec,
            pl.BlockSpec((None, H, C), lambda i, c: (i, 0, 0)),
            pl.BlockSpec((None, s, C), lambda i, c: (i, 0, 0)),
        ],
        out_shape=[
            jax.ShapeDtypeStruct(x3.shape, F32),
            jax.ShapeDtypeStruct((NB, H, C), F32),
            jax.ShapeDtypeStruct((NB, s, C), F32),
        ],
        scratch_shapes=[
            pltpu.VMEM((H + R, C), F32),
            pltpu.VMEM((Hs + R, C), F32),
            pltpu.VMEM((Hs + R, C), F32),
            pltpu.VMEM((s, C), F32),
        ],
        compiler_params=_cparams(2),
        name="lru_layer",
    )(x3, halo, h0, win, cw, cb, wg, bg, lam, wout, lng, lnb)


def _lru_gate_weights(wa, wx):
    def bd(w):
        w4 = w.reshape(4, 4, LRU_BS, LRU_BS)
        eye = jnp.eye(4, dtype=w.dtype)
        return jnp.einsum("gaij,ab->gaibj", w4, eye).reshape(4, 256, 256)
    return jnp.concatenate([bd(wa), bd(wx)], axis=2).astype(BF)


def _gated_out_ln(o, g_ref, x_ref, w_ref, lng_ref, lnb_ref, out_ref):
    bb, nh, Rt, _ = g_ref.shape
    rows = bb * Rt
    g = jnp.concatenate([g_ref[:, p, :, :].reshape(rows, LANES) for p in range(nh)], axis=1)
    y = _mm(o * _silu(g), w_ref[...])
    x = x_ref[...].reshape(rows, D_MODEL)
    out_ref[...] = _layer_norm(DN_ALPHA * x + y, lng_ref[...], lnb_ref[...]).reshape(bb, Rt, D_MODEL)


def _post_kernel(x_ref, o_ref, g_ref, w_ref, lng_ref, lnb_ref, out_ref):
    o = jnp.concatenate([o_ref[0, p] for p in range(o_ref.shape[1])], axis=1)
    _gated_out_ln(o, g_ref, x_ref, w_ref, lng_ref, lnb_ref, out_ref)


def _post_layer(x3, o4, g4, w, lng, lnb):
    return pl.pallas_call(
        _post_kernel,
        out_shape=jax.ShapeDtypeStruct(x3.shape, F32),
        compiler_params=pltpu.CompilerParams(vmem_limit_bytes=VMEM_LIMIT),
        name="post_layer",
    )(x3, o4, g4, w, lng, lnb)


def _rwkv_pre_kernel(x_ref, halo_ref, mu_ref, win_ref, w0_ref, w1_ref, w2_ref, a0_ref, a1_ref, a2_ref,
                     r_ref, k_ref, v_ref, g_ref, lw_ref, a_ref, xscr, xxscr, *, s, R, H):
    c = pl.program_id(1)

    @pl.when(c == 0)
    def _init():
        xscr[0:H, :] = halo_ref[...]

    x = x_ref[...]
    xscr[H:H + R, :] = x
    xxscr[...] = xscr[pl.ds(H - s, R), :] - x
    xscr[0:H, :] = xscr[pl.ds(R, H), :]
    xx = xxscr[...]
    xm = lambda n: x + xx * mu_ref[n]

    def put(ref, val):
        for p in range(RW_W // LANES):
            ref[p] = val[:, LANES * p:LANES * (p + 1)]

    put(r_ref, _mm(xm(0), win_ref[0]))
    put(k_ref, _mm(xm(1), win_ref[1]))
    put(v_ref, _mm(xm(2), win_ref[2]))
    put(g_ref, _mm(xm(3), win_ref[3]))
    w_raw = w0_ref[...] + _mm(jnp.tanh(_mm(xm(4), w1_ref[...])), w2_ref[...])
    put(lw_ref, (-math.exp(-0.5)) * jax.nn.sigmoid(w_raw))
    put(a_ref, jax.nn.sigmoid(a0_ref[...] + _mm(_mm(xm(5), a1_ref[...]), a2_ref[...])))


def _rwkv_pre(x3, halo, mu, win, w0, w1, w2, a0, a1, a2, *, s, R):
    NB, TT, D = x3.shape
    H = halo.shape[1]
    nh = RW_W // LANES
    kern = functools.partial(_rwkv_pre_kernel, s=s, R=R, H=H)
    ospec = pl.BlockSpec((None, nh, R, LANES), lambda i, c: (i, 0, c, 0))
    oshape = jax.ShapeDtypeStruct((NB, nh, TT, LANES), F32)
    return pl.pallas_call(
        kern,
        grid=(NB, TT // R),
        in_specs=[
            pl.BlockSpec((None, R, D), lambda i, c: (i, c, 0)),
            pl.BlockSpec((None, H, D), lambda i, c: (i, 0, 0)),
            _const_spec(mu.shape), _const_spec(win.shape), _const_spec(w0.shape), _const_spec(w1.shape),
            _const_spec(w2.shape), _const_spec(a0.shape), _const_spec(a1.shape), _const_spec(a2.shape),
        ],
        out_specs=[ospec] * 6,
        out_shape=[oshape] * 6,
        scratch_shapes=[pltpu.VMEM((H + R, D), F32), pltpu.VMEM((R, D), F32)],
        compiler_params=_cparams(2),
        name="rwkv_pre",
    )(x3, halo, mu, win, w0, w1, w2, a0, a1, a2)


def _seg_sum(x):
    lane = lax.broadcasted_iota(jnp.int32, x.shape, 1)
    lo = lane < RW_N
    s0 = jnp.sum(jnp.where(lo, x, 0.0), axis=-1, keepdims=True)
    s1 = jnp.sum(jnp.where(lo, 0.0, x), axis=-1, keepdims=True)
    return jnp.where(lo, s0, s1)


def _stack2(x):
    lane = lax.broadcasted_iota(jnp.int32, x.shape, 1)
    lo = lane < x.shape[1] // 2
    return jnp.concatenate([jnp.where(lo, x, 0.0), jnp.where(lo, 0.0, x)], axis=0)


def _each(f, *lists):
    return [f(*t) for t in zip(*lists)]


def _packed_masks(C):
    t = lax.broadcasted_iota(jnp.int32, (C, 2 * C), 0)
    s = lax.broadcasted_iota(jnp.int32, (C, 2 * C), 1) % C
    return s < t, s <= t, s == t


def _unit_lower_inverse(Ls, eye):
    n = Ls[0].shape[0]
    invs = [jnp.where(eye, 1.0, L) for L in Ls]
    Lps = Ls
    span = 2
    while span < n:
        Lps = _each(lambda Lp: _dot(Lp, _stack2(Lp)), Lps)
        yield
        invs = _each(lambda inv, Lp: inv + _dot(inv, _stack2(Lp)), invs, Lps)
        yield
        span *= 2
    return invs


def _run_pipelined(part_a, part_b, batches):
    ctx = {}
    prev = None
    for batch in batches:
        gens = [part_a(batch, ctx)] + ([part_b(prev, ctx)] if prev is not None else [])
        while gens:
            for g in list(gens):
                try:
                    next(g)
                except StopIteration:
                    gens.remove(g)
        prev = batch
    for _ in part_b(prev, ctx):
        pass


def _rec_batches(bb, nchunk, gb):
    if nchunk > 1:
        assert bb == 1
        return tuple(tuple((0, c) for c in range(c0, min(c0 + 2, nchunk))) for c0 in range(0, nchunk, 2))
    return tuple(tuple((b, 0) for b in range(b0, b0 + gb)) for b0 in range(0, bb, gb))


def _rwkv_rec_kernel(r_ref, k_ref, v_ref, lw_ref, a_ref, g_ref, x_ref, s0_ref, kk_ref, ka_ref, rk_ref, gg_ref,
                     gb_ref, wout_ref, lng_ref, lnb_ref, out_ref, sT_ref, sscr, oscr, *, C, batches):
    tb = pl.program_id(1)
    bb, npair, Rt, _ = r_ref.shape
    C2 = 2 * C
    lane_sq = lax.broadcasted_iota(jnp.int32, (LANES, LANES), 1)
    row_sq = lax.broadcasted_iota(jnp.int32, (LANES, LANES), 0)
    same_head = (lane_sq < RW_N) == (row_sq < RW_N)

    @pl.when(tb == 0)
    def _init():
        def init_b(b, carry):
            for p in range(npair):
                s2 = s0_ref[b, p]
                sscr[b * npair + p] = jnp.where(same_head, jnp.concatenate([s2, s2], axis=1), 0.0)
            return carry
        lax.fori_loop(0, bb, init_b, 0)

    strict, incl, eye = _packed_masks(C)
    tri = (lax.broadcasted_iota(jnp.int32, (C, C), 1) <= lax.broadcasted_iota(jnp.int32, (C, C), 0)).astype(F32)

    def part_a(items, ctx):
        CH = [(b, p, ci) for b, ci in items for p in range(npair)]
        rs = lambda ci: pl.ds(ci * C, C)
        r = [r_ref[b, p, rs(ci), :] for b, p, ci in CH]
        k = [k_ref[b, p, rs(ci), :] for b, p, ci in CH]
        v = [v_ref[b, p, rs(ci), :] for b, p, ci in CH]
        lw = [lw_ref[b, p, rs(ci), :] for b, p, ci in CH]
        a = [a_ref[b, p, rs(ci), :] for b, p, ci in CH]
        kn = [k_ * kk_ref[p] for k_, (b, p, ci) in zip(k, CH)]
        kk = _each(lambda z: z * lax.rsqrt(_seg_sum(z * z) + RW_NORM_EPS), kn)
        kh = [k_ * (1.0 + (a_ - 1.0) * ka_ref[p]) for k_, a_, (b, p, ci) in zip(k, a, CH)]
        bvec = _each(lambda x, y: x * y, kk, a)
        cum = _each(lambda x: _dot_exact_lhs(tri, x), lw)
        yield
        cum_last = [c_[C - 1:C, :] for c_ in cum]
        e_neg = _each(lambda c_: jnp.exp(-c_), cum)
        e_dec = _each(lambda cl, c_: jnp.exp(cl - c_), cum_last, cum)
        At = _each(lambda kk_, c_, lw_: -kk_ * jnp.exp(c_ - lw_), kk, cum, lw)
        Rt_ = _each(lambda r_, c_: r_ * jnp.exp(c_), r, cum)
        AR = _each(lambda x, y: jnp.concatenate([x, y], axis=0), At, Rt_)
        BKs = _each(lambda b_, kh_, e: jnp.concatenate([_stack2(b_ * e), _stack2(kh_ * e)], axis=0), bvec, kh, e_neg)
        V2 = _each(_stack2, v)
        sc = _each(lambda x, y: _dot(x, y, _NT), AR, BKs)
        yield
        Lab = [jnp.where(strict, z[:C, :C2], 0.0) for z in sc]
        Lak = [jnp.where(strict, z[:C, C2:], 0.0) for z in sc]
        Mrbk = [jnp.concatenate([jnp.where(incl, z[C:, :C2], 0.0), jnp.where(incl, z[C:, C2:], 0.0)], axis=1)
                for z in sc]
        LV = _each(_dot, Lak, V2)
        Tinv = yield from _unit_lower_inverse(Lab, eye)
        for n, key in enumerate(CH):
            ctx[key] = dict(AR=AR[n], V2=V2[n], LV=LV[n], Tinv=Tinv[n], Mrbk=Mrbk[n], v=v[n], r=r[n], kh=kh[n],
                            sdec=jnp.exp(cum_last[n]), bkdec=jnp.concatenate([bvec[n] * e_dec[n], kh[n] * e_dec[n]], 0))

    def part_b(items, ctx):
        for ci in sorted({ci for _, ci in items}):
            CH = [(b, p, ci) for b, c_ in items if c_ == ci for p in range(npair)]
            X = [ctx.pop(key) for key in CH]
            S = [sscr[b * npair + p] for b, p, _ in CH]
            UY0 = _each(lambda x, s_: _dot(x["AR"], s_, _NT), X, S)
            yield
            U = _each(lambda x, u: _dot(x["Tinv"], _stack2(u[:C, :] + x["LV"])), X, UY0)
            yield
            y = _each(lambda x, u0, u: u0[C:, :] + _dot(x["Mrbk"], jnp.concatenate([_stack2(u), x["V2"]], axis=0)),
                      X, UY0, U)
            Snew = _each(lambda x, s_, u: s_ * x["sdec"] + jnp.where(
                same_head, _dot(jnp.concatenate([u, x["v"]], axis=0), x["bkdec"], _TN), 0.0), X, S, U)
            yield
            for n, (b, p, _) in enumerate(CH):
                sscr[b * npair + p] = Snew[n]
                m = _seg_sum(y[n]) * (1.0 / RW_N)
                yc = y[n] - m
                var = _seg_sum(yc * yc) * (1.0 / RW_N)
                yn = yc * lax.rsqrt(var + RW_GN_EPS) * gg_ref[p] + gb_ref[p]
                bonus = _seg_sum(X[n]["r"] * X[n]["kh"] * rk_ref[p]) * X[n]["v"]
                oscr[b * Rt + ci * C:b * Rt + (ci + 1) * C, LANES * p:LANES * (p + 1)] = yn + bonus
            yield

    _run_pipelined(part_a, part_b, batches)

    row_h = lax.broadcasted_iota(jnp.int32, (LANES, RW_N), 0) < RW_N

    def fin_b(b, carry):
        for p in range(npair):
            S = sscr[b * npair + p]
            sT_ref[b, p] = jnp.where(row_h, S[:, :RW_N], S[:, RW_N:])
        return carry
    lax.fori_loop(0, bb, fin_b, 0)
    _gated_out_ln(oscr[...], g_ref, x_ref, wout_ref, lng_ref, lnb_ref, out_ref)


def _rwkv_rec(r4, k4, v4, lw4, a4, g4, x, s0, k_k, k_a, r_k, gn_g, gn_b, wout, lng, lnb, *, C, Rt, bb, gb):
    B, npair, T, _ = r4.shape
    D = x.shape[-1]
    kern = functools.partial(_rwkv_rec_kernel, C=C, batches=_rec_batches(bb, Rt // C, gb))
    tspec = pl.BlockSpec((bb, npair, Rt, LANES), lambda b, t: (b, 0, t, 0))
    xspec = pl.BlockSpec((bb, Rt, D), lambda b, t: (b, t, 0))
    sspec = pl.BlockSpec((bb, npair, LANES, RW_N), lambda b, t: (b, 0, 0, 0))
    wspec = _const_spec((npair, 1, LANES))
    return pl.pallas_call(
        kern,
        grid=(B // bb, T // Rt),
        in_specs=[tspec] * 6 + [xspec, sspec] + [wspec] * 5 + [_const_spec(wout.shape), _const_spec(lng.shape),
                                                              _const_spec(lnb.shape)],
        out_specs=[xspec, sspec],
        out_shape=[jax.ShapeDtypeStruct(x.shape, F32), jax.ShapeDtypeStruct(s0.shape, F32)],
        scratch_shapes=[pltpu.VMEM((bb * npair, LANES, LANES), F32), pltpu.VMEM((bb * Rt, RW_W), F32)],
        compiler_params=_cparams(2),
        name="rwkv_rec",
    )(r4, k4, v4, lw4, a4, g4, x, s0, k_k, k_a, r_k, gn_g, gn_b, wout, lng, lnb)


def _rwkv_step_kernel(r_ref, k_ref, v_ref, lw_ref, a_ref, s0_ref, kk_ref, ka_ref, rk_ref, gg_ref, gb_ref,
                      o_ref, sT_ref, vscr, yscr, *, T, B):
    N = RW_N
    IB = SUBLANES
    for t in range(T):
        rows = pl.ds(t * B, B)
        rT, kT, vT, aT = r_ref[rows, :].T, k_ref[rows, :].T, v_ref[rows, :].T, a_ref[rows, :].T
        w = jnp.exp(lw_ref[rows, :].T)
        kn = kT * kk_ref[...]
        kh = kT * (1.0 + (aT - 1.0) * ka_ref[...])
        vscr[...] = vT
        src = s0_ref if t == 0 else sT_ref
        bonus = []
        for h in range(2):
            hs = slice(N * h, N * (h + 1))
            kk = kn[hs] * lax.rsqrt(jnp.sum(kn[hs] * kn[hs], axis=0, keepdims=True) + RW_NORM_EPS)
            a_h, b_h, k_h, w_h, r_h = -kk, kk * aT[hs], kh[hs], w[hs], rT[hs]

            def step(ib, carry, h=h, a_h=a_h, b_h=b_h, k_h=k_h, w_h=w_h, r_h=r_h, src=src):
                i0 = pl.multiple_of(ib * IB, IB)
                S = src[h, pl.ds(i0, IB), :, :]
                sa = jnp.sum(S * a_h[None], axis=1)
                vb = vscr[pl.ds(N * h + i0, IB), :]
                Sn = S * w_h[None] + sa[:, None, :] * b_h[None] + vb[:, None, :] * k_h[None]
                sT_ref[h, pl.ds(i0, IB), :, :] = Sn
                yscr[pl.ds(N * h + i0, IB), :] = jnp.sum(Sn * r_h[None], axis=1)
                return carry

            lax.fori_loop(0, N // IB, step, 0)
            bonus.append(jnp.sum(r_h * k_h * rk_ref[hs, :], axis=0, keepdims=True) * vT[hs])
        y = yscr[...]
        outs = []
        for h in range(2):
            hs = slice(N * h, N * (h + 1))
            m = jnp.mean(y[hs], axis=0, keepdims=True)
            yc = y[hs] - m
            var = jnp.mean(yc * yc, axis=0, keepdims=True)
            outs.append(yc * lax.rsqrt(var + RW_GN_EPS) * gg_ref[hs, :] + gb_ref[hs, :] + bonus[h])
        o_ref[rows, :] = jnp.concatenate(outs, axis=0).T


def _rwkv_step(r4, k4, v4, lw4, a4, s0, k_k, k_a, r_k, gn_g, gn_b, *, T, B):
    npair = r4.shape[1]
    kern = functools.partial(_rwkv_step_kernel, T=T, B=B)
    tspec = pl.BlockSpec((None, None, T * B, LANES), lambda p: (0, p, 0, 0))
    sspec = pl.BlockSpec((None, 2, RW_N, RW_N, B), lambda p: (p, 0, 0, 0, 0))
    wspec = pl.BlockSpec((None, LANES, 1), lambda p: (p, 0, 0))
    return pl.pallas_call(
        kern,
        grid=(npair,),
        in_specs=[tspec] * 5 + [sspec] + [wspec] * 5,
        out_specs=[tspec, sspec],
        out_shape=[jax.ShapeDtypeStruct(r4.shape, F32), jax.ShapeDtypeStruct(s0.shape, F32)],
        scratch_shapes=[pltpu.VMEM((LANES, B), F32), pltpu.VMEM((LANES, B), F32)],
        compiler_params=_cparams(1),
        name="rwkv_step",
    )(r4, k4, v4, lw4, a4, s0, k_k, k_a, r_k, gn_g, gn_b)


def _gdn_pre_kernel(x_ref, halo_ref, win_ref, cw_ref, alog_ref, dtb_ref,
                    q_ref, k_ref, v_ref, z_ref, bg_ref, tail_ref, xscr, gscr, *, s, R, H, Hg, C):
    c = pl.program_id(1)
    CH = GDN_CONV_CH
    tt = R // s

    @pl.when(c == 0)
    def _init():
        xscr[0:H, :] = halo_ref[...]
        gscr[0:Hg, :] = jnp.zeros((Hg, LANES), F32)

    x = x_ref[...]
    u = _mm(x, win_ref[...])
    xb = u[:, :CH]
    xscr[H:H + R, :] = xb
    y = xb * cw_ref[3]
    for d in (1, 2, 3):
        y = y + xscr[pl.ds(H - d * s, R), :] * cw_ref[3 - d]
    tail = xscr[pl.ds(R, H), :]
    xscr[0:H, :] = tail
    tail_ref[...] = tail
    qkv = _silu(y)

    def l2n(z, scale):
        return z * (lax.rsqrt(jnp.sum(z * z, axis=-1, keepdims=True) + GDN_EPS) * scale)

    for h in range(GDN_HK):
        q_ref[h] = l2n(qkv[:, LANES * h:LANES * (h + 1)], GDN_DK ** -0.5)
        k_ref[h] = l2n(qkv[:, GDN_KEY_W + LANES * h:GDN_KEY_W + LANES * (h + 1)], 1.0)
    for h in range(GDN_HV):
        v_ref[h] = qkv[:, 2 * GDN_KEY_W + LANES * h:2 * GDN_KEY_W + LANES * (h + 1)]
        z_ref[h] = u[:, CH + LANES * h:CH + LANES * (h + 1)]

    bg = u[:, CH + GDN_VAL_W:CH + GDN_VAL_W + LANES]
    beta = jax.nn.sigmoid(bg)
    g = -jnp.exp(alog_ref[...]) * jax.nn.softplus(bg + dtb_ref[...])
    t_in = (lax.broadcasted_iota(jnp.int32, (R, 1), 0) // s) % C
    k = 1
    while k < min(C, tt):
        gscr[Hg:Hg + R, :] = g
        g = g + jnp.where(t_in >= k, gscr[pl.ds(Hg - k * s, R), :], 0.0)
        k *= 2
    lane = lax.broadcasted_iota(jnp.int32, (R, LANES), 1)
    bg_ref[...] = jnp.where(lane < GDN_HV, beta, g)


def _gdn_pre(x3, halo, win, cw, alog, dtb, *, s, R, C):
    NB, TT, D = x3.shape
    H = halo.shape[1]
    tt = R // s
    Hg = max(SUBLANES, (min(C, tt) // 2) * s)
    kern = functools.partial(_gdn_pre_kernel, s=s, R=R, H=H, Hg=Hg, C=C)

    def ospec(nh):
        return pl.BlockSpec((None, nh, R, LANES), lambda i, c: (i, 0, c, 0))

    def oshape(nh):
        return jax.ShapeDtypeStruct((NB, nh, TT, LANES), F32)

    return pl.pallas_call(
        kern,
        grid=(NB, TT // R),
        in_specs=[
            pl.BlockSpec((None, R, D), lambda i, c: (i, c, 0)),
            pl.BlockSpec((None, H, GDN_CONV_CH), lambda i, c: (i, 0, 0)),
            _const_spec(win.shape), _const_spec(cw.shape), _const_spec(alog.shape), _const_spec(dtb.shape),
        ],
        out_specs=[ospec(GDN_HK), ospec(GDN_HK), ospec(GDN_HV), ospec(GDN_HV),
                   pl.BlockSpec((None, R, LANES), lambda i, c: (i, c, 0)),
                   pl.BlockSpec((None, H, GDN_CONV_CH), lambda i, c: (i, 0, 0))],
        out_shape=[oshape(GDN_HK), oshape(GDN_HK), oshape(GDN_HV), oshape(GDN_HV),
                   jax.ShapeDtypeStruct((NB, TT, LANES), F32),
                   jax.ShapeDtypeStruct((NB, H, GDN_CONV_CH), F32)],
        scratch_shapes=[pltpu.VMEM((H + R, GDN_CONV_CH), F32), pltpu.VMEM((Hg + R, LANES), F32)],
        compiler_params=_cparams(2),
        name="gdn_pre",
    )(x3, halo, win, cw, alog, dtb)


def _gdn_rec_kernel(q_ref, k_ref, v_ref, z_ref, bg_ref, x_ref, s0_ref, ng_ref, wout_ref, lng_ref, lnb_ref,
                    out_ref, sT_ref, sscr, oscr, *, C, batches):
    tb = pl.program_id(1)
    bb, nhv, Rt, _ = v_ref.shape
    rep = GDN_HV // GDN_HK

    @pl.when(tb == 0)
    def _init():
        sscr[...] = s0_ref[...].reshape(sscr.shape)

    strict, incl, eye = _packed_masks(C)
    lo = lax.broadcasted_iota(jnp.int32, (C, 2 * C), 1) < C

    def part_a(items, ctx):
        CH = [(b, h, ci) for b, ci in items for h in range(nhv)]
        KH = [(b, m, ci) for b, ci in items for m in range(GDN_HK)]
        rs = lambda ci: pl.ds(ci * C, C)
        bgs = {(b, ci): bg_ref[b, rs(ci), :] for b, ci in items}
        bgT = {key: x.T for key, x in bgs.items()}

        kq = [jnp.concatenate([k_ref[b, m, rs(ci), :], q_ref[b, m, rs(ci), :]], axis=0) for b, m, ci in KH]
        sc = _each(lambda x: _dot(x, jnp.concatenate([x[:C, :], x[:C, :]], axis=0), _NT), kq)
        yield

        def col2(x, lane):
            return jnp.where(lo, jnp.broadcast_to(x[:, lane:lane + 1], (C, 2 * C)),
                             jnp.broadcast_to(x[:, lane + 1:lane + 2], (C, 2 * C)))

        bcol = [col2(bgs[b, ci], rep * m) for b, m, ci in KH]
        gcol = [col2(bgs[b, ci], GDN_HV + rep * m) for b, m, ci in KH]
        grow = [jnp.concatenate([bgT[b, ci][GDN_HV + rep * m:GDN_HV + rep * m + 1, :],
                                 bgT[b, ci][GDN_HV + rep * m + 1:GDN_HV + rep * m + 2, :]], axis=1) for b, m, ci in KH]
        diff = _each(lambda c_, r_: c_ - r_, gcol, grow)
        Lp = _each(lambda z, bc, d: jnp.where(strict, z[:C, :] * bc * jnp.exp(jnp.where(strict, d, 0.0)), 0.0),
                   sc, bcol, diff)
        Ap = _each(lambda z, d: jnp.where(incl, z[C:, :] * jnp.exp(jnp.where(incl, d, 0.0)), 0.0), sc, diff)
        Tinv_p = yield from _unit_lower_inverse([-l_ for l_ in Lp], eye)

        half = lambda xs, n, h: xs[n // rep][:, (h % rep) * C:(h % rep + 1) * C]
        Tinv = [half(Tinv_p, n, h) for n, (b, h, ci) in enumerate(CH)]
        A = [half(Ap, n, h) for n, (b, h, ci) in enumerate(CH)]
        k = [kq[n // rep][:C, :] for n in range(len(CH))]
        q = [kq[n // rep][C:, :] for n in range(len(CH))]
        v = [v_ref[b, h, rs(ci), :] for b, h, ci in CH]
        beta = [jnp.broadcast_to(bgs[b, ci][:, h:h + 1], (C, LANES)) for b, h, ci in CH]
        gc = [jnp.broadcast_to(bgs[b, ci][:, GDN_HV + h:GDN_HV + h + 1], (C, LANES)) for b, h, ci in CH]
        kb = _each(lambda x, y: x * y, k, beta)
        eg = _each(jnp.exp, gc)
        UW = _each(lambda t, v_, b_, kb_, e: _dot(t, jnp.concatenate([v_ * b_, kb_ * e], axis=1)),
                   Tinv, v, beta, kb, eg)
        yield
        for n, key in enumerate(CH):
            g_last = gc[n][C - 1:C, :]
            ctx[key] = dict(U=UW[n][:, :GDN_DV], WQl=jnp.concatenate([UW[n][:, GDN_DV:], q[n] * eg[n]], axis=0),
                            A=A[n], kdec=k[n] * jnp.exp(g_last - gc[n]), sdec=jnp.exp(g_last))

    def part_b(items, ctx):
        for ci in sorted({ci for _, ci in items}):
            CH = [(b, h, ci) for b, c_ in items if c_ == ci for h in range(nhv)]
            X = [ctx.pop(key) for key in CH]
            S = [sscr[b * nhv + h] for b, h, _ in CH]
            WQ = _each(lambda x, s_: _dot(x["WQl"], s_), X, S)
            yield
            v_new = _each(lambda x, wq: x["U"] - wq[:C, :], X, WQ)
            o = _each(lambda x, wq, vn: wq[C:, :] + _dot(x["A"], vn), X, WQ, v_new)
            Snew = _each(lambda x, s_, vn: s_ * x["sdec"] + _dot(x["kdec"], vn, _TN), X, S, v_new)
            yield
            for n, (b, h, _) in enumerate(CH):
                sscr[b * nhv + h] = Snew[n]
                oh = o[n]
                oscr[b * Rt + ci * C:b * Rt + (ci + 1) * C, LANES * h:LANES * (h + 1)] = (
                    oh * lax.rsqrt(jnp.mean(oh * oh, axis=-1, keepdims=True) + GDN_EPS) * ng_ref[...])
            yield

    _run_pipelined(part_a, part_b, batches)
    sT_ref[...] = sscr[...].reshape(sT_ref.shape)
    _gated_out_ln(oscr[...], z_ref, x_ref, wout_ref, lng_ref, lnb_ref, out_ref)


def _gdn_rec(q4, k4, v4, z4, bg, x, s0, norm_g, wout, lng, lnb, *, C, Rt, bb, gb):
    B, _, T, _ = v4.shape
    D = x.shape[-1]
    kern = functools.partial(_gdn_rec_kernel, C=C, batches=_rec_batches(bb, Rt // C, gb))

    def tspec(nh):
        return pl.BlockSpec((bb, nh, Rt, LANES), lambda b, t: (b, 0, t, 0))

    xspec = pl.BlockSpec((bb, Rt, D), lambda b, t: (b, t, 0))
    sspec = pl.BlockSpec((bb, GDN_HV, GDN_DK, GDN_DV), lambda b, t: (b, 0, 0, 0))
    return pl.pallas_call(
        kern,
        grid=(B // bb, T // Rt),
        in_specs=[tspec(GDN_HK), tspec(GDN_HK), tspec(GDN_HV), tspec(GDN_HV),
                  pl.BlockSpec((bb, Rt, LANES), lambda b, t: (b, t, 0)), xspec, sspec,
                  _const_spec(norm_g.shape), _const_spec(wout.shape), _const_spec(lng.shape), _const_spec(lnb.shape)],
        out_specs=[xspec, sspec],
        out_shape=[jax.ShapeDtypeStruct(x.shape, F32), jax.ShapeDtypeStruct(s0.shape, F32)],
        scratch_shapes=[pltpu.VMEM((bb * GDN_HV, GDN_DK, GDN_DV), F32), pltpu.VMEM((bb * Rt, GDN_VAL_W), F32)],
        compiler_params=_cparams(2),
        name="gdn_rec",
    )(q4, k4, v4, z4, bg, x, s0, norm_g, wout, lng, lnb)


class _Group:
    def __init__(self, B, T, time_major):
        self.B, self.T, self.time_major = B, T, time_major
        if time_major:
            self.s, self.NB, self.TT, self.R = B, 1, T * B, T * B
        else:
            self.s, self.NB, self.TT, self.R = 1, B, T, min(ROW_BLOCK, T)

    def to_rows(self, x):
        if self.time_major:
            return jnp.swapaxes(x, 0, 1).reshape(1, self.TT, x.shape[-1])
        return x

    def from_rows(self, x3):
        if self.time_major:
            return jnp.swapaxes(x3.reshape(self.T, self.B, x3.shape[-1]), 0, 1)
        return x3

    def halo(self, st, n_steps):
        if self.time_major:
            return jnp.swapaxes(st, 0, 1).reshape(1, n_steps * self.B, st.shape[-1])
        H = _halo_rows(n_steps, 1)
        return jnp.pad(st, ((0, 0), (H - n_steps, 0), (0, 0)))

    def unhalo(self, tail, n_steps):
        if self.time_major:
            return jnp.swapaxes(tail.reshape(n_steps, self.B, tail.shape[-1]), 0, 1)
        return tail[:, tail.shape[1] - n_steps:]

    def vec(self, st):
        return st[None] if self.time_major else st[:, None, :]

    def unvec(self, v):
        return v[0] if self.time_major else v[:, 0, :]

    def heads_to_batch(self, a4, Tpad):
        if not self.time_major:
            return a4
        nh = a4.shape[1]
        a = a4.reshape(nh, self.T, self.B, LANES).transpose(2, 0, 1, 3)
        return jnp.pad(a, ((0, 0), (0, 0), (0, Tpad - self.T), (0, 0)))

    def rows_to_batch(self, a3, Tpad):
        if not self.time_major:
            return a3
        return jnp.pad(self.from_rows(a3), ((0, 0), (0, Tpad - self.T), (0, 0)))

    def rows_from_batch(self, a3):
        if not self.time_major:
            return a3
        return self.to_rows(a3[:, :self.T])

    def rec_tiling(self, chunk):
        if self.time_major:
            Tp = -(-self.T // SUBLANES) * SUBLANES
            return Tp, Tp, Tp, min(self.B, SAMPLE_SEQ_BLOCK), min(self.B, SAMPLE_SEQ_GROUP)
        return min(chunk, self.T), self.T, min(ROW_BLOCK, self.T), 1, 1


def _lru_apply(g, x3, conv_st, h_st, p, lng, lnb):
    swap = not g.time_major and g.B % SUBLANES == 0
    lay = _Group(g.B, g.T, True) if swap else g
    out, tail, hT = _lru_layer(x3, lay.halo(conv_st, CONV_W - 1), lay.vec(h_st), p["win"], p["cw"], p["cb"],
                               p["wg"], p["bg"], p["lam"], p["wout"], lng, lnb, s=lay.s,
                               R=min(ROW_BLOCK, g.T * g.B) if swap else g.R, swap=swap)
    return out, lay.unhalo(tail, CONV_W - 1), lay.unvec(hT)


def _rwkv_apply(g, x3, shift_st, wkv_st, p, lng, lnb):
    B = g.B
    pre = _rwkv_pre(x3, g.halo(shift_st[:, None, :], 1), p["mu"], p["win"], p["w0"], p["w1"], p["w2"], p["a0"],
                    p["a1"], p["a2"], s=g.s, R=g.R)
    new_shift = g.from_rows(x3)[:, -1]
    if g.time_major:
        r4, k4, v4, g4, lw4, a4 = pre
        s0 = jnp.transpose(wkv_st, (1, 2, 3, 0)).reshape(RW_H // 2, 2, RW_N, RW_N, B)
        col = lambda v: jnp.swapaxes(v, 1, 2)
        o4, sT = _rwkv_step(r4, k4, v4, lw4, a4, s0, col(p["k_k"]), col(p["k_a"]), col(p["r_k"]), col(p["gn_g"]),
                            col(p["gn_b"]), T=g.T, B=B)
        out = _post_layer(x3, o4, g4, p["wout"], lng, lnb)
        return out, new_shift, jnp.transpose(sT.reshape(RW_H, RW_N, RW_N, B), (3, 0, 1, 2))
    C, Tp, Rt, bb, gb = g.rec_tiling(RW_CHUNK)
    r4, k4, v4, g4, lw4, a4 = [g.heads_to_batch(a, Tp) for a in pre]
    s0 = wkv_st.reshape(B, RW_H // 2, 2 * RW_N, RW_N)
    out, sT = _rwkv_rec(r4, k4, v4, lw4, a4, g4, g.rows_to_batch(x3, Tp), s0, p["k_k"], p["k_a"], p["r_k"],
                        p["gn_g"], p["gn_b"], p["wout"], lng, lnb, C=C, Rt=Rt, bb=bb, gb=gb)
    new_shift = g.from_rows(x3)[:, -1]
    return g.rows_from_batch(out), new_shift, sT.reshape(B, RW_H, RW_N, RW_N)


def _gdn_apply(g, x3, conv_st, S_st, p, lng, lnb):
    T = g.T
    C, Tp, Rt, bb, gb = g.rec_tiling(GDN_CHUNK)
    q4, k4, v4, z4, bg, tail = _gdn_pre(x3, g.halo(conv_st, CONV_W - 1), p["win"], p["cw"], p["alog"], p["dtb"],
                                        s=g.s, R=g.R, C=min(GDN_CHUNK, T))
    bgb = g.rows_to_batch(bg, T)
    if Tp != T:
        held = jnp.where(jnp.arange(LANES) < GDN_HV, 0.0, bgb[:, T - 1:T, :])
        bgb = jnp.concatenate([bgb, jnp.broadcast_to(held, (g.B, Tp - T, LANES))], axis=1)
    tb = lambda a: g.heads_to_batch(a, Tp)
    out, sT = _gdn_rec(tb(q4), tb(k4), tb(v4), tb(z4), bgb, g.rows_to_batch(x3, Tp), S_st, p["norm_g"], p["wout"],
                       lng, lnb, C=C, Rt=Rt, bb=bb, gb=gb)
    return g.rows_from_batch(out), g.unhalo(tail, CONV_W - 1), sT


def _trunk(g, x, st, params, ln_g, ln_b):
    lru_conv, lru_h, rw_shift, rw_S, gdn_conv, gdn_S = st
    new = ([], [], [], [], [], [])
    x3 = g.to_rows(x)
    ia = ib = ic = 0
    for layer in range(DEPTH):
        kind = layer % N_MIXERS
        lng, lnb = ln_g[layer][None, :], ln_b[layer][None, :]
        if kind == 0:
            x3, c, h = _lru_apply(g, x3, lru_conv[ia], lru_h[ia], params["lru"][ia], lng, lnb)
            new[0].append(c)
            new[1].append(h)
            ia += 1
        elif kind == 1:
            x3, sh, S = _rwkv_apply(g, x3, rw_shift[ib], rw_S[ib], params["rwkv"][ib], lng, lnb)
            new[2].append(sh)
            new[3].append(S)
            ib += 1
        else:
            x3, c, S = _gdn_apply(g, x3, gdn_conv[ic], gdn_S[ic], params["gdn"][ic], lng, lnb)
            new[4].append(c)
            new[5].append(S)
            ic += 1
    return g.from_rows(x3), tuple(s[0][None] if len(s) == 1 else jnp.stack(s) for s in new)


def _prep_params(lru_w_in, lru_conv_w, lru_conv_b, lru_wa, lru_ba, lru_wx, lru_bx, lru_lambda, lru_w_out, rw_mu,
                 rw_w_in, rw_w0, rw_w1, rw_w2, rw_a0, rw_a1, rw_a2, rw_k_k, rw_k_a, rw_r_k, rw_gn_g, rw_gn_b,
                 rw_w_out, gdn_w_in, gdn_conv_w, gdn_a_log, gdn_dt_bias, gdn_norm_g, gdn_w_out):
    row = lambda v: v[None, :]
    lru = []
    for n in range(lru_w_in.shape[0]):
        lru.append(dict(win=lru_w_in[n].astype(BF), cw=lru_conv_w[n][:, None, :], cb=row(lru_conv_b[n]),
                        wg=_lru_gate_weights(lru_wa[n], lru_wx[n]), bg=jnp.stack([lru_ba[n], lru_bx[n]])[:, None, :],
                        lam=row(lru_lambda[n]), wout=lru_w_out[n].astype(BF)))
    pairw = lambda v: v.reshape(RW_W // LANES, 1, LANES)
    rwkv = []
    for n in range(rw_w_in.shape[0]):
        rwkv.append(dict(mu=rw_mu[n][:, None, :],win=rw_w_in[n].astype(BF), w0=row(rw_w0[n]), w1=rw_w1[n].astype(BF),
                         w2=rw_w2[n].astype(BF), a0=row(rw_a0[n]), a1=rw_a1[n].astype(BF), a2=rw_a2[n].astype(BF),
                         k_k=pairw(rw_k_k[n]), k_a=pairw(rw_k_a[n]), r_k=pairw(rw_r_k[n]), gn_g=pairw(rw_gn_g[n]),
                         gn_b=pairw(rw_gn_b[n]), wout=rw_w_out[n].astype(BF)))
    gdn = []
    for n in range(gdn_w_in.shape[0]):
        w = gdn_w_in[n]
        o2 = GDN_CONV_CH + GDN_VAL_W
        wpad = jnp.pad(w[:, o2:], ((0, 0), (0, LANES - 2 * GDN_HV)))
        lanes = lambda v: jnp.pad(v, (GDN_HV, LANES - 2 * GDN_HV))[None, :]
        gdn.append(dict(win=jnp.concatenate([w[:, :o2], wpad], axis=1).astype(BF), cw=gdn_conv_w[n][:, None, :],
                        alog=lanes(gdn_a_log[n]), dtb=lanes(gdn_dt_bias[n]), norm_g=row(gdn_norm_g[n]),
                        wout=gdn_w_out[n].astype(BF)))
    return dict(lru=lru, rwkv=rwkv, gdn=gdn)


def kernel(x_prompt, x_sample, state_lru_conv, state_lru_h, state_rwkv_shift, state_rwkv_wkv, state_gdn_conv, state_gdn_S, ln_g, ln_b, lru_w_in, lru_conv_w, lru_conv_b, lru_wa, lru_ba, lru_wx, lru_bx, lru_lambda, lru_w_out, rw_mu, rw_w_in, rw_w0, rw_w1, rw_w2, rw_a0, rw_a1, rw_a2, rw_k_k, rw_k_a, rw_r_k, rw_gn_g, rw_gn_b, rw_w_out, gdn_w_in, gdn_conv_w, gdn_a_log, gdn_dt_bias, gdn_norm_g, gdn_w_out):
    params = _prep_params(lru_w_in, lru_conv_w, lru_conv_b, lru_wa, lru_ba, lru_wx, lru_bx, lru_lambda, lru_w_out,
                          rw_mu, rw_w_in, rw_w0, rw_w1, rw_w2, rw_a0, rw_a1, rw_a2, rw_k_k, rw_k_a, rw_r_k, rw_gn_g,
                          rw_gn_b, rw_w_out, gdn_w_in, gdn_conv_w, gdn_a_log, gdn_dt_bias, gdn_norm_g, gdn_w_out)
    bp, tp, _ = x_prompt.shape
    bs, ts, _ = x_sample.shape
    n_a, n_b, n_c = state_lru_conv.shape[0], state_rwkv_shift.shape[0], state_gdn_conv.shape[0]
    zero_state = (jnp.zeros((n_a, bp, CONV_W - 1, LRU_W), F32),
                  jnp.zeros((n_a, bp, LRU_W), F32),
                  jnp.zeros((n_b, bp, D_MODEL), F32),
                  jnp.zeros((n_b, bp, RW_H, RW_N, RW_N), F32),
                  jnp.zeros((n_c, bp, CONV_W - 1, GDN_CONV_CH), F32),
                  jnp.zeros((n_c, bp, GDN_HV, GDN_DK, GDN_DV), F32))
    y_prompt, sp = _trunk(_Group(bp, tp, False), x_prompt, zero_state, params, ln_g, ln_b)
    y_sample, ss = _trunk(_Group(bs, ts, True), x_sample,
                          (state_lru_conv, state_lru_h, state_rwkv_shift, state_rwkv_wkv, state_gdn_conv,
                           state_gdn_S), params, ln_g, ln_b)
    return (y_prompt, y_sample, sp[0], ss[0], sp[1], ss[1], sp[2], ss[2], sp[3], ss[3], sp[4], ss[4], sp[5], ss[5])
```

```python
import functools
import math

import jax
import jax.numpy as jnp
from jax import lax
from jax.experimental import pallas as pl
from jax.experimental.pallas import tpu as pltpu

F32 = jnp.float32
BF = jnp.bfloat16

D_MODEL = 1024
DEPTH = 4
N_MIXERS = 3
DN_ALPHA = (2.0 * DEPTH) ** 0.25
LN_EPS = 1e-5
CONV_W = 4

LRU_W = D_MODEL
LRU_BLOCKS = 16
LRU_BS = LRU_W // LRU_BLOCKS
LRU_C = 8.0

RW_W = D_MODEL
RW_N = 64
RW_H = RW_W // RW_N
RW_GN_EPS = 64e-5
RW_NORM_EPS = 1e-12

GDN_HK = 4
GDN_HV = 8
GDN_DK = 128
GDN_DV = 128
GDN_KEY_W = GDN_HK * GDN_DK
GDN_VAL_W = GDN_HV * GDN_DV
GDN_CONV_CH = 2 * GDN_KEY_W + GDN_VAL_W
GDN_CHUNK = 64
GDN_EPS = 1e-6

LANES = 128
SUBLANES = 8
VMEM_LIMIT = 56 * 1024 * 1024
ROW_BLOCK = 256
LRU_BLOCK = 512
RW_BLOCK = 256
GDN_BLOCK = 256
PROMPT_SEQ_BLOCK = 2
SAMPLE_SEQ_BLOCK = 8
SAMPLE_SEQ_GROUP = 4
RW_CHUNK = 64

_NN = (((1,), (0,)), ((), ()))
_NT = (((1,), (1,)), ((), ()))
_TN = (((0,), (0,)), ((), ()))


def _mm(a, b):
    return jnp.dot(a.astype(BF), b.astype(BF), preferred_element_type=F32)


def _dot(a, b, dims=_NN):
    return lax.dot_general(a.astype(BF), b.astype(BF), dims, preferred_element_type=F32)


def _dot_exact_lhs(a01, b):
    a = a01.astype(BF)
    b1 = b.astype(BF)
    r1 = b - b1.astype(F32)
    b2 = r1.astype(BF)
    b3 = (r1 - b2.astype(F32)).astype(BF)
    d = lambda q: jnp.dot(a, q, preferred_element_type=F32)
    return d(b1) + (d(b2) + d(b3))


def _layer_norm(z, g, b):
    mu = jnp.mean(z, axis=-1, keepdims=True)
    zc = z - mu
    var = jnp.mean(zc * zc, axis=-1, keepdims=True)
    return zc * lax.rsqrt(var + LN_EPS) * g + b


def _silu(x):
    return x * jax.nn.sigmoid(x)


def _cparams(n_axes):
    return pltpu.CompilerParams(dimension_semantics=("arbitrary",) * n_axes, vmem_limit_bytes=VMEM_LIMIT)


def _const_spec(shape):
    nd = len(shape)
    return pl.BlockSpec(shape, lambda *_: (0,) * nd)


def _halo_rows(n_steps, s):
    rows = n_steps * s
    return rows if rows % SUBLANES == 0 else SUBLANES


def _lru_kernel(x_ref, halo_ref, h0_ref, win_ref, cw_ref, cb_ref, wg_ref, bg_ref, lam_ref, wout_ref, lng_ref,
                lnb_ref, out_ref, tail_ref, hT_ref, xscr, hscr, *, s, R, H, swap):
    c = pl.program_id(1)
    C = LRU_W
    tt = R // s

    @pl.when(c == 0)
    def _init():
        xscr[0:H, :] = halo_ref[...]
        hscr[...] = h0_ref[...]

    x = x_ref[...]
    if swap:
        x = jnp.swapaxes(x, 0, 1).reshape(R, D_MODEL)
    u = _mm(x, win_ref[...])
    xb = u[:, :C]
    gate = u[:, C:]

    xscr[H:H + R, :] = xb
    xc = xb * cw_ref[3] + cb_ref[...]
    for d in (1, 2, 3):
        xc = xc + xscr[pl.ds(H - d * s, R), :] * cw_ref[3 - d]
    tail = xscr[pl.ds(R, H), :]
    xscr[0:H, :] = tail
    tail_ref[...] = tail

    xcb = xc.astype(BF)
    ra, ix = [], []
    for g in range(C // 256):
        gt = jnp.dot(xcb[:, 256 * g:256 * (g + 1)], wg_ref[g], preferred_element_type=F32)
        ra.append(gt[:, :256])
        ix.append(gt[:, 256:])
    r = jax.nn.sigmoid(jnp.concatenate(ra, axis=1) + bg_ref[0])
    i = jax.nn.sigmoid(jnp.concatenate(ix, axis=1) + bg_ref[1])
    log_a = (-LRU_C) * r * jax.nn.softplus(-lam_ref[...])
    a = jnp.exp(log_a)
    th = jnp.tanh(log_a)
    b = jnp.sqrt(-2.0 * th / (1.0 - th)) * i * xc

    hlast = hscr[...]
    hs = []
    for t in range(tt):
        hlast = a[t * s:(t + 1) * s, :] * hlast + b[t * s:(t + 1) * s, :]
        hs.append(hlast)
    h = jnp.concatenate(hs, axis=0)
    hscr[...] = hlast
    hT_ref[...] = hlast

    y = _mm(h * _silu(gate), wout_ref[...])
    out = _layer_norm(DN_ALPHA * x + y, lng_ref[...], lnb_ref[...])
    if swap:
        out = jnp.swapaxes(out.reshape(tt, s, D_MODEL), 0, 1)
    out_ref[...] = out


def _lru_layer(x3, halo, h0, win, cw, cb, wg, bg, lam, wout, lng, lnb, *, s, R, swap):
    C = LRU_W
    D = x3.shape[-1]
    H = halo.shape[1]
    tt = R // s
    assert s % SUBLANES == 0
    if swap:
        NB, TT = 1, x3.shape[1] * s
        xspec = pl.BlockSpec((s, tt, D), lambda i, c: (0, c, 0))
    else:
        NB, TT = x3.shape[:2]
        xspec = pl.BlockSpec((None, R, D), lambda i, c: (i, c, 0))
    kern = functools.partial(_lru_kernel, s=s, R=R, H=H, swap=swap)
    return pl.pallas_call(
        kern,
        grid=(NB, TT // R),
        in_specs=[
            xspec,
            pl.BlockSpec((None, H, C), lambda i, c: (i, 0, 0)),
            pl.BlockSpec((None, s, C), lambda i, c: (i, 0, 0)),
            _const_spec(win.shape), _const_spec(cw.shape), _const_spec(cb.shape), _const_spec(wg.shape),
            _const_spec(bg.shape), _const_spec(lam.shape), _const_spec(wout.shape), _const_spec(lng.shape),
            _const_spec(lnb.shape),
        ],
        out_specs=[
            xspec,
            pl.BlockSpec((None, H, C), lambda i, c: (i, 0, 0)),
            pl.BlockSpec((None, s, C), lambda i, c: (i, 0, 0)),
        ],
        out_shape=[
            jax.ShapeDtypeStruct(x3.shape, F32),
            jax.ShapeDtypeStruct((NB, H, C), F32),
            jax.ShapeDtypeStruct((NB, s, C), F32),
        ],
        scratch_shapes=[
            pltpu.VMEM((H + R, C), F32),
            pltpu.VMEM((s, C), F32),
        ],
        compiler_params=_cparams(2),
        name="lru_layer",
    )(x3, halo, h0, win, cw, cb, wg, bg, lam, wout, lng, lnb)


def _lru_gate_weights(wa, wx):
    def bd(w):
        w4 = w.reshape(4, 4, LRU_BS, LRU_BS)
        eye = jnp.eye(4, dtype=w.dtype)
        return jnp.einsum("gaij,ab->gaibj", w4, eye).reshape(4, 256, 256)
    return jnp.concatenate([bd(wa), bd(wx)], axis=2).astype(BF)


def _gated_out_ln(o, g_ref, x_ref, w_ref, lng_ref, lnb_ref, out_ref):
    bb, nh, Rt, _ = g_ref.shape
    rows = bb * Rt
    g = jnp.concatenate([g_ref[:, p, :, :].reshape(rows, LANES) for p in range(nh)], axis=1)
    y = _mm(o * _silu(g), w_ref[...])
    x = x_ref[...].reshape(rows, D_MODEL)
    out_ref[...] = _layer_norm(DN_ALPHA * x + y, lng_ref[...], lnb_ref[...]).reshape(bb, Rt, D_MODEL)


def _post_kernel(x_ref, o_ref, g_ref, w_ref, lng_ref, lnb_ref, out_ref):
    o = jnp.concatenate([o_ref[0, p] for p in range(o_ref.shape[1])], axis=1)
    _gated_out_ln(o, g_ref, x_ref, w_ref, lng_ref, lnb_ref, out_ref)


def _post_layer(x3, o4, g4, w, lng, lnb):
    return pl.pallas_call(
        _post_kernel,
        out_shape=jax.ShapeDtypeStruct(x3.shape, F32),
        compiler_params=pltpu.CompilerParams(vmem_limit_bytes=VMEM_LIMIT),
        name="post_layer",
    )(x3, o4, g4, w, lng, lnb)


def _rwkv_pre_kernel(x_ref, halo_ref, mu_ref, win_ref, w0_ref, w1_ref, w2_ref, a0_ref, a1_ref, a2_ref,
                     r_ref, k_ref, v_ref, g_ref, lw_ref, a_ref, xscr, xxscr, *, s, R, H):
    c = pl.program_id(1)

    @pl.when(c == 0)
    def _init():
        xscr[0:H, :] = halo_ref[...]

    x = x_ref[...]
    xscr[H:H + R, :] = x
    xxscr[...] = xscr[pl.ds(H - s, R), :] - x
    xscr[0:H, :] = xscr[pl.ds(R, H), :]
    xx = xxscr[...]
    xm = lambda n: x + xx * mu_ref[n]

    def put(ref, val):
        for p in range(RW_W // LANES):
            ref[p] = val[:, LANES * p:LANES * (p + 1)]

    put(r_ref, _mm(xm(0), win_ref[0]))
    put(k_ref, _mm(xm(1), win_ref[1]))
    put(v_ref, _mm(xm(2), win_ref[2]))
    put(g_ref, _mm(xm(3), win_ref[3]))
    w_raw = w0_ref[...] + _mm(jnp.tanh(_mm(xm(4), w1_ref[...])), w2_ref[...])
    put(lw_ref, (-math.exp(-0.5)) * jax.nn.sigmoid(w_raw))
    put(a_ref, jax.nn.sigmoid(a0_ref[...] + _mm(_mm(xm(5), a1_ref[...]), a2_ref[...])))


def _rwkv_pre(x3, halo, mu, win, w0, w1, w2, a0, a1, a2, *, s, R):
    NB, TT, D = x3.shape
    H = halo.shape[1]
    nh = RW_W // LANES
    kern = functools.partial(_rwkv_pre_kernel, s=s, R=R, H=H)
    ospec = pl.BlockSpec((None, nh, R, LANES), lambda i, c: (i, 0, c, 0))
    oshape = jax.ShapeDtypeStruct((NB, nh, TT, LANES), F32)
    return pl.pallas_call(
        kern,
        grid=(NB, TT // R),
        in_specs=[
            pl.BlockSpec((None, R, D), lambda i, c: (i, c, 0)),
            pl.BlockSpec((None, H, D), lambda i, c: (i, 0, 0)),
            _const_spec(mu.shape), _const_spec(win.shape), _const_spec(w0.shape), _const_spec(w1.shape),
            _const_spec(w2.shape), _const_spec(a0.shape), _const_spec(a1.shape), _const_spec(a2.shape),
        ],
        out_specs=[ospec] * 6,
        out_shape=[oshape] * 6,
        scratch_shapes=[pltpu.VMEM((H + R, D), F32), pltpu.VMEM((R, D), F32)],
        compiler_params=_cparams(2),
        name="rwkv_pre",
    )(x3, halo, mu, win, w0, w1, w2, a0, a1, a2)


def _seg_sum(x):
    lane = lax.broadcasted_iota(jnp.int32, x.shape, 1)
    lo = lane < RW_N
    s0 = jnp.sum(jnp.where(lo, x, 0.0), axis=-1, keepdims=True)
    s1 = jnp.sum(jnp.where(lo, 0.0, x), axis=-1, keepdims=True)
    return jnp.where(lo, s0, s1)


def _stack2(x):
    lane = lax.broadcasted_iota(jnp.int32, x.shape, 1)
    lo = lane < x.shape[1] // 2
    return jnp.concatenate([jnp.where(lo, x, 0.0), jnp.where(lo, 0.0, x)], axis=0)


def _each(f, *lists):
    return [f(*t) for t in zip(*lists)]


def _packed_masks(C):
    t = lax.broadcasted_iota(jnp.int32, (C, 2 * C), 0)
    s = lax.broadcasted_iota(jnp.int32, (C, 2 * C), 1) % C
    return s < t, s <= t, s == t


def _unit_lower_inverse(Ls, eye):
    n = Ls[0].shape[0]
    invs = [jnp.where(eye, 1.0, L) for L in Ls]
    Lps = Ls
    span = 2
    while span < n:
        Lps = _each(lambda Lp: _dot(Lp, _stack2(Lp)), Lps)
        yield
        invs = _each(lambda inv, Lp: inv + _dot(inv, _stack2(Lp)), invs, Lps)
        yield
        span *= 2
    return invs


def _run_pipelined(part_a, part_b, batches):
    ctx = {}
    prev = None
    for batch in batches:
        gens = [part_a(batch, ctx)] + ([part_b(prev, ctx)] if prev is not None else [])
        while gens:
            for g in list(gens):
                try:
                    next(g)
                except StopIteration:
                    gens.remove(g)
        prev = batch
    for _ in part_b(prev, ctx):
        pass


def _rec_batches(bb, nchunk, gb):
    if nchunk > 1:
        return tuple(tuple((b, c) for b in range(bb) for c in range(c0, min(c0 + 2, nchunk)))
                     for c0 in range(0, nchunk, 2))
    return tuple(tuple((b, 0) for b in range(b0, b0 + gb)) for b0 in range(0, bb, gb))


def _rwkv_rec_kernel(r_ref, k_ref, v_ref, lw_ref, a_ref, g_ref, x_ref, s0_ref, kk_ref, ka_ref, rk_ref, gg_ref,
                     gb_ref, wout_ref, lng_ref, lnb_ref, out_ref, sT_ref, sscr, oscr, *, C, batches):
    tb = pl.program_id(1)
    bb, npair, Rt, _ = r_ref.shape
    C2 = 2 * C
    lane_sq = lax.broadcasted_iota(jnp.int32, (LANES, LANES), 1)
    row_sq = lax.broadcasted_iota(jnp.int32, (LANES, LANES), 0)
    same_head = (lane_sq < RW_N) == (row_sq < RW_N)

    @pl.when(tb == 0)
    def _init():
        def init_b(b, carry):
            for p in range(npair):
                s2 = s0_ref[b, p]
                sscr[b * npair + p] = jnp.where(same_head, jnp.concatenate([s2, s2], axis=1), 0.0)
            return carry
        lax.fori_loop(0, bb, init_b, 0)

    strict, incl, eye = _packed_masks(C)
    tri = (lax.broadcasted_iota(jnp.int32, (C, C), 1) <= lax.broadcasted_iota(jnp.int32, (C, C), 0)).astype(F32)

    def part_a(items, ctx):
        CH = [(b, p, ci) for b, ci in items for p in range(npair)]
        rs = lambda ci: pl.ds(ci * C, C)
        r = [r_ref[b, p, rs(ci), :] for b, p, ci in CH]
        k = [k_ref[b, p, rs(ci), :] for b, p, ci in CH]
        v = [v_ref[b, p, rs(ci), :] for b, p, ci in CH]
        lw = [lw_ref[b, p, rs(ci), :] for b, p, ci in CH]
        a = [a_ref[b, p, rs(ci), :] for b, p, ci in CH]
        kn = [k_ * kk_ref[p] for k_, (b, p, ci) in zip(k, CH)]
        kk = _each(lambda z: z * lax.rsqrt(_seg_sum(z * z) + RW_NORM_EPS), kn)
        kh = [k_ * (1.0 + (a_ - 1.0) * ka_ref[p]) for k_, a_, (b, p, ci) in zip(k, a, CH)]
        bvec = _each(lambda x, y: x * y, kk, a)
        cum = _each(lambda x: _dot_exact_lhs(tri, x), lw)
        yield
        cum_last = [c_[C - 1:C, :] for c_ in cum]
        e_neg = _each(lambda c_: jnp.exp(-c_), cum)
        e_dec = _each(lambda cl, c_: jnp.exp(cl - c_), cum_last, cum)
        At = _each(lambda kk_, c_, lw_: -kk_ * jnp.exp(c_ - lw_), kk, cum, lw)
        Rt_ = _each(lambda r_, c_: r_ * jnp.exp(c_), r, cum)
        AR = _each(lambda x, y: jnp.concatenate([x, y], axis=0), At, Rt_)
        BKs = _each(lambda b_, kh_, e: jnp.concatenate([_stack2(b_ * e), _stack2(kh_ * e)], axis=0), bvec, kh, e_neg)
        V2 = _each(_stack2, v)
        sc = _each(lambda x, y: _dot(x, y, _NT), AR, BKs)
        yield
        Lab = [jnp.where(strict, z[:C, :C2], 0.0) for z in sc]
        Lak = [jnp.where(strict, z[:C, C2:], 0.0) for z in sc]
        Mrbk = [jnp.concatenate([jnp.where(incl, z[C:, :C2], 0.0), jnp.where(incl, z[C:, C2:], 0.0)], axis=1)
                for z in sc]
        LV = _each(_dot, Lak, V2)
        Tinv = yield from _unit_lower_inverse(Lab, eye)
        for n, key in enumerate(CH):
            ctx[key] = dict(AR=AR[n], V2=V2[n], LV=LV[n], Tinv=Tinv[n], Mrbk=Mrbk[n], v=v[n], r=r[n], kh=kh[n],
                            sdec=jnp.exp(cum_last[n]), bkdec=jnp.concatenate([bvec[n] * e_dec[n], kh[n] * e_dec[n]], 0))

    def part_b(items, ctx):
        for ci in sorted({ci for _, ci in items}):
            CH = [(b, p, ci) for b, c_ in items if c_ == ci for p in range(npair)]
            X = [ctx.pop(key) for key in CH]
            S = [sscr[b * npair + p] for b, p, _ in CH]
            UY0 = _each(lambda x, s_: _dot(x["AR"], s_, _NT), X, S)
            yield
            U = _each(lambda x, u: _dot(x["Tinv"], _stack2(u[:C, :] + x["LV"])), X, UY0)
            yield
            y = _each(lambda x, u0, u: u0[C:, :] + _dot(x["Mrbk"], jnp.concatenate([_stack2(u), x["V2"]], axis=0)),
                      X, UY0, U)
            Snew = _each(lambda x, s_, u: s_ * x["sdec"] + jnp.where(
                same_head, _dot(jnp.concatenate([u, x["v"]], axis=0), x["bkdec"], _TN), 0.0), X, S, U)
            yield
            for n, (b, p, _) in enumerate(CH):
                sscr[b * npair + p] = Snew[n]
                m = _seg_sum(y[n]) * (1.0 / RW_N)
                yc = y[n] - m
                var = _seg_sum(yc * yc) * (1.0 / RW_N)
                yn = yc * lax.rsqrt(var + RW_GN_EPS) * gg_ref[p] + gb_ref[p]
                bonus = _seg_sum(X[n]["r"] * X[n]["kh"] * rk_ref[p]) * X[n]["v"]
                oscr[b * Rt + ci * C:b * Rt + (ci + 1) * C, LANES * p:LANES * (p + 1)] = yn + bonus
            yield

    _run_pipelined(part_a, part_b, batches)

    row_h = lax.broadcasted_iota(jnp.int32, (LANES, RW_N), 0) < RW_N

    def fin_b(b, carry):
        for p in range(npair):
            S = sscr[b * npair + p]
            sT_ref[b, p] = jnp.where(row_h, S[:, :RW_N], S[:, RW_N:])
        return carry
    lax.fori_loop(0, bb, fin_b, 0)
    _gated_out_ln(oscr[...], g_ref, x_ref, wout_ref, lng_ref, lnb_ref, out_ref)


def _rwkv_rec(r4, k4, v4, lw4, a4, g4, x, s0, k_k, k_a, r_k, gn_g, gn_b, wout, lng, lnb, *, C, Rt, bb, gb):
    B, npair, T, _ = r4.shape
    D = x.shape[-1]
    kern = functools.partial(_rwkv_rec_kernel, C=C, batches=_rec_batches(bb, Rt // C, gb))
    tspec = pl.BlockSpec((bb, npair, Rt, LANES), lambda b, t: (b, 0, t, 0))
    xspec = pl.BlockSpec((bb, Rt, D), lambda b, t: (b, t, 0))
    sspec = pl.BlockSpec((bb, npair, LANES, RW_N), lambda b, t: (b, 0, 0, 0))
    wspec = _const_spec((npair, 1, LANES))
    return pl.pallas_call(
        kern,
        grid=(B // bb, T // Rt),
        in_specs=[tspec] * 6 + [xspec, sspec] + [wspec] * 5 + [_const_spec(wout.shape), _const_spec(lng.shape),
                                                              _const_spec(lnb.shape)],
        out_specs=[xspec, sspec],
        out_shape=[jax.ShapeDtypeStruct(x.shape, F32), jax.ShapeDtypeStruct(s0.shape, F32)],
        scratch_shapes=[pltpu.VMEM((bb * npair, LANES, LANES), F32), pltpu.VMEM((bb * Rt, RW_W), F32)],
        compiler_params=_cparams(2),
        name="rwkv_rec",
    )(r4, k4, v4, lw4, a4, g4, x, s0, k_k, k_a, r_k, gn_g, gn_b, wout, lng, lnb)


def _rwkv_step_kernel(r_ref, k_ref, v_ref, lw_ref, a_ref, s0_ref, kk_ref, ka_ref, rk_ref, gg_ref, gb_ref,
                      o_ref, sT_ref, vscr, yscr, *, T, B):
    N = RW_N
    IB = SUBLANES
    for t in range(T):
        rows = pl.ds(t * B, B)
        rT, kT, vT, aT = r_ref[rows, :].T, k_ref[rows, :].T, v_ref[rows, :].T, a_ref[rows, :].T
        w = jnp.exp(lw_ref[rows, :].T)
        kn = kT * kk_ref[...]
        kh = kT * (1.0 + (aT - 1.0) * ka_ref[...])
        vscr[...] = vT
        src = s0_ref if t == 0 else sT_ref
        bonus = []
        for h in range(2):
            hs = slice(N * h, N * (h + 1))
            kk = kn[hs] * lax.rsqrt(jnp.sum(kn[hs] * kn[hs], axis=0, keepdims=True) + RW_NORM_EPS)
            a_h, b_h, k_h, w_h, r_h = -kk, kk * aT[hs], kh[hs], w[hs], rT[hs]

            def step(ib, carry, h=h, a_h=a_h, b_h=b_h, k_h=k_h, w_h=w_h, r_h=r_h, src=src):
                i0 = pl.multiple_of(ib * IB, IB)
                S = src[h, pl.ds(i0, IB), :, :]
                sa = jnp.sum(S * a_h[None], axis=1)
                vb = vscr[pl.ds(N * h + i0, IB), :]
                Sn = S * w_h[None] + sa[:, None, :] * b_h[None] + vb[:, None, :] * k_h[None]
                sT_ref[h, pl.ds(i0, IB), :, :] = Sn
                yscr[pl.ds(N * h + i0, IB), :] = jnp.sum(Sn * r_h[None], axis=1)
                return carry

            lax.fori_loop(0, N // IB, step, 0)
            bonus.append(jnp.sum(r_h * k_h * rk_ref[hs, :], axis=0, keepdims=True) * vT[hs])
        y = yscr[...]
        outs = []
        for h in range(2):
            hs = slice(N * h, N * (h + 1))
            m = jnp.mean(y[hs], axis=0, keepdims=True)
            yc = y[hs] - m
            var = jnp.mean(yc * yc, axis=0, keepdims=True)
            outs.append(yc * lax.rsqrt(var + RW_GN_EPS) * gg_ref[hs, :] + gb_ref[hs, :] + bonus[h])
        o_ref[rows, :] = jnp.concatenate(outs, axis=0).T


def _rwkv_step(r4, k4, v4, lw4, a4, s0, k_k, k_a, r_k, gn_g, gn_b, *, T, B):
    npair = r4.shape[1]
    kern = functools.partial(_rwkv_step_kernel, T=T, B=B)
    tspec = pl.BlockSpec((None, None, T * B, LANES), lambda p: (0, p, 0, 0))
    sspec = pl.BlockSpec((None, 2, RW_N, RW_N, B), lambda p: (p, 0, 0, 0, 0))
    wspec = pl.BlockSpec((None, LANES, 1), lambda p: (p, 0, 0))
    return pl.pallas_call(
        kern,
        grid=(npair,),
        in_specs=[tspec] * 5 + [sspec] + [wspec] * 5,
        out_specs=[tspec, sspec],
        out_shape=[jax.ShapeDtypeStruct(r4.shape, F32), jax.ShapeDtypeStruct(s0.shape, F32)],
        scratch_shapes=[pltpu.VMEM((LANES, B), F32), pltpu.VMEM((LANES, B), F32)],
        compiler_params=_cparams(1),
        name="rwkv_step",
    )(r4, k4, v4, lw4, a4, s0, k_k, k_a, r_k, gn_g, gn_b)


def _gdn_pre_kernel(x_ref, halo_ref, win_ref, cw_ref, alog_ref, dtb_ref,
                    q_ref, k_ref, v_ref, z_ref, bg_ref, tail_ref, xscr, gscr, *, s, R, H, Hg, C):
    c = pl.program_id(1)
    CH = GDN_CONV_CH
    tt = R // s

    @pl.when(c == 0)
    def _init():
        xscr[0:H, :] = halo_ref[...]
        gscr[0:Hg, :] = jnp.zeros((Hg, LANES), F32)

    x = x_ref[...]
    u = _mm(x, win_ref[...])
    xb = u[:, :CH]
    xscr[H:H + R, :] = xb
    y = xb * cw_ref[3]
    for d in (1, 2, 3):
        y = y + xscr[pl.ds(H - d * s, R), :] * cw_ref[3 - d]
    tail = xscr[pl.ds(R, H), :]
    xscr[0:H, :] = tail
    tail_ref[...] = tail
    qkv = _silu(y)

    def l2n(z, scale):
        return z * (lax.rsqrt(jnp.sum(z * z, axis=-1, keepdims=True) + GDN_EPS) * scale)

    for h in range(GDN_HK):
        q_ref[h] = l2n(qkv[:, LANES * h:LANES * (h + 1)], GDN_DK ** -0.5)
        k_ref[h] = l2n(qkv[:, GDN_KEY_W + LANES * h:GDN_KEY_W + LANES * (h + 1)], 1.0)
    for h in range(GDN_HV):
        v_ref[h] = qkv[:, 2 * GDN_KEY_W + LANES * h:2 * GDN_KEY_W + LANES * (h + 1)]
        z_ref[h] = u[:, CH + LANES * h:CH + LANES * (h + 1)]

    bg = u[:, CH + GDN_VAL_W:CH + GDN_VAL_W + LANES]
    beta = jax.nn.sigmoid(bg)
    g = -jnp.exp(alog_ref[...]) * jax.nn.softplus(bg + dtb_ref[...])
    t_in = (lax.broadcasted_iota(jnp.int32, (R, 1), 0) // s) % C
    k = 1
    while k < min(C, tt):
        gscr[Hg:Hg + R, :] = g
        g = g + jnp.where(t_in >= k, gscr[pl.ds(Hg - k * s, R), :], 0.0)
        k *= 2
    lane = lax.broadcasted_iota(jnp.int32, (R, LANES), 1)
    bg_ref[...] = jnp.where(lane < GDN_HV, beta, g)


def _gdn_pre(x3, halo, win, cw, alog, dtb, *, s, R, C):
    NB, TT, D = x3.shape
    H = halo.shape[1]
    tt = R // s
    Hg = max(SUBLANES, (min(C, tt) // 2) * s)
    kern = functools.partial(_gdn_pre_kernel, s=s, R=R, H=H, Hg=Hg, C=C)

    def ospec(nh):
        return pl.BlockSpec((None, nh, R, LANES), lambda i, c: (i, 0, c, 0))

    def oshape(nh):
        return jax.ShapeDtypeStruct((NB, nh, TT, LANES), F32)

    return pl.pallas_call(
        kern,
        grid=(NB, TT // R),
        in_specs=[
            pl.BlockSpec((None, R, D), lambda i, c: (i, c, 0)),
            pl.BlockSpec((None, H, GDN_CONV_CH), lambda i, c: (i, 0, 0)),
            _const_spec(win.shape), _const_spec(cw.shape), _const_spec(alog.shape), _const_spec(dtb.shape),
        ],
        out_specs=[ospec(GDN_HK), ospec(GDN_HK), ospec(GDN_HV), ospec(GDN_HV),
                   pl.BlockSpec((None, R, LANES), lambda i, c: (i, c, 0)),
                   pl.BlockSpec((None, H, GDN_CONV_CH), lambda i, c: (i, 0, 0))],
        out_shape=[oshape(GDN_HK), oshape(GDN_HK), oshape(GDN_HV), oshape(GDN_HV),
                   jax.ShapeDtypeStruct((NB, TT, LANES), F32),
                   jax.ShapeDtypeStruct((NB, H, GDN_CONV_CH), F32)],
        scratch_shapes=[pltpu.VMEM((H + R, GDN_CONV_CH), F32), pltpu.VMEM((Hg + R, LANES), F32)],
        compiler_params=_cparams(2),
        name="gdn_pre",
    )(x3, halo, win, cw, alog, dtb)


def _gdn_rec_kernel(q_ref, k_ref, v_ref, z_ref, bg_ref, x_ref, s0_ref, ng_ref, wout_ref, lng_ref, lnb_ref,
                    out_ref, sT_ref, sscr, oscr, *, C, batches):
    tb = pl.program_id(1)
    bb, nhv, Rt, _ = v_ref.shape
    rep = GDN_HV // GDN_HK

    @pl.when(tb == 0)
    def _init():
        sscr[...] = s0_ref[...].reshape(sscr.shape)

    strict, incl, eye = _packed_masks(C)
    lo = lax.broadcasted_iota(jnp.int32, (C, 2 * C), 1) < C

    def part_a(items, ctx):
        CH = [(b, h, ci) for b, ci in items for h in range(nhv)]
        KH = [(b, m, ci) for b, ci in items for m in range(GDN_HK)]
        rs = lambda ci: pl.ds(ci * C, C)
        bgs = {(b, ci): bg_ref[b, rs(ci), :] for b, ci in items}
        bgT = {key: x.T for key, x in bgs.items()}

        kq = [jnp.concatenate([k_ref[b, m, rs(ci), :], q_ref[b, m, rs(ci), :]], axis=0) for b, m, ci in KH]
        sc = _each(lambda x: _dot(x, jnp.concatenate([x[:C, :], x[:C, :]], axis=0), _NT), kq)
        yield

        def col2(x, lane):
            return jnp.where(lo, jnp.broadcast_to(x[:, lane:lane + 1], (C, 2 * C)),
                             jnp.broadcast_to(x[:, lane + 1:lane + 2], (C, 2 * C)))

        bcol = [col2(bgs[b, ci], rep * m) for b, m, ci in KH]
        gcol = [col2(bgs[b, ci], GDN_HV + rep * m) for b, m, ci in KH]
        grow = [jnp.concatenate([bgT[b, ci][GDN_HV + rep * m:GDN_HV + rep * m + 1, :],
                                 bgT[b, ci][GDN_HV + rep * m + 1:GDN_HV + rep * m + 2, :]], axis=1) for b, m, ci in KH]
        diff = _each(lambda c_, r_: c_ - r_, gcol, grow)
        Lp = _each(lambda z, bc, d: jnp.where(strict, z[:C, :] * bc * jnp.exp(jnp.where(strict, d, 0.0)), 0.0),
                   sc, bcol, diff)
        Ap = _each(lambda z, d: jnp.where(incl, z[C:, :] * jnp.exp(jnp.where(incl, d, 0.0)), 0.0), sc, diff)
        Tinv_p = yield from _unit_lower_inverse([-l_ for l_ in Lp], eye)

        half = lambda xs, n, h: xs[n // rep][:, (h % rep) * C:(h % rep + 1) * C]
        Tinv = [half(Tinv_p, n, h) for n, (b, h, ci) in enumerate(CH)]
        A = [half(Ap, n, h) for n, (b, h, ci) in enumerate(CH)]
        k = [kq[n // rep][:C, :] for n in range(len(CH))]
        q = [kq[n // rep][C:, :] for n in range(len(CH))]
        v = [v_ref[b, h, rs(ci), :] for b, h, ci in CH]
        beta = [jnp.broadcast_to(bgs[b, ci][:, h:h + 1], (C, LANES)) for b, h, ci in CH]
        gc = [jnp.broadcast_to(bgs[b, ci][:, GDN_HV + h:GDN_HV + h + 1], (C, LANES)) for b, h, ci in CH]
        kb = _each(lambda x, y: x * y, k, beta)
        eg = _each(jnp.exp, gc)
        UW = _each(lambda t, v_, b_, kb_, e: _dot(t, jnp.concatenate([v_ * b_, kb_ * e], axis=1)),
                   Tinv, v, beta, kb, eg)
        yield
        for n, key in enumerate(CH):
            g_last = gc[n][C - 1:C, :]
            ctx[key] = dict(U=UW[n][:, :GDN_DV], WQl=jnp.concatenate([UW[n][:, GDN_DV:], q[n] * eg[n]], axis=0),
                            A=A[n], kdec=k[n] * jnp.exp(g_last - gc[n]), sdec=jnp.exp(g_last))

    def part_b(items, ctx):
        for ci in sorted({ci for _, ci in items}):
            CH = [(b, h, ci) for b, c_ in items if c_ == ci for h in range(nhv)]
            X = [ctx.pop(key) for key in CH]
            S = [sscr[b * nhv + h] for b, h, _ in CH]
            WQ = _each(lambda x, s_: _dot(x["WQl"], s_), X, S)
            yield
            v_new = _each(lambda x, wq: x["U"] - wq[:C, :], X, WQ)
            o = _each(lambda x, wq, vn: wq[C:, :] + _dot(x["A"], vn), X, WQ, v_new)
            Snew = _each(lambda x, s_, vn: s_ * x["sdec"] + _dot(x["kdec"], vn, _TN), X, S, v_new)
            yield
            for n, (b, h, _) in enumerate(CH):
                sscr[b * nhv + h] = Snew[n]
                oh = o[n]
                oscr[b * Rt + ci * C:b * Rt + (ci + 1) * C, LANES * h:LANES * (h + 1)] = (
                    oh * lax.rsqrt(jnp.mean(oh * oh, axis=-1, keepdims=True) + GDN_EPS) * ng_ref[...])
            yield

    _run_pipelined(part_a, part_b, batches)
    sT_ref[...] = sscr[...].reshape(sT_ref.shape)
    _gated_out_ln(oscr[...], z_ref, x_ref, wout_ref, lng_ref, lnb_ref, out_ref)


def _gdn_rec(q4, k4, v4, z4, bg, x, s0, norm_g, wout, lng, lnb, *, C, Rt, bb, gb):
    B, _, T, _ = v4.shape
    D = x.shape[-1]
    kern = functools.partial(_gdn_rec_kernel, C=C, batches=_rec_batches(bb, Rt // C, gb))

    def tspec(nh):
        return pl.BlockSpec((bb, nh, Rt, LANES), lambda b, t: (b, 0, t, 0))

    xspec = pl.BlockSpec((bb, Rt, D), lambda b, t: (b, t, 0))
    sspec = pl.BlockSpec((bb, GDN_HV, GDN_DK, GDN_DV), lambda b, t: (b, 0, 0, 0))
    return pl.pallas_call(
        kern,
        grid=(B // bb, T // Rt),
        in_specs=[tspec(GDN_HK), tspec(GDN_HK), tspec(GDN_HV), tspec(GDN_HV),
                  pl.BlockSpec((bb, Rt, LANES), lambda b, t: (b, t, 0)), xspec, sspec,
                  _const_spec(norm_g.shape), _const_spec(wout.shape), _const_spec(lng.shape), _const_spec(lnb.shape)],
        out_specs=[xspec, sspec],
        out_shape=[jax.ShapeDtypeStruct(x.shape, F32), jax.ShapeDtypeStruct(s0.shape, F32)],
        scratch_shapes=[pltpu.VMEM((bb * GDN_HV, GDN_DK, GDN_DV), F32), pltpu.VMEM((bb * Rt, GDN_VAL_W), F32)],
        compiler_params=_cparams(2),
        name="gdn_rec",
    )(q4, k4, v4, z4, bg, x, s0, norm_g, wout, lng, lnb)


class _Group:
    def __init__(self, B, T, time_major):
        self.B, self.T, self.time_major = B, T, time_major
        if time_major:
            self.s, self.NB, self.TT, self.R = B, 1, T * B, T * B
        else:
            self.s, self.NB, self.TT, self.R = 1, B, T, min(ROW_BLOCK, T)

    def to_rows(self, x):
        if self.time_major:
            return jnp.swapaxes(x, 0, 1).reshape(1, self.TT, x.shape[-1])
        return x

    def from_rows(self, x3):
        if self.time_major:
            return jnp.swapaxes(x3.reshape(self.T, self.B, x3.shape[-1]), 0, 1)
        return x3

    def halo(self, st, n_steps):
        if self.time_major:
            return jnp.swapaxes(st, 0, 1).reshape(1, n_steps * self.B, st.shape[-1])
        H = _halo_rows(n_steps, 1)
        return jnp.pad(st, ((0, 0), (H - n_steps, 0), (0, 0)))

    def unhalo(self, tail, n_steps):
        if self.time_major:
            return jnp.swapaxes(tail.reshape(n_steps, self.B, tail.shape[-1]), 0, 1)
        return tail[:, tail.shape[1] - n_steps:]

    def vec(self, st):
        return st[None] if self.time_major else st[:, None, :]

    def unvec(self, v):
        return v[0] if self.time_major else v[:, 0, :]

    def heads_to_batch(self, a4, Tpad):
        if not self.time_major:
            return a4
        nh = a4.shape[1]
        a = a4.reshape(nh, self.T, self.B, LANES).transpose(2, 0, 1, 3)
        return jnp.pad(a, ((0, 0), (0, 0), (0, Tpad - self.T), (0, 0)))

    def rows_to_batch(self, a3, Tpad):
        if not self.time_major:
            return a3
        return jnp.pad(self.from_rows(a3), ((0, 0), (0, Tpad - self.T), (0, 0)))

    def rows_from_batch(self, a3):
        if not self.time_major:
            return a3
        return self.to_rows(a3[:, :self.T])

    def rec_tiling(self, chunk, block):
        if self.time_major:
            Tp = -(-self.T // SUBLANES) * SUBLANES
            return Tp, Tp, Tp, min(self.B, SAMPLE_SEQ_BLOCK), min(self.B, SAMPLE_SEQ_GROUP)
        return min(chunk, self.T), self.T, min(block, self.T), PROMPT_SEQ_BLOCK if self.B % PROMPT_SEQ_BLOCK == 0 else 1, 1


def _lru_apply(g, x3, conv_st, h_st, p, lng, lnb):
    swap = not g.time_major
    assert g.B % SUBLANES == 0, "RG-LRU kernel needs the batch to fill whole sublane tiles"
    lay = _Group(g.B, g.T, True) if swap else g
    out, tail, hT = _lru_layer(x3, lay.halo(conv_st, CONV_W - 1), lay.vec(h_st), p["win"], p["cw"], p["cb"],
                               p["wg"], p["bg"], p["lam"], p["wout"], lng, lnb, s=lay.s,
                               R=min(LRU_BLOCK, g.T * g.B) if swap else g.R, swap=swap)
    return out, lay.unhalo(tail, CONV_W - 1), lay.unvec(hT)


def _rwkv_apply(g, x3, shift_st, wkv_st, p, lng, lnb):
    B = g.B
    pre = _rwkv_pre(x3, g.halo(shift_st[:, None, :], 1), p["mu"], p["win"], p["w0"], p["w1"], p["w2"], p["a0"],
                    p["a1"], p["a2"], s=g.s, R=g.R)
    new_shift = g.from_rows(x3)[:, -1]
    if g.time_major:
        r4, k4, v4, g4, lw4, a4 = pre
        s0 = jnp.transpose(wkv_st, (1, 2, 3, 0)).reshape(RW_H // 2, 2, RW_N, RW_N, B)
        col = lambda v: jnp.swapaxes(v, 1, 2)
        o4, sT = _rwkv_step(r4, k4, v4, lw4, a4, s0, col(p["k_k"]), col(p["k_a"]), col(p["r_k"]), col(p["gn_g"]),
                            col(p["gn_b"]), T=g.T, B=B)
        out = _post_layer(x3, o4, g4, p["wout"], lng, lnb)
        return out, new_shift, jnp.transpose(sT.reshape(RW_H, RW_N, RW_N, B), (3, 0, 1, 2))
    C, Tp, Rt, bb, gb = g.rec_tiling(RW_CHUNK, RW_BLOCK)
    r4, k4, v4, g4, lw4, a4 = [g.heads_to_batch(a, Tp) for a in pre]
    s0 = wkv_st.reshape(B, RW_H // 2, 2 * RW_N, RW_N)
    out, sT = _rwkv_rec(r4, k4, v4, lw4, a4, g4, g.rows_to_batch(x3, Tp), s0, p["k_k"], p["k_a"], p["r_k"],
                        p["gn_g"], p["gn_b"], p["wout"], lng, lnb, C=C, Rt=Rt, bb=bb, gb=gb)
    new_shift = g.from_rows(x3)[:, -1]
    return g.rows_from_batch(out), new_shift, sT.reshape(B, RW_H, RW_N, RW_N)


def _gdn_apply(g, x3, conv_st, S_st, p, lng, lnb):
    T = g.T
    C, Tp, Rt, bb, gb = g.rec_tiling(GDN_CHUNK, GDN_BLOCK)
    q4, k4, v4, z4, bg, tail = _gdn_pre(x3, g.halo(conv_st, CONV_W - 1), p["win"], p["cw"], p["alog"], p["dtb"],
                                        s=g.s, R=g.R, C=min(GDN_CHUNK, T))
    bgb = g.rows_to_batch(bg, T)
    if Tp != T:
        held = jnp.where(jnp.arange(LANES) < GDN_HV, 0.0, bgb[:, T - 1:T, :])
        bgb = jnp.concatenate([bgb, jnp.broadcast_to(held, (g.B, Tp - T, LANES))], axis=1)
    tb = lambda a: g.heads_to_batch(a, Tp)
    out, sT = _gdn_rec(tb(q4), tb(k4), tb(v4), tb(z4), bgb, g.rows_to_batch(x3, Tp), S_st, p["norm_g"], p["wout"],
                       lng, lnb, C=C, Rt=Rt, bb=bb, gb=gb)
    return g.rows_from_batch(out), g.unhalo(tail, CONV_W - 1), sT


def _trunk(g, x, st, params, ln_g, ln_b):
    lru_conv, lru_h, rw_shift, rw_S, gdn_conv, gdn_S = st
    new = ([], [], [], [], [], [])
    x3 = g.to_rows(x)
    ia = ib = ic = 0
    for layer in range(DEPTH):
        kind = layer % N_MIXERS
        lng, lnb = ln_g[layer][None, :], ln_b[layer][None, :]
        if kind == 0:
            x3, c, h = _lru_apply(g, x3, lru_conv[ia], lru_h[ia], params["lru"][ia], lng, lnb)
            new[0].append(c)
            new[1].append(h)
            ia += 1
        elif kind == 1:
            x3, sh, S = _rwkv_apply(g, x3, rw_shift[ib], rw_S[ib], params["rwkv"][ib], lng, lnb)
            new[2].append(sh)
            new[3].append(S)
            ib += 1
        else:
            x3, c, S = _gdn_apply(g, x3, gdn_conv[ic], gdn_S[ic], params["gdn"][ic], lng, lnb)
            new[4].append(c)
            new[5].append(S)
            ic += 1
    return g.from_rows(x3), tuple(s[0][None] if len(s) == 1 else jnp.stack(s) for s in new)


def _prep_params(lru_w_in, lru_conv_w, lru_conv_b, lru_wa, lru_ba, lru_wx, lru_bx, lru_lambda, lru_w_out, rw_mu,
                 rw_w_in, rw_w0, rw_w1, rw_w2, rw_a0, rw_a1, rw_a2, rw_k_k, rw_k_a, rw_r_k, rw_gn_g, rw_gn_b,
                 rw_w_out, gdn_w_in, gdn_conv_w, gdn_a_log, gdn_dt_bias, gdn_norm_g, gdn_w_out):
    row = lambda v: v[None, :]
    lru = []
    for n in range(lru_w_in.shape[0]):
        lru.append(dict(win=lru_w_in[n].astype(BF), cw=lru_conv_w[n][:, None, :], cb=row(lru_conv_b[n]),
                        wg=_lru_gate_weights(lru_wa[n], lru_wx[n]), bg=jnp.stack([lru_ba[n], lru_bx[n]])[:, None, :],
                        lam=row(lru_lambda[n]), wout=lru_w_out[n].astype(BF)))
    pairw = lambda v: v.reshape(RW_W // LANES, 1, LANES)
    rwkv = []
    for n in range(rw_w_in.shape[0]):
        rwkv.append(dict(mu=rw_mu[n][:, None, :],win=rw_w_in[n].astype(BF), w0=row(rw_w0[n]), w1=rw_w1[n].astype(BF),
                         w2=rw_w2[n].astype(BF), a0=row(rw_a0[n]), a1=rw_a1[n].astype(BF), a2=rw_a2[n].astype(BF),
                         k_k=pairw(rw_k_k[n]), k_a=pairw(rw_k_a[n]), r_k=pairw(rw_r_k[n]), gn_g=pairw(rw_gn_g[n]),
                         gn_b=pairw(rw_gn_b[n]), wout=rw_w_out[n].astype(BF)))
    gdn = []
    for n in range(gdn_w_in.shape[0]):
        w = gdn_w_in[n]
        o2 = GDN_CONV_CH + GDN_VAL_W
        wpad = jnp.pad(w[:, o2:], ((0, 0), (0, LANES - 2 * GDN_HV)))
        lanes = lambda v: jnp.pad(v, (GDN_HV, LANES - 2 * GDN_HV))[None, :]
        gdn.append(dict(win=jnp.concatenate([w[:, :o2], wpad], axis=1).astype(BF), cw=gdn_conv_w[n][:, None, :],
                        alog=lanes(gdn_a_log[n]), dtb=lanes(gdn_dt_bias[n]), norm_g=row(gdn_norm_g[n]),
                        wout=gdn_w_out[n].astype(BF)))
    return dict(lru=lru, rwkv=rwkv, gdn=gdn)


def kernel(x_prompt, x_sample, state_lru_conv, state_lru_h, state_rwkv_shift, state_rwkv_wkv, state_gdn_conv, state_gdn_S, ln_g, ln_b, lru_w_in, lru_conv_w, lru_conv_b, lru_wa, lru_ba, lru_wx, lru_bx, lru_lambda, lru_w_out, rw_mu, rw_w_in, rw_w0, rw_w1, rw_w2, rw_a0, rw_a1, rw_a2, rw_k_k, rw_k_a, rw_r_k, rw_gn_g, rw_gn_b, rw_w_out, gdn_w_in, gdn_conv_w, gdn_a_log, gdn_dt_bias, gdn_norm_g, gdn_w_out):
    params = _prep_params(lru_w_in, lru_conv_w, lru_conv_b, lru_wa, lru_ba, lru_wx, lru_bx, lru_lambda, lru_w_out,
                          rw_mu, rw_w_in, rw_w0, rw_w1, rw_w2, rw_a0, rw_a1, rw_a2, rw_k_k, rw_k_a, rw_r_k, rw_gn_g,
                          rw_gn_b, rw_w_out, gdn_w_in, gdn_conv_w, gdn_a_log, gdn_dt_bias, gdn_norm_g, gdn_w_out)
    bp, tp, _ = x_prompt.shape
    bs, ts, _ = x_sample.shape
    n_a, n_b, n_c = state_lru_conv.shape[0], state_rwkv_shift.shape[0], state_gdn_conv.shape[0]
    zero_state = (jnp.zeros((n_a, bp, CONV_W - 1, LRU_W), F32),
                  jnp.zeros((n_a, bp, LRU_W), F32),
                  jnp.zeros((n_b, bp, D_MODEL), F32),
                  jnp.zeros((n_b, bp, RW_H, RW_N, RW_N), F32),
                  jnp.zeros((n_c, bp, CONV_W - 1, GDN_CONV_CH), F32),
                  jnp.zeros((n_c, bp, GDN_HV, GDN_DK, GDN_DV), F32))
    y_prompt, sp = _trunk(_Group(bp, tp, False), x_prompt, zero_state, params, ln_g, ln_b)
    y_sample, ss = _trunk(_Group(bs, ts, True), x_sample,
                          (state_lru_conv, state_lru_h, state_rwkv_shift, state_rwkv_wkv, state_gdn_conv,
                           state_gdn_S), params, ln_g, ln_b)
    return (y_prompt, y_sample, sp[0], ss[0], sp[1], ss[1], sp[2], ss[2], sp[3], ss[3], sp[4], ss[4], sp[5], ss[5])
```

```python
import functools
import math

import jax
import jax.numpy as jnp
from jax import lax
from jax.experimental import pallas as pl
from jax.experimental.pallas import tpu as pltpu

F32 = jnp.float32
BF = jnp.bfloat16

D_MODEL = 1024
DEPTH = 4
N_MIXERS = 3
DN_ALPHA = (2.0 * DEPTH) ** 0.25
LN_EPS = 1e-5
CONV_W = 4

LRU_W = D_MODEL
LRU_BLOCKS = 16
LRU_BS = LRU_W // LRU_BLOCKS
LRU_C = 8.0

RW_W = D_MODEL
RW_N = 64
RW_H = RW_W // RW_N
RW_GN_EPS = 64e-5
RW_NORM_EPS = 1e-12

GDN_HK = 4
GDN_HV = 8
GDN_DK = 128
GDN_DV = 128
GDN_KEY_W = GDN_HK * GDN_DK
GDN_VAL_W = GDN_HV * GDN_DV
GDN_CONV_CH = 2 * GDN_KEY_W + GDN_VAL_W
GDN_CHUNK = 64
GDN_EPS = 1e-6

LANES = 128
SUBLANES = 8
VMEM_LIMIT = 56 * 1024 * 1024
ROW_BLOCK = 512
LRU_BLOCK = 512
RW_BLOCK = 256
GDN_BLOCK = 256
RW_CHUNKS_PER_BATCH = 1
GDN_CHUNKS_PER_BATCH = 2
PROMPT_SEQ_BLOCK = 2
SAMPLE_SEQ_BLOCK = 8
SAMPLE_SEQ_GROUP = 4
RW_CHUNK = 64

_NN = (((1,), (0,)), ((), ()))
_NT = (((1,), (1,)), ((), ()))
_TN = (((0,), (0,)), ((), ()))


def _mm(a, b):
    return jnp.dot(a.astype(BF), b.astype(BF), preferred_element_type=F32)


def _dot(a, b, dims=_NN):
    return lax.dot_general(a.astype(BF), b.astype(BF), dims, preferred_element_type=F32)


def _dot_exact_lhs(a01, b):
    a = a01.astype(BF)
    b1 = b.astype(BF)
    r1 = b - b1.astype(F32)
    b2 = r1.astype(BF)
    b3 = (r1 - b2.astype(F32)).astype(BF)
    d = lambda q: jnp.dot(a, q, preferred_element_type=F32)
    return d(b1) + (d(b2) + d(b3))


def _layer_norm(z, g, b):
    mu = jnp.mean(z, axis=-1, keepdims=True)
    zc = z - mu
    var = jnp.mean(zc * zc, axis=-1, keepdims=True)
    return zc * lax.rsqrt(var + LN_EPS) * g + b


def _silu(x):
    return x * jax.nn.sigmoid(x)


def _cparams(n_axes):
    return pltpu.CompilerParams(dimension_semantics=("arbitrary",) * n_axes, vmem_limit_bytes=VMEM_LIMIT)


def _const_spec(shape):
    nd = len(shape)
    return pl.BlockSpec(shape, lambda *_: (0,) * nd, pipeline_mode=pl.Buffered(1))


def _halo_rows(n_steps, s):
    rows = n_steps * s
    return rows if rows % SUBLANES == 0 else SUBLANES


def _lru_kernel(x_ref, halo_ref, h0_ref, win_ref, cw_ref, cb_ref, wg_ref, bg_ref, lam_ref, wout_ref, lng_ref,
                lnb_ref, out_ref, tail_ref, hT_ref, xscr, hscr, *, s, R, H, swap):
    c = pl.program_id(1)
    C = LRU_W
    tt = R // s

    @pl.when(c == 0)
    def _init():
        xscr[0:H, :] = halo_ref[...]
        hscr[...] = h0_ref[...]

    x = x_ref[...]
    if swap:
        x = jnp.swapaxes(x, 0, 1).reshape(R, D_MODEL)
    u = _mm(x, win_ref[...])
    xb = u[:, :C]
    gate = u[:, C:]

    xscr[H:H + R, :] = xb
    xc = xb * cw_ref[3] + cb_ref[...]
    for d in (1, 2, 3):
        xc = xc + xscr[pl.ds(H - d * s, R), :] * cw_ref[3 - d]
    tail = xscr[pl.ds(R, H), :]
    xscr[0:H, :] = tail
    tail_ref[...] = tail

    xcb = xc.astype(BF)
    ra, ix = [], []
    for g in range(C // 256):
        gt = jnp.dot(xcb[:, 256 * g:256 * (g + 1)], wg_ref[g], preferred_element_type=F32)
        ra.append(gt[:, :256])
        ix.append(gt[:, 256:])
    r = jax.nn.sigmoid(jnp.concatenate(ra, axis=1) + bg_ref[0])
    i = jax.nn.sigmoid(jnp.concatenate(ix, axis=1) + bg_ref[1])
    log_a = (-LRU_C) * r * jax.nn.softplus(-lam_ref[...])
    a = jnp.exp(log_a)
    th = jnp.tanh(log_a)
    b = jnp.sqrt(-2.0 * th / (1.0 - th)) * i * xc

    hlast = hscr[...]
    hs = []
    for t in range(tt):
        hlast = a[t * s:(t + 1) * s, :] * hlast + b[t * s:(t + 1) * s, :]
        hs.append(hlast)
    h = jnp.concatenate(hs, axis=0)
    hscr[...] = hlast
    hT_ref[...] = hlast

    y = _mm(h * _silu(gate), wout_ref[...])
    out = _layer_norm(DN_ALPHA * x + y, lng_ref[...], lnb_ref[...])
    if swap:
        out = jnp.swapaxes(out.reshape(tt, s, D_MODEL), 0, 1)
    out_ref[...] = out


def _lru_layer(x3, halo, h0, win, cw, cb, wg, bg, lam, wout, lng, lnb, *, s, R, swap):
    C = LRU_W
    D = x3.shape[-1]
    H = halo.shape[1]
    tt = R // s
    assert s % SUBLANES == 0
    if swap:
        NB, TT = 1, x3.shape[1] * s
        xspec = pl.BlockSpec((s, tt, D), lambda i, c: (0, c, 0))
    else:
        NB, TT = x3.shape[:2]
        xspec = pl.BlockSpec((None, R, D), lambda i, c: (i, c, 0))
    kern = functools.partial(_lru_kernel, s=s, R=R, H=H, swap=swap)
    return pl.pallas_call(
        kern,
        grid=(NB, TT // R),
        in_specs=[
            xspec,
            pl.BlockSpec((None, H, C), lambda i, c: (i, 0, 0)),
            pl.BlockSpec((None, s, C), lambda i, c: (i, 0, 0)),
            _const_spec(win.shape), _const_spec(cw.shape), _const_spec(cb.shape), _const_spec(wg.shape),
            _const_spec(bg.shape), _const_spec(lam.shape), _const_spec(wout.shape), _const_spec(lng.shape),
            _const_spec(lnb.shape),
        ],
        out_specs=[
            xspec,
            pl.BlockSpec((None, H, C), lambda i, c: (i, 0, 0)),
            pl.BlockSpec((None, s, C), lambda i, c: (i, 0, 0)),
        ],
        out_shape=[
            jax.ShapeDtypeStruct(x3.shape, F32),
            jax.ShapeDtypeStruct((NB, H, C), F32),
            jax.ShapeDtypeStruct((NB, s, C), F32),
        ],
        scratch_shapes=[
            pltpu.VMEM((H + R, C), F32),
            pltpu.VMEM((s, C), F32),
        ],
        compiler_params=_cparams(2),
        name="lru_layer",
    )(x3, halo, h0, win, cw, cb, wg, bg, lam, wout, lng, lnb)


def _lru_gate_weights(wa, wx):
    def bd(w):
        w4 = w.reshape(4, 4, LRU_BS, LRU_BS)
        eye = jnp.eye(4, dtype=w.dtype)
        return jnp.einsum("gaij,ab->gaibj", w4, eye).reshape(4, 256, 256)
    return jnp.concatenate([bd(wa), bd(wx)], axis=2).astype(BF)


def _gated_out_ln(o, g_ref, x_ref, w_ref, lng_ref, lnb_ref, out_ref):
    bb, nh, Rt, _ = g_ref.shape
    rows = bb * Rt
    g = jnp.concatenate([g_ref[:, p, :, :].reshape(rows, LANES) for p in range(nh)], axis=1)
    y = _mm(o * _silu(g), w_ref[...])
    x = x_ref[...].reshape(rows, D_MODEL)
    out_ref[...] = _layer_norm(DN_ALPHA * x + y, lng_ref[...], lnb_ref[...]).reshape(bb, Rt, D_MODEL)


def _post_kernel(x_ref, o_ref, g_ref, w_ref, lng_ref, lnb_ref, out_ref):
    o = jnp.concatenate([o_ref[0, p] for p in range(o_ref.shape[1])], axis=1)
    _gated_out_ln(o, g_ref, x_ref, w_ref, lng_ref, lnb_ref, out_ref)


def _post_layer(x3, o4, g4, w, lng, lnb):
    return pl.pallas_call(
        _post_kernel,
        out_shape=jax.ShapeDtypeStruct(x3.shape, F32),
        compiler_params=pltpu.CompilerParams(vmem_limit_bytes=VMEM_LIMIT),
        name="post_layer",
    )(x3, o4, g4, w, lng, lnb)


def _rwkv_pre_kernel(x_ref, halo_ref, mu_ref, win_ref, w0_ref, w1_ref, w2_ref, a0_ref, a1_ref, a2_ref,
                     r_ref, k_ref, v_ref, g_ref, lw_ref, a_ref, xscr, xxscr, *, s, R, H):
    c = pl.program_id(1)

    @pl.when(c == 0)
    def _init():
        xscr[0:H, :] = halo_ref[...]

    x = x_ref[...]
    xscr[H:H + R, :] = x
    xxscr[...] = xscr[pl.ds(H - s, R), :] - x
    xscr[0:H, :] = xscr[pl.ds(R, H), :]
    xx = xxscr[...]
    xm = lambda n: x + xx * mu_ref[n]

    def put(ref, val):
        for p in range(RW_W // LANES):
            ref[p] = val[:, LANES * p:LANES * (p + 1)]

    put(r_ref, _mm(xm(0), win_ref[0]))
    put(k_ref, _mm(xm(1), win_ref[1]))
    put(v_ref, _mm(xm(2), win_ref[2]))
    put(g_ref, _mm(xm(3), win_ref[3]))
    w_raw = w0_ref[...] + _mm(jnp.tanh(_mm(xm(4), w1_ref[...])), w2_ref[...])
    put(lw_ref, (-math.exp(-0.5)) * jax.nn.sigmoid(w_raw))
    put(a_ref, jax.nn.sigmoid(a0_ref[...] + _mm(_mm(xm(5), a1_ref[...]), a2_ref[...])))


def _rwkv_pre(x3, halo, mu, win, w0, w1, w2, a0, a1, a2, *, s, R):
    NB, TT, D = x3.shape
    H = halo.shape[1]
    nh = RW_W // LANES
    kern = functools.partial(_rwkv_pre_kernel, s=s, R=R, H=H)
    ospec = pl.BlockSpec((None, nh, R, LANES), lambda i, c: (i, 0, c, 0))
    oshape = jax.ShapeDtypeStruct((NB, nh, TT, LANES), F32)
    return pl.pallas_call(
        kern,
        grid=(NB, TT // R),
        in_specs=[
            pl.BlockSpec((None, R, D), lambda i, c: (i, c, 0)),
            pl.BlockSpec((None, H, D), lambda i, c: (i, 0, 0)),
            _const_spec(mu.shape), _const_spec(win.shape), _const_spec(w0.shape), _const_spec(w1.shape),
            _const_spec(w2.shape), _const_spec(a0.shape), _const_spec(a1.shape), _const_spec(a2.shape),
        ],
        out_specs=[ospec] * 6,
        out_shape=[oshape] * 6,
        scratch_shapes=[pltpu.VMEM((H + R, D), F32), pltpu.VMEM((R, D), F32)],
        compiler_params=_cparams(2),
        name="rwkv_pre",
    )(x3, halo, mu, win, w0, w1, w2, a0, a1, a2)


def _seg_sum(x):
    lane = lax.broadcasted_iota(jnp.int32, x.shape, 1)
    lo = lane < RW_N
    s0 = jnp.sum(jnp.where(lo, x, 0.0), axis=-1, keepdims=True)
    s1 = jnp.sum(jnp.where(lo, 0.0, x), axis=-1, keepdims=True)
    return jnp.where(lo, s0, s1)


def _stack2(x):
    lane = lax.broadcasted_iota(jnp.int32, x.shape, 1)
    lo = lane < x.shape[1] // 2
    return jnp.concatenate([jnp.where(lo, x, 0.0), jnp.where(lo, 0.0, x)], axis=0)


def _each(f, *lists):
    return [f(*t) for t in zip(*lists)]


def _packed_masks(C):
    t = lax.broadcasted_iota(jnp.int32, (C, 2 * C), 0)
    s = lax.broadcasted_iota(jnp.int32, (C, 2 * C), 1) % C
    return s < t, s <= t, s == t


def _unit_lower_inverse(Ls, eye):
    n = Ls[0].shape[0]
    invs = [jnp.where(eye, 1.0, L) for L in Ls]
    Lps = Ls
    span = 2
    while span < n:
        Lps = _each(lambda Lp: _dot(Lp, _stack2(Lp)), Lps)
        yield
        invs = _each(lambda inv, Lp: inv + _dot(inv, _stack2(Lp)), invs, Lps)
        yield
        span *= 2
    return invs


def _run_pipelined(part_a, part_b, batches):
    ctx = {}
    prev = None
    for batch in batches:
        gens = [part_a(batch, ctx)] + ([part_b(prev, ctx)] if prev is not None else [])
        while gens:
            for g in list(gens):
                try:
                    next(g)
                except StopIteration:
                    gens.remove(g)
        prev = batch
    for _ in part_b(prev, ctx):
        pass


def _rec_batches(bb, nchunk, gb, cpb):
    if nchunk > 1:
        return tuple(tuple((b, c) for b in range(bb) for c in range(c0, min(c0 + cpb, nchunk)))
                     for c0 in range(0, nchunk, cpb))
    return tuple(tuple((b, 0) for b in range(b0, b0 + gb)) for b0 in range(0, bb, gb))


def _rwkv_rec_kernel(r_ref, k_ref, v_ref, lw_ref, a_ref, g_ref, x_ref, s0_ref, kk_ref, ka_ref, rk_ref, gg_ref,
                     gb_ref, wout_ref, lng_ref, lnb_ref, out_ref, sT_ref, sscr, oscr, *, C, batches):
    tb = pl.program_id(1)
    bb, npair, Rt, _ = r_ref.shape
    C2 = 2 * C
    lane_sq = lax.broadcasted_iota(jnp.int32, (LANES, LANES), 1)
    row_sq = lax.broadcasted_iota(jnp.int32, (LANES, LANES), 0)
    same_head = (lane_sq < RW_N) == (row_sq < RW_N)

    @pl.when(tb == 0)
    def _init():
        def init_b(b, carry):
            for p in range(npair):
                s2 = s0_ref[b, p]
                sscr[b * npair + p] = jnp.where(same_head, jnp.concatenate([s2, s2], axis=1), 0.0)
            return carry
        lax.fori_loop(0, bb, init_b, 0)

    strict, incl, eye = _packed_masks(C)
    tri = (lax.broadcasted_iota(jnp.int32, (C, C), 1) <= lax.broadcasted_iota(jnp.int32, (C, C), 0)).astype(F32)

    def part_a(items, ctx):
        CH = [(b, p, ci) for b, ci in items for p in range(npair)]
        rs = lambda ci: pl.ds(ci * C, C)
        r = [r_ref[b, p, rs(ci), :] for b, p, ci in CH]
        k = [k_ref[b, p, rs(ci), :] for b, p, ci in CH]
        v = [v_ref[b, p, rs(ci), :] for b, p, ci in CH]
        lw = [lw_ref[b, p, rs(ci), :] for b, p, ci in CH]
        a = [a_ref[b, p, rs(ci), :] for b, p, ci in CH]
        kn = [k_ * kk_ref[p] for k_, (b, p, ci) in zip(k, CH)]
        kk = _each(lambda z: z * lax.rsqrt(_seg_sum(z * z) + RW_NORM_EPS), kn)
        kh = [k_ * (1.0 + (a_ - 1.0) * ka_ref[p]) for k_, a_, (b, p, ci) in zip(k, a, CH)]
        bvec = _each(lambda x, y: x * y, kk, a)
        cum = _each(lambda x: _dot_exact_lhs(tri, x), lw)
        yield
        cum_last = [c_[C - 1:C, :] for c_ in cum]
        e_neg = _each(lambda c_: jnp.exp(-c_), cum)
        e_dec = _each(lambda cl, c_: jnp.exp(cl - c_), cum_last, cum)
        At = _each(lambda kk_, c_, lw_: -kk_ * jnp.exp(c_ - lw_), kk, cum, lw)
        Rt_ = _each(lambda r_, c_: r_ * jnp.exp(c_), r, cum)
        AR = _each(lambda x, y: jnp.concatenate([x, y], axis=0), At, Rt_)
        BKs = _each(lambda b_, kh_, e: jnp.concatenate([_stack2(b_ * e), _stack2(kh_ * e)], axis=0), bvec, kh, e_neg)
        V2 = _each(_stack2, v)
        sc = _each(lambda x, y: _dot(x, y, _NT), AR, BKs)
        yield
        Lab = [jnp.where(strict, z[:C, :C2], 0.0) for z in sc]
        Lak = [jnp.where(strict, z[:C, C2:], 0.0) for z in sc]
        Mrbk = [jnp.concatenate([jnp.where(incl, z[C:, :C2], 0.0), jnp.where(incl, z[C:, C2:], 0.0)], axis=1)
                for z in sc]
        LV = _each(_dot, Lak, V2)
        Tinv = yield from _unit_lower_inverse(Lab, eye)
        for n, key in enumerate(CH):
            ctx[key] = dict(AR=AR[n], V2=V2[n], LV=LV[n], Tinv=Tinv[n], Mrbk=Mrbk[n], v=v[n], r=r[n], kh=kh[n],
                            sdec=jnp.exp(cum_last[n]), bkdec=jnp.concatenate([bvec[n] * e_dec[n], kh[n] * e_dec[n]], 0))

    def part_b(items, ctx):
        for ci in sorted({ci for _, ci in items}):
            CH = [(b, p, ci) for b, c_ in items if c_ == ci for p in range(npair)]
            X = [ctx.pop(key) for key in CH]
            S = [sscr[b * npair + p] for b, p, _ in CH]
            UY0 = _each(lambda x, s_: _dot(x["AR"], s_, _NT), X, S)
            yield
            U = _each(lambda x, u: _dot(x["Tinv"], _stack2(u[:C, :] + x["LV"])), X, UY0)
            yield
            y = _each(lambda x, u0, u: u0[C:, :] + _dot(x["Mrbk"], jnp.concatenate([_stack2(u), x["V2"]], axis=0)),
                      X, UY0, U)
            Snew = _each(lambda x, s_, u: s_ * x["sdec"] + jnp.where(
                same_head, _dot(jnp.concatenate([u, x["v"]], axis=0), x["bkdec"], _TN), 0.0), X, S, U)
            yield
            for n, (b, p, _) in enumerate(CH):
                sscr[b * npair + p] = Snew[n]
                m = _seg_sum(y[n]) * (1.0 / RW_N)
                yc = y[n] - m
                var = _seg_sum(yc * yc) * (1.0 / RW_N)
                yn = yc * lax.rsqrt(var + RW_GN_EPS) * gg_ref[p] + gb_ref[p]
                bonus = _seg_sum(X[n]["r"] * X[n]["kh"] * rk_ref[p]) * X[n]["v"]
                oscr[b * Rt + ci * C:b * Rt + (ci + 1) * C, LANES * p:LANES * (p + 1)] = yn + bonus
            yield

    _run_pipelined(part_a, part_b, batches)

    row_h = lax.broadcasted_iota(jnp.int32, (LANES, RW_N), 0) < RW_N

    def fin_b(b, carry):
        for p in range(npair):
            S = sscr[b * npair + p]
            sT_ref[b, p] = jnp.where(row_h, S[:, :RW_N], S[:, RW_N:])
        return carry
    lax.fori_loop(0, bb, fin_b, 0)
    _gated_out_ln(oscr[...], g_ref, x_ref, wout_ref, lng_ref, lnb_ref, out_ref)


def _rwkv_rec(r4, k4, v4, lw4, a4, g4, x, s0, k_k, k_a, r_k, gn_g, gn_b, wout, lng, lnb, *, C, Rt, bb, gb):
    B, npair, T, _ = r4.shape
    D = x.shape[-1]
    kern = functools.partial(_rwkv_rec_kernel, C=C, batches=_rec_batches(bb, Rt // C, gb, RW_CHUNKS_PER_BATCH))
    tspec = pl.BlockSpec((bb, npair, Rt, LANES), lambda b, t: (b, 0, t, 0))
    xspec = pl.BlockSpec((bb, Rt, D), lambda b, t: (b, t, 0))
    sspec = pl.BlockSpec((bb, npair, LANES, RW_N), lambda b, t: (b, 0, 0, 0))
    wspec = _const_spec((npair, 1, LANES))
    return pl.pallas_call(
        kern,
        grid=(B // bb, T // Rt),
        in_specs=[tspec] * 6 + [xspec, sspec] + [wspec] * 5 + [_const_spec(wout.shape), _const_spec(lng.shape),
                                                              _const_spec(lnb.shape)],
        out_specs=[xspec, sspec],
        out_shape=[jax.ShapeDtypeStruct(x.shape, F32), jax.ShapeDtypeStruct(s0.shape, F32)],
        scratch_shapes=[pltpu.VMEM((bb * npair, LANES, LANES), F32), pltpu.VMEM((bb * Rt, RW_W), F32)],
        compiler_params=_cparams(2),
        name="rwkv_rec",
    )(r4, k4, v4, lw4, a4, g4, x, s0, k_k, k_a, r_k, gn_g, gn_b, wout, lng, lnb)


def _rwkv_step_kernel(r_ref, k_ref, v_ref, lw_ref, a_ref, s0_ref, kk_ref, ka_ref, rk_ref, gg_ref, gb_ref,
                      o_ref, sT_ref, vscr, yscr, *, T, B):
    N = RW_N
    IB = SUBLANES
    for t in range(T):
        rows = pl.ds(t * B, B)
        rT, kT, vT, aT = r_ref[rows, :].T, k_ref[rows, :].T, v_ref[rows, :].T, a_ref[rows, :].T
        w = jnp.exp(lw_ref[rows, :].T)
        kn = kT * kk_ref[...]
        kh = kT * (1.0 + (aT - 1.0) * ka_ref[...])
        vscr[...] = vT
        src = s0_ref if t == 0 else sT_ref
        bonus = []
        for h in range(2):
            hs = slice(N * h, N * (h + 1))
            kk = kn[hs] * lax.rsqrt(jnp.sum(kn[hs] * kn[hs], axis=0, keepdims=True) + RW_NORM_EPS)
            a_h, b_h, k_h, w_h, r_h = -kk, kk * aT[hs], kh[hs], w[hs], rT[hs]

            def step(ib, carry, h=h, a_h=a_h, b_h=b_h, k_h=k_h, w_h=w_h, r_h=r_h, src=src):
                i0 = pl.multiple_of(ib * IB, IB)
                S = src[h, pl.ds(i0, IB), :, :]
                sa = jnp.sum(S * a_h[None], axis=1)
                vb = vscr[pl.ds(N * h + i0, IB), :]
                Sn = S * w_h[None] + sa[:, None, :] * b_h[None] + vb[:, None, :] * k_h[None]
                sT_ref[h, pl.ds(i0, IB), :, :] = Sn
                yscr[pl.ds(N * h + i0, IB), :] = jnp.sum(Sn * r_h[None], axis=1)
                return carry

            lax.fori_loop(0, N // IB, step, 0)
            bonus.append(jnp.sum(r_h * k_h * rk_ref[hs, :], axis=0, keepdims=True) * vT[hs])
        y = yscr[...]
        outs = []
        for h in range(2):
            hs = slice(N * h, N * (h + 1))
            m = jnp.mean(y[hs], axis=0, keepdims=True)
            yc = y[hs] - m
            var = jnp.mean(yc * yc, axis=0, keepdims=True)
            outs.append(yc * lax.rsqrt(var + RW_GN_EPS) * gg_ref[hs, :] + gb_ref[hs, :] + bonus[h])
        o_ref[rows, :] = jnp.concatenate(outs, axis=0).T


def _rwkv_step(r4, k4, v4, lw4, a4, s0, k_k, k_a, r_k, gn_g, gn_b, *, T, B):
    npair = r4.shape[1]
    kern = functools.partial(_rwkv_step_kernel, T=T, B=B)
    tspec = pl.BlockSpec((None, None, T * B, LANES), lambda p: (0, p, 0, 0))
    sspec = pl.BlockSpec((None, 2, RW_N, RW_N, B), lambda p: (p, 0, 0, 0, 0))
    wspec = pl.BlockSpec((None, LANES, 1), lambda p: (p, 0, 0))
    return pl.pallas_call(
        kern,
        grid=(npair,),
        in_specs=[tspec] * 5 + [sspec] + [wspec] * 5,
        out_specs=[tspec, sspec],
        out_shape=[jax.ShapeDtypeStruct(r4.shape, F32), jax.ShapeDtypeStruct(s0.shape, F32)],
        scratch_shapes=[pltpu.VMEM((LANES, B), F32), pltpu.VMEM((LANES, B), F32)],
        compiler_params=_cparams(1),
        name="rwkv_step",
    )(r4, k4, v4, lw4, a4, s0, k_k, k_a, r_k, gn_g, gn_b)


def _gdn_pre_kernel(x_ref, halo_ref, win_ref, cw_ref, alog_ref, dtb_ref,
                    q_ref, k_ref, v_ref, z_ref, bg_ref, tail_ref, xscr, gscr, *, s, R, H, Hg, C):
    c = pl.program_id(1)
    CH = GDN_CONV_CH
    tt = R // s

    @pl.when(c == 0)
    def _init():
        xscr[0:H, :] = halo_ref[...]
        gscr[0:Hg, :] = jnp.zeros((Hg, LANES), F32)

    x = x_ref[...]
    u = _mm(x, win_ref[...])
    xb = u[:, :CH]
    xscr[H:H + R, :] = xb
    y = xb * cw_ref[3]
    for d in (1, 2, 3):
        y = y + xscr[pl.ds(H - d * s, R), :] * cw_ref[3 - d]
    tail = xscr[pl.ds(R, H), :]
    xscr[0:H, :] = tail
    tail_ref[...] = tail
    qkv = _silu(y)

    def l2n(z, scale):
        return z * (lax.rsqrt(jnp.sum(z * z, axis=-1, keepdims=True) + GDN_EPS) * scale)

    for h in range(GDN_HK):
        q_ref[h] = l2n(qkv[:, LANES * h:LANES * (h + 1)], GDN_DK ** -0.5)
        k_ref[h] = l2n(qkv[:, GDN_KEY_W + LANES * h:GDN_KEY_W + LANES * (h + 1)], 1.0)
    for h in range(GDN_HV):
        v_ref[h] = qkv[:, 2 * GDN_KEY_W + LANES * h:2 * GDN_KEY_W + LANES * (h + 1)]
        z_ref[h] = u[:, CH + LANES * h:CH + LANES * (h + 1)]

    bg = u[:, CH + GDN_VAL_W:CH + GDN_VAL_W + LANES]
    beta = jax.nn.sigmoid(bg)
    g = -jnp.exp(alog_ref[...]) * jax.nn.softplus(bg + dtb_ref[...])
    t_in = (lax.broadcasted_iota(jnp.int32, (R, 1), 0) // s) % C
    k = 1
    while k < min(C, tt):
        gscr[Hg:Hg + R, :] = g
        g = g + jnp.where(t_in >= k, gscr[pl.ds(Hg - k * s, R), :], 0.0)
        k *= 2
    lane = lax.broadcasted_iota(jnp.int32, (R, LANES), 1)
    bg_ref[...] = jnp.where(lane < GDN_HV, beta, g)


def _gdn_pre(x3, halo, win, cw, alog, dtb, *, s, R, C):
    NB, TT, D = x3.shape
    H = halo.shape[1]
    tt = R // s
    Hg = max(SUBLANES, (min(C, tt) // 2) * s)
    kern = functools.partial(_gdn_pre_kernel, s=s, R=R, H=H, Hg=Hg, C=C)

    def ospec(nh):
        return pl.BlockSpec((None, nh, R, LANES), lambda i, c: (i, 0, c, 0))

    def oshape(nh):
        return jax.ShapeDtypeStruct((NB, nh, TT, LANES), F32)

    return pl.pallas_call(
        kern,
        grid=(NB, TT // R),
        in_specs=[
            pl.BlockSpec((None, R, D), lambda i, c: (i, c, 0)),
            pl.BlockSpec((None, H, GDN_CONV_CH), lambda i, c: (i, 0, 0)),
            _const_spec(win.shape), _const_spec(cw.shape), _const_spec(alog.shape), _const_spec(dtb.shape),
        ],
        out_specs=[ospec(GDN_HK), ospec(GDN_HK), ospec(GDN_HV), ospec(GDN_HV),
                   pl.BlockSpec((None, R, LANES), lambda i, c: (i, c, 0)),
                   pl.BlockSpec((None, H, GDN_CONV_CH), lambda i, c: (i, 0, 0))],
        out_shape=[oshape(GDN_HK), oshape(GDN_HK), oshape(GDN_HV), oshape(GDN_HV),
                   jax.ShapeDtypeStruct((NB, TT, LANES), F32),
                   jax.ShapeDtypeStruct((NB, H, GDN_CONV_CH), F32)],
        scratch_shapes=[pltpu.VMEM((H + R, GDN_CONV_CH), F32), pltpu.VMEM((Hg + R, LANES), F32)],
        compiler_params=_cparams(2),
        name="gdn_pre",
    )(x3, halo, win, cw, alog, dtb)


def _gdn_rec_kernel(q_ref, k_ref, v_ref, z_ref, bg_ref, x_ref, s0_ref, ng_ref, wout_ref, lng_ref, lnb_ref,
                    out_ref, sT_ref, sscr, oscr, *, C, batches):
    tb = pl.program_id(1)
    bb, nhv, Rt, _ = v_ref.shape
    rep = GDN_HV // GDN_HK

    @pl.when(tb == 0)
    def _init():
        sscr[...] = s0_ref[...].reshape(sscr.shape)

    strict, incl, eye = _packed_masks(C)
    lo = lax.broadcasted_iota(jnp.int32, (C, 2 * C), 1) < C

    def part_a(items, ctx):
        CH = [(b, h, ci) for b, ci in items for h in range(nhv)]
        KH = [(b, m, ci) for b, ci in items for m in range(GDN_HK)]
        rs = lambda ci: pl.ds(ci * C, C)
        bgs = {(b, ci): bg_ref[b, rs(ci), :] for b, ci in items}
        bgT = {key: x.T for key, x in bgs.items()}

        kq = [jnp.concatenate([k_ref[b, m, rs(ci), :], q_ref[b, m, rs(ci), :]], axis=0) for b, m, ci in KH]
        sc = _each(lambda x: _dot(x, jnp.concatenate([x[:C, :], x[:C, :]], axis=0), _NT), kq)
        yield

        def col2(x, lane):
            return jnp.where(lo, jnp.broadcast_to(x[:, lane:lane + 1], (C, 2 * C)),
                             jnp.broadcast_to(x[:, lane + 1:lane + 2], (C, 2 * C)))

        bcol = [col2(bgs[b, ci], rep * m) for b, m, ci in KH]
        gcol = [col2(bgs[b, ci], GDN_HV + rep * m) for b, m, ci in KH]
        grow = [jnp.concatenate([bgT[b, ci][GDN_HV + rep * m:GDN_HV + rep * m + 1, :],
                                 bgT[b, ci][GDN_HV + rep * m + 1:GDN_HV + rep * m + 2, :]], axis=1) for b, m, ci in KH]
        diff = _each(lambda c_, r_: c_ - r_, gcol, grow)
        Lp = _each(lambda z, bc, d: jnp.where(strict, z[:C, :] * bc * jnp.exp(jnp.where(strict, d, 0.0)), 0.0),
                   sc, bcol, diff)
        Ap = _each(lambda z, d: jnp.where(incl, z[C:, :] * jnp.exp(jnp.where(incl, d, 0.0)), 0.0), sc, diff)
        Tinv_p = yield from _unit_lower_inverse([-l_ for l_ in Lp], eye)

        half = lambda xs, n, h: xs[n // rep][:, (h % rep) * C:(h % rep + 1) * C]
        Tinv = [half(Tinv_p, n, h) for n, (b, h, ci) in enumerate(CH)]
        A = [half(Ap, n, h) for n, (b, h, ci) in enumerate(CH)]
        k = [kq[n // rep][:C, :] for n in range(len(CH))]
        q = [kq[n // rep][C:, :] for n in range(len(CH))]
        v = [v_ref[b, h, rs(ci), :] for b, h, ci in CH]
        beta = [jnp.broadcast_to(bgs[b, ci][:, h:h + 1], (C, LANES)) for b, h, ci in CH]
        gc = [jnp.broadcast_to(bgs[b, ci][:, GDN_HV + h:GDN_HV + h + 1], (C, LANES)) for b, h, ci in CH]
        kb = _each(lambda x, y: x * y, k, beta)
        eg = _each(jnp.exp, gc)
        UW = _each(lambda t, v_, b_, kb_, e: _dot(t, jnp.concatenate([v_ * b_, kb_ * e], axis=1)),
                   Tinv, v, beta, kb, eg)
        yield
        for n, key in enumerate(CH):
            g_last = gc[n][C - 1:C, :]
            ctx[key] = dict(U=UW[n][:, :GDN_DV], WQl=jnp.concatenate([UW[n][:, GDN_DV:], q[n] * eg[n]], axis=0),
                            A=A[n], kdec=k[n] * jnp.exp(g_last - gc[n]), sdec=jnp.exp(g_last))

    def part_b(items, ctx):
        for ci in sorted({ci for _, ci in items}):
            CH = [(b, h, ci) for b, c_ in items if c_ == ci for h in range(nhv)]
            X = [ctx.pop(key) for key in CH]
            S = [sscr[b * nhv + h] for b, h, _ in CH]
            WQ = _each(lambda x, s_: _dot(x["WQl"], s_), X, S)
            yield
            v_new = _each(lambda x, wq: x["U"] - wq[:C, :], X, WQ)
            o = _each(lambda x, wq, vn: wq[C:, :] + _dot(x["A"], vn), X, WQ, v_new)
            Snew = _each(lambda x, s_, vn: s_ * x["sdec"] + _dot(x["kdec"], vn, _TN), X, S, v_new)
            yield
            for n, (b, h, _) in enumerate(CH):
                sscr[b * nhv + h] = Snew[n]
                oh = o[n]
                oscr[b * Rt + ci * C:b * Rt + (ci + 1) * C, LANES * h:LANES * (h + 1)] = (
                    oh * lax.rsqrt(jnp.mean(oh * oh, axis=-1, keepdims=True) + GDN_EPS) * ng_ref[...])
            yield

    _run_pipelined(part_a, part_b, batches)
    sT_ref[...] = sscr[...].reshape(sT_ref.shape)
    _gated_out_ln(oscr[...], z_ref, x_ref, wout_ref, lng_ref, lnb_ref, out_ref)


def _gdn_rec(q4, k4, v4, z4, bg, x, s0, norm_g, wout, lng, lnb, *, C, Rt, bb, gb):
    B, _, T, _ = v4.shape
    D = x.shape[-1]
    kern = functools.partial(_gdn_rec_kernel, C=C, batches=_rec_batches(bb, Rt // C, gb, GDN_CHUNKS_PER_BATCH))

    def tspec(nh):
        return pl.BlockSpec((bb, nh, Rt, LANES), lambda b, t: (b, 0, t, 0))

    xspec = pl.BlockSpec((bb, Rt, D), lambda b, t: (b, t, 0))
    sspec = pl.BlockSpec((bb, GDN_HV, GDN_DK, GDN_DV), lambda b, t: (b, 0, 0, 0))
    return pl.pallas_call(
        kern,
        grid=(B // bb, T // Rt),
        in_specs=[tspec(GDN_HK), tspec(GDN_HK), tspec(GDN_HV), tspec(GDN_HV),
                  pl.BlockSpec((bb, Rt, LANES), lambda b, t: (b, t, 0)), xspec, sspec,
                  _const_spec(norm_g.shape), _const_spec(wout.shape), _const_spec(lng.shape), _const_spec(lnb.shape)],
        out_specs=[xspec, sspec],
        out_shape=[jax.ShapeDtypeStruct(x.shape, F32), jax.ShapeDtypeStruct(s0.shape, F32)],
        scratch_shapes=[pltpu.VMEM((bb * GDN_HV, GDN_DK, GDN_DV), F32), pltpu.VMEM((bb * Rt, GDN_VAL_W), F32)],
        compiler_params=_cparams(2),
        name="gdn_rec",
    )(q4, k4, v4, z4, bg, x, s0, norm_g, wout, lng, lnb)


class _Group:
    def __init__(self, B, T, time_major):
        self.B, self.T, self.time_major = B, T, time_major
        if time_major:
            self.s, self.NB, self.TT, self.R = B, 1, T * B, T * B
        else:
            self.s, self.NB, self.TT, self.R = 1, B, T, min(ROW_BLOCK, T)

    def to_rows(self, x):
        if self.time_major:
            return jnp.swapaxes(x, 0, 1).reshape(1, self.TT, x.shape[-1])
        return x

    def from_rows(self, x3):
        if self.time_major:
            return jnp.swapaxes(x3.reshape(self.T, self.B, x3.shape[-1]), 0, 1)
        return x3

    def halo(self, st, n_steps):
        if self.time_major:
            return jnp.swapaxes(st, 0, 1).reshape(1, n_steps * self.B, st.shape[-1])
        H = _halo_rows(n_steps, 1)
        return jnp.pad(st, ((0, 0), (H - n_steps, 0), (0, 0)))

    def unhalo(self, tail, n_steps):
        if self.time_major:
            return jnp.swapaxes(tail.reshape(n_steps, self.B, tail.shape[-1]), 0, 1)
        return tail[:, tail.shape[1] - n_steps:]

    def vec(self, st):
        return st[None] if self.time_major else st[:, None, :]

    def unvec(self, v):
        return v[0] if self.time_major else v[:, 0, :]

    def heads_to_batch(self, a4, Tpad):
        if not self.time_major:
            return a4
        nh = a4.shape[1]
        a = a4.reshape(nh, self.T, self.B, LANES).transpose(2, 0, 1, 3)
        return jnp.pad(a, ((0, 0), (0, 0), (0, Tpad - self.T), (0, 0)))

    def rows_to_batch(self, a3, Tpad):
        if not self.time_major:
            return a3
        return jnp.pad(self.from_rows(a3), ((0, 0), (0, Tpad - self.T), (0, 0)))

    def rows_from_batch(self, a3):
        if not self.time_major:
            return a3
        return self.to_rows(a3[:, :self.T])

    def rec_tiling(self, chunk, block):
        if self.time_major:
            Tp = -(-self.T // SUBLANES) * SUBLANES
            return Tp, Tp, Tp, min(self.B, SAMPLE_SEQ_BLOCK), min(self.B, SAMPLE_SEQ_GROUP)
        return min(chunk, self.T), self.T, min(block, self.T), PROMPT_SEQ_BLOCK if self.B % PROMPT_SEQ_BLOCK == 0 else 1, 1


def _lru_apply(g, x3, conv_st, h_st, p, lng, lnb):
    swap = not g.time_major
    assert g.B % SUBLANES == 0, "RG-LRU kernel needs the batch to fill whole sublane tiles"
    lay = _Group(g.B, g.T, True) if swap else g
    out, tail, hT = _lru_layer(x3, lay.halo(conv_st, CONV_W - 1), lay.vec(h_st), p["win"], p["cw"], p["cb"],
                               p["wg"], p["bg"], p["lam"], p["wout"], lng, lnb, s=lay.s,
                               R=min(LRU_BLOCK, g.T * g.B) if swap else g.R, swap=swap)
    return out, lay.unhalo(tail, CONV_W - 1), lay.unvec(hT)


def _rwkv_apply(g, x3, shift_st, wkv_st, p, lng, lnb):
    B = g.B
    pre = _rwkv_pre(x3, g.halo(shift_st[:, None, :], 1), p["mu"], p["win"], p["w0"], p["w1"], p["w2"], p["a0"],
                    p["a1"], p["a2"], s=g.s, R=g.R)
    new_shift = g.from_rows(x3)[:, -1]
    if g.time_major:
        r4, k4, v4, g4, lw4, a4 = pre
        s0 = jnp.transpose(wkv_st, (1, 2, 3, 0)).reshape(RW_H // 2, 2, RW_N, RW_N, B)
        col = lambda v: jnp.swapaxes(v, 1, 2)
        o4, sT = _rwkv_step(r4, k4, v4, lw4, a4, s0, col(p["k_k"]), col(p["k_a"]), col(p["r_k"]), col(p["gn_g"]),
                            col(p["gn_b"]), T=g.T, B=B)
        out = _post_layer(x3, o4, g4, p["wout"], lng, lnb)
        return out, new_shift, jnp.transpose(sT.reshape(RW_H, RW_N, RW_N, B), (3, 0, 1, 2))
    C, Tp, Rt, bb, gb = g.rec_tiling(RW_CHUNK, RW_BLOCK)
    r4, k4, v4, g4, lw4, a4 = [g.heads_to_batch(a, Tp) for a in pre]
    s0 = wkv_st.reshape(B, RW_H // 2, 2 * RW_N, RW_N)
    out, sT = _rwkv_rec(r4, k4, v4, lw4, a4, g4, g.rows_to_batch(x3, Tp), s0, p["k_k"], p["k_a"], p["r_k"],
                        p["gn_g"], p["gn_b"], p["wout"], lng, lnb, C=C, Rt=Rt, bb=bb, gb=gb)
    new_shift = g.from_rows(x3)[:, -1]
    return g.rows_from_batch(out), new_shift, sT.reshape(B, RW_H, RW_N, RW_N)


def _gdn_apply(g, x3, conv_st, S_st, p, lng, lnb):
    T = g.T
    C, Tp, Rt, bb, gb = g.rec_tiling(GDN_CHUNK, GDN_BLOCK)
    q4, k4, v4, z4, bg, tail = _gdn_pre(x3, g.halo(conv_st, CONV_W - 1), p["win"], p["cw"], p["alog"], p["dtb"],
                                        s=g.s, R=g.R, C=min(GDN_CHUNK, T))
    bgb = g.rows_to_batch(bg, T)
    if Tp != T:
        held = jnp.where(jnp.arange(LANES) < GDN_HV, 0.0, bgb[:, T - 1:T, :])
        bgb = jnp.concatenate([bgb, jnp.broadcast_to(held, (g.B, Tp - T, LANES))], axis=1)
    tb = lambda a: g.heads_to_batch(a, Tp)
    out, sT = _gdn_rec(tb(q4), tb(k4), tb(v4), tb(z4), bgb, g.rows_to_batch(x3, Tp), S_st, p["norm_g"], p["wout"],
                       lng, lnb, C=C, Rt=Rt, bb=bb, gb=gb)
    return g.rows_from_batch(out), g.unhalo(tail, CONV_W - 1), sT


def _trunk(g, x, st, params, ln_g, ln_b):
    lru_conv, lru_h, rw_shift, rw_S, gdn_conv, gdn_S = st
    new = ([], [], [], [], [], [])
    x3 = g.to_rows(x)
    ia = ib = ic = 0
    for layer in range(DEPTH):
        kind = layer % N_MIXERS
        lng, lnb = ln_g[layer][None, :], ln_b[layer][None, :]
        if kind == 0:
            x3, c, h = _lru_apply(g, x3, lru_conv[ia], lru_h[ia], params["lru"][ia], lng, lnb)
            new[0].append(c)
            new[1].append(h)
            ia += 1
        elif kind == 1:
            x3, sh, S = _rwkv_apply(g, x3, rw_shift[ib], rw_S[ib], params["rwkv"][ib], lng, lnb)
            new[2].append(sh)
            new[3].append(S)
            ib += 1
        else:
            x3, c, S = _gdn_apply(g, x3, gdn_conv[ic], gdn_S[ic], params["gdn"][ic], lng, lnb)
            new[4].append(c)
            new[5].append(S)
            ic += 1
    return g.from_rows(x3), tuple(s[0][None] if len(s) == 1 else jnp.stack(s) for s in new)


def _prep_params(lru_w_in, lru_conv_w, lru_conv_b, lru_wa, lru_ba, lru_wx, lru_bx, lru_lambda, lru_w_out, rw_mu,
                 rw_w_in, rw_w0, rw_w1, rw_w2, rw_a0, rw_a1, rw_a2, rw_k_k, rw_k_a, rw_r_k, rw_gn_g, rw_gn_b,
                 rw_w_out, gdn_w_in, gdn_conv_w, gdn_a_log, gdn_dt_bias, gdn_norm_g, gdn_w_out):
    row = lambda v: v[None, :]
    lru = []
    for n in range(lru_w_in.shape[0]):
        lru.append(dict(win=lru_w_in[n].astype(BF), cw=lru_conv_w[n][:, None, :], cb=row(lru_conv_b[n]),
                        wg=_lru_gate_weights(lru_wa[n], lru_wx[n]), bg=jnp.stack([lru_ba[n], lru_bx[n]])[:, None, :],
                        lam=row(lru_lambda[n]), wout=lru_w_out[n].astype(BF)))
    pairw = lambda v: v.reshape(RW_W // LANES, 1, LANES)
    rwkv = []
    for n in range(rw_w_in.shape[0]):
        rwkv.append(dict(mu=rw_mu[n][:, None, :],win=rw_w_in[n].astype(BF), w0=row(rw_w0[n]), w1=rw_w1[n].astype(BF),
                         w2=rw_w2[n].astype(BF), a0=row(rw_a0[n]), a1=rw_a1[n].astype(BF), a2=rw_a2[n].astype(BF),
                         k_k=pairw(rw_k_k[n]), k_a=pairw(rw_k_a[n]), r_k=pairw(rw_r_k[n]), gn_g=pairw(rw_gn_g[n]),
                         gn_b=pairw(rw_gn_b[n]), wout=rw_w_out[n].astype(BF)))
    gdn = []
    for n in range(gdn_w_in.shape[0]):
        w = gdn_w_in[n]
        o2 = GDN_CONV_CH + GDN_VAL_W
        wpad = jnp.pad(w[:, o2:], ((0, 0), (0, LANES - 2 * GDN_HV)))
        lanes = lambda v: jnp.pad(v, (GDN_HV, LANES - 2 * GDN_HV))[None, :]
        gdn.append(dict(win=jnp.concatenate([w[:, :o2], wpad], axis=1).astype(BF), cw=gdn_conv_w[n][:, None, :],
                        alog=lanes(gdn_a_log[n]), dtb=lanes(gdn_dt_bias[n]), norm_g=row(gdn_norm_g[n]),
                        wout=gdn_w_out[n].astype(BF)))
    return dict(lru=lru, rwkv=rwkv, gdn=gdn)


def kernel(x_prompt, x_sample, state_lru_conv, state_lru_h, state_rwkv_shift, state_rwkv_wkv, state_gdn_conv, state_gdn_S, ln_g, ln_b, lru_w_in, lru_conv_w, lru_conv_b, lru_wa, lru_ba, lru_wx, lru_bx, lru_lambda, lru_w_out, rw_mu, rw_w_in, rw_w0, rw_w1, rw_w2, rw_a0, rw_a1, rw_a2, rw_k_k, rw_k_a, rw_r_k, rw_gn_g, rw_gn_b, rw_w_out, gdn_w_in, gdn_conv_w, gdn_a_log, gdn_dt_bias, gdn_norm_g, gdn_w_out):
    params = _prep_params(lru_w_in, lru_conv_w, lru_conv_b, lru_wa, lru_ba, lru_wx, lru_bx, lru_lambda, lru_w_out,
                          rw_mu, rw_w_in, rw_w0, rw_w1, rw_w2, rw_a0, rw_a1, rw_a2, rw_k_k, rw_k_a, rw_r_k, rw_gn_g,
                          rw_gn_b, rw_w_out, gdn_w_in, gdn_conv_w, gdn_a_log, gdn_dt_bias, gdn_norm_g, gdn_w_out)
    bp, tp, _ = x_prompt.shape
    bs, ts, _ = x_sample.shape
    n_a, n_b, n_c = state_lru_conv.shape[0], state_rwkv_shift.shape[0], state_gdn_conv.shape[0]
    zero_state = (jnp.zeros((n_a, bp, CONV_W - 1, LRU_W), F32),
                  jnp.zeros((n_a, bp, LRU_W), F32),
                  jnp.zeros((n_b, bp, D_MODEL), F32),
                  jnp.zeros((n_b, bp, RW_H, RW_N, RW_N), F32),
                  jnp.zeros((n_c, bp, CONV_W - 1, GDN_CONV_CH), F32),
                  jnp.zeros((n_c, bp, GDN_HV, GDN_DK, GDN_DV), F32))
    y_prompt, sp = _trunk(_Group(bp, tp, False), x_prompt, zero_state, params, ln_g, ln_b)
    y_sample, ss = _trunk(_Group(bs, ts, True), x_sample,
                          (state_lru_conv, state_lru_h, state_rwkv_shift, state_rwkv_wkv, state_gdn_conv,
                           state_gdn_S), params, ln_g, ln_b)
    return (y_prompt, y_sample, sp[0], ss[0], sp[1], ss[1], sp[2], ss[2], sp[3], ss[3], sp[4], ss[4], sp[5], ss[5])
```

```python
import functools
import math

import jax
import jax.numpy as jnp
from jax import lax
from jax.experimental import pallas as pl
from jax.experimental.pallas import tpu as pltpu

F32 = jnp.float32
BF = jnp.bfloat16

D_MODEL = 1024
DEPTH = 4
N_MIXERS = 3
DN_ALPHA = (2.0 * DEPTH) ** 0.25
LN_EPS = 1e-5
CONV_W = 4

LRU_W = D_MODEL
LRU_BLOCKS = 16
LRU_BS = LRU_W // LRU_BLOCKS
LRU_C = 8.0

RW_W = D_MODEL
RW_N = 64
RW_H = RW_W // RW_N
RW_GN_EPS = 64e-5
RW_NORM_EPS = 1e-12

GDN_HK = 4
GDN_HV = 8
GDN_DK = 128
GDN_DV = 128
GDN_KEY_W = GDN_HK * GDN_DK
GDN_VAL_W = GDN_HV * GDN_DV
GDN_CONV_CH = 2 * GDN_KEY_W + GDN_VAL_W
GDN_CHUNK = 64
GDN_EPS = 1e-6

LANES = 128
SUBLANES = 8
VMEM_LIMIT = 56 * 1024 * 1024
ROW_BLOCK = 512
LRU_BLOCK = 1024
RW_BLOCK = 256
GDN_BLOCK = 256
RW_CHUNKS_PER_BATCH = 1
GDN_CHUNKS_PER_BATCH = 2
PROMPT_SEQ_BLOCK = 2
SAMPLE_SEQ_BLOCK = 8
SAMPLE_SEQ_GROUP = 4
RW_CHUNK = 64

_NN = (((1,), (0,)), ((), ()))
_NT = (((1,), (1,)), ((), ()))
_TN = (((0,), (0,)), ((), ()))


def _mm(a, b):
    return jnp.dot(a.astype(BF), b.astype(BF), preferred_element_type=F32)


def _dot(a, b, dims=_NN):
    return lax.dot_general(a.astype(BF), b.astype(BF), dims, preferred_element_type=F32)


def _dot_exact_lhs(a01, b):
    a = a01.astype(BF)
    b1 = b.astype(BF)
    r1 = b - b1.astype(F32)
    b2 = r1.astype(BF)
    b3 = (r1 - b2.astype(F32)).astype(BF)
    d = lambda q: jnp.dot(a, q, preferred_element_type=F32)
    return d(b1) + (d(b2) + d(b3))


def _layer_norm(z, g, b):
    mu = jnp.mean(z, axis=-1, keepdims=True)
    zc = z - mu
    var = jnp.mean(zc * zc, axis=-1, keepdims=True)
    return zc * lax.rsqrt(var + LN_EPS) * g + b


def _silu(x):
    return x * jax.nn.sigmoid(x)


def _cparams(n_axes):
    return pltpu.CompilerParams(dimension_semantics=("arbitrary",) * n_axes, vmem_limit_bytes=VMEM_LIMIT)


def _const_spec(shape):
    nd = len(shape)
    return pl.BlockSpec(shape, lambda *_: (0,) * nd, pipeline_mode=pl.Buffered(1))


def _halo_rows(n_steps, s):
    rows = n_steps * s
    return rows if rows % SUBLANES == 0 else SUBLANES


def _lru_kernel(x_ref, halo_ref, h0_ref, win_ref, cw_ref, cb_ref, wg_ref, bg_ref, lam_ref, wout_ref, lng_ref,
                lnb_ref, out_ref, tail_ref, hT_ref, xscr, hscr, *, s, R, H, swap):
    c = pl.program_id(1)
    C = LRU_W
    tt = R // s

    @pl.when(c == 0)
    def _init():
        xscr[0:H, :] = halo_ref[...]
        hscr[...] = h0_ref[...]

    x = x_ref[...]
    if swap:
        x = jnp.swapaxes(x, 0, 1).reshape(R, D_MODEL)
    u = _mm(x, win_ref[...])
    xb = u[:, :C]
    gate = u[:, C:]

    xscr[H:H + R, :] = xb
    xc = xb * cw_ref[3] + cb_ref[...]
    for d in (1, 2, 3):
        xc = xc + xscr[pl.ds(H - d * s, R), :] * cw_ref[3 - d]
    tail = xscr[pl.ds(R, H), :]
    xscr[0:H, :] = tail
    tail_ref[...] = tail

    xcb = xc.astype(BF)
    ra, ix = [], []
    for g in range(C // 256):
        gt = jnp.dot(xcb[:, 256 * g:256 * (g + 1)], wg_ref[g], preferred_element_type=F32)
        ra.append(gt[:, :256])
        ix.append(gt[:, 256:])
    r = jax.nn.sigmoid(jnp.concatenate(ra, axis=1) + bg_ref[0])
    i = jax.nn.sigmoid(jnp.concatenate(ix, axis=1) + bg_ref[1])
    log_a = (-LRU_C) * r * jax.nn.softplus(-lam_ref[...])
    a = jnp.exp(log_a)
    th = jnp.tanh(log_a)
    b = jnp.sqrt(-2.0 * th / (1.0 - th)) * i * xc

    hlast = hscr[...]
    hs = []
    for t in range(tt):
        hlast = a[t * s:(t + 1) * s, :] * hlast + b[t * s:(t + 1) * s, :]
        hs.append(hlast)
    h = jnp.concatenate(hs, axis=0)
    hscr[...] = hlast
    hT_ref[...] = hlast

    y = _mm(h * _silu(gate), wout_ref[...])
    out = _layer_norm(DN_ALPHA * x + y, lng_ref[...], lnb_ref[...])
    if swap:
        out = jnp.swapaxes(out.reshape(tt, s, D_MODEL), 0, 1)
    out_ref[...] = out


def _lru_layer(x3, halo, h0, win, cw, cb, wg, bg, lam, wout, lng, lnb, *, s, R, swap):
    C = LRU_W
    D = x3.shape[-1]
    H = halo.shape[1]
    tt = R // s
    assert s % SUBLANES == 0
    if swap:
        NB, TT = 1, x3.shape[1] * s
        xspec = pl.BlockSpec((s, tt, D), lambda i, c: (0, c, 0))
    else:
        NB, TT = x3.shape[:2]
        xspec = pl.BlockSpec((None, R, D), lambda i, c: (i, c, 0))
    kern = functools.partial(_lru_kernel, s=s, R=R, H=H, swap=swap)
    return pl.pallas_call(
        kern,
        grid=(NB, TT // R),
        in_specs=[
            xspec,
            pl.BlockSpec((None, H, C), lambda i, c: (i, 0, 0)),
            pl.BlockSpec((None, s, C), lambda i, c: (i, 0, 0)),
            _const_spec(win.shape), _const_spec(cw.shape), _const_spec(cb.shape), _const_spec(wg.shape),
            _const_spec(bg.shape), _const_spec(lam.shape), _const_spec(wout.shape), _const_spec(lng.shape),
            _const_spec(lnb.shape),
        ],
        out_specs=[
            xspec,
            pl.BlockSpec((None, H, C), lambda i, c: (i, 0, 0)),
            pl.BlockSpec((None, s, C), lambda i, c: (i, 0, 0)),
        ],
        out_shape=[
            jax.ShapeDtypeStruct(x3.shape, F32),
            jax.ShapeDtypeStruct((NB, H, C), F32),
            jax.ShapeDtypeStruct((NB, s, C), F32),
        ],
        scratch_shapes=[
            pltpu.VMEM((H + R, C), F32),
            pltpu.VMEM((s, C), F32),
        ],
        compiler_params=_cparams(2),
        name="lru_layer",
    )(x3, halo, h0, win, cw, cb, wg, bg, lam, wout, lng, lnb)


def _lru_gate_weights(wa, wx):
    def bd(w):
        w4 = w.reshape(4, 4, LRU_BS, LRU_BS)
        eye = jnp.eye(4, dtype=w.dtype)
        return jnp.einsum("gaij,ab->gaibj", w4, eye).reshape(4, 256, 256)
    return jnp.concatenate([bd(wa), bd(wx)], axis=2).astype(BF)


def _gated_out_ln(o, g_ref, x_ref, w_ref, lng_ref, lnb_ref, out_ref):
    bb, nh, Rt, _ = g_ref.shape
    rows = bb * Rt
    g = jnp.concatenate([g_ref[:, p, :, :].reshape(rows, LANES) for p in range(nh)], axis=1)
    y = _mm(o * _silu(g), w_ref[...])
    x = x_ref[...].reshape(rows, D_MODEL)
    out_ref[...] = _layer_norm(DN_ALPHA * x + y, lng_ref[...], lnb_ref[...]).reshape(bb, Rt, D_MODEL)


def _post_kernel(x_ref, o_ref, g_ref, w_ref, lng_ref, lnb_ref, out_ref):
    o = jnp.concatenate([o_ref[0, p] for p in range(o_ref.shape[1])], axis=1)
    _gated_out_ln(o, g_ref, x_ref, w_ref, lng_ref, lnb_ref, out_ref)


def _post_layer(x3, o4, g4, w, lng, lnb):
    return pl.pallas_call(
        _post_kernel,
        out_shape=jax.ShapeDtypeStruct(x3.shape, F32),
        compiler_params=pltpu.CompilerParams(vmem_limit_bytes=VMEM_LIMIT),
        name="post_layer",
    )(x3, o4, g4, w, lng, lnb)


def _rwkv_pre_kernel(x_ref, halo_ref, mu_ref, win_ref, w0_ref, w1_ref, w2_ref, a0_ref, a1_ref, a2_ref,
                     r_ref, k_ref, v_ref, g_ref, lw_ref, a_ref, xscr, xxscr, *, s, R, H):
    c = pl.program_id(1)

    @pl.when(c == 0)
    def _init():
        xscr[0:H, :] = halo_ref[...]

    x = x_ref[...]
    xscr[H:H + R, :] = x
    xxscr[...] = xscr[pl.ds(H - s, R), :] - x
    xscr[0:H, :] = xscr[pl.ds(R, H), :]
    xx = xxscr[...]
    xm = lambda n: x + xx * mu_ref[n]

    def put(ref, val):
        for p in range(RW_W // LANES):
            ref[p] = val[:, LANES * p:LANES * (p + 1)]

    put(r_ref, _mm(xm(0), win_ref[0]))
    put(k_ref, _mm(xm(1), win_ref[1]))
    put(v_ref, _mm(xm(2), win_ref[2]))
    put(g_ref, _mm(xm(3), win_ref[3]))
    w_raw = w0_ref[...] + _mm(jnp.tanh(_mm(xm(4), w1_ref[...])), w2_ref[...])
    put(lw_ref, (-math.exp(-0.5)) * jax.nn.sigmoid(w_raw))
    put(a_ref, jax.nn.sigmoid(a0_ref[...] + _mm(_mm(xm(5), a1_ref[...]), a2_ref[...])))


def _rwkv_pre(x3, halo, mu, win, w0, w1, w2, a0, a1, a2, *, s, R):
    NB, TT, D = x3.shape
    H = halo.shape[1]
    nh = RW_W // LANES
    kern = functools.partial(_rwkv_pre_kernel, s=s, R=R, H=H)
    ospec = pl.BlockSpec((None, nh, R, LANES), lambda i, c: (i, 0, c, 0))
    oshape = jax.ShapeDtypeStruct((NB, nh, TT, LANES), F32)
    return pl.pallas_call(
        kern,
        grid=(NB, TT // R),
        in_specs=[
            pl.BlockSpec((None, R, D), lambda i, c: (i, c, 0)),
            pl.BlockSpec((None, H, D), lambda i, c: (i, 0, 0)),
            _const_spec(mu.shape), _const_spec(win.shape), _const_spec(w0.shape), _const_spec(w1.shape),
            _const_spec(w2.shape), _const_spec(a0.shape), _const_spec(a1.shape), _const_spec(a2.shape),
        ],
        out_specs=[ospec] * 6,
        out_shape=[oshape] * 6,
        scratch_shapes=[pltpu.VMEM((H + R, D), F32), pltpu.VMEM((R, D), F32)],
        compiler_params=_cparams(2),
        name="rwkv_pre",
    )(x3, halo, mu, win, w0, w1, w2, a0, a1, a2)


def _seg_sum(x):
    lane = lax.broadcasted_iota(jnp.int32, x.shape, 1)
    lo = lane < RW_N
    s0 = jnp.sum(jnp.where(lo, x, 0.0), axis=-1, keepdims=True)
    s1 = jnp.sum(jnp.where(lo, 0.0, x), axis=-1, keepdims=True)
    return jnp.where(lo, s0, s1)


def _stack2(x):
    lane = lax.broadcasted_iota(jnp.int32, x.shape, 1)
    lo = lane < x.shape[1] // 2
    return jnp.concatenate([jnp.where(lo, x, 0.0), jnp.where(lo, 0.0, x)], axis=0)


def _each(f, *lists):
    return [f(*t) for t in zip(*lists)]


def _packed_masks(C):
    t = lax.broadcasted_iota(jnp.int32, (C, 2 * C), 0)
    s = lax.broadcasted_iota(jnp.int32, (C, 2 * C), 1) % C
    return s < t, s <= t, s == t


def _unit_lower_inverse(Ls, eye):
    n = Ls[0].shape[0]
    invs = [jnp.where(eye, 1.0, L) for L in Ls]
    Lps = Ls
    span = 2
    while span < n:
        Lps = _each(lambda Lp: _dot(Lp, _stack2(Lp)), Lps)
        yield
        invs = _each(lambda inv, Lp: inv + _dot(inv, _stack2(Lp)), invs, Lps)
        yield
        span *= 2
    return invs


def _run_pipelined(part_a, part_b, batches):
    ctx = {}
    prev = None
    for batch in batches:
        gens = [part_a(batch, ctx)] + ([part_b(prev, ctx)] if prev is not None else [])
        while gens:
            for g in list(gens):
                try:
                    next(g)
                except StopIteration:
                    gens.remove(g)
        prev = batch
    for _ in part_b(prev, ctx):
        pass


def _rec_batches(bb, nchunk, gb, cpb):
    if nchunk > 1:
        return tuple(tuple((b, c) for b in range(bb) for c in range(c0, min(c0 + cpb, nchunk)))
                     for c0 in range(0, nchunk, cpb))
    return tuple(tuple((b, 0) for b in range(b0, b0 + gb)) for b0 in range(0, bb, gb))


def _rwkv_rec_kernel(r_ref, k_ref, v_ref, lw_ref, a_ref, g_ref, x_ref, s0_ref, kk_ref, ka_ref, rk_ref, gg_ref,
                     gb_ref, wout_ref, lng_ref, lnb_ref, out_ref, sT_ref, sscr, oscr, *, C, batches):
    tb = pl.program_id(1)
    bb, npair, Rt, _ = r_ref.shape
    C2 = 2 * C
    lane_sq = lax.broadcasted_iota(jnp.int32, (LANES, LANES), 1)
    row_sq = lax.broadcasted_iota(jnp.int32, (LANES, LANES), 0)
    same_head = (lane_sq < RW_N) == (row_sq < RW_N)

    @pl.when(tb == 0)
    def _init():
        def init_b(b, carry):
            for p in range(npair):
                s2 = s0_ref[b, p]
                sscr[b * npair + p] = jnp.where(same_head, jnp.concatenate([s2, s2], axis=1), 0.0)
            return carry
        lax.fori_loop(0, bb, init_b, 0)

    strict, incl, eye = _packed_masks(C)
    tri = (lax.broadcasted_iota(jnp.int32, (C, C), 1) <= lax.broadcasted_iota(jnp.int32, (C, C), 0)).astype(F32)

    def part_a(items, ctx):
        CH = [(b, p, ci) for b, ci in items for p in range(npair)]
        rs = lambda ci: pl.ds(ci * C, C)
        r = [r_ref[b, p, rs(ci), :] for b, p, ci in CH]
        k = [k_ref[b, p, rs(ci), :] for b, p, ci in CH]
        v = [v_ref[b, p, rs(ci), :] for b, p, ci in CH]
        lw = [lw_ref[b, p, rs(ci), :] for b, p, ci in CH]
        a = [a_ref[b, p, rs(ci), :] for b, p, ci in CH]
        kn = [k_ * kk_ref[p] for k_, (b, p, ci) in zip(k, CH)]
        kk = _each(lambda z: z * lax.rsqrt(_seg_sum(z * z) + RW_NORM_EPS), kn)
        kh = [k_ * (1.0 + (a_ - 1.0) * ka_ref[p]) for k_, a_, (b, p, ci) in zip(k, a, CH)]
        bvec = _each(lambda x, y: x * y, kk, a)
        cum = _each(lambda x: _dot_exact_lhs(tri, x), lw)
        yield
        cum_last = [c_[C - 1:C, :] for c_ in cum]
        e_neg = _each(lambda c_: jnp.exp(-c_), cum)
        e_dec = _each(lambda cl, c_: jnp.exp(cl - c_), cum_last, cum)
        At = _each(lambda kk_, c_, lw_: -kk_ * jnp.exp(c_ - lw_), kk, cum, lw)
        Rt_ = _each(lambda r_, c_: r_ * jnp.exp(c_), r, cum)
        AR = _each(lambda x, y: jnp.concatenate([x, y], axis=0), At, Rt_)
        BKs = _each(lambda b_, kh_, e: jnp.concatenate([_stack2(b_ * e), _stack2(kh_ * e)], axis=0), bvec, kh, e_neg)
        V2 = _each(_stack2, v)
        sc = _each(lambda x, y: _dot(x, y, _NT), AR, BKs)
        yield
        Lab = [jnp.where(strict, z[:C, :C2], 0.0) for z in sc]
        Lak = [jnp.where(strict, z[:C, C2:], 0.0) for z in sc]
        Mrbk = [jnp.concatenate([jnp.where(incl, z[C:, :C2], 0.0), jnp.where(incl, z[C:, C2:], 0.0)], axis=1)
                for z in sc]
        LV = _each(_dot, Lak, V2)
        Tinv = yield from _unit_lower_inverse(Lab, eye)
        for n, key in enumerate(CH):
            ctx[key] = dict(AR=AR[n], V2=V2[n], LV=LV[n], Tinv=Tinv[n], Mrbk=Mrbk[n], v=v[n], r=r[n], kh=kh[n],
                            sdec=jnp.exp(cum_last[n]), bkdec=jnp.concatenate([bvec[n] * e_dec[n], kh[n] * e_dec[n]], 0))

    def part_b(items, ctx):
        for ci in sorted({ci for _, ci in items}):
            CH = [(b, p, ci) for b, c_ in items if c_ == ci for p in range(npair)]
            X = [ctx.pop(key) for key in CH]
            S = [sscr[b * npair + p] for b, p, _ in CH]
            UY0 = _each(lambda x, s_: _dot(x["AR"], s_, _NT), X, S)
            yield
            U = _each(lambda x, u: _dot(x["Tinv"], _stack2(u[:C, :] + x["LV"])), X, UY0)
            yield
            y = _each(lambda x, u0, u: u0[C:, :] + _dot(x["Mrbk"], jnp.concatenate([_stack2(u), x["V2"]], axis=0)),
                      X, UY0, U)
            Snew = _each(lambda x, s_, u: s_ * x["sdec"] + jnp.where(
                same_head, _dot(jnp.concatenate([u, x["v"]], axis=0), x["bkdec"], _TN), 0.0), X, S, U)
            yield
            for n, (b, p, _) in enumerate(CH):
                sscr[b * npair + p] = Snew[n]
                m = _seg_sum(y[n]) * (1.0 / RW_N)
                yc = y[n] - m
                var = _seg_sum(yc * yc) * (1.0 / RW_N)
                yn = yc * lax.rsqrt(var + RW_GN_EPS) * gg_ref[p] + gb_ref[p]
                bonus = _seg_sum(X[n]["r"] * X[n]["kh"] * rk_ref[p]) * X[n]["v"]
                oscr[b * Rt + ci * C:b * Rt + (ci + 1) * C, LANES * p:LANES * (p + 1)] = yn + bonus
            yield

    _run_pipelined(part_a, part_b, batches)

    row_h = lax.broadcasted_iota(jnp.int32, (LANES, RW_N), 0) < RW_N

    @pl.when(tb == pl.num_programs(1) - 1)
    def _final_state():
        def fin_b(b, carry):
            for p in range(npair):
                S = sscr[b * npair + p]
                sT_ref[b, p] = jnp.where(row_h, S[:, :RW_N], S[:, RW_N:])
            return carry
        lax.fori_loop(0, bb, fin_b, 0)

    _gated_out_ln(oscr[...], g_ref, x_ref, wout_ref, lng_ref, lnb_ref, out_ref)


def _rwkv_rec(r4, k4, v4, lw4, a4, g4, x, s0, k_k, k_a, r_k, gn_g, gn_b, wout, lng, lnb, *, C, Rt, bb, gb):
    B, npair, T, _ = r4.shape
    D = x.shape[-1]
    kern = functools.partial(_rwkv_rec_kernel, C=C, batches=_rec_batches(bb, Rt // C, gb, RW_CHUNKS_PER_BATCH))
    tspec = pl.BlockSpec((bb, npair, Rt, LANES), lambda b, t: (b, 0, t, 0))
    xspec = pl.BlockSpec((bb, Rt, D), lambda b, t: (b, t, 0))
    sspec = pl.BlockSpec((bb, npair, LANES, RW_N), lambda b, t: (b, 0, 0, 0))
    wspec = _const_spec((npair, 1, LANES))
    return pl.pallas_call(
        kern,
        grid=(B // bb, T // Rt),
        in_specs=[tspec] * 6 + [xspec, sspec] + [wspec] * 5 + [_const_spec(wout.shape), _const_spec(lng.shape),
                                                              _const_spec(lnb.shape)],
        out_specs=[xspec, sspec],
        out_shape=[jax.ShapeDtypeStruct(x.shape, F32), jax.ShapeDtypeStruct(s0.shape, F32)],
        scratch_shapes=[pltpu.VMEM((bb * npair, LANES, LANES), F32), pltpu.VMEM((bb * Rt, RW_W), F32)],
        compiler_params=_cparams(2),
        name="rwkv_rec",
    )(r4, k4, v4, lw4, a4, g4, x, s0, k_k, k_a, r_k, gn_g, gn_b, wout, lng, lnb)


def _rwkv_step_kernel(r_ref, k_ref, v_ref, lw_ref, a_ref, s0_ref, kk_ref, ka_ref, rk_ref, gg_ref, gb_ref,
                      o_ref, sT_ref, vscr, yscr, *, T, B):
    N = RW_N
    IB = SUBLANES
    for t in range(T):
        rows = pl.ds(t * B, B)
        rT, kT, vT, aT = r_ref[rows, :].T, k_ref[rows, :].T, v_ref[rows, :].T, a_ref[rows, :].T
        w = jnp.exp(lw_ref[rows, :].T)
        kn = kT * kk_ref[...]
        kh = kT * (1.0 + (aT - 1.0) * ka_ref[...])
        vscr[...] = vT
        src = s0_ref if t == 0 else sT_ref
        bonus = []
        for h in range(2):
            hs = slice(N * h, N * (h + 1))
            kk = kn[hs] * lax.rsqrt(jnp.sum(kn[hs] * kn[hs], axis=0, keepdims=True) + RW_NORM_EPS)
            a_h, b_h, k_h, w_h, r_h = -kk, kk * aT[hs], kh[hs], w[hs], rT[hs]

            def step(ib, carry, h=h, a_h=a_h, b_h=b_h, k_h=k_h, w_h=w_h, r_h=r_h, src=src):
                i0 = pl.multiple_of(ib * IB, IB)
                S = src[h, pl.ds(i0, IB), :, :]
                sa = jnp.sum(S * a_h[None], axis=1)
                vb = vscr[pl.ds(N * h + i0, IB), :]
                Sn = S * w_h[None] + sa[:, None, :] * b_h[None] + vb[:, None, :] * k_h[None]
                sT_ref[h, pl.ds(i0, IB), :, :] = Sn
                yscr[pl.ds(N * h + i0, IB), :] = jnp.sum(Sn * r_h[None], axis=1)
                return carry

            lax.fori_loop(0, N // IB, step, 0)
            bonus.append(jnp.sum(r_h * k_h * rk_ref[hs, :], axis=0, keepdims=True) * vT[hs])
        y = yscr[...]
        outs = []
        for h in range(2):
            hs = slice(N * h, N * (h + 1))
            m = jnp.mean(y[hs], axis=0, keepdims=True)
            yc = y[hs] - m
            var = jnp.mean(yc * yc, axis=0, keepdims=True)
            outs.append(yc * lax.rsqrt(var + RW_GN_EPS) * gg_ref[hs, :] + gb_ref[hs, :] + bonus[h])
        o_ref[rows, :] = jnp.concatenate(outs, axis=0).T


def _rwkv_step(r4, k4, v4, lw4, a4, s0, k_k, k_a, r_k, gn_g, gn_b, *, T, B):
    npair = r4.shape[1]
    kern = functools.partial(_rwkv_step_kernel, T=T, B=B)
    tspec = pl.BlockSpec((None, None, T * B, LANES), lambda p: (0, p, 0, 0))
    sspec = pl.BlockSpec((None, 2, RW_N, RW_N, B), lambda p: (p, 0, 0, 0, 0))
    wspec = pl.BlockSpec((None, LANES, 1), lambda p: (p, 0, 0))
    return pl.pallas_call(
        kern,
        grid=(npair,),
        in_specs=[tspec] * 5 + [sspec] + [wspec] * 5,
        out_specs=[tspec, sspec],
        out_shape=[jax.ShapeDtypeStruct(r4.shape, F32), jax.ShapeDtypeStruct(s0.shape, F32)],
        scratch_shapes=[pltpu.VMEM((LANES, B), F32), pltpu.VMEM((LANES, B), F32)],
        compiler_params=_cparams(1),
        name="rwkv_step",
    )(r4, k4, v4, lw4, a4, s0, k_k, k_a, r_k, gn_g, gn_b)


def _gdn_pre_kernel(x_ref, halo_ref, win_ref, cw_ref, alog_ref, dtb_ref,
                    q_ref, k_ref, v_ref, z_ref, bg_ref, tail_ref, xscr, gscr, *, s, R, H, Hg, C):
    c = pl.program_id(1)
    CH = GDN_CONV_CH
    tt = R // s

    @pl.when(c == 0)
    def _init():
        xscr[0:H, :] = halo_ref[...]
        gscr[0:Hg, :] = jnp.zeros((Hg, LANES), F32)

    x = x_ref[...]
    u = _mm(x, win_ref[...])
    xb = u[:, :CH]
    xscr[H:H + R, :] = xb
    y = xb * cw_ref[3]
    for d in (1, 2, 3):
        y = y + xscr[pl.ds(H - d * s, R), :] * cw_ref[3 - d]
    tail = xscr[pl.ds(R, H), :]
    xscr[0:H, :] = tail
    tail_ref[...] = tail
    qkv = _silu(y)

    def l2n(z, scale):
        return z * (lax.rsqrt(jnp.sum(z * z, axis=-1, keepdims=True) + GDN_EPS) * scale)

    for h in range(GDN_HK):
        q_ref[h] = l2n(qkv[:, LANES * h:LANES * (h + 1)], GDN_DK ** -0.5)
        k_ref[h] = l2n(qkv[:, GDN_KEY_W + LANES * h:GDN_KEY_W + LANES * (h + 1)], 1.0)
    for h in range(GDN_HV):
        v_ref[h] = qkv[:, 2 * GDN_KEY_W + LANES * h:2 * GDN_KEY_W + LANES * (h + 1)]
        z_ref[h] = u[:, CH + LANES * h:CH + LANES * (h + 1)]

    bg = u[:, CH + GDN_VAL_W:CH + GDN_VAL_W + LANES]
    beta = jax.nn.sigmoid(bg)
    g = -jnp.exp(alog_ref[...]) * jax.nn.softplus(bg + dtb_ref[...])
    t_in = (lax.broadcasted_iota(jnp.int32, (R, 1), 0) // s) % C
    k = 1
    while k < min(C, tt):
        gscr[Hg:Hg + R, :] = g
        g = g + jnp.where(t_in >= k, gscr[pl.ds(Hg - k * s, R), :], 0.0)
        k *= 2
    lane = lax.broadcasted_iota(jnp.int32, (R, LANES), 1)
    bg_ref[...] = jnp.where(lane < GDN_HV, beta, g)


def _gdn_pre(x3, halo, win, cw, alog, dtb, *, s, R, C):
    NB, TT, D = x3.shape
    H = halo.shape[1]
    tt = R // s
    Hg = max(SUBLANES, (min(C, tt) // 2) * s)
    kern = functools.partial(_gdn_pre_kernel, s=s, R=R, H=H, Hg=Hg, C=C)

    def ospec(nh):
        return pl.BlockSpec((None, nh, R, LANES), lambda i, c: (i, 0, c, 0))

    def oshape(nh):
        return jax.ShapeDtypeStruct((NB, nh, TT, LANES), F32)

    return pl.pallas_call(
        kern,
        grid=(NB, TT // R),
        in_specs=[
            pl.BlockSpec((None, R, D), lambda i, c: (i, c, 0)),
            pl.BlockSpec((None, H, GDN_CONV_CH), lambda i, c: (i, 0, 0)),
            _const_spec(win.shape), _const_spec(cw.shape), _const_spec(alog.shape), _const_spec(dtb.shape),
        ],
        out_specs=[ospec(GDN_HK), ospec(GDN_HK), ospec(GDN_HV), ospec(GDN_HV),
                   pl.BlockSpec((None, R, LANES), lambda i, c: (i, c, 0)),
                   pl.BlockSpec((None, H, GDN_CONV_CH), lambda i, c: (i, 0, 0))],
        out_shape=[oshape(GDN_HK), oshape(GDN_HK), oshape(GDN_HV), oshape(GDN_HV),
                   jax.ShapeDtypeStruct((NB, TT, LANES), F32),
                   jax.ShapeDtypeStruct((NB, H, GDN_CONV_CH), F32)],
        scratch_shapes=[pltpu.VMEM((H + R, GDN_CONV_CH), F32), pltpu.VMEM((Hg + R, LANES), F32)],
        compiler_params=_cparams(2),
        name="gdn_pre",
    )(x3, halo, win, cw, alog, dtb)


def _gdn_rec_kernel(q_ref, k_ref, v_ref, z_ref, bg_ref, x_ref, s0_ref, ng_ref, wout_ref, lng_ref, lnb_ref,
                    out_ref, sT_ref, sscr, oscr, *, C, batches):
    tb = pl.program_id(1)
    bb, nhv, Rt, _ = v_ref.shape
    rep = GDN_HV // GDN_HK

    @pl.when(tb == 0)
    def _init():
        sscr[...] = s0_ref[...].reshape(sscr.shape)

    strict, incl, eye = _packed_masks(C)
    lo = lax.broadcasted_iota(jnp.int32, (C, 2 * C), 1) < C

    def part_a(items, ctx):
        CH = [(b, h, ci) for b, ci in items for h in range(nhv)]
        KH = [(b, m, ci) for b, ci in items for m in range(GDN_HK)]
        rs = lambda ci: pl.ds(ci * C, C)
        bgs = {(b, ci): bg_ref[b, rs(ci), :] for b, ci in items}
        bgT = {key: x.T for key, x in bgs.items()}

        kq = [jnp.concatenate([k_ref[b, m, rs(ci), :], q_ref[b, m, rs(ci), :]], axis=0) for b, m, ci in KH]
        sc = _each(lambda x: _dot(x, jnp.concatenate([x[:C, :], x[:C, :]], axis=0), _NT), kq)
        yield

        def col2(x, lane):
            return jnp.where(lo, jnp.broadcast_to(x[:, lane:lane + 1], (C, 2 * C)),
                             jnp.broadcast_to(x[:, lane + 1:lane + 2], (C, 2 * C)))

        bcol = [col2(bgs[b, ci], rep * m) for b, m, ci in KH]
        gcol = [col2(bgs[b, ci], GDN_HV + rep * m) for b, m, ci in KH]
        grow = [jnp.concatenate([bgT[b, ci][GDN_HV + rep * m:GDN_HV + rep * m + 1, :],
                                 bgT[b, ci][GDN_HV + rep * m + 1:GDN_HV + rep * m + 2, :]], axis=1) for b, m, ci in KH]
        diff = _each(lambda c_, r_: c_ - r_, gcol, grow)
        Lp = _each(lambda z, bc, d: jnp.where(strict, z[:C, :] * bc * jnp.exp(jnp.where(strict, d, 0.0)), 0.0),
                   sc, bcol, diff)
        Ap = _each(lambda z, d: jnp.where(incl, z[C:, :] * jnp.exp(jnp.where(incl, d, 0.0)), 0.0), sc, diff)
        Tinv_p = yield from _unit_lower_inverse([-l_ for l_ in Lp], eye)

        half = lambda xs, n, h: xs[n // rep][:, (h % rep) * C:(h % rep + 1) * C]
        Tinv = [half(Tinv_p, n, h) for n, (b, h, ci) in enumerate(CH)]
        A = [half(Ap, n, h) for n, (b, h, ci) in enumerate(CH)]
        k = [kq[n // rep][:C, :] for n in range(len(CH))]
        q = [kq[n // rep][C:, :] for n in range(len(CH))]
        v = [v_ref[b, h, rs(ci), :] for b, h, ci in CH]
        beta = [jnp.broadcast_to(bgs[b, ci][:, h:h + 1], (C, LANES)) for b, h, ci in CH]
        gc = [jnp.broadcast_to(bgs[b, ci][:, GDN_HV + h:GDN_HV + h + 1], (C, LANES)) for b, h, ci in CH]
        kb = _each(lambda x, y: x * y, k, beta)
        eg = _each(jnp.exp, gc)
        UW = _each(lambda t, v_, b_, kb_, e: _dot(t, jnp.concatenate([v_ * b_, kb_ * e], axis=1)),
                   Tinv, v, beta, kb, eg)
        yield
        for n, key in enumerate(CH):
            g_last = gc[n][C - 1:C, :]
            ctx[key] = dict(U=UW[n][:, :GDN_DV], WQl=jnp.concatenate([UW[n][:, GDN_DV:], q[n] * eg[n]], axis=0),
                            A=A[n], kdec=k[n] * jnp.exp(g_last - gc[n]), sdec=jnp.exp(g_last))

    def part_b(items, ctx):
        for ci in sorted({ci for _, ci in items}):
            CH = [(b, h, ci) for b, c_ in items if c_ == ci for h in range(nhv)]
            X = [ctx.pop(key) for key in CH]
            S = [sscr[b * nhv + h] for b, h, _ in CH]
            WQ = _each(lambda x, s_: _dot(x["WQl"], s_), X, S)
            yield
            v_new = _each(lambda x, wq: x["U"] - wq[:C, :], X, WQ)
            o = _each(lambda x, wq, vn: wq[C:, :] + _dot(x["A"], vn), X, WQ, v_new)
            Snew = _each(lambda x, s_, vn: s_ * x["sdec"] + _dot(x["kdec"], vn, _TN), X, S, v_new)
            yield
            for n, (b, h, _) in enumerate(CH):
                sscr[b * nhv + h] = Snew[n]
                oh = o[n]
                oscr[b * Rt + ci * C:b * Rt + (ci + 1) * C, LANES * h:LANES * (h + 1)] = (
                    oh * lax.rsqrt(jnp.mean(oh * oh, axis=-1, keepdims=True) + GDN_EPS) * ng_ref[...])
            yield

    _run_pipelined(part_a, part_b, batches)

    @pl.when(tb == pl.num_programs(1) - 1)
    def _final_state():
        sT_ref[...] = sscr[...].reshape(sT_ref.shape)

    _gated_out_ln(oscr[...], z_ref, x_ref, wout_ref, lng_ref, lnb_ref, out_ref)


def _gdn_rec(q4, k4, v4, z4, bg, x, s0, norm_g, wout, lng, lnb, *, C, Rt, bb, gb):
    B, _, T, _ = v4.shape
    D = x.shape[-1]
    kern = functools.partial(_gdn_rec_kernel, C=C, batches=_rec_batches(bb, Rt // C, gb, GDN_CHUNKS_PER_BATCH))

    def tspec(nh):
        return pl.BlockSpec((bb, nh, Rt, LANES), lambda b, t: (b, 0, t, 0))

    xspec = pl.BlockSpec((bb, Rt, D), lambda b, t: (b, t, 0))
    sspec = pl.BlockSpec((bb, GDN_HV, GDN_DK, GDN_DV), lambda b, t: (b, 0, 0, 0))
    return pl.pallas_call(
        kern,
        grid=(B // bb, T // Rt),
        in_specs=[tspec(GDN_HK), tspec(GDN_HK), tspec(GDN_HV), tspec(GDN_HV),
                  pl.BlockSpec((bb, Rt, LANES), lambda b, t: (b, t, 0)), xspec, sspec,
                  _const_spec(norm_g.shape), _const_spec(wout.shape), _const_spec(lng.shape), _const_spec(lnb.shape)],
        out_specs=[xspec, sspec],
        out_shape=[jax.ShapeDtypeStruct(x.shape, F32), jax.ShapeDtypeStruct(s0.shape, F32)],
        scratch_shapes=[pltpu.VMEM((bb * GDN_HV, GDN_DK, GDN_DV), F32), pltpu.VMEM((bb * Rt, GDN_VAL_W), F32)],
        compiler_params=_cparams(2),
        name="gdn_rec",
    )(q4, k4, v4, z4, bg, x, s0, norm_g, wout, lng, lnb)


class _Group:
    def __init__(self, B, T, time_major):
        self.B, self.T, self.time_major = B, T, time_major
        if time_major:
            self.s, self.NB, self.TT, self.R = B, 1, T * B, T * B
        else:
            self.s, self.NB, self.TT, self.R = 1, B, T, min(ROW_BLOCK, T)

    def to_rows(self, x):
        if self.time_major:
            return jnp.swapaxes(x, 0, 1).reshape(1, self.TT, x.shape[-1])
        return x

    def from_rows(self, x3):
        if self.time_major:
            return jnp.swapaxes(x3.reshape(self.T, self.B, x3.shape[-1]), 0, 1)
        return x3

    def halo(self, st, n_steps):
        if self.time_major:
            return jnp.swapaxes(st, 0, 1).reshape(1, n_steps * self.B, st.shape[-1])
        H = _halo_rows(n_steps, 1)
        return jnp.pad(st, ((0, 0), (H - n_steps, 0), (0, 0)))

    def unhalo(self, tail, n_steps):
        if self.time_major:
            return jnp.swapaxes(tail.reshape(n_steps, self.B, tail.shape[-1]), 0, 1)
        return tail[:, tail.shape[1] - n_steps:]

    def vec(self, st):
        return st[None] if self.time_major else st[:, None, :]

    def unvec(self, v):
        return v[0] if self.time_major else v[:, 0, :]

    def heads_to_batch(self, a4, Tpad):
        if not self.time_major:
            return a4
        nh = a4.shape[1]
        a = a4.reshape(nh, self.T, self.B, LANES).transpose(2, 0, 1, 3)
        return jnp.pad(a, ((0, 0), (0, 0), (0, Tpad - self.T), (0, 0)))

    def rows_to_batch(self, a3, Tpad):
        if not self.time_major:
            return a3
        return jnp.pad(self.from_rows(a3), ((0, 0), (0, Tpad - self.T), (0, 0)))

    def rows_from_batch(self, a3):
        if not self.time_major:
            return a3
        return self.to_rows(a3[:, :self.T])

    def rec_tiling(self, chunk, block):
        if self.time_major:
            Tp = -(-self.T // SUBLANES) * SUBLANES
            return Tp, Tp, Tp, min(self.B, SAMPLE_SEQ_BLOCK), min(self.B, SAMPLE_SEQ_GROUP)
        return min(chunk, self.T), self.T, min(block, self.T), PROMPT_SEQ_BLOCK if self.B % PROMPT_SEQ_BLOCK == 0 else 1, 1


def _lru_apply(g, x3, conv_st, h_st, p, lng, lnb):
    swap = not g.time_major
    assert g.B % SUBLANES == 0, "RG-LRU kernel needs the batch to fill whole sublane tiles"
    lay = _Group(g.B, g.T, True) if swap else g
    out, tail, hT = _lru_layer(x3, lay.halo(conv_st, CONV_W - 1), lay.vec(h_st), p["win"], p["cw"], p["cb"],
                               p["wg"], p["bg"], p["lam"], p["wout"], lng, lnb, s=lay.s,
                               R=min(LRU_BLOCK, g.T * g.B) if swap else g.R, swap=swap)
    return out, lay.unhalo(tail, CONV_W - 1), lay.unvec(hT)


def _rwkv_apply(g, x3, shift_st, wkv_st, p, lng, lnb):
    B = g.B
    pre = _rwkv_pre(x3, g.halo(shift_st[:, None, :], 1), p["mu"], p["win"], p["w0"], p["w1"], p["w2"], p["a0"],
                    p["a1"], p["a2"], s=g.s, R=g.R)
    new_shift = g.from_rows(x3)[:, -1]
    if g.time_major:
        r4, k4, v4, g4, lw4, a4 = pre
        s0 = jnp.transpose(wkv_st, (1, 2, 3, 0)).reshape(RW_H // 2, 2, RW_N, RW_N, B)
        col = lambda v: jnp.swapaxes(v, 1, 2)
        o4, sT = _rwkv_step(r4, k4, v4, lw4, a4, s0, col(p["k_k"]), col(p["k_a"]), col(p["r_k"]), col(p["gn_g"]),
                            col(p["gn_b"]), T=g.T, B=B)
        out = _post_layer(x3, o4, g4, p["wout"], lng, lnb)
        return out, new_shift, jnp.transpose(sT.reshape(RW_H, RW_N, RW_N, B), (3, 0, 1, 2))
    C, Tp, Rt, bb, gb = g.rec_tiling(RW_CHUNK, RW_BLOCK)
    r4, k4, v4, g4, lw4, a4 = [g.heads_to_batch(a, Tp) for a in pre]
    s0 = wkv_st.reshape(B, RW_H // 2, 2 * RW_N, RW_N)
    out, sT = _rwkv_rec(r4, k4, v4, lw4, a4, g4, g.rows_to_batch(x3, Tp), s0, p["k_k"], p["k_a"], p["r_k"],
                        p["gn_g"], p["gn_b"], p["wout"], lng, lnb, C=C, Rt=Rt, bb=bb, gb=gb)
    new_shift = g.from_rows(x3)[:, -1]
    return g.rows_from_batch(out), new_shift, sT.reshape(B, RW_H, RW_N, RW_N)


def _gdn_apply(g, x3, conv_st, S_st, p, lng, lnb):
    T = g.T
    C, Tp, Rt, bb, gb = g.rec_tiling(GDN_CHUNK, GDN_BLOCK)
    q4, k4, v4, z4, bg, tail = _gdn_pre(x3, g.halo(conv_st, CONV_W - 1), p["win"], p["cw"], p["alog"], p["dtb"],
                                        s=g.s, R=g.R, C=min(GDN_CHUNK, T))
    bgb = g.rows_to_batch(bg, T)
    if Tp != T:
        held = jnp.where(jnp.arange(LANES) < GDN_HV, 0.0, bgb[:, T - 1:T, :])
        bgb = jnp.concatenate([bgb, jnp.broadcast_to(held, (g.B, Tp - T, LANES))], axis=1)
    tb = lambda a: g.heads_to_batch(a, Tp)
    out, sT = _gdn_rec(tb(q4), tb(k4), tb(v4), tb(z4), bgb, g.rows_to_batch(x3, Tp), S_st, p["norm_g"], p["wout"],
                       lng, lnb, C=C, Rt=Rt, bb=bb, gb=gb)
    return g.rows_from_batch(out), g.unhalo(tail, CONV_W - 1), sT


def _trunk(g, x, st, params, ln_g, ln_b):
    lru_conv, lru_h, rw_shift, rw_S, gdn_conv, gdn_S = st
    new = ([], [], [], [], [], [])
    x3 = g.to_rows(x)
    ia = ib = ic = 0
    for layer in range(DEPTH):
        kind = layer % N_MIXERS
        lng, lnb = ln_g[layer][None, :], ln_b[layer][None, :]
        if kind == 0:
            x3, c, h = _lru_apply(g, x3, lru_conv[ia], lru_h[ia], params["lru"][ia], lng, lnb)
            new[0].append(c)
            new[1].append(h)
            ia += 1
        elif kind == 1:
            x3, sh, S = _rwkv_apply(g, x3, rw_shift[ib], rw_S[ib], params["rwkv"][ib], lng, lnb)
            new[2].append(sh)
            new[3].append(S)
            ib += 1
        else:
            x3, c, S = _gdn_apply(g, x3, gdn_conv[ic], gdn_S[ic], params["gdn"][ic], lng, lnb)
            new[4].append(c)
            new[5].append(S)
            ic += 1
    return g.from_rows(x3), tuple(s[0][None] if len(s) == 1 else jnp.stack(s) for s in new)


def _prep_params(lru_w_in, lru_conv_w, lru_conv_b, lru_wa, lru_ba, lru_wx, lru_bx, lru_lambda, lru_w_out, rw_mu,
                 rw_w_in, rw_w0, rw_w1, rw_w2, rw_a0, rw_a1, rw_a2, rw_k_k, rw_k_a, rw_r_k, rw_gn_g, rw_gn_b,
                 rw_w_out, gdn_w_in, gdn_conv_w, gdn_a_log, gdn_dt_bias, gdn_norm_g, gdn_w_out):
    row = lambda v: v[None, :]
    lru = []
    for n in range(lru_w_in.shape[0]):
        lru.append(dict(win=lru_w_in[n].astype(BF), cw=lru_conv_w[n][:, None, :], cb=row(lru_conv_b[n]),
                        wg=_lru_gate_weights(lru_wa[n], lru_wx[n]), bg=jnp.stack([lru_ba[n], lru_bx[n]])[:, None, :],
                        lam=row(lru_lambda[n]), wout=lru_w_out[n].astype(BF)))
    pairw = lambda v: v.reshape(RW_W // LANES, 1, LANES)
    rwkv = []
    for n in range(rw_w_in.shape[0]):
        rwkv.append(dict(mu=rw_mu[n][:, None, :],win=rw_w_in[n].astype(BF), w0=row(rw_w0[n]), w1=rw_w1[n].astype(BF),
                         w2=rw_w2[n].astype(BF), a0=row(rw_a0[n]), a1=rw_a1[n].astype(BF), a2=rw_a2[n].astype(BF),
                         k_k=pairw(rw_k_k[n]), k_a=pairw(rw_k_a[n]), r_k=pairw(rw_r_k[n]), gn_g=pairw(rw_gn_g[n]),
                         gn_b=pairw(rw_gn_b[n]), wout=rw_w_out[n].astype(BF)))
    gdn = []
    for n in range(gdn_w_in.shape[0]):
        w = gdn_w_in[n]
        o2 = GDN_CONV_CH + GDN_VAL_W
        wpad = jnp.pad(w[:, o2:], ((0, 0), (0, LANES - 2 * GDN_HV)))
        lanes = lambda v: jnp.pad(v, (GDN_HV, LANES - 2 * GDN_HV))[None, :]
        gdn.append(dict(win=jnp.concatenate([w[:, :o2], wpad], axis=1).astype(BF), cw=gdn_conv_w[n][:, None, :],
                        alog=lanes(gdn_a_log[n]), dtb=lanes(gdn_dt_bias[n]), norm_g=row(gdn_norm_g[n]),
                        wout=gdn_w_out[n].astype(BF)))
    return dict(lru=lru, rwkv=rwkv, gdn=gdn)


def kernel(x_prompt, x_sample, state_lru_conv, state_lru_h, state_rwkv_shift, state_rwkv_wkv, state_gdn_conv, state_gdn_S, ln_g, ln_b, lru_w_in, lru_conv_w, lru_conv_b, lru_wa, lru_ba, lru_wx, lru_bx, lru_lambda, lru_w_out, rw_mu, rw_w_in, rw_w0, rw_w1, rw_w2, rw_a0, rw_a1, rw_a2, rw_k_k, rw_k_a, rw_r_k, rw_gn_g, rw_gn_b, rw_w_out, gdn_w_in, gdn_conv_w, gdn_a_log, gdn_dt_bias, gdn_norm_g, gdn_w_out):
    params = _prep_params(lru_w_in, lru_conv_w, lru_conv_b, lru_wa, lru_ba, lru_wx, lru_bx, lru_lambda, lru_w_out,
                          rw_mu, rw_w_in, rw_w0, rw_w1, rw_w2, rw_a0, rw_a1, rw_a2, rw_k_k, rw_k_a, rw_r_k, rw_gn_g,
                          rw_gn_b, rw_w_out, gdn_w_in, gdn_conv_w, gdn_a_log, gdn_dt_bias, gdn_norm_g, gdn_w_out)
    bp, tp, _ = x_prompt.shape
    bs, ts, _ = x_sample.shape
    n_a, n_b, n_c = state_lru_conv.shape[0], state_rwkv_shift.shape[0], state_gdn_conv.shape[0]
    zero_state = (jnp.zeros((n_a, bp, CONV_W - 1, LRU_W), F32),
                  jnp.zeros((n_a, bp, LRU_W), F32),
                  jnp.zeros((n_b, bp, D_MODEL), F32),
                  jnp.zeros((n_b, bp, RW_H, RW_N, RW_N), F32),
                  jnp.zeros((n_c, bp, CONV_W - 1, GDN_CONV_CH), F32),
                  jnp.zeros((n_c, bp, GDN_HV, GDN_DK, GDN_DV), F32))
    y_prompt, sp = _trunk(_Group(bp, tp, False), x_prompt, zero_state, params, ln_g, ln_b)
    y_sample, ss = _trunk(_Group(bs, ts, True), x_sample,
                          (state_lru_conv, state_lru_h, state_rwkv_shift, state_rwkv_wkv, state_gdn_conv,
                           state_gdn_S), params, ln_g, ln_b)
    return (y_prompt, y_sample, sp[0], ss[0], sp[1], ss[1], sp[2], ss[2], sp[3], ss[3], sp[4], ss[4], sp[5], ss[5])
```

```python
import functools
import math

import jax
import jax.numpy as jnp
from jax import lax
from jax.experimental import pallas as pl
from jax.experimental.pallas import tpu as pltpu

F32 = jnp.float32
BF = jnp.bfloat16

D_MODEL = 1024
DEPTH = 4
N_MIXERS = 3
DN_ALPHA = (2.0 * DEPTH) ** 0.25
LN_EPS = 1e-5
CONV_W = 4

LRU_W = D_MODEL
LRU_BLOCKS = 16
LRU_BS = LRU_W // LRU_BLOCKS
LRU_C = 8.0

RW_W = D_MODEL
RW_N = 64
RW_H = RW_W // RW_N
RW_GN_EPS = 64e-5
RW_NORM_EPS = 1e-12

GDN_HK = 4
GDN_HV = 8
GDN_DK = 128
GDN_DV = 128
GDN_KEY_W = GDN_HK * GDN_DK
GDN_VAL_W = GDN_HV * GDN_DV
GDN_CONV_CH = 2 * GDN_KEY_W + GDN_VAL_W
GDN_CHUNK = 64
GDN_EPS = 1e-6

LANES = 128
SUBLANES = 8
VMEM_LIMIT = 56 * 1024 * 1024
ROW_BLOCK = 512
LRU_BLOCK = 1024
RW_BLOCK = 256
GDN_BLOCK = 256
RW_CHUNKS_PER_BATCH = 1
GDN_CHUNKS_PER_BATCH = 2
PROMPT_SEQ_BLOCK = 2
SAMPLE_SEQ_BLOCK = 8
SAMPLE_SEQ_GROUP = 4
RW_CHUNK = 64

_NN = (((1,), (0,)), ((), ()))
_NT = (((1,), (1,)), ((), ()))
_TN = (((0,), (0,)), ((), ()))


def _mm(a, b):
    return jnp.dot(a.astype(BF), b.astype(BF), preferred_element_type=F32)


def _dot(a, b, dims=_NN):
    return lax.dot_general(a.astype(BF), b.astype(BF), dims, preferred_element_type=F32)


def _dot_exact_lhs(a01, b):
    a = a01.astype(BF)
    b1 = b.astype(BF)
    r1 = b - b1.astype(F32)
    b2 = r1.astype(BF)
    b3 = (r1 - b2.astype(F32)).astype(BF)
    d = lambda q: jnp.dot(a, q, preferred_element_type=F32)
    return d(b1) + (d(b2) + d(b3))


def _layer_norm(z, g, b):
    mu = jnp.mean(z, axis=-1, keepdims=True)
    zc = z - mu
    var = jnp.mean(zc * zc, axis=-1, keepdims=True)
    return zc * lax.rsqrt(var + LN_EPS) * g + b


def _silu(x):
    return x * jax.nn.sigmoid(x)


def _cparams(n_axes):
    return pltpu.CompilerParams(dimension_semantics=("arbitrary",) * n_axes, vmem_limit_bytes=VMEM_LIMIT)


def _const_spec(shape):
    nd = len(shape)
    return pl.BlockSpec(shape, lambda *_: (0,) * nd, pipeline_mode=pl.Buffered(1))


def _halo_rows(n_steps, s):
    rows = n_steps * s
    return rows if rows % SUBLANES == 0 else SUBLANES


def _lru_kernel(x_ref, halo_ref, h0_ref, win_ref, cw_ref, cb_ref, wg_ref, bg_ref, lam_ref, wout_ref, lng_ref,
                lnb_ref, out_ref, tail_ref, hT_ref, xscr, hscr, *, s, R, H, swap):
    c = pl.program_id(1)
    C = LRU_W
    tt = R // s

    @pl.when(c == 0)
    def _init():
        xscr[0:H, :] = halo_ref[...]
        hscr[...] = h0_ref[...]

    x = x_ref[...]
    if swap:
        x = jnp.swapaxes(x, 0, 1).reshape(R, D_MODEL)
    u = _mm(x, win_ref[...])
    xb = u[:, :C]
    gate = u[:, C:]

    xscr[H:H + R, :] = xb
    xc = xb * cw_ref[3] + cb_ref[...]
    for d in (1, 2, 3):
        xc = xc + xscr[pl.ds(H - d * s, R), :] * cw_ref[3 - d]
    tail = xscr[pl.ds(R, H), :]
    xscr[0:H, :] = tail
    tail_ref[...] = tail

    xcb = xc.astype(BF)
    ra, ix = [], []
    for g in range(C // 256):
        gt = jnp.dot(xcb[:, 256 * g:256 * (g + 1)], wg_ref[g], preferred_element_type=F32)
        ra.append(gt[:, :256])
        ix.append(gt[:, 256:])
    r = jax.nn.sigmoid(jnp.concatenate(ra, axis=1) + bg_ref[0])
    i = jax.nn.sigmoid(jnp.concatenate(ix, axis=1) + bg_ref[1])
    log_a = (-LRU_C) * r * jax.nn.softplus(-lam_ref[...])
    a = jnp.exp(log_a)
    th = jnp.tanh(log_a)
    b = jnp.sqrt(-2.0 * th / (1.0 - th)) * i * xc

    hlast = hscr[...]
    hs = []
    for t in range(tt):
        hlast = a[t * s:(t + 1) * s, :] * hlast + b[t * s:(t + 1) * s, :]
        hs.append(hlast)
    h = jnp.concatenate(hs, axis=0)
    hscr[...] = hlast
    hT_ref[...] = hlast

    y = _mm(h * _silu(gate), wout_ref[...])
    out = _layer_norm(DN_ALPHA * x + y, lng_ref[...], lnb_ref[...])
    if swap:
        out = jnp.swapaxes(out.reshape(tt, s, D_MODEL), 0, 1)
    out_ref[...] = out


def _lru_layer(x3, halo, h0, win, cw, cb, wg, bg, lam, wout, lng, lnb, *, s, R, swap):
    C = LRU_W
    D = x3.shape[-1]
    H = halo.shape[1]
    tt = R // s
    assert s % SUBLANES == 0
    if swap:
        NB, TT = 1, x3.shape[1] * s
        xspec = pl.BlockSpec((s, tt, D), lambda i, c: (0, c, 0))
    else:
        NB, TT = x3.shape[:2]
        xspec = pl.BlockSpec((None, R, D), lambda i, c: (i, c, 0))
    kern = functools.partial(_lru_kernel, s=s, R=R, H=H, swap=swap)
    return pl.pallas_call(
        kern,
        grid=(NB, TT // R),
        in_specs=[
            xspec,
            pl.BlockSpec((None, H, C), lambda i, c: (i, 0, 0)),
            pl.BlockSpec((None, s, C), lambda i, c: (i, 0, 0)),
            _const_spec(win.shape), _const_spec(cw.shape), _const_spec(cb.shape), _const_spec(wg.shape),
            _const_spec(bg.shape), _const_spec(lam.shape), _const_spec(wout.shape), _const_spec(lng.shape),
            _const_spec(lnb.shape),
        ],
        out_specs=[
            xspec,
            pl.BlockSpec((None, H, C), lambda i, c: (i, 0, 0)),
            pl.BlockSpec((None, s, C), lambda i, c: (i, 0, 0)),
        ],
        out_shape=[
            jax.ShapeDtypeStruct(x3.shape, F32),
            jax.ShapeDtypeStruct((NB, H, C), F32),
            jax.ShapeDtypeStruct((NB, s, C), F32),
        ],
        scratch_shapes=[
            pltpu.VMEM((H + R, C), F32),
            pltpu.VMEM((s, C), F32),
        ],
        compiler_params=_cparams(2),
        name="lru_layer",
    )(x3, halo, h0, win, cw, cb, wg, bg, lam, wout, lng, lnb)


def _lru_gate_weights(wa, wx):
    def bd(w):
        w4 = w.reshape(4, 4, LRU_BS, LRU_BS)
        eye = jnp.eye(4, dtype=w.dtype)
        return jnp.einsum("gaij,ab->gaibj", w4, eye).reshape(4, 256, 256)
    return jnp.concatenate([bd(wa), bd(wx)], axis=2).astype(BF)


def _gated_out_ln(o, g_ref, x_ref, w_ref, lng_ref, lnb_ref, out_ref):
    bb, nh, Rt, _ = g_ref.shape
    rows = bb * Rt
    g = jnp.concatenate([g_ref[:, p, :, :].reshape(rows, LANES) for p in range(nh)], axis=1)
    y = _mm(o * _silu(g), w_ref[...])
    x = x_ref[...].reshape(rows, D_MODEL)
    out_ref[...] = _layer_norm(DN_ALPHA * x + y, lng_ref[...], lnb_ref[...]).reshape(bb, Rt, D_MODEL)


def _post_kernel(x_ref, o_ref, g_ref, w_ref, lng_ref, lnb_ref, out_ref):
    o = jnp.concatenate([o_ref[0, p] for p in range(o_ref.shape[1])], axis=1)
    _gated_out_ln(o, g_ref, x_ref, w_ref, lng_ref, lnb_ref, out_ref)


def _post_layer(x3, o4, g4, w, lng, lnb):
    return pl.pallas_call(
        _post_kernel,
        out_shape=jax.ShapeDtypeStruct(x3.shape, F32),
        compiler_params=pltpu.CompilerParams(vmem_limit_bytes=VMEM_LIMIT),
        name="post_layer",
    )(x3, o4, g4, w, lng, lnb)


def _rwkv_pre_kernel(x_ref, halo_ref, mu_ref, win_ref, w0_ref, w1_ref, w2_ref, a0_ref, a1_ref, a2_ref,
                     r_ref, k_ref, v_ref, g_ref, lw_ref, a_ref, xscr, xxscr, *, s, R, H):
    c = pl.program_id(1)

    @pl.when(c == 0)
    def _init():
        xscr[0:H, :] = halo_ref[...]

    x = x_ref[...]
    xscr[H:H + R, :] = x
    xxscr[...] = xscr[pl.ds(H - s, R), :] - x
    xscr[0:H, :] = xscr[pl.ds(R, H), :]
    xx = xxscr[...]
    xm = lambda n: x + xx * mu_ref[n]

    def put(ref, val):
        for p in range(RW_W // LANES):
            ref[p] = val[:, LANES * p:LANES * (p + 1)]

    put(r_ref, _mm(xm(0), win_ref[0]))
    put(k_ref, _mm(xm(1), win_ref[1]))
    put(v_ref, _mm(xm(2), win_ref[2]))
    put(g_ref, _mm(xm(3), win_ref[3]))
    w_raw = w0_ref[...] + _mm(jnp.tanh(_mm(xm(4), w1_ref[...])), w2_ref[...])
    put(lw_ref, (-math.exp(-0.5)) * jax.nn.sigmoid(w_raw))
    put(a_ref, jax.nn.sigmoid(a0_ref[...] + _mm(_mm(xm(5), a1_ref[...]), a2_ref[...])))


def _rwkv_pre(x3, halo, mu, win, w0, w1, w2, a0, a1, a2, *, s, R):
    NB, TT, D = x3.shape
    H = halo.shape[1]
    nh = RW_W // LANES
    kern = functools.partial(_rwkv_pre_kernel, s=s, R=R, H=H)
    ospec = pl.BlockSpec((None, nh, R, LANES), lambda i, c: (i, 0, c, 0))
    oshape = jax.ShapeDtypeStruct((NB, nh, TT, LANES), F32)
    return pl.pallas_call(
        kern,
        grid=(NB, TT // R),
        in_specs=[
            pl.BlockSpec((None, R, D), lambda i, c: (i, c, 0)),
            pl.BlockSpec((None, H, D), lambda i, c: (i, 0, 0)),
            _const_spec(mu.shape), _const_spec(win.shape), _const_spec(w0.shape), _const_spec(w1.shape),
            _const_spec(w2.shape), _const_spec(a0.shape), _const_spec(a1.shape), _const_spec(a2.shape),
        ],
        out_specs=[ospec] * 6,
        out_shape=[oshape] * 6,
        scratch_shapes=[pltpu.VMEM((H + R, D), F32), pltpu.VMEM((R, D), F32)],
        compiler_params=_cparams(2),
        name="rwkv_pre",
    )(x3, halo, mu, win, w0, w1, w2, a0, a1, a2)


def _seg_sum(x):
    lane = lax.broadcasted_iota(jnp.int32, x.shape, 1)
    lo = lane < RW_N
    s0 = jnp.sum(jnp.where(lo, x, 0.0), axis=-1, keepdims=True)
    s1 = jnp.sum(jnp.where(lo, 0.0, x), axis=-1, keepdims=True)
    return jnp.where(lo, s0, s1)


def _stack2(x):
    lane = lax.broadcasted_iota(jnp.int32, x.shape, 1)
    lo = lane < x.shape[1] // 2
    return jnp.concatenate([jnp.where(lo, x, 0.0), jnp.where(lo, 0.0, x)], axis=0)


def _each(f, *lists):
    return [f(*t) for t in zip(*lists)]


def _packed_masks(C):
    t = lax.broadcasted_iota(jnp.int32, (C, 2 * C), 0)
    s = lax.broadcasted_iota(jnp.int32, (C, 2 * C), 1) % C
    return s < t, s <= t


def _stackn(x, n):
    grp = lax.broadcasted_iota(jnp.int32, x.shape, 1) // (x.shape[1] // n)
    return jnp.concatenate([jnp.where(grp == k, x, 0.0) for k in range(n)], axis=0)


def _neumann_inverse(Ps, nblk):
    m = Ps[0].shape[0]
    eye = (lax.broadcasted_iota(jnp.int32, Ps[0].shape, 1) % m) == lax.broadcasted_iota(jnp.int32, Ps[0].shape, 0)
    invs = [jnp.where(eye, 1.0, P) for P in Ps]
    pw = Ps
    span = 2
    while span < m:
        pw = _each(lambda p: _dot(p, _stackn(p, nblk)), pw)
        yield
        invs = _each(lambda inv, p: inv + _dot(inv, _stackn(p, nblk)), invs, pw)
        yield
        span *= 2
    return invs


def _unit_lower_inverse(Ls):
    C = Ls[0].shape[0]
    h = C // 2
    if h % SUBLANES:
        return (yield from _neumann_inverse(Ls, 2))
    grp = lax.broadcasted_iota(jnp.int32, (h, 2 * C), 1) // h
    diag0 = (grp % 2) == 0
    zero = jnp.zeros((h, 2 * C), F32)
    D = yield from _neumann_inverse([jnp.where(diag0, L[:h, :], L[h:, :]) for L in Ls], 4)
    L21 = [jnp.where(diag0, L[h:, :], 0.0) for L in Ls]
    M1 = _each(lambda d, q: _dot(d, jnp.concatenate(
        [zero, jnp.where(grp == 0, q, 0.0), zero, jnp.where(grp == 2, q, 0.0)], axis=0)), D, L21)
    yield
    X21 = _each(lambda m1, d: _dot(m1, jnp.concatenate(
        [jnp.where(grp == 0, d, 0.0), zero, jnp.where(grp == 2, d, 0.0), zero], axis=0)), M1, D)
    yield
    return _each(lambda d, x21: jnp.concatenate([jnp.where(diag0, d, 0.0), x21 + jnp.where(diag0, 0.0, d)], axis=0),
                 D, X21)


def _run_pipelined(part_a, part_b, batches):
    ctx = {}
    prev = None
    for batch in batches:
        gens = [part_a(batch, ctx)] + ([part_b(prev, ctx)] if prev is not None else [])
        while gens:
            for g in list(gens):
                try:
                    next(g)
                except StopIteration:
                    gens.remove(g)
        prev = batch
    for _ in part_b(prev, ctx):
        pass


def _rec_batches(bb, nchunk, gb, cpb):
    if nchunk > 1:
        return tuple(tuple((b, c) for b in range(bb) for c in range(c0, min(c0 + cpb, nchunk)))
                     for c0 in range(0, nchunk, cpb))
    return tuple(tuple((b, 0) for b in range(b0, b0 + gb)) for b0 in range(0, bb, gb))


def _rwkv_rec_kernel(r_ref, k_ref, v_ref, lw_ref, a_ref, g_ref, x_ref, s0_ref, kk_ref, ka_ref, rk_ref, gg_ref,
                     gb_ref, wout_ref, lng_ref, lnb_ref, out_ref, sT_ref, sscr, oscr, *, C, batches):
    tb = pl.program_id(1)
    bb, npair, Rt, _ = r_ref.shape
    C2 = 2 * C
    lane_sq = lax.broadcasted_iota(jnp.int32, (LANES, LANES), 1)
    row_sq = lax.broadcasted_iota(jnp.int32, (LANES, LANES), 0)
    same_head = (lane_sq < RW_N) == (row_sq < RW_N)

    @pl.when(tb == 0)
    def _init():
        def init_b(b, carry):
            for p in range(npair):
                s2 = s0_ref[b, p]
                sscr[b * npair + p] = jnp.where(same_head, jnp.concatenate([s2, s2], axis=1), 0.0)
            return carry
        lax.fori_loop(0, bb, init_b, 0)

    strict, incl = _packed_masks(C)
    tri = (lax.broadcasted_iota(jnp.int32, (C, C), 1) <= lax.broadcasted_iota(jnp.int32, (C, C), 0)).astype(F32)

    def part_a(items, ctx):
        CH = [(b, p, ci) for b, ci in items for p in range(npair)]
        rs = lambda ci: pl.ds(ci * C, C)
        r = [r_ref[b, p, rs(ci), :] for b, p, ci in CH]
        k = [k_ref[b, p, rs(ci), :] for b, p, ci in CH]
        v = [v_ref[b, p, rs(ci), :] for b, p, ci in CH]
        lw = [lw_ref[b, p, rs(ci), :] for b, p, ci in CH]
        a = [a_ref[b, p, rs(ci), :] for b, p, ci in CH]
        kn = [k_ * kk_ref[p] for k_, (b, p, ci) in zip(k, CH)]
        kk = _each(lambda z: z * lax.rsqrt(_seg_sum(z * z) + RW_NORM_EPS), kn)
        kh = [k_ * (1.0 + (a_ - 1.0) * ka_ref[p]) for k_, a_, (b, p, ci) in zip(k, a, CH)]
        bvec = _each(lambda x, y: x * y, kk, a)
        cum = _each(lambda x: _dot_exact_lhs(tri, x), lw)
        yield
        cum_last = [c_[C - 1:C, :] for c_ in cum]
        e_neg = _each(lambda c_: jnp.exp(-c_), cum)
        e_dec = _each(lambda cl, c_: jnp.exp(cl - c_), cum_last, cum)
        At = _each(lambda kk_, c_, lw_: -kk_ * jnp.exp(c_ - lw_), kk, cum, lw)
        Rt_ = _each(lambda r_, c_: r_ * jnp.exp(c_), r, cum)
        AR = _each(lambda x, y: jnp.concatenate([x, y], axis=0), At, Rt_)
        BKs = _each(lambda b_, kh_, e: jnp.concatenate([_stack2(b_ * e), _stack2(kh_ * e)], axis=0), bvec, kh, e_neg)
        V2 = _each(_stack2, v)
        sc = _each(lambda x, y: _dot(x, y, _NT), AR, BKs)
        yield
        Lab = [jnp.where(strict, z[:C, :C2], 0.0) for z in sc]
        Lak = [jnp.where(strict, z[:C, C2:], 0.0) for z in sc]
        Mrbk = [jnp.concatenate([jnp.where(incl, z[C:, :C2], 0.0), jnp.where(incl, z[C:, C2:], 0.0)], axis=1)
                for z in sc]
        LV = _each(_dot, Lak, V2)
        Tinv = yield from _unit_lower_inverse(Lab)
        for n, key in enumerate(CH):
            ctx[key] = dict(AR=AR[n], V2=V2[n], LV=LV[n], Tinv=Tinv[n], Mrbk=Mrbk[n], v=v[n], r=r[n], kh=kh[n],
                            sdec=jnp.exp(cum_last[n]), bkdec=jnp.concatenate([bvec[n] * e_dec[n], kh[n] * e_dec[n]], 0))

    def part_b(items, ctx):
        for ci in sorted({ci for _, ci in items}):
            CH = [(b, p, ci) for b, c_ in items if c_ == ci for p in range(npair)]
            X = [ctx.pop(key) for key in CH]
            S = [sscr[b * npair + p] for b, p, _ in CH]
            UY0 = _each(lambda x, s_: _dot(x["AR"], s_, _NT), X, S)
            yield
            U = _each(lambda x, u: _dot(x["Tinv"], _stack2(u[:C, :] + x["LV"])), X, UY0)
            yield
            y = _each(lambda x, u0, u: u0[C:, :] + _dot(x["Mrbk"], jnp.concatenate([_stack2(u), x["V2"]], axis=0)),
                      X, UY0, U)
            Snew = _each(lambda x, s_, u: s_ * x["sdec"] + jnp.where(
                same_head, _dot(jnp.concatenate([u, x["v"]], axis=0), x["bkdec"], _TN), 0.0), X, S, U)
            yield
            for n, (b, p, _) in enumerate(CH):
                sscr[b * npair + p] = Snew[n]
                m = _seg_sum(y[n]) * (1.0 / RW_N)
                yc = y[n] - m
                var = _seg_sum(yc * yc) * (1.0 / RW_N)
                yn = yc * lax.rsqrt(var + RW_GN_EPS) * gg_ref[p] + gb_ref[p]
                bonus = _seg_sum(X[n]["r"] * X[n]["kh"] * rk_ref[p]) * X[n]["v"]
                oscr[b * Rt + ci * C:b * Rt + (ci + 1) * C, LANES * p:LANES * (p + 1)] = yn + bonus
            yield

    _run_pipelined(part_a, part_b, batches)

    row_h = lax.broadcasted_iota(jnp.int32, (LANES, RW_N), 0) < RW_N

    @pl.when(tb == pl.num_programs(1) - 1)
    def _final_state():
        def fin_b(b, carry):
            for p in range(npair):
                S = sscr[b * npair + p]
                sT_ref[b, p] = jnp.where(row_h, S[:, :RW_N], S[:, RW_N:])
            return carry
        lax.fori_loop(0, bb, fin_b, 0)

    _gated_out_ln(oscr[...], g_ref, x_ref, wout_ref, lng_ref, lnb_ref, out_ref)


def _rwkv_rec(r4, k4, v4, lw4, a4, g4, x, s0, k_k, k_a, r_k, gn_g, gn_b, wout, lng, lnb, *, C, Rt, bb, gb):
    B, npair, T, _ = r4.shape
    D = x.shape[-1]
    kern = functools.partial(_rwkv_rec_kernel, C=C, batches=_rec_batches(bb, Rt // C, gb, RW_CHUNKS_PER_BATCH))
    tspec = pl.BlockSpec((bb, npair, Rt, LANES), lambda b, t: (b, 0, t, 0))
    xspec = pl.BlockSpec((bb, Rt, D), lambda b, t: (b, t, 0))
    sspec = pl.BlockSpec((bb, npair, LANES, RW_N), lambda b, t: (b, 0, 0, 0))
    wspec = _const_spec((npair, 1, LANES))
    return pl.pallas_call(
        kern,
        grid=(B // bb, T // Rt),
        in_specs=[tspec] * 6 + [xspec, sspec] + [wspec] * 5 + [_const_spec(wout.shape), _const_spec(lng.shape),
                                                              _const_spec(lnb.shape)],
        out_specs=[xspec, sspec],
        out_shape=[jax.ShapeDtypeStruct(x.shape, F32), jax.ShapeDtypeStruct(s0.shape, F32)],
        scratch_shapes=[pltpu.VMEM((bb * npair, LANES, LANES), F32), pltpu.VMEM((bb * Rt, RW_W), F32)],
        compiler_params=_cparams(2),
        name="rwkv_rec",
    )(r4, k4, v4, lw4, a4, g4, x, s0, k_k, k_a, r_k, gn_g, gn_b, wout, lng, lnb)


def _rwkv_step_kernel(r_ref, k_ref, v_ref, lw_ref, a_ref, s0_ref, kk_ref, ka_ref, rk_ref, gg_ref, gb_ref,
                      o_ref, sT_ref, vscr, yscr, *, T, B):
    N = RW_N
    IB = SUBLANES
    for t in range(T):
        rows = pl.ds(t * B, B)
        rT, kT, vT, aT = r_ref[rows, :].T, k_ref[rows, :].T, v_ref[rows, :].T, a_ref[rows, :].T
        w = jnp.exp(lw_ref[rows, :].T)
        kn = kT * kk_ref[...]
        kh = kT * (1.0 + (aT - 1.0) * ka_ref[...])
        vscr[...] = vT
        src = s0_ref if t == 0 else sT_ref
        bonus = []
        for h in range(2):
            hs = slice(N * h, N * (h + 1))
            kk = kn[hs] * lax.rsqrt(jnp.sum(kn[hs] * kn[hs], axis=0, keepdims=True) + RW_NORM_EPS)
            a_h, b_h, k_h, w_h, r_h = -kk, kk * aT[hs], kh[hs], w[hs], rT[hs]

            def step(ib, carry, h=h, a_h=a_h, b_h=b_h, k_h=k_h, w_h=w_h, r_h=r_h, src=src):
                i0 = pl.multiple_of(ib * IB, IB)
                S = src[h, pl.ds(i0, IB), :, :]
                sa = jnp.sum(S * a_h[None], axis=1)
                vb = vscr[pl.ds(N * h + i0, IB), :]
                Sn = S * w_h[None] + sa[:, None, :] * b_h[None] + vb[:, None, :] * k_h[None]
                sT_ref[h, pl.ds(i0, IB), :, :] = Sn
                yscr[pl.ds(N * h + i0, IB), :] = jnp.sum(Sn * r_h[None], axis=1)
                return carry

            lax.fori_loop(0, N // IB, step, 0)
            bonus.append(jnp.sum(r_h * k_h * rk_ref[hs, :], axis=0, keepdims=True) * vT[hs])
        y = yscr[...]
        outs = []
        for h in range(2):
            hs = slice(N * h, N * (h + 1))
            m = jnp.mean(y[hs], axis=0, keepdims=True)
            yc = y[hs] - m
            var = jnp.mean(yc * yc, axis=0, keepdims=True)
            outs.append(yc * lax.rsqrt(var + RW_GN_EPS) * gg_ref[hs, :] + gb_ref[hs, :] + bonus[h])
        o_ref[rows, :] = jnp.concatenate(outs, axis=0).T


def _rwkv_step(r4, k4, v4, lw4, a4, s0, k_k, k_a, r_k, gn_g, gn_b, *, T, B):
    npair = r4.shape[1]
    kern = functools.partial(_rwkv_step_kernel, T=T, B=B)
    tspec = pl.BlockSpec((None, None, T * B, LANES), lambda p: (0, p, 0, 0))
    sspec = pl.BlockSpec((None, 2, RW_N, RW_N, B), lambda p: (p, 0, 0, 0, 0))
    wspec = pl.BlockSpec((None, LANES, 1), lambda p: (p, 0, 0))
    return pl.pallas_call(
        kern,
        grid=(npair,),
        in_specs=[tspec] * 5 + [sspec] + [wspec] * 5,
        out_specs=[tspec, sspec],
        out_shape=[jax.ShapeDtypeStruct(r4.shape, F32), jax.ShapeDtypeStruct(s0.shape, F32)],
        scratch_shapes=[pltpu.VMEM((LANES, B), F32), pltpu.VMEM((LANES, B), F32)],
        compiler_params=_cparams(1),
        name="rwkv_step",
    )(r4, k4, v4, lw4, a4, s0, k_k, k_a, r_k, gn_g, gn_b)


def _gdn_pre_kernel(x_ref, halo_ref, win_ref, cw_ref, alog_ref, dtb_ref,
                    q_ref, k_ref, v_ref, z_ref, bg_ref, tail_ref, xscr, gscr, *, s, R, H, Hg, C):
    c = pl.program_id(1)
    CH = GDN_CONV_CH
    tt = R // s

    @pl.when(c == 0)
    def _init():
        xscr[0:H, :] = halo_ref[...]
        gscr[0:Hg, :] = jnp.zeros((Hg, LANES), F32)

    x = x_ref[...]
    u = _mm(x, win_ref[...])
    xb = u[:, :CH]
    xscr[H:H + R, :] = xb
    y = xb * cw_ref[3]
    for d in (1, 2, 3):
        y = y + xscr[pl.ds(H - d * s, R), :] * cw_ref[3 - d]
    tail = xscr[pl.ds(R, H), :]
    xscr[0:H, :] = tail
    tail_ref[...] = tail
    qkv = _silu(y)

    def l2n(z, scale):
        return z * (lax.rsqrt(jnp.sum(z * z, axis=-1, keepdims=True) + GDN_EPS) * scale)

    for h in range(GDN_HK):
        q_ref[h] = l2n(qkv[:, LANES * h:LANES * (h + 1)], GDN_DK ** -0.5)
        k_ref[h] = l2n(qkv[:, GDN_KEY_W + LANES * h:GDN_KEY_W + LANES * (h + 1)], 1.0)
    for h in range(GDN_HV):
        v_ref[h] = qkv[:, 2 * GDN_KEY_W + LANES * h:2 * GDN_KEY_W + LANES * (h + 1)]
        z_ref[h] = u[:, CH + LANES * h:CH + LANES * (h + 1)]

    bg = u[:, CH + GDN_VAL_W:CH + GDN_VAL_W + LANES]
    beta = jax.nn.sigmoid(bg)
    g = -jnp.exp(alog_ref[...]) * jax.nn.softplus(bg + dtb_ref[...])
    t_in = (lax.broadcasted_iota(jnp.int32, (R, 1), 0) // s) % C
    k = 1
    while k < min(C, tt):
        gscr[Hg:Hg + R, :] = g
        g = g + jnp.where(t_in >= k, gscr[pl.ds(Hg - k * s, R), :], 0.0)
        k *= 2
    lane = lax.broadcasted_iota(jnp.int32, (R, LANES), 1)
    bg_ref[...] = jnp.where(lane < GDN_HV, beta, g)


def _gdn_pre(x3, halo, win, cw, alog, dtb, *, s, R, C):
    NB, TT, D = x3.shape
    H = halo.shape[1]
    tt = R // s
    Hg = max(SUBLANES, (min(C, tt) // 2) * s)
    kern = functools.partial(_gdn_pre_kernel, s=s, R=R, H=H, Hg=Hg, C=C)

    def ospec(nh):
        return pl.BlockSpec((None, nh, R, LANES), lambda i, c: (i, 0, c, 0))

    def oshape(nh):
        return jax.ShapeDtypeStruct((NB, nh, TT, LANES), F32)

    return pl.pallas_call(
        kern,
        grid=(NB, TT // R),
        in_specs=[
            pl.BlockSpec((None, R, D), lambda i, c: (i, c, 0)),
            pl.BlockSpec((None, H, GDN_CONV_CH), lambda i, c: (i, 0, 0)),
            _const_spec(win.shape), _const_spec(cw.shape), _const_spec(alog.shape), _const_spec(dtb.shape),
        ],
        out_specs=[ospec(GDN_HK), ospec(GDN_HK), ospec(GDN_HV), ospec(GDN_HV),
                   pl.BlockSpec((None, R, LANES), lambda i, c: (i, c, 0)),
                   pl.BlockSpec((None, H, GDN_CONV_CH), lambda i, c: (i, 0, 0))],
        out_shape=[oshape(GDN_HK), oshape(GDN_HK), oshape(GDN_HV), oshape(GDN_HV),
                   jax.ShapeDtypeStruct((NB, TT, LANES), F32),
                   jax.ShapeDtypeStruct((NB, H, GDN_CONV_CH), F32)],
        scratch_shapes=[pltpu.VMEM((H + R, GDN_CONV_CH), F32), pltpu.VMEM((Hg + R, LANES), F32)],
        compiler_params=_cparams(2),
        name="gdn_pre",
    )(x3, halo, win, cw, alog, dtb)


def _gdn_rec_kernel(q_ref, k_ref, v_ref, z_ref, bg_ref, x_ref, s0_ref, ng_ref, wout_ref, lng_ref, lnb_ref,
                    out_ref, sT_ref, sscr, oscr, *, C, batches):
    tb = pl.program_id(1)
    bb, nhv, Rt, _ = v_ref.shape
    rep = GDN_HV // GDN_HK

    @pl.when(tb == 0)
    def _init():
        sscr[...] = s0_ref[...].reshape(sscr.shape)

    strict, incl = _packed_masks(C)
    lo = lax.broadcasted_iota(jnp.int32, (C, 2 * C), 1) < C

    def part_a(items, ctx):
        CH = [(b, h, ci) for b, ci in items for h in range(nhv)]
        KH = [(b, m, ci) for b, ci in items for m in range(GDN_HK)]
        rs = lambda ci: pl.ds(ci * C, C)
        bgs = {(b, ci): bg_ref[b, rs(ci), :] for b, ci in items}
        bgT = {key: x.T for key, x in bgs.items()}

        kq = [jnp.concatenate([k_ref[b, m, rs(ci), :], q_ref[b, m, rs(ci), :]], axis=0) for b, m, ci in KH]
        sc = _each(lambda x: _dot(x, jnp.concatenate([x[:C, :], x[:C, :]], axis=0), _NT), kq)
        yield

        def col2(x, lane):
            return jnp.where(lo, jnp.broadcast_to(x[:, lane:lane + 1], (C, 2 * C)),
                             jnp.broadcast_to(x[:, lane + 1:lane + 2], (C, 2 * C)))

        bcol = [col2(bgs[b, ci], rep * m) for b, m, ci in KH]
        gcol = [col2(bgs[b, ci], GDN_HV + rep * m) for b, m, ci in KH]
        grow = [jnp.concatenate([bgT[b, ci][GDN_HV + rep * m:GDN_HV + rep * m + 1, :],
                                 bgT[b, ci][GDN_HV + rep * m + 1:GDN_HV + rep * m + 2, :]], axis=1) for b, m, ci in KH]
        diff = _each(lambda c_, r_: c_ - r_, gcol, grow)
        Lp = _each(lambda z, bc, d: jnp.where(strict, z[:C, :] * bc * jnp.exp(jnp.where(strict, d, 0.0)), 0.0),
                   sc, bcol, diff)
        Ap = _each(lambda z, d: jnp.where(incl, z[C:, :] * jnp.exp(jnp.where(incl, d, 0.0)), 0.0), sc, diff)
        Tinv_p = yield from _unit_lower_inverse([-l_ for l_ in Lp])

        half = lambda xs, n, h: xs[n // rep][:, (h % rep) * C:(h % rep + 1) * C]
        Tinv = [half(Tinv_p, n, h) for n, (b, h, ci) in enumerate(CH)]
        A = [half(Ap, n, h) for n, (b, h, ci) in enumerate(CH)]
        k = [kq[n // rep][:C, :] for n in range(len(CH))]
        q = [kq[n // rep][C:, :] for n in range(len(CH))]
        v = [v_ref[b, h, rs(ci), :] for b, h, ci in CH]
        beta = [jnp.broadcast_to(bgs[b, ci][:, h:h + 1], (C, LANES)) for b, h, ci in CH]
        gc = [jnp.broadcast_to(bgs[b, ci][:, GDN_HV + h:GDN_HV + h + 1], (C, LANES)) for b, h, ci in CH]
        kb = _each(lambda x, y: x * y, k, beta)
        eg = _each(jnp.exp, gc)
        UW = _each(lambda t, v_, b_, kb_, e: _dot(t, jnp.concatenate([v_ * b_, kb_ * e], axis=1)),
                   Tinv, v, beta, kb, eg)
        yield
        for n, key in enumerate(CH):
            g_last = gc[n][C - 1:C, :]
            ctx[key] = dict(U=UW[n][:, :GDN_DV], WQl=jnp.concatenate([UW[n][:, GDN_DV:], q[n] * eg[n]], axis=0),
                            A=A[n], kdec=k[n] * jnp.exp(g_last - gc[n]), sdec=jnp.exp(g_last))

    def part_b(items, ctx):
        for ci in sorted({ci for _, ci in items}):
            CH = [(b, h, ci) for b, c_ in items if c_ == ci for h in range(nhv)]
            X = [ctx.pop(key) for key in CH]
            S = [sscr[b * nhv + h] for b, h, _ in CH]
            WQ = _each(lambda x, s_: _dot(x["WQl"], s_), X, S)
            yield
            v_new = _each(lambda x, wq: x["U"] - wq[:C, :], X, WQ)
            o = _each(lambda x, wq, vn: wq[C:, :] + _dot(x["A"], vn), X, WQ, v_new)
            Snew = _each(lambda x, s_, vn: s_ * x["sdec"] + _dot(x["kdec"], vn, _TN), X, S, v_new)
            yield
            for n, (b, h, _) in enumerate(CH):
                sscr[b * nhv + h] = Snew[n]
                oh = o[n]
                oscr[b * Rt + ci * C:b * Rt + (ci + 1) * C, LANES * h:LANES * (h + 1)] = (
                    oh * lax.rsqrt(jnp.mean(oh * oh, axis=-1, keepdims=True) + GDN_EPS) * ng_ref[...])
            yield

    _run_pipelined(part_a, part_b, batches)

    @pl.when(tb == pl.num_programs(1) - 1)
    def _final_state():
        sT_ref[...] = sscr[...].reshape(sT_ref.shape)

    _gated_out_ln(oscr[...], z_ref, x_ref, wout_ref, lng_ref, lnb_ref, out_ref)


def _gdn_rec(q4, k4, v4, z4, bg, x, s0, norm_g, wout, lng, lnb, *, C, Rt, bb, gb):
    B, _, T, _ = v4.shape
    D = x.shape[-1]
    kern = functools.partial(_gdn_rec_kernel, C=C, batches=_rec_batches(bb, Rt // C, gb, GDN_CHUNKS_PER_BATCH))

    def tspec(nh):
        return pl.BlockSpec((bb, nh, Rt, LANES), lambda b, t: (b, 0, t, 0))

    xspec = pl.BlockSpec((bb, Rt, D), lambda b, t: (b, t, 0))
    sspec = pl.BlockSpec((bb, GDN_HV, GDN_DK, GDN_DV), lambda b, t: (b, 0, 0, 0))
    return pl.pallas_call(
        kern,
        grid=(B // bb, T // Rt),
        in_specs=[tspec(GDN_HK), tspec(GDN_HK), tspec(GDN_HV), tspec(GDN_HV),
                  pl.BlockSpec((bb, Rt, LANES), lambda b, t: (b, t, 0)), xspec, sspec,
                  _const_spec(norm_g.shape), _const_spec(wout.shape), _const_spec(lng.shape), _const_spec(lnb.shape)],
        out_specs=[xspec, sspec],
        out_shape=[jax.ShapeDtypeStruct(x.shape, F32), jax.ShapeDtypeStruct(s0.shape, F32)],
        scratch_shapes=[pltpu.VMEM((bb * GDN_HV, GDN_DK, GDN_DV), F32), pltpu.VMEM((bb * Rt, GDN_VAL_W), F32)],
        compiler_params=_cparams(2),
        name="gdn_rec",
    )(q4, k4, v4, z4, bg, x, s0, norm_g, wout, lng, lnb)


class _Group:
    def __init__(self, B, T, time_major):
        self.B, self.T, self.time_major = B, T, time_major
        if time_major:
            self.s, self.NB, self.TT, self.R = B, 1, T * B, T * B
        else:
            self.s, self.NB, self.TT, self.R = 1, B, T, min(ROW_BLOCK, T)

    def to_rows(self, x):
        if self.time_major:
            return jnp.swapaxes(x, 0, 1).reshape(1, self.TT, x.shape[-1])
        return x

    def from_rows(self, x3):
        if self.time_major:
            return jnp.swapaxes(x3.reshape(self.T, self.B, x3.shape[-1]), 0, 1)
        return x3

    def halo(self, st, n_steps):
        if self.time_major:
            return jnp.swapaxes(st, 0, 1).reshape(1, n_steps * self.B, st.shape[-1])
        H = _halo_rows(n_steps, 1)
        return jnp.pad(st, ((0, 0), (H - n_steps, 0), (0, 0)))

    def unhalo(self, tail, n_steps):
        if self.time_major:
            return jnp.swapaxes(tail.reshape(n_steps, self.B, tail.shape[-1]), 0, 1)
        return tail[:, tail.shape[1] - n_steps:]

    def vec(self, st):
        return st[None] if self.time_major else st[:, None, :]

    def unvec(self, v):
        return v[0] if self.time_major else v[:, 0, :]

    def heads_to_batch(self, a4, Tpad):
        if not self.time_major:
            return a4
        nh = a4.shape[1]
        a = a4.reshape(nh, self.T, self.B, LANES).transpose(2, 0, 1, 3)
        return jnp.pad(a, ((0, 0), (0, 0), (0, Tpad - self.T), (0, 0)))

    def rows_to_batch(self, a3, Tpad):
        if not self.time_major:
            return a3
        return jnp.pad(self.from_rows(a3), ((0, 0), (0, Tpad - self.T), (0, 0)))

    def rows_from_batch(self, a3):
        if not self.time_major:
            return a3
        return self.to_rows(a3[:, :self.T])

    def rec_tiling(self, chunk, block):
        if self.time_major:
            Tp = -(-self.T // SUBLANES) * SUBLANES
            return Tp, Tp, Tp, min(self.B, SAMPLE_SEQ_BLOCK), min(self.B, SAMPLE_SEQ_GROUP)
        return min(chunk, self.T), self.T, min(block, self.T), PROMPT_SEQ_BLOCK if self.B % PROMPT_SEQ_BLOCK == 0 else 1, 1


def _lru_apply(g, x3, conv_st, h_st, p, lng, lnb):
    swap = not g.time_major
    assert g.B % SUBLANES == 0, "RG-LRU kernel needs the batch to fill whole sublane tiles"
    lay = _Group(g.B, g.T, True) if swap else g
    out, tail, hT = _lru_layer(x3, lay.halo(conv_st, CONV_W - 1), lay.vec(h_st), p["win"], p["cw"], p["cb"],
                               p["wg"], p["bg"], p["lam"], p["wout"], lng, lnb, s=lay.s,
                               R=min(LRU_BLOCK, g.T * g.B) if swap else g.R, swap=swap)
    return out, lay.unhalo(tail, CONV_W - 1), lay.unvec(hT)


def _rwkv_apply(g, x3, shift_st, wkv_st, p, lng, lnb):
    B = g.B
    pre = _rwkv_pre(x3, g.halo(shift_st[:, None, :], 1), p["mu"], p["win"], p["w0"], p["w1"], p["w2"], p["a0"],
                    p["a1"], p["a2"], s=g.s, R=g.R)
    new_shift = g.from_rows(x3)[:, -1]
    if g.time_major:
        r4, k4, v4, g4, lw4, a4 = pre
        s0 = jnp.transpose(wkv_st, (1, 2, 3, 0)).reshape(RW_H // 2, 2, RW_N, RW_N, B)
        col = lambda v: jnp.swapaxes(v, 1, 2)
        o4, sT = _rwkv_step(r4, k4, v4, lw4, a4, s0, col(p["k_k"]), col(p["k_a"]), col(p["r_k"]), col(p["gn_g"]),
                            col(p["gn_b"]), T=g.T, B=B)
        out = _post_layer(x3, o4, g4, p["wout"], lng, lnb)
        return out, new_shift, jnp.transpose(sT.reshape(RW_H, RW_N, RW_N, B), (3, 0, 1, 2))
    C, Tp, Rt, bb, gb = g.rec_tiling(RW_CHUNK, RW_BLOCK)
    r4, k4, v4, g4, lw4, a4 = [g.heads_to_batch(a, Tp) for a in pre]
    s0 = wkv_st.reshape(B, RW_H // 2, 2 * RW_N, RW_N)
    out, sT = _rwkv_rec(r4, k4, v4, lw4, a4, g4, g.rows_to_batch(x3, Tp), s0, p["k_k"], p["k_a"], p["r_k"],
                        p["gn_g"], p["gn_b"], p["wout"], lng, lnb, C=C, Rt=Rt, bb=bb, gb=gb)
    new_shift = g.from_rows(x3)[:, -1]
    return g.rows_from_batch(out), new_shift, sT.reshape(B, RW_H, RW_N, RW_N)


def _gdn_apply(g, x3, conv_st, S_st, p, lng, lnb):
    T = g.T
    C, Tp, Rt, bb, gb = g.rec_tiling(GDN_CHUNK, GDN_BLOCK)
    q4, k4, v4, z4, bg, tail = _gdn_pre(x3, g.halo(conv_st, CONV_W - 1), p["win"], p["cw"], p["alog"], p["dtb"],
                                        s=g.s, R=g.R, C=min(GDN_CHUNK, T))
    bgb = g.rows_to_batch(bg, T)
    if Tp != T:
        held = jnp.where(jnp.arange(LANES) < GDN_HV, 0.0, bgb[:, T - 1:T, :])
        bgb = jnp.concatenate([bgb, jnp.broadcast_to(held, (g.B, Tp - T, LANES))], axis=1)
    tb = lambda a: g.heads_to_batch(a, Tp)
    out, sT = _gdn_rec(tb(q4), tb(k4), tb(v4), tb(z4), bgb, g.rows_to_batch(x3, Tp), S_st, p["norm_g"], p["wout"],
                       lng, lnb, C=C, Rt=Rt, bb=bb, gb=gb)
    return g.rows_from_batch(out), g.unhalo(tail, CONV_W - 1), sT


def _trunk(g, x, st, params, ln_g, ln_b):
    lru_conv, lru_h, rw_shift, rw_S, gdn_conv, gdn_S = st
    new = ([], [], [], [], [], [])
    x3 = g.to_rows(x)
    ia = ib = ic = 0
    for layer in range(DEPTH):
        kind = layer % N_MIXERS
        lng, lnb = ln_g[layer][None, :], ln_b[layer][None, :]
        if kind == 0:
            x3, c, h = _lru_apply(g, x3, lru_conv[ia], lru_h[ia], params["lru"][ia], lng, lnb)
            new[0].append(c)
            new[1].append(h)
            ia += 1
        elif kind == 1:
            x3, sh, S = _rwkv_apply(g, x3, rw_shift[ib], rw_S[ib], params["rwkv"][ib], lng, lnb)
            new[2].append(sh)
            new[3].append(S)
            ib += 1
        else:
            x3, c, S = _gdn_apply(g, x3, gdn_conv[ic], gdn_S[ic], params["gdn"][ic], lng, lnb)
            new[4].append(c)
            new[5].append(S)
            ic += 1
    return g.from_rows(x3), tuple(s[0][None] if len(s) == 1 else jnp.stack(s) for s in new)


def _prep_params(lru_w_in, lru_conv_w, lru_conv_b, lru_wa, lru_ba, lru_wx, lru_bx, lru_lambda, lru_w_out, rw_mu,
                 rw_w_in, rw_w0, rw_w1, rw_w2, rw_a0, rw_a1, rw_a2, rw_k_k, rw_k_a, rw_r_k, rw_gn_g, rw_gn_b,
                 rw_w_out, gdn_w_in, gdn_conv_w, gdn_a_log, gdn_dt_bias, gdn_norm_g, gdn_w_out):
    row = lambda v: v[None, :]
    lru = []
    for n in range(lru_w_in.shape[0]):
        lru.append(dict(win=lru_w_in[n].astype(BF), cw=lru_conv_w[n][:, None, :], cb=row(lru_conv_b[n]),
                        wg=_lru_gate_weights(lru_wa[n], lru_wx[n]), bg=jnp.stack([lru_ba[n], lru_bx[n]])[:, None, :],
                        lam=row(lru_lambda[n]), wout=lru_w_out[n].astype(BF)))
    pairw = lambda v: v.reshape(RW_W // LANES, 1, LANES)
    rwkv = []
    for n in range(rw_w_in.shape[0]):
        rwkv.append(dict(mu=rw_mu[n][:, None, :],win=rw_w_in[n].astype(BF), w0=row(rw_w0[n]), w1=rw_w1[n].astype(BF),
                         w2=rw_w2[n].astype(BF), a0=row(rw_a0[n]), a1=rw_a1[n].astype(BF), a2=rw_a2[n].astype(BF),
                         k_k=pairw(rw_k_k[n]), k_a=pairw(rw_k_a[n]), r_k=pairw(rw_r_k[n]), gn_g=pairw(rw_gn_g[n]),
                         gn_b=pairw(rw_gn_b[n]), wout=rw_w_out[n].astype(BF)))
    gdn = []
    for n in range(gdn_w_in.shape[0]):
        w = gdn_w_in[n]
        o2 = GDN_CONV_CH + GDN_VAL_W
        wpad = jnp.pad(w[:, o2:], ((0, 0), (0, LANES - 2 * GDN_HV)))
        lanes = lambda v: jnp.pad(v, (GDN_HV, LANES - 2 * GDN_HV))[None, :]
        gdn.append(dict(win=jnp.concatenate([w[:, :o2], wpad], axis=1).astype(BF), cw=gdn_conv_w[n][:, None, :],
                        alog=lanes(gdn_a_log[n]), dtb=lanes(gdn_dt_bias[n]), norm_g=row(gdn_norm_g[n]),
                        wout=gdn_w_out[n].astype(BF)))
    return dict(lru=lru, rwkv=rwkv, gdn=gdn)


def kernel(x_prompt, x_sample, state_lru_conv, state_lru_h, state_rwkv_shift, state_rwkv_wkv, state_gdn_conv, state_gdn_S, ln_g, ln_b, lru_w_in, lru_conv_w, lru_conv_b, lru_wa, lru_ba, lru_wx, lru_bx, lru_lambda, lru_w_out, rw_mu, rw_w_in, rw_w0, rw_w1, rw_w2, rw_a0, rw_a1, rw_a2, rw_k_k, rw_k_a, rw_r_k, rw_gn_g, rw_gn_b, rw_w_out, gdn_w_in, gdn_conv_w, gdn_a_log, gdn_dt_bias, gdn_norm_g, gdn_w_out):
    params = _prep_params(lru_w_in, lru_conv_w, lru_conv_b, lru_wa, lru_ba, lru_wx, lru_bx, lru_lambda, lru_w_out,
                          rw_mu, rw_w_in, rw_w0, rw_w1, rw_w2, rw_a0, rw_a1, rw_a2, rw_k_k, rw_k_a, rw_r_k, rw_gn_g,
                          rw_gn_b, rw_w_out, gdn_w_in, gdn_conv_w, gdn_a_log, gdn_dt_bias, gdn_norm_g, gdn_w_out)
    bp, tp, _ = x_prompt.shape
    bs, ts, _ = x_sample.shape
    n_a, n_b, n_c = state_lru_conv.shape[0], state_rwkv_shift.shape[0], state_gdn_conv.shape[0]
    zero_state = (jnp.zeros((n_a, bp, CONV_W - 1, LRU_W), F32),
                  jnp.zeros((n_a, bp, LRU_W), F32),
                  jnp.zeros((n_b, bp, D_MODEL), F32),
                  jnp.zeros((n_b, bp, RW_H, RW_N, RW_N), F32),
                  jnp.zeros((n_c, bp, CONV_W - 1, GDN_CONV_CH), F32),
                  jnp.zeros((n_c, bp, GDN_HV, GDN_DK, GDN_DV), F32))
    y_prompt, sp = _trunk(_Group(bp, tp, False), x_prompt, zero_state, params, ln_g, ln_b)
    y_sample, ss = _trunk(_Group(bs, ts, True), x_sample,
                          (state_lru_conv, state_lru_h, state_rwkv_shift, state_rwkv_wkv, state_gdn_conv,
                           state_gdn_S), params, ln_g, ln_b)
    return (y_prompt, y_sample, sp[0], ss[0], sp[1], ss[1], sp[2], ss[2], sp[3], ss[3], sp[4], ss[4], sp[5], ss[5])
```

```python
import functools
import math

import jax
import jax.numpy as jnp
from jax import lax
from jax.experimental import pallas as pl
from jax.experimental.pallas import tpu as pltpu

F32 = jnp.float32
BF = jnp.bfloat16

D_MODEL = 1024
DEPTH = 4
N_MIXERS = 3
DN_ALPHA = (2.0 * DEPTH) ** 0.25
LN_EPS = 1e-5
CONV_W = 4

LRU_W = D_MODEL
LRU_BLOCKS = 16
LRU_BS = LRU_W // LRU_BLOCKS
LRU_C = 8.0

RW_W = D_MODEL
RW_N = 64
RW_H = RW_W // RW_N
RW_GN_EPS = 64e-5
RW_NORM_EPS = 1e-12

GDN_HK = 4
GDN_HV = 8
GDN_DK = 128
GDN_DV = 128
GDN_KEY_W = GDN_HK * GDN_DK
GDN_VAL_W = GDN_HV * GDN_DV
GDN_CONV_CH = 2 * GDN_KEY_W + GDN_VAL_W
GDN_CHUNK = 64
GDN_EPS = 1e-6

LANES = 128
SUBLANES = 8
VMEM_LIMIT = 56 * 1024 * 1024
ROW_BLOCK = 512
LRU_BLOCK = 1024
RW_BLOCK = 256
GDN_BLOCK = 256
RW_CHUNKS_PER_BATCH = 1
GDN_CHUNKS_PER_BATCH = 2
PROMPT_SEQ_BLOCK = 2
SAMPLE_SEQ_BLOCK = 8
SAMPLE_SEQ_GROUP = 4
RW_CHUNK = 64

_NN = (((1,), (0,)), ((), ()))
_NT = (((1,), (1,)), ((), ()))
_TN = (((0,), (0,)), ((), ()))


def _mm(a, b):
    return jnp.dot(a.astype(BF), b.astype(BF), preferred_element_type=F32)


def _dot(a, b, dims=_NN):
    return lax.dot_general(a.astype(BF), b.astype(BF), dims, preferred_element_type=F32)


def _cumsum_rows(x, tri3):
    b1 = x.astype(BF)
    r1 = x - b1.astype(F32)
    b2 = r1.astype(BF)
    b3 = (r1 - b2.astype(F32)).astype(BF)
    return jnp.dot(tri3, jnp.concatenate([b1, b2, b3], axis=0), preferred_element_type=F32)


def _layer_norm(z, g, b):
    mu = jnp.mean(z, axis=-1, keepdims=True)
    zc = z - mu
    var = jnp.mean(zc * zc, axis=-1, keepdims=True)
    return zc * lax.rsqrt(var + LN_EPS) * g + b


def _silu(x):
    return x * jax.nn.sigmoid(x)


def _cparams(n_axes):
    return pltpu.CompilerParams(dimension_semantics=("arbitrary",) * n_axes, vmem_limit_bytes=VMEM_LIMIT)


def _const_spec(shape):
    nd = len(shape)
    return pl.BlockSpec(shape, lambda *_: (0,) * nd, pipeline_mode=pl.Buffered(1))


def _halo_rows(n_steps, s):
    rows = n_steps * s
    return rows if rows % SUBLANES == 0 else SUBLANES


def _lru_kernel(x_ref, halo_ref, h0_ref, win_ref, cw_ref, cb_ref, wg_ref, bg_ref, lam_ref, wout_ref, lng_ref,
                lnb_ref, out_ref, tail_ref, hT_ref, xscr, hscr, *, s, R, H, swap):
    c = pl.program_id(1)
    C = LRU_W
    tt = R // s

    @pl.when(c == 0)
    def _init():
        xscr[0:H, :] = halo_ref[...]
        hscr[...] = h0_ref[...]

    x = x_ref[...]
    if swap:
        x = jnp.swapaxes(x, 0, 1).reshape(R, D_MODEL)
    u = _mm(x, win_ref[...])
    xb = u[:, :C]
    gate = u[:, C:]

    xscr[H:H + R, :] = xb
    xc = xb * cw_ref[3] + cb_ref[...]
    for d in (1, 2, 3):
        xc = xc + xscr[pl.ds(H - d * s, R), :] * cw_ref[3 - d]
    tail = xscr[pl.ds(R, H), :]
    xscr[0:H, :] = tail
    tail_ref[...] = tail

    xcb = xc.astype(BF)
    ra, ix = [], []
    for g in range(C // 256):
        gt = jnp.dot(xcb[:, 256 * g:256 * (g + 1)], wg_ref[g], preferred_element_type=F32)
        ra.append(gt[:, :256])
        ix.append(gt[:, 256:])
    r = jax.nn.sigmoid(jnp.concatenate(ra, axis=1) + bg_ref[0])
    i = jax.nn.sigmoid(jnp.concatenate(ix, axis=1) + bg_ref[1])
    log_a = r * ((-LRU_C) * jax.nn.softplus(-lam_ref[...]))
    a = jnp.exp(log_a)
    b = jnp.sqrt(1.0 - a * a) * i * xc

    hlast = hscr[...]
    hs = []
    for t in range(tt):
        hlast = a[t * s:(t + 1) * s, :] * hlast + b[t * s:(t + 1) * s, :]
        hs.append(hlast)
    h = jnp.concatenate(hs, axis=0)
    hscr[...] = hlast
    hT_ref[...] = hlast

    y = _mm(h * _silu(gate), wout_ref[...])
    out = _layer_norm(DN_ALPHA * x + y, lng_ref[...], lnb_ref[...])
    if swap:
        out = jnp.swapaxes(out.reshape(tt, s, D_MODEL), 0, 1)
    out_ref[...] = out


def _lru_layer(x3, halo, h0, win, cw, cb, wg, bg, lam, wout, lng, lnb, *, s, R, swap):
    C = LRU_W
    D = x3.shape[-1]
    H = halo.shape[1]
    tt = R // s
    assert s % SUBLANES == 0
    if swap:
        NB, TT = 1, x3.shape[1] * s
        xspec = pl.BlockSpec((s, tt, D), lambda i, c: (0, c, 0))
    else:
        NB, TT = x3.shape[:2]
        xspec = pl.BlockSpec((None, R, D), lambda i, c: (i, c, 0))
    kern = functools.partial(_lru_kernel, s=s, R=R, H=H, swap=swap)
    return pl.pallas_call(
        kern,
        grid=(NB, TT // R),
        in_specs=[
            xspec,
            pl.BlockSpec((None, H, C), lambda i, c: (i, 0, 0)),
            pl.BlockSpec((None, s, C), lambda i, c: (i, 0, 0)),
            _const_spec(win.shape), _const_spec(cw.shape), _const_spec(cb.shape), _const_spec(wg.shape),
            _const_spec(bg.shape), _const_spec(lam.shape), _const_spec(wout.shape), _const_spec(lng.shape),
            _const_spec(lnb.shape),
        ],
        out_specs=[
            xspec,
            pl.BlockSpec((None, H, C), lambda i, c: (i, 0, 0)),
            pl.BlockSpec((None, s, C), lambda i, c: (i, 0, 0)),
        ],
        out_shape=[
            jax.ShapeDtypeStruct(x3.shape, F32),
            jax.ShapeDtypeStruct((NB, H, C), F32),
            jax.ShapeDtypeStruct((NB, s, C), F32),
        ],
        scratch_shapes=[
            pltpu.VMEM((H + R, C), F32),
            pltpu.VMEM((s, C), F32),
        ],
        compiler_params=_cparams(2),
        name="lru_layer",
    )(x3, halo, h0, win, cw, cb, wg, bg, lam, wout, lng, lnb)


def _lru_gate_weights(wa, wx):
    def bd(w):
        w4 = w.reshape(4, 4, LRU_BS, LRU_BS)
        eye = jnp.eye(4, dtype=w.dtype)
        return jnp.einsum("gaij,ab->gaibj", w4, eye).reshape(4, 256, 256)
    return jnp.concatenate([bd(wa), bd(wx)], axis=2).astype(BF)


def _gated_out_ln(o, g_ref, x_ref, w_ref, lng_ref, lnb_ref, out_ref):
    bb, nh, Rt, _ = g_ref.shape
    rows = bb * Rt
    g = jnp.concatenate([g_ref[:, p, :, :].reshape(rows, LANES) for p in range(nh)], axis=1)
    y = _mm(o * _silu(g), w_ref[...])
    x = x_ref[...].reshape(rows, D_MODEL)
    out_ref[...] = _layer_norm(DN_ALPHA * x + y, lng_ref[...], lnb_ref[...]).reshape(bb, Rt, D_MODEL)


def _post_kernel(x_ref, o_ref, g_ref, w_ref, lng_ref, lnb_ref, out_ref):
    o = jnp.concatenate([o_ref[0, p] for p in range(o_ref.shape[1])], axis=1)
    _gated_out_ln(o, g_ref, x_ref, w_ref, lng_ref, lnb_ref, out_ref)


def _post_layer(x3, o4, g4, w, lng, lnb):
    return pl.pallas_call(
        _post_kernel,
        out_shape=jax.ShapeDtypeStruct(x3.shape, F32),
        compiler_params=pltpu.CompilerParams(vmem_limit_bytes=VMEM_LIMIT),
        name="post_layer",
    )(x3, o4, g4, w, lng, lnb)


def _rwkv_pre_kernel(x_ref, halo_ref, mu_ref, win_ref, w0_ref, w1_ref, w2_ref, a0_ref, a1_ref, a2_ref,
                     r_ref, k_ref, v_ref, g_ref, lw_ref, a_ref, xscr, xxscr, *, s, R, H):
    c = pl.program_id(1)

    @pl.when(c == 0)
    def _init():
        xscr[0:H, :] = halo_ref[...]

    x = x_ref[...]
    xscr[H:H + R, :] = x
    xxscr[...] = xscr[pl.ds(H - s, R), :] - x
    xscr[0:H, :] = xscr[pl.ds(R, H), :]
    xx = xxscr[...]
    xm = lambda n: x + xx * mu_ref[n]

    def put(ref, val):
        for p in range(RW_W // LANES):
            ref[p] = val[:, LANES * p:LANES * (p + 1)]

    put(r_ref, _mm(xm(0), win_ref[0]))
    put(k_ref, _mm(xm(1), win_ref[1]))
    put(v_ref, _mm(xm(2), win_ref[2]))
    put(g_ref, _mm(xm(3), win_ref[3]))
    w_raw = w0_ref[...] + _mm(jnp.tanh(_mm(xm(4), w1_ref[...])), w2_ref[...])
    put(lw_ref, (-math.exp(-0.5)) * jax.nn.sigmoid(w_raw))
    put(a_ref, jax.nn.sigmoid(a0_ref[...] + _mm(_mm(xm(5), a1_ref[...]), a2_ref[...])))


def _rwkv_pre(x3, halo, mu, win, w0, w1, w2, a0, a1, a2, *, s, R):
    NB, TT, D = x3.shape
    H = halo.shape[1]
    nh = RW_W // LANES
    kern = functools.partial(_rwkv_pre_kernel, s=s, R=R, H=H)
    ospec = pl.BlockSpec((None, nh, R, LANES), lambda i, c: (i, 0, c, 0))
    oshape = jax.ShapeDtypeStruct((NB, nh, TT, LANES), F32)
    return pl.pallas_call(
        kern,
        grid=(NB, TT // R),
        in_specs=[
            pl.BlockSpec((None, R, D), lambda i, c: (i, c, 0)),
            pl.BlockSpec((None, H, D), lambda i, c: (i, 0, 0)),
            _const_spec(mu.shape), _const_spec(win.shape), _const_spec(w0.shape), _const_spec(w1.shape),
            _const_spec(w2.shape), _const_spec(a0.shape), _const_spec(a1.shape), _const_spec(a2.shape),
        ],
        out_specs=[ospec] * 6,
        out_shape=[oshape] * 6,
        scratch_shapes=[pltpu.VMEM((H + R, D), F32), pltpu.VMEM((R, D), F32)],
        compiler_params=_cparams(2),
        name="rwkv_pre",
    )(x3, halo, mu, win, w0, w1, w2, a0, a1, a2)


def _seg_sum(x):
    lane = lax.broadcasted_iota(jnp.int32, x.shape, 1)
    lo = lane < RW_N
    s0 = jnp.sum(jnp.where(lo, x, 0.0), axis=-1, keepdims=True)
    s1 = jnp.sum(jnp.where(lo, 0.0, x), axis=-1, keepdims=True)
    return jnp.where(lo, s0, s1)


def _stack2(x):
    lane = lax.broadcasted_iota(jnp.int32, x.shape, 1)
    lo = lane < x.shape[1] // 2
    return jnp.concatenate([jnp.where(lo, x, 0.0), jnp.where(lo, 0.0, x)], axis=0)


def _each(f, *lists):
    return [f(*t) for t in zip(*lists)]


def _packed_masks(C):
    t = lax.broadcasted_iota(jnp.int32, (C, 2 * C), 0)
    s = lax.broadcasted_iota(jnp.int32, (C, 2 * C), 1) % C
    return s < t, s <= t


def _stackn(x, n):
    grp = lax.broadcasted_iota(jnp.int32, x.shape, 1) // (x.shape[1] // n)
    return jnp.concatenate([jnp.where(grp == k, x, 0.0) for k in range(n)], axis=0)


def _neumann_inverse(Ps, nblk):
    m = Ps[0].shape[0]
    eye = (lax.broadcasted_iota(jnp.int32, Ps[0].shape, 1) % m) == lax.broadcasted_iota(jnp.int32, Ps[0].shape, 0)
    invs = [jnp.where(eye, 1.0, P) for P in Ps]
    pw = Ps
    span = 2
    while span < m:
        pw = _each(lambda p: _dot(p, _stackn(p, nblk)), pw)
        yield
        invs = _each(lambda inv, p: inv + _dot(inv, _stackn(p, nblk)), invs, pw)
        yield
        span *= 2
    return invs


def _unit_lower_inverse(Ls):
    C = Ls[0].shape[0]
    h = C // 2
    if h % SUBLANES:
        return (yield from _neumann_inverse(Ls, 2))
    grp = lax.broadcasted_iota(jnp.int32, (h, 2 * C), 1) // h
    diag0 = (grp % 2) == 0
    zero = jnp.zeros((h, 2 * C), F32)
    D = yield from _neumann_inverse([jnp.where(diag0, L[:h, :], L[h:, :]) for L in Ls], 4)
    L21 = [jnp.where(diag0, L[h:, :], 0.0) for L in Ls]
    M1 = _each(lambda d, q: _dot(d, jnp.concatenate(
        [zero, jnp.where(grp == 0, q, 0.0), zero, jnp.where(grp == 2, q, 0.0)], axis=0)), D, L21)
    yield
    X21 = _each(lambda m1, d: _dot(m1, jnp.concatenate(
        [jnp.where(grp == 0, d, 0.0), zero, jnp.where(grp == 2, d, 0.0), zero], axis=0)), M1, D)
    yield
    return _each(lambda d, x21: jnp.concatenate([jnp.where(diag0, d, 0.0), x21 + jnp.where(diag0, 0.0, d)], axis=0),
                 D, X21)


def _run_pipelined(part_a, part_b, batches):
    ctx = {}
    prev = None
    for batch in batches:
        gens = [part_a(batch, ctx)] + ([part_b(prev, ctx)] if prev is not None else [])
        while gens:
            for g in list(gens):
                try:
                    next(g)
                except StopIteration:
                    gens.remove(g)
        prev = batch
    for _ in part_b(prev, ctx):
        pass


def _rec_batches(bb, nchunk, gb, cpb):
    if nchunk > 1:
        return tuple(tuple((b, c) for b in range(bb) for c in range(c0, min(c0 + cpb, nchunk)))
                     for c0 in range(0, nchunk, cpb))
    return tuple(tuple((b, 0) for b in range(b0, b0 + gb)) for b0 in range(0, bb, gb))


def _rwkv_rec_kernel(r_ref, k_ref, v_ref, lw_ref, a_ref, g_ref, x_ref, s0_ref, kk_ref, ka_ref, rk_ref, gg_ref,
                     gb_ref, wout_ref, lng_ref, lnb_ref, out_ref, sT_ref, sscr, oscr, *, C, batches):
    tb = pl.program_id(1)
    bb, npair, Rt, _ = r_ref.shape
    C2 = 2 * C
    lane_sq = lax.broadcasted_iota(jnp.int32, (LANES, LANES), 1)
    row_sq = lax.broadcasted_iota(jnp.int32, (LANES, LANES), 0)
    same_head = (lane_sq < RW_N) == (row_sq < RW_N)

    @pl.when(tb == 0)
    def _init():
        def init_b(b, carry):
            for p in range(npair):
                s2 = s0_ref[b, p]
                sscr[b * npair + p] = jnp.where(same_head, jnp.concatenate([s2, s2], axis=1), 0.0)
            return carry
        lax.fori_loop(0, bb, init_b, 0)

    strict, incl = _packed_masks(C)
    tri3 = ((lax.broadcasted_iota(jnp.int32, (C, 3 * C), 1) % C)
            <= lax.broadcasted_iota(jnp.int32, (C, 3 * C), 0)).astype(BF)

    def part_a(items, ctx):
        CH = [(b, p, ci) for b, ci in items for p in range(npair)]
        rs = lambda ci: pl.ds(ci * C, C)
        r = [r_ref[b, p, rs(ci), :] for b, p, ci in CH]
        k = [k_ref[b, p, rs(ci), :] for b, p, ci in CH]
        v = [v_ref[b, p, rs(ci), :] for b, p, ci in CH]
        lw = [lw_ref[b, p, rs(ci), :] for b, p, ci in CH]
        a = [a_ref[b, p, rs(ci), :] for b, p, ci in CH]
        kn = [k_ * kk_ref[p] for k_, (b, p, ci) in zip(k, CH)]
        kk = _each(lambda z: z * lax.rsqrt(_seg_sum(z * z) + RW_NORM_EPS), kn)
        kh = [k_ * (1.0 + (a_ - 1.0) * ka_ref[p]) for k_, a_, (b, p, ci) in zip(k, a, CH)]
        bvec = _each(lambda x, y: x * y, kk, a)
        cum = _each(lambda x: _cumsum_rows(x, tri3), lw)
        yield
        cum_last = [c_[C - 1:C, :] for c_ in cum]
        e_neg = _each(lambda c_: jnp.exp(-c_), cum)
        e_dec = _each(lambda cl, c_: jnp.exp(cl - c_), cum_last, cum)
        At = _each(lambda kk_, c_, lw_: -kk_ * jnp.exp(c_ - lw_), kk, cum, lw)
        Rt_ = _each(lambda r_, c_: r_ * jnp.exp(c_), r, cum)
        AR = _each(lambda x, y: jnp.concatenate([x, y], axis=0), At, Rt_)
        BKs = _each(lambda b_, kh_, e: jnp.concatenate([_stack2(b_ * e), _stack2(kh_ * e)], axis=0), bvec, kh, e_neg)
        V2 = _each(_stack2, v)
        sc = _each(lambda x, y: _dot(x, y, _NT), AR, BKs)
        yield
        Lab = [jnp.where(strict, z[:C, :C2], 0.0) for z in sc]
        Lak = [jnp.where(strict, z[:C, C2:], 0.0) for z in sc]
        Mrbk = [jnp.concatenate([jnp.where(incl, z[C:, :C2], 0.0), jnp.where(incl, z[C:, C2:], 0.0)], axis=1)
                for z in sc]
        LV = _each(_dot, Lak, V2)
        Tinv = yield from _unit_lower_inverse(Lab)
        for n, key in enumerate(CH):
            ctx[key] = dict(AR=AR[n], V2=V2[n], LV=LV[n], Tinv=Tinv[n], Mrbk=Mrbk[n], v=v[n], r=r[n], kh=kh[n],
                            sdec=jnp.exp(cum_last[n]), bkdec=jnp.concatenate([bvec[n] * e_dec[n], kh[n] * e_dec[n]], 0))

    def part_b(items, ctx):
        for ci in sorted({ci for _, ci in items}):
            CH = [(b, p, ci) for b, c_ in items if c_ == ci for p in range(npair)]
            X = [ctx.pop(key) for key in CH]
            S = [sscr[b * npair + p] for b, p, _ in CH]
            UY0 = _each(lambda x, s_: _dot(x["AR"], s_, _NT), X, S)
            yield
            U = _each(lambda x, u: _dot(x["Tinv"], _stack2(u[:C, :] + x["LV"])), X, UY0)
            yield
            y = _each(lambda x, u0, u: u0[C:, :] + _dot(x["Mrbk"], jnp.concatenate([_stack2(u), x["V2"]], axis=0)),
                      X, UY0, U)
            Snew = _each(lambda x, s_, u: s_ * x["sdec"] + jnp.where(
                same_head, _dot(jnp.concatenate([u, x["v"]], axis=0), x["bkdec"], _TN), 0.0), X, S, U)
            yield
            for n, (b, p, _) in enumerate(CH):
                sscr[b * npair + p] = Snew[n]
                m = _seg_sum(y[n]) * (1.0 / RW_N)
                yc = y[n] - m
                var = _seg_sum(yc * yc) * (1.0 / RW_N)
                yn = yc * lax.rsqrt(var + RW_GN_EPS) * gg_ref[p] + gb_ref[p]
                bonus = _seg_sum(X[n]["r"] * X[n]["kh"] * rk_ref[p]) * X[n]["v"]
                oscr[b * Rt + ci * C:b * Rt + (ci + 1) * C, LANES * p:LANES * (p + 1)] = yn + bonus
            yield

    _run_pipelined(part_a, part_b, batches)

    row_h = lax.broadcasted_iota(jnp.int32, (LANES, RW_N), 0) < RW_N

    @pl.when(tb == pl.num_programs(1) - 1)
    def _final_state():
        def fin_b(b, carry):
            for p in range(npair):
                S = sscr[b * npair + p]
                sT_ref[b, p] = jnp.where(row_h, S[:, :RW_N], S[:, RW_N:])
            return carry
        lax.fori_loop(0, bb, fin_b, 0)

    _gated_out_ln(oscr[...], g_ref, x_ref, wout_ref, lng_ref, lnb_ref, out_ref)


def _rwkv_rec(r4, k4, v4, lw4, a4, g4, x, s0, k_k, k_a, r_k, gn_g, gn_b, wout, lng, lnb, *, C, Rt, bb, gb):
    B, npair, T, _ = r4.shape
    D = x.shape[-1]
    kern = functools.partial(_rwkv_rec_kernel, C=C, batches=_rec_batches(bb, Rt // C, gb, RW_CHUNKS_PER_BATCH))
    tspec = pl.BlockSpec((bb, npair, Rt, LANES), lambda b, t: (b, 0, t, 0))
    xspec = pl.BlockSpec((bb, Rt, D), lambda b, t: (b, t, 0))
    sspec = pl.BlockSpec((bb, npair, LANES, RW_N), lambda b, t: (b, 0, 0, 0))
    wspec = _const_spec((npair, 1, LANES))
    return pl.pallas_call(
        kern,
        grid=(B // bb, T // Rt),
        in_specs=[tspec] * 6 + [xspec, sspec] + [wspec] * 5 + [_const_spec(wout.shape), _const_spec(lng.shape),
                                                              _const_spec(lnb.shape)],
        out_specs=[xspec, sspec],
        out_shape=[jax.ShapeDtypeStruct(x.shape, F32), jax.ShapeDtypeStruct(s0.shape, F32)],
        scratch_shapes=[pltpu.VMEM((bb * npair, LANES, LANES), F32), pltpu.VMEM((bb * Rt, RW_W), F32)],
        compiler_params=_cparams(2),
        name="rwkv_rec",
    )(r4, k4, v4, lw4, a4, g4, x, s0, k_k, k_a, r_k, gn_g, gn_b, wout, lng, lnb)


def _rwkv_step_kernel(r_ref, k_ref, v_ref, lw_ref, a_ref, s0_ref, kk_ref, ka_ref, rk_ref, gg_ref, gb_ref,
                      o_ref, sT_ref, vscr, yscr, *, T, B):
    N = RW_N
    IB = SUBLANES
    for t in range(T):
        rows = pl.ds(t * B, B)
        rT, kT, vT, aT = r_ref[rows, :].T, k_ref[rows, :].T, v_ref[rows, :].T, a_ref[rows, :].T
        w = jnp.exp(lw_ref[rows, :].T)
        kn = kT * kk_ref[...]
        kh = kT * (1.0 + (aT - 1.0) * ka_ref[...])
        vscr[...] = vT
        src = s0_ref if t == 0 else sT_ref
        bonus = []
        for h in range(2):
            hs = slice(N * h, N * (h + 1))
            kk = kn[hs] * lax.rsqrt(jnp.sum(kn[hs] * kn[hs], axis=0, keepdims=True) + RW_NORM_EPS)
            a_h, b_h, k_h, w_h, r_h = -kk, kk * aT[hs], kh[hs], w[hs], rT[hs]

            def step(ib, carry, h=h, a_h=a_h, b_h=b_h, k_h=k_h, w_h=w_h, r_h=r_h, src=src):
                i0 = pl.multiple_of(ib * IB, IB)
                S = src[h, pl.ds(i0, IB), :, :]
                sa = jnp.sum(S * a_h[None], axis=1)
                vb = vscr[pl.ds(N * h + i0, IB), :]
                Sn = S * w_h[None] + sa[:, None, :] * b_h[None] + vb[:, None, :] * k_h[None]
                sT_ref[h, pl.ds(i0, IB), :, :] = Sn
                yscr[pl.ds(N * h + i0, IB), :] = jnp.sum(Sn * r_h[None], axis=1)
                return carry

            lax.fori_loop(0, N // IB, step, 0)
            bonus.append(jnp.sum(r_h * k_h * rk_ref[hs, :], axis=0, keepdims=True) * vT[hs])
        y = yscr[...]
        outs = []
        for h in range(2):
            hs = slice(N * h, N * (h + 1))
            m = jnp.mean(y[hs], axis=0, keepdims=True)
            yc = y[hs] - m
            var = jnp.mean(yc * yc, axis=0, keepdims=True)
            outs.append(yc * lax.rsqrt(var + RW_GN_EPS) * gg_ref[hs, :] + gb_ref[hs, :] + bonus[h])
        o_ref[rows, :] = jnp.concatenate(outs, axis=0).T


def _rwkv_step(r4, k4, v4, lw4, a4, s0, k_k, k_a, r_k, gn_g, gn_b, *, T, B):
    npair = r4.shape[1]
    kern = functools.partial(_rwkv_step_kernel, T=T, B=B)
    tspec = pl.BlockSpec((None, None, T * B, LANES), lambda p: (0, p, 0, 0))
    sspec = pl.BlockSpec((None, 2, RW_N, RW_N, B), lambda p: (p, 0, 0, 0, 0))
    wspec = pl.BlockSpec((None, LANES, 1), lambda p: (p, 0, 0))
    return pl.pallas_call(
        kern,
        grid=(npair,),
        in_specs=[tspec] * 5 + [sspec] + [wspec] * 5,
        out_specs=[tspec, sspec],
        out_shape=[jax.ShapeDtypeStruct(r4.shape, F32), jax.ShapeDtypeStruct(s0.shape, F32)],
        scratch_shapes=[pltpu.VMEM((LANES, B), F32), pltpu.VMEM((LANES, B), F32)],
        compiler_params=_cparams(1),
        name="rwkv_step",
    )(r4, k4, v4, lw4, a4, s0, k_k, k_a, r_k, gn_g, gn_b)


def _gdn_pre_kernel(x_ref, halo_ref, win_ref, cw_ref, alog_ref, dtb_ref,
                    q_ref, k_ref, v_ref, z_ref, bg_ref, tail_ref, xscr, gscr, *, s, R, H, Hg, C):
    c = pl.program_id(1)
    CH = GDN_CONV_CH
    tt = R // s

    @pl.when(c == 0)
    def _init():
        xscr[0:H, :] = halo_ref[...]
        gscr[0:Hg, :] = jnp.zeros((Hg, LANES), F32)

    x = x_ref[...]
    u = _mm(x, win_ref[...])
    xb = u[:, :CH]
    xscr[H:H + R, :] = xb
    y = xb * cw_ref[3]
    for d in (1, 2, 3):
        y = y + xscr[pl.ds(H - d * s, R), :] * cw_ref[3 - d]
    tail = xscr[pl.ds(R, H), :]
    xscr[0:H, :] = tail
    tail_ref[...] = tail
    qkv = _silu(y)

    def l2n(z, scale):
        return z * (lax.rsqrt(jnp.sum(z * z, axis=-1, keepdims=True) + GDN_EPS) * scale)

    for h in range(GDN_HK):
        q_ref[h] = l2n(qkv[:, LANES * h:LANES * (h + 1)], GDN_DK ** -0.5)
        k_ref[h] = l2n(qkv[:, GDN_KEY_W + LANES * h:GDN_KEY_W + LANES * (h + 1)], 1.0)
    for h in range(GDN_HV):
        v_ref[h] = qkv[:, 2 * GDN_KEY_W + LANES * h:2 * GDN_KEY_W + LANES * (h + 1)]
        z_ref[h] = u[:, CH + LANES * h:CH + LANES * (h + 1)]

    bg = u[:, CH + GDN_VAL_W:CH + GDN_VAL_W + LANES]
    beta = jax.nn.sigmoid(bg)
    g = -jnp.exp(alog_ref[...]) * jax.nn.softplus(bg + dtb_ref[...])
    t_in = (lax.broadcasted_iota(jnp.int32, (R, 1), 0) // s) % C
    k = 1
    while k < min(C, tt):
        gscr[Hg:Hg + R, :] = g
        g = g + jnp.where(t_in >= k, gscr[pl.ds(Hg - k * s, R), :], 0.0)
        k *= 2
    lane = lax.broadcasted_iota(jnp.int32, (R, LANES), 1)
    bg_ref[...] = jnp.where(lane < GDN_HV, beta, g)


def _gdn_pre(x3, halo, win, cw, alog, dtb, *, s, R, C):
    NB, TT, D = x3.shape
    H = halo.shape[1]
    tt = R // s
    Hg = max(SUBLANES, (min(C, tt) // 2) * s)
    kern = functools.partial(_gdn_pre_kernel, s=s, R=R, H=H, Hg=Hg, C=C)

    def ospec(nh):
        return pl.BlockSpec((None, nh, R, LANES), lambda i, c: (i, 0, c, 0))

    def oshape(nh):
        return jax.ShapeDtypeStruct((NB, nh, TT, LANES), F32)

    return pl.pallas_call(
        kern,
        grid=(NB, TT // R),
        in_specs=[
            pl.BlockSpec((None, R, D), lambda i, c: (i, c, 0)),
            pl.BlockSpec((None, H, GDN_CONV_CH), lambda i, c: (i, 0, 0)),
            _const_spec(win.shape), _const_spec(cw.shape), _const_spec(alog.shape), _const_spec(dtb.shape),
        ],
        out_specs=[ospec(GDN_HK), ospec(GDN_HK), ospec(GDN_HV), ospec(GDN_HV),
                   pl.BlockSpec((None, R, LANES), lambda i, c: (i, c, 0)),
                   pl.BlockSpec((None, H, GDN_CONV_CH), lambda i, c: (i, 0, 0))],
        out_shape=[oshape(GDN_HK), oshape(GDN_HK), oshape(GDN_HV), oshape(GDN_HV),
                   jax.ShapeDtypeStruct((NB, TT, LANES), F32),
                   jax.ShapeDtypeStruct((NB, H, GDN_CONV_CH), F32)],
        scratch_shapes=[pltpu.VMEM((H + R, GDN_CONV_CH), F32), pltpu.VMEM((Hg + R, LANES), F32)],
        compiler_params=_cparams(2),
        name="gdn_pre",
    )(x3, halo, win, cw, alog, dtb)


def _gdn_rec_kernel(q_ref, k_ref, v_ref, z_ref, bg_ref, x_ref, s0_ref, ng_ref, wout_ref, lng_ref, lnb_ref,
                    out_ref, sT_ref, sscr, oscr, *, C, batches):
    tb = pl.program_id(1)
    bb, nhv, Rt, _ = v_ref.shape
    rep = GDN_HV // GDN_HK

    @pl.when(tb == 0)
    def _init():
        sscr[...] = s0_ref[...].reshape(sscr.shape)

    strict, incl = _packed_masks(C)
    lo = lax.broadcasted_iota(jnp.int32, (C, 2 * C), 1) < C

    def part_a(items, ctx):
        CH = [(b, h, ci) for b, ci in items for h in range(nhv)]
        KH = [(b, m, ci) for b, ci in items for m in range(GDN_HK)]
        rs = lambda ci: pl.ds(ci * C, C)
        bgs = {(b, ci): bg_ref[b, rs(ci), :] for b, ci in items}
        bgT = {key: x.T for key, x in bgs.items()}

        kq = [jnp.concatenate([k_ref[b, m, rs(ci), :], q_ref[b, m, rs(ci), :]], axis=0) for b, m, ci in KH]
        sc = _each(lambda x: _dot(x, jnp.concatenate([x[:C, :], x[:C, :]], axis=0), _NT), kq)
        yield

        def col2(x, lane):
            return jnp.where(lo, jnp.broadcast_to(x[:, lane:lane + 1], (C, 2 * C)),
                             jnp.broadcast_to(x[:, lane + 1:lane + 2], (C, 2 * C)))

        bcol = [col2(bgs[b, ci], rep * m) for b, m, ci in KH]
        gcol = [col2(bgs[b, ci], GDN_HV + rep * m) for b, m, ci in KH]
        grow = [jnp.concatenate([bgT[b, ci][GDN_HV + rep * m:GDN_HV + rep * m + 1, :],
                                 bgT[b, ci][GDN_HV + rep * m + 1:GDN_HV + rep * m + 2, :]], axis=1) for b, m, ci in KH]
        diff = _each(lambda c_, r_: c_ - r_, gcol, grow)
        Lp = _each(lambda z, bc, d: jnp.where(strict, z[:C, :] * bc * jnp.exp(jnp.where(strict, d, 0.0)), 0.0),
                   sc, bcol, diff)
        Ap = _each(lambda z, d: jnp.where(incl, z[C:, :] * jnp.exp(jnp.where(incl, d, 0.0)), 0.0), sc, diff)
        Tinv_p = yield from _unit_lower_inverse([-l_ for l_ in Lp])

        half = lambda xs, n, h: xs[n // rep][:, (h % rep) * C:(h % rep + 1) * C]
        Tinv = [half(Tinv_p, n, h) for n, (b, h, ci) in enumerate(CH)]
        A = [half(Ap, n, h) for n, (b, h, ci) in enumerate(CH)]
        k = [kq[n // rep][:C, :] for n in range(len(CH))]
        q = [kq[n // rep][C:, :] for n in range(len(CH))]
        v = [v_ref[b, h, rs(ci), :] for b, h, ci in CH]
        beta = [jnp.broadcast_to(bgs[b, ci][:, h:h + 1], (C, LANES)) for b, h, ci in CH]
        gc = [jnp.broadcast_to(bgs[b, ci][:, GDN_HV + h:GDN_HV + h + 1], (C, LANES)) for b, h, ci in CH]
        kb = _each(lambda x, y: x * y, k, beta)
        eg = _each(jnp.exp, gc)
        UW = _each(lambda t, v_, b_, kb_, e: _dot(t, jnp.concatenate([v_ * b_, kb_ * e], axis=1)),
                   Tinv, v, beta, kb, eg)
        yield
        for n, key in enumerate(CH):
            g_last = gc[n][C - 1:C, :]
            ctx[key] = dict(U=UW[n][:, :GDN_DV], WQl=jnp.concatenate([UW[n][:, GDN_DV:], q[n] * eg[n]], axis=0),
                            A=A[n], kdec=k[n] * jnp.exp(g_last - gc[n]), sdec=jnp.exp(g_last))

    def part_b(items, ctx):
        for ci in sorted({ci for _, ci in items}):
            CH = [(b, h, ci) for b, c_ in items if c_ == ci for h in range(nhv)]
            X = [ctx.pop(key) for key in CH]
            S = [sscr[b * nhv + h] for b, h, _ in CH]
            WQ = _each(lambda x, s_: _dot(x["WQl"], s_), X, S)
            yield
            v_new = _each(lambda x, wq: x["U"] - wq[:C, :], X, WQ)
            o = _each(lambda x, wq, vn: wq[C:, :] + _dot(x["A"], vn), X, WQ, v_new)
            Snew = _each(lambda x, s_, vn: s_ * x["sdec"] + _dot(x["kdec"], vn, _TN), X, S, v_new)
            yield
            for n, (b, h, _) in enumerate(CH):
                sscr[b * nhv + h] = Snew[n]
                oh = o[n]
                oscr[b * Rt + ci * C:b * Rt + (ci + 1) * C, LANES * h:LANES * (h + 1)] = (
                    oh * lax.rsqrt(jnp.mean(oh * oh, axis=-1, keepdims=True) + GDN_EPS) * ng_ref[...])
            yield

    _run_pipelined(part_a, part_b, batches)

    @pl.when(tb == pl.num_programs(1) - 1)
    def _final_state():
        sT_ref[...] = sscr[...].reshape(sT_ref.shape)

    _gated_out_ln(oscr[...], z_ref, x_ref, wout_ref, lng_ref, lnb_ref, out_ref)


def _gdn_rec(q4, k4, v4, z4, bg, x, s0, norm_g, wout, lng, lnb, *, C, Rt, bb, gb):
    B, _, T, _ = v4.shape
    D = x.shape[-1]
    kern = functools.partial(_gdn_rec_kernel, C=C, batches=_rec_batches(bb, Rt // C, gb, GDN_CHUNKS_PER_BATCH))

    def tspec(nh):
        return pl.BlockSpec((bb, nh, Rt, LANES), lambda b, t: (b, 0, t, 0))

    xspec = pl.BlockSpec((bb, Rt, D), lambda b, t: (b, t, 0))
    sspec = pl.BlockSpec((bb, GDN_HV, GDN_DK, GDN_DV), lambda b, t: (b, 0, 0, 0))
    return pl.pallas_call(
        kern,
        grid=(B // bb, T // Rt),
        in_specs=[tspec(GDN_HK), tspec(GDN_HK), tspec(GDN_HV), tspec(GDN_HV),
                  pl.BlockSpec((bb, Rt, LANES), lambda b, t: (b, t, 0)), xspec, sspec,
                  _const_spec(norm_g.shape), _const_spec(wout.shape), _const_spec(lng.shape), _const_spec(lnb.shape)],
        out_specs=[xspec, sspec],
        out_shape=[jax.ShapeDtypeStruct(x.shape, F32), jax.ShapeDtypeStruct(s0.shape, F32)],
        scratch_shapes=[pltpu.VMEM((bb * GDN_HV, GDN_DK, GDN_DV), F32), pltpu.VMEM((bb * Rt, GDN_VAL_W), F32)],
        compiler_params=_cparams(2),
        name="gdn_rec",
    )(q4, k4, v4, z4, bg, x, s0, norm_g, wout, lng, lnb)


class _Group:
    def __init__(self, B, T, time_major):
        self.B, self.T, self.time_major = B, T, time_major
        if time_major:
            self.s, self.NB, self.TT, self.R = B, 1, T * B, T * B
        else:
            self.s, self.NB, self.TT, self.R = 1, B, T, min(ROW_BLOCK, T)

    def to_rows(self, x):
        if self.time_major:
            return jnp.swapaxes(x, 0, 1).reshape(1, self.TT, x.shape[-1])
        return x

    def from_rows(self, x3):
        if self.time_major:
            return jnp.swapaxes(x3.reshape(self.T, self.B, x3.shape[-1]), 0, 1)
        return x3

    def halo(self, st, n_steps):
        if self.time_major:
            return jnp.swapaxes(st, 0, 1).reshape(1, n_steps * self.B, st.shape[-1])
        H = _halo_rows(n_steps, 1)
        return jnp.pad(st, ((0, 0), (H - n_steps, 0), (0, 0)))

    def unhalo(self, tail, n_steps):
        if self.time_major:
            return jnp.swapaxes(tail.reshape(n_steps, self.B, tail.shape[-1]), 0, 1)
        return tail[:, tail.shape[1] - n_steps:]

    def vec(self, st):
        return st[None] if self.time_major else st[:, None, :]

    def unvec(self, v):
        return v[0] if self.time_major else v[:, 0, :]

    def heads_to_batch(self, a4, Tpad):
        if not self.time_major:
            return a4
        nh = a4.shape[1]
        a = a4.reshape(nh, self.T, self.B, LANES).transpose(2, 0, 1, 3)
        return jnp.pad(a, ((0, 0), (0, 0), (0, Tpad - self.T), (0, 0)))

    def rows_to_batch(self, a3, Tpad):
        if not self.time_major:
            return a3
        return jnp.pad(self.from_rows(a3), ((0, 0), (0, Tpad - self.T), (0, 0)))

    def rows_from_batch(self, a3):
        if not self.time_major:
            return a3
        return self.to_rows(a3[:, :self.T])

    def rec_tiling(self, chunk, block):
        if self.time_major:
            Tp = -(-self.T // SUBLANES) * SUBLANES
            return Tp, Tp, Tp, min(self.B, SAMPLE_SEQ_BLOCK), min(self.B, SAMPLE_SEQ_GROUP)
        return min(chunk, self.T), self.T, min(block, self.T), PROMPT_SEQ_BLOCK if self.B % PROMPT_SEQ_BLOCK == 0 else 1, 1


def _lru_apply(g, x3, conv_st, h_st, p, lng, lnb):
    swap = not g.time_major
    assert g.B % SUBLANES == 0, "RG-LRU kernel needs the batch to fill whole sublane tiles"
    lay = _Group(g.B, g.T, True) if swap else g
    out, tail, hT = _lru_layer(x3, lay.halo(conv_st, CONV_W - 1), lay.vec(h_st), p["win"], p["cw"], p["cb"],
                               p["wg"], p["bg"], p["lam"], p["wout"], lng, lnb, s=lay.s,
                               R=min(LRU_BLOCK, g.T * g.B) if swap else g.R, swap=swap)
    return out, lay.unhalo(tail, CONV_W - 1), lay.unvec(hT)


def _rwkv_apply(g, x3, shift_st, wkv_st, p, lng, lnb):
    B = g.B
    pre = _rwkv_pre(x3, g.halo(shift_st[:, None, :], 1), p["mu"], p["win"], p["w0"], p["w1"], p["w2"], p["a0"],
                    p["a1"], p["a2"], s=g.s, R=g.R)
    new_shift = g.from_rows(x3)[:, -1]
    if g.time_major:
        r4, k4, v4, g4, lw4, a4 = pre
        s0 = jnp.transpose(wkv_st, (1, 2, 3, 0)).reshape(RW_H // 2, 2, RW_N, RW_N, B)
        col = lambda v: jnp.swapaxes(v, 1, 2)
        o4, sT = _rwkv_step(r4, k4, v4, lw4, a4, s0, col(p["k_k"]), col(p["k_a"]), col(p["r_k"]), col(p["gn_g"]),
                            col(p["gn_b"]), T=g.T, B=B)
        out = _post_layer(x3, o4, g4, p["wout"], lng, lnb)
        return out, new_shift, jnp.transpose(sT.reshape(RW_H, RW_N, RW_N, B), (3, 0, 1, 2))
    C, Tp, Rt, bb, gb = g.rec_tiling(RW_CHUNK, RW_BLOCK)
    r4, k4, v4, g4, lw4, a4 = [g.heads_to_batch(a, Tp) for a in pre]
    s0 = wkv_st.reshape(B, RW_H // 2, 2 * RW_N, RW_N)
    out, sT = _rwkv_rec(r4, k4, v4, lw4, a4, g4, g.rows_to_batch(x3, Tp), s0, p["k_k"], p["k_a"], p["r_k"],
                        p["gn_g"], p["gn_b"], p["wout"], lng, lnb, C=C, Rt=Rt, bb=bb, gb=gb)
    new_shift = g.from_rows(x3)[:, -1]
    return g.rows_from_batch(out), new_shift, sT.reshape(B, RW_H, RW_N, RW_N)


def _gdn_apply(g, x3, conv_st, S_st, p, lng, lnb):
    T = g.T
    C, Tp, Rt, bb, gb = g.rec_tiling(GDN_CHUNK, GDN_BLOCK)
    q4, k4, v4, z4, bg, tail = _gdn_pre(x3, g.halo(conv_st, CONV_W - 1), p["win"], p["cw"], p["alog"], p["dtb"],
                                        s=g.s, R=g.R, C=min(GDN_CHUNK, T))
    bgb = g.rows_to_batch(bg, T)
    if Tp != T:
        held = jnp.where(jnp.arange(LANES) < GDN_HV, 0.0, bgb[:, T - 1:T, :])
        bgb = jnp.concatenate([bgb, jnp.broadcast_to(held, (g.B, Tp - T, LANES))], axis=1)
    tb = lambda a: g.heads_to_batch(a, Tp)
    out, sT = _gdn_rec(tb(q4), tb(k4), tb(v4), tb(z4), bgb, g.rows_to_batch(x3, Tp), S_st, p["norm_g"], p["wout"],
                       lng, lnb, C=C, Rt=Rt, bb=bb, gb=gb)
    return g.rows_from_batch(out), g.unhalo(tail, CONV_W - 1), sT


def _trunk(g, x, st, params, ln_g, ln_b):
    lru_conv, lru_h, rw_shift, rw_S, gdn_conv, gdn_S = st
    new = ([], [], [], [], [], [])
    x3 = g.to_rows(x)
    ia = ib = ic = 0
    for layer in range(DEPTH):
        kind = layer % N_MIXERS
        lng, lnb = ln_g[layer][None, :], ln_b[layer][None, :]
        if kind == 0:
            x3, c, h = _lru_apply(g, x3, lru_conv[ia], lru_h[ia], params["lru"][ia], lng, lnb)
            new[0].append(c)
            new[1].append(h)
            ia += 1
        elif kind == 1:
            x3, sh, S = _rwkv_apply(g, x3, rw_shift[ib], rw_S[ib], params["rwkv"][ib], lng, lnb)
            new[2].append(sh)
            new[3].append(S)
            ib += 1
        else:
            x3, c, S = _gdn_apply(g, x3, gdn_conv[ic], gdn_S[ic], params["gdn"][ic], lng, lnb)
            new[4].append(c)
            new[5].append(S)
            ic += 1
    return g.from_rows(x3), tuple(s[0][None] if len(s) == 1 else jnp.stack(s) for s in new)


def _prep_params(lru_w_in, lru_conv_w, lru_conv_b, lru_wa, lru_ba, lru_wx, lru_bx, lru_lambda, lru_w_out, rw_mu,
                 rw_w_in, rw_w0, rw_w1, rw_w2, rw_a0, rw_a1, rw_a2, rw_k_k, rw_k_a, rw_r_k, rw_gn_g, rw_gn_b,
                 rw_w_out, gdn_w_in, gdn_conv_w, gdn_a_log, gdn_dt_bias, gdn_norm_g, gdn_w_out):
    row = lambda v: v[None, :]
    lru = []
    for n in range(lru_w_in.shape[0]):
        lru.append(dict(win=lru_w_in[n].astype(BF), cw=lru_conv_w[n][:, None, :], cb=row(lru_conv_b[n]),
                        wg=_lru_gate_weights(lru_wa[n], lru_wx[n]), bg=jnp.stack([lru_ba[n], lru_bx[n]])[:, None, :],
                        lam=row(lru_lambda[n]), wout=lru_w_out[n].astype(BF)))
    pairw = lambda v: v.reshape(RW_W // LANES, 1, LANES)
    rwkv = []
    for n in range(rw_w_in.shape[0]):
        rwkv.append(dict(mu=rw_mu[n][:, None, :],win=rw_w_in[n].astype(BF), w0=row(rw_w0[n]), w1=rw_w1[n].astype(BF),
                         w2=rw_w2[n].astype(BF), a0=row(rw_a0[n]), a1=rw_a1[n].astype(BF), a2=rw_a2[n].astype(BF),
                         k_k=pairw(rw_k_k[n]), k_a=pairw(rw_k_a[n]), r_k=pairw(rw_r_k[n]), gn_g=pairw(rw_gn_g[n]),
                         gn_b=pairw(rw_gn_b[n]), wout=rw_w_out[n].astype(BF)))
    gdn = []
    for n in range(gdn_w_in.shape[0]):
        w = gdn_w_in[n]
        o2 = GDN_CONV_CH + GDN_VAL_W
        wpad = jnp.pad(w[:, o2:], ((0, 0), (0, LANES - 2 * GDN_HV)))
        lanes = lambda v: jnp.pad(v, (GDN_HV, LANES - 2 * GDN_HV))[None, :]
        gdn.append(dict(win=jnp.concatenate([w[:, :o2], wpad], axis=1).astype(BF), cw=gdn_conv_w[n][:, None, :],
                        alog=lanes(gdn_a_log[n]), dtb=lanes(gdn_dt_bias[n]), norm_g=row(gdn_norm_g[n]),
                        wout=gdn_w_out[n].astype(BF)))
    return dict(lru=lru, rwkv=rwkv, gdn=gdn)


def kernel(x_prompt, x_sample, state_lru_conv, state_lru_h, state_rwkv_shift, state_rwkv_wkv, state_gdn_conv, state_gdn_S, ln_g, ln_b, lru_w_in, lru_conv_w, lru_conv_b, lru_wa, lru_ba, lru_wx, lru_bx, lru_lambda, lru_w_out, rw_mu, rw_w_in, rw_w0, rw_w1, rw_w2, rw_a0, rw_a1, rw_a2, rw_k_k, rw_k_a, rw_r_k, rw_gn_g, rw_gn_b, rw_w_out, gdn_w_in, gdn_conv_w, gdn_a_log, gdn_dt_bias, gdn_norm_g, gdn_w_out):
    params = _prep_params(lru_w_in, lru_conv_w, lru_conv_b, lru_wa, lru_ba, lru_wx, lru_bx, lru_lambda, lru_w_out,
                          rw_mu, rw_w_in, rw_w0, rw_w1, rw_w2, rw_a0, rw_a1, rw_a2, rw_k_k, rw_k_a, rw_r_k, rw_gn_g,
                          rw_gn_b, rw_w_out, gdn_w_in, gdn_conv_w, gdn_a_log, gdn_dt_bias, gdn_norm_g, gdn_w_out)
    bp, tp, _ = x_prompt.shape
    bs, ts, _ = x_sample.shape
    n_a, n_b, n_c = state_lru_conv.shape[0], state_rwkv_shift.shape[0], state_gdn_conv.shape[0]
    zero_state = (jnp.zeros((n_a, bp, CONV_W - 1, LRU_W), F32),
                  jnp.zeros((n_a, bp, LRU_W), F32),
                  jnp.zeros((n_b, bp, D_MODEL), F32),
                  jnp.zeros((n_b, bp, RW_H, RW_N, RW_N), F32),
                  jnp.zeros((n_c, bp, CONV_W - 1, GDN_CONV_CH), F32),
                  jnp.zeros((n_c, bp, GDN_HV, GDN_DK, GDN_DV), F32))
    y_prompt, sp = _trunk(_Group(bp, tp, False), x_prompt, zero_state, params, ln_g, ln_b)
    y_sample, ss = _trunk(_Group(bs, ts, True), x_sample,
                          (state_lru_conv, state_lru_h, state_rwkv_shift, state_rwkv_wkv, state_gdn_conv,
                           state_gdn_S), params, ln_g, ln_b)
    return (y_prompt, y_sample, sp[0], ss[0], sp[1], ss[1], sp[2], ss[2], sp[3], ss[3], sp[4], ss[4], sp[5], ss[5])
```

```python
import functools
import math

import jax
import jax.numpy as jnp
from jax import lax
from jax.experimental import pallas as pl
from jax.experimental.pallas import tpu as pltpu

F32 = jnp.float32
BF = jnp.bfloat16

D_MODEL = 1024
DEPTH = 4
N_MIXERS = 3
DN_ALPHA = (2.0 * DEPTH) ** 0.25
LN_EPS = 1e-5
CONV_W = 4

LRU_W = D_MODEL
LRU_BLOCKS = 16
LRU_BS = LRU_W // LRU_BLOCKS
LRU_C = 8.0

RW_W = D_MODEL
RW_N = 64
RW_H = RW_W // RW_N
RW_GN_EPS = 64e-5
RW_NORM_EPS = 1e-12

GDN_HK = 4
GDN_HV = 8
GDN_DK = 128
GDN_DV = 128
GDN_KEY_W = GDN_HK * GDN_DK
GDN_VAL_W = GDN_HV * GDN_DV
GDN_CONV_CH = 2 * GDN_KEY_W + GDN_VAL_W
GDN_CHUNK = 64
GDN_EPS = 1e-6

LANES = 128
SUBLANES = 8
MXU_WIDTH = 256
VMEM_LIMIT = 56 * 1024 * 1024
ROW_BLOCK = 512
LRU_BLOCK = 1024
RW_BLOCK = 256
GDN_BLOCK = 256
RW_CHUNKS_PER_BATCH = 1
GDN_CHUNKS_PER_BATCH = 2
PROMPT_SEQ_BLOCK = 2
SAMPLE_SEQ_BLOCK = 8
SAMPLE_SEQ_GROUP = 4
RW_CHUNK = 64

_NN = (((1,), (0,)), ((), ()))
_NT = (((1,), (1,)), ((), ()))
_TN = (((0,), (0,)), ((), ()))


def _mm(a, b):
    return jnp.dot(a.astype(BF), b.astype(BF), preferred_element_type=F32)


def _dot(a, b, dims=_NN):
    return lax.dot_general(a.astype(BF), b.astype(BF), dims, preferred_element_type=F32)


def _cumsum_rows(x, tri3):
    b1 = x.astype(BF)
    r1 = x - b1.astype(F32)
    b2 = r1.astype(BF)
    b3 = (r1 - b2.astype(F32)).astype(BF)
    return jnp.dot(tri3, jnp.concatenate([b1, b2, b3], axis=0), preferred_element_type=F32)


def _layer_norm(z, g, b):
    mu = jnp.mean(z, axis=-1, keepdims=True)
    zc = z - mu
    var = jnp.mean(zc * zc, axis=-1, keepdims=True)
    return zc * lax.rsqrt(var + LN_EPS) * g + b


def _silu(x):
    return x * jax.nn.sigmoid(x)


def _cparams(n_axes):
    return pltpu.CompilerParams(dimension_semantics=("arbitrary",) * n_axes, vmem_limit_bytes=VMEM_LIMIT)


def _const_spec(shape):
    nd = len(shape)
    return pl.BlockSpec(shape, lambda *_: (0,) * nd, pipeline_mode=pl.Buffered(1))


def _halo_rows(n_steps, s):
    rows = n_steps * s
    return rows if rows % SUBLANES == 0 else SUBLANES


def _lru_kernel(x_ref, halo_ref, h0_ref, win_ref, cw_ref, cb_ref, wg_ref, bg_ref, lam_ref, wout_ref, lng_ref,
                lnb_ref, out_ref, tail_ref, hT_ref, xscr, hscr, *, s, R, H, swap):
    c = pl.program_id(1)
    C = LRU_W
    tt = R // s

    @pl.when(c == 0)
    def _init():
        xscr[0:H, :] = halo_ref[...]
        hscr[...] = h0_ref[...]

    x = x_ref[...]
    if swap:
        x = jnp.swapaxes(x, 0, 1).reshape(R, D_MODEL)
    u = _mm(x, win_ref[...])
    xb = u[:, :C]
    gate = u[:, C:]

    xscr[H:H + R, :] = xb
    xc = xb * cw_ref[3] + cb_ref[...]
    for d in (1, 2, 3):
        xc = xc + xscr[pl.ds(H - d * s, R), :] * cw_ref[3 - d]
    tail = xscr[pl.ds(R, H), :]
    xscr[0:H, :] = tail
    tail_ref[...] = tail

    G = MXU_WIDTH
    xcb = xc.astype(BF)
    ra, ix = [], []
    for g in range(C // G):
        gt = jnp.dot(xcb[:, G * g:G * (g + 1)], wg_ref[g], preferred_element_type=F32)
        ra.append(gt[:, :G])
        ix.append(gt[:, G:])
    r = jax.nn.sigmoid(jnp.concatenate(ra, axis=1) + bg_ref[0])
    i = jax.nn.sigmoid(jnp.concatenate(ix, axis=1) + bg_ref[1])
    log_a = r * ((-LRU_C) * jax.nn.softplus(-lam_ref[...]))
    a = jnp.exp(log_a)
    b = jnp.sqrt(1.0 - a * a) * i * xc

    hlast = hscr[...]
    hs = []
    for t in range(tt):
        hlast = a[t * s:(t + 1) * s, :] * hlast + b[t * s:(t + 1) * s, :]
        hs.append(hlast)
    h = jnp.concatenate(hs, axis=0)
    hscr[...] = hlast
    hT_ref[...] = hlast

    y = _mm(h * _silu(gate), wout_ref[...])
    out = _layer_norm(DN_ALPHA * x + y, lng_ref[...], lnb_ref[...])
    if swap:
        out = jnp.swapaxes(out.reshape(tt, s, D_MODEL), 0, 1)
    out_ref[...] = out


def _lru_layer(x3, halo, h0, win, cw, cb, wg, bg, lam, wout, lng, lnb, *, s, R, swap):
    C = LRU_W
    D = x3.shape[-1]
    H = halo.shape[1]
    tt = R // s
    assert s % SUBLANES == 0
    if swap:
        NB, TT = 1, x3.shape[1] * s
        xspec = pl.BlockSpec((s, tt, D), lambda i, c: (0, c, 0))
    else:
        NB, TT = x3.shape[:2]
        xspec = pl.BlockSpec((None, R, D), lambda i, c: (i, c, 0))
    kern = functools.partial(_lru_kernel, s=s, R=R, H=H, swap=swap)
    return pl.pallas_call(
        kern,
        grid=(NB, TT // R),
        in_specs=[
            xspec,
            pl.BlockSpec((None, H, C), lambda i, c: (i, 0, 0)),
            pl.BlockSpec((None, s, C), lambda i, c: (i, 0, 0)),
            _const_spec(win.shape), _const_spec(cw.shape), _const_spec(cb.shape), _const_spec(wg.shape),
            _const_spec(bg.shape), _const_spec(lam.shape), _const_spec(wout.shape), _const_spec(lng.shape),
            _const_spec(lnb.shape),
        ],
        out_specs=[
            xspec,
            pl.BlockSpec((None, H, C), lambda i, c: (i, 0, 0)),
            pl.BlockSpec((None, s, C), lambda i, c: (i, 0, 0)),
        ],
        out_shape=[
            jax.ShapeDtypeStruct(x3.shape, F32),
            jax.ShapeDtypeStruct((NB, H, C), F32),
            jax.ShapeDtypeStruct((NB, s, C), F32),
        ],
        scratch_shapes=[
            pltpu.VMEM((H + R, C), F32),
            pltpu.VMEM((s, C), F32),
        ],
        compiler_params=_cparams(2),
        name="lru_layer",
    )(x3, halo, h0, win, cw, cb, wg, bg, lam, wout, lng, lnb)


def _lru_gate_weights(wa, wx):
    G = MXU_WIDTH
    nb = G // LRU_BS

    def bd(w):
        w4 = w.reshape(LRU_BLOCKS // nb, nb, LRU_BS, LRU_BS)
        eye = jnp.eye(nb, dtype=w.dtype)
        return jnp.einsum("gaij,ab->gaibj", w4, eye).reshape(LRU_BLOCKS // nb, G, G)
    return jnp.concatenate([bd(wa), bd(wx)], axis=2).astype(BF)


def _gated_out_ln(o, g_ref, x_ref, w_ref, lng_ref, lnb_ref, out_ref):
    bb, nh, Rt, _ = g_ref.shape
    rows = bb * Rt
    g = jnp.concatenate([g_ref[:, p, :, :].reshape(rows, LANES) for p in range(nh)], axis=1)
    y = _mm(o * _silu(g), w_ref[...])
    x = x_ref[...].reshape(rows, D_MODEL)
    out_ref[...] = _layer_norm(DN_ALPHA * x + y, lng_ref[...], lnb_ref[...]).reshape(bb, Rt, D_MODEL)


def _post_kernel(x_ref, o_ref, g_ref, w_ref, lng_ref, lnb_ref, out_ref):
    o = jnp.concatenate([o_ref[0, p] for p in range(o_ref.shape[1])], axis=1)
    _gated_out_ln(o, g_ref, x_ref, w_ref, lng_ref, lnb_ref, out_ref)


def _post_layer(x3, o4, g4, w, lng, lnb):
    return pl.pallas_call(
        _post_kernel,
        out_shape=jax.ShapeDtypeStruct(x3.shape, F32),
        compiler_params=pltpu.CompilerParams(vmem_limit_bytes=VMEM_LIMIT),
        name="post_layer",
    )(x3, o4, g4, w, lng, lnb)


def _rwkv_pre_kernel(x_ref, halo_ref, mu_ref, win_ref, w0_ref, w1_ref, w2_ref, a0_ref, a1_ref, a2_ref,
                     r_ref, k_ref, v_ref, g_ref, lw_ref, a_ref, xscr, xxscr, *, s, R, H):
    c = pl.program_id(1)

    @pl.when(c == 0)
    def _init():
        xscr[0:H, :] = halo_ref[...]

    x = x_ref[...]
    xscr[H:H + R, :] = x
    xxscr[...] = xscr[pl.ds(H - s, R), :] - x
    xscr[0:H, :] = xscr[pl.ds(R, H), :]
    xx = xxscr[...]
    xm = lambda n: x + xx * mu_ref[n]

    def put(ref, val):
        for p in range(RW_W // LANES):
            ref[p] = val[:, LANES * p:LANES * (p + 1)]

    put(r_ref, _mm(xm(0), win_ref[0]))
    put(k_ref, _mm(xm(1), win_ref[1]))
    put(v_ref, _mm(xm(2), win_ref[2]))
    put(g_ref, _mm(xm(3), win_ref[3]))
    w_raw = w0_ref[...] + _mm(jnp.tanh(_mm(xm(4), w1_ref[...])), w2_ref[...])
    put(lw_ref, (-math.exp(-0.5)) * jax.nn.sigmoid(w_raw))
    put(a_ref, jax.nn.sigmoid(a0_ref[...] + _mm(_mm(xm(5), a1_ref[...]), a2_ref[...])))


def _rwkv_pre(x3, halo, mu, win, w0, w1, w2, a0, a1, a2, *, s, R):
    NB, TT, D = x3.shape
    H = halo.shape[1]
    nh = RW_W // LANES
    kern = functools.partial(_rwkv_pre_kernel, s=s, R=R, H=H)
    ospec = pl.BlockSpec((None, nh, R, LANES), lambda i, c: (i, 0, c, 0))
    oshape = jax.ShapeDtypeStruct((NB, nh, TT, LANES), F32)
    return pl.pallas_call(
        kern,
        grid=(NB, TT // R),
        in_specs=[
            pl.BlockSpec((None, R, D), lambda i, c: (i, c, 0)),
            pl.BlockSpec((None, H, D), lambda i, c: (i, 0, 0)),
            _const_spec(mu.shape), _const_spec(win.shape), _const_spec(w0.shape), _const_spec(w1.shape),
            _const_spec(w2.shape), _const_spec(a0.shape), _const_spec(a1.shape), _const_spec(a2.shape),
        ],
        out_specs=[ospec] * 6,
        out_shape=[oshape] * 6,
        scratch_shapes=[pltpu.VMEM((H + R, D), F32), pltpu.VMEM((R, D), F32)],
        compiler_params=_cparams(2),
        name="rwkv_pre",
    )(x3, halo, mu, win, w0, w1, w2, a0, a1, a2)


def _seg_sum(x):
    lane = lax.broadcasted_iota(jnp.int32, x.shape, 1)
    lo = lane < RW_N
    s0 = jnp.sum(jnp.where(lo, x, 0.0), axis=-1, keepdims=True)
    s1 = jnp.sum(jnp.where(lo, 0.0, x), axis=-1, keepdims=True)
    return jnp.where(lo, s0, s1)


def _stack2(x):
    lane = lax.broadcasted_iota(jnp.int32, x.shape, 1)
    lo = lane < x.shape[1] // 2
    return jnp.concatenate([jnp.where(lo, x, 0.0), jnp.where(lo, 0.0, x)], axis=0)


def _each(f, *lists):
    return [f(*t) for t in zip(*lists)]


def _packed_masks(C):
    t = lax.broadcasted_iota(jnp.int32, (C, 2 * C), 0)
    s = lax.broadcasted_iota(jnp.int32, (C, 2 * C), 1) % C
    return s < t, s <= t


def _stackn(x, n):
    grp = lax.broadcasted_iota(jnp.int32, x.shape, 1) // (x.shape[1] // n)
    return jnp.concatenate([jnp.where(grp == k, x, 0.0) for k in range(n)], axis=0)


def _neumann_inverse(Ps, nblk):
    m = Ps[0].shape[0]
    eye = (lax.broadcasted_iota(jnp.int32, Ps[0].shape, 1) % m) == lax.broadcasted_iota(jnp.int32, Ps[0].shape, 0)
    invs = [jnp.where(eye, 1.0, P) for P in Ps]
    pw = Ps
    span = 2
    while span < m:
        pw = _each(lambda p: _dot(p, _stackn(p, nblk)), pw)
        yield
        invs = _each(lambda inv, p: inv + _dot(inv, _stackn(p, nblk)), invs, pw)
        yield
        span *= 2
    return invs


def _unit_lower_inverse(Ls):
    C = Ls[0].shape[0]
    h = C // 2
    if h % SUBLANES:
        return (yield from _neumann_inverse(Ls, 2))
    grp = lax.broadcasted_iota(jnp.int32, (h, 2 * C), 1) // h
    diag0 = (grp % 2) == 0
    zero = jnp.zeros((h, 2 * C), F32)
    D = yield from _neumann_inverse([jnp.where(diag0, L[:h, :], L[h:, :]) for L in Ls], 4)
    L21 = [jnp.where(diag0, L[h:, :], 0.0) for L in Ls]
    M1 = _each(lambda d, q: _dot(d, jnp.concatenate(
        [zero, jnp.where(grp == 0, q, 0.0), zero, jnp.where(grp == 2, q, 0.0)], axis=0)), D, L21)
    yield
    X21 = _each(lambda m1, d: _dot(m1, jnp.concatenate(
        [jnp.where(grp == 0, d, 0.0), zero, jnp.where(grp == 2, d, 0.0), zero], axis=0)), M1, D)
    yield
    return _each(lambda d, x21: jnp.concatenate([jnp.where(diag0, d, 0.0), x21 + jnp.where(diag0, 0.0, d)], axis=0),
                 D, X21)


def _run_pipelined(part_a, part_b, batches):
    ctx = {}
    prev = None
    for batch in batches:
        gens = [part_a(batch, ctx)] + ([part_b(prev, ctx)] if prev is not None else [])
        while gens:
            for g in list(gens):
                try:
                    next(g)
                except StopIteration:
                    gens.remove(g)
        prev = batch
    for _ in part_b(prev, ctx):
        pass


def _rec_batches(bb, nchunk, gb, cpb):
    if nchunk > 1:
        return tuple(tuple((b, c) for b in range(bb) for c in range(c0, min(c0 + cpb, nchunk)))
                     for c0 in range(0, nchunk, cpb))
    return tuple(tuple((b, 0) for b in range(b0, b0 + gb)) for b0 in range(0, bb, gb))


def _rwkv_rec_kernel(r_ref, k_ref, v_ref, lw_ref, a_ref, g_ref, x_ref, s0_ref, kk_ref, ka_ref, rk_ref, gg_ref,
                     gb_ref, wout_ref, lng_ref, lnb_ref, out_ref, sT_ref, sscr, oscr, *, C, batches):
    tb = pl.program_id(1)
    bb, npair, Rt, _ = r_ref.shape
    C2 = 2 * C
    lane_sq = lax.broadcasted_iota(jnp.int32, (LANES, LANES), 1)
    row_sq = lax.broadcasted_iota(jnp.int32, (LANES, LANES), 0)
    same_head = (lane_sq < RW_N) == (row_sq < RW_N)

    @pl.when(tb == 0)
    def _init():
        def init_b(b, carry):
            for p in range(npair):
                s2 = s0_ref[b, p]
                sscr[b * npair + p] = jnp.where(same_head, jnp.concatenate([s2, s2], axis=1), 0.0)
            return carry
        lax.fori_loop(0, bb, init_b, 0)

    strict, incl = _packed_masks(C)
    tri3 = ((lax.broadcasted_iota(jnp.int32, (C, 3 * C), 1) % C)
            <= lax.broadcasted_iota(jnp.int32, (C, 3 * C), 0)).astype(BF)

    def part_a(items, ctx):
        CH = [(b, p, ci) for b, ci in items for p in range(npair)]
        rs = lambda ci: pl.ds(ci * C, C)
        r = [r_ref[b, p, rs(ci), :] for b, p, ci in CH]
        k = [k_ref[b, p, rs(ci), :] for b, p, ci in CH]
        v = [v_ref[b, p, rs(ci), :] for b, p, ci in CH]
        lw = [lw_ref[b, p, rs(ci), :] for b, p, ci in CH]
        a = [a_ref[b, p, rs(ci), :] for b, p, ci in CH]
        kn = [k_ * kk_ref[p] for k_, (b, p, ci) in zip(k, CH)]
        kk = _each(lambda z: z * lax.rsqrt(_seg_sum(z * z) + RW_NORM_EPS), kn)
        kh = [k_ * (1.0 + (a_ - 1.0) * ka_ref[p]) for k_, a_, (b, p, ci) in zip(k, a, CH)]
        bvec = _each(lambda x, y: x * y, kk, a)
        cum = _each(lambda x: _cumsum_rows(x, tri3), lw)
        yield
        cum_last = [c_[C - 1:C, :] for c_ in cum]
        e_neg = _each(lambda c_: jnp.exp(-c_), cum)
        e_dec = _each(lambda cl, c_: jnp.exp(cl - c_), cum_last, cum)
        At = _each(lambda kk_, c_, lw_: -kk_ * jnp.exp(c_ - lw_), kk, cum, lw)
        Rt_ = _each(lambda r_, c_: r_ * jnp.exp(c_), r, cum)
        AR = _each(lambda x, y: jnp.concatenate([x, y], axis=0), At, Rt_)
        BKs = _each(lambda b_, kh_, e: jnp.concatenate([_stack2(b_ * e), _stack2(kh_ * e)], axis=0), bvec, kh, e_neg)
        V2 = _each(_stack2, v)
        sc = _each(lambda x, y: _dot(x, y, _NT), AR, BKs)
        yield
        Lab = [jnp.where(strict, z[:C, :C2], 0.0) for z in sc]
        Lak = [jnp.where(strict, z[:C, C2:], 0.0) for z in sc]
        Mrbk = [jnp.concatenate([jnp.where(incl, z[C:, :C2], 0.0), jnp.where(incl, z[C:, C2:], 0.0)], axis=1)
                for z in sc]
        LV = _each(_dot, Lak, V2)
        Tinv = yield from _unit_lower_inverse(Lab)
        for n, key in enumerate(CH):
            ctx[key] = dict(AR=AR[n], V2=V2[n], LV=LV[n], Tinv=Tinv[n], Mrbk=Mrbk[n], v=v[n], r=r[n], kh=kh[n],
                            sdec=jnp.exp(cum_last[n]), bkdec=jnp.concatenate([bvec[n] * e_dec[n], kh[n] * e_dec[n]], 0))

    def part_b(items, ctx):
        for ci in sorted({ci for _, ci in items}):
            CH = [(b, p, ci) for b, c_ in items if c_ == ci for p in range(npair)]
            X = [ctx.pop(key) for key in CH]
            S = [sscr[b * npair + p] for b, p, _ in CH]
            UY0 = _each(lambda x, s_: _dot(x["AR"], s_, _NT), X, S)
            yield
            U = _each(lambda x, u: _dot(x["Tinv"], _stack2(u[:C, :] + x["LV"])), X, UY0)
            yield
            y = _each(lambda x, u0, u: u0[C:, :] + _dot(x["Mrbk"], jnp.concatenate([_stack2(u), x["V2"]], axis=0)),
                      X, UY0, U)
            Snew = _each(lambda x, s_, u: s_ * x["sdec"] + jnp.where(
                same_head, _dot(jnp.concatenate([u, x["v"]], axis=0), x["bkdec"], _TN), 0.0), X, S, U)
            yield
            for n, (b, p, _) in enumerate(CH):
                sscr[b * npair + p] = Snew[n]
                m = _seg_sum(y[n]) * (1.0 / RW_N)
                yc = y[n] - m
                var = _seg_sum(yc * yc) * (1.0 / RW_N)
                yn = yc * lax.rsqrt(var + RW_GN_EPS) * gg_ref[p] + gb_ref[p]
                bonus = _seg_sum(X[n]["r"] * X[n]["kh"] * rk_ref[p]) * X[n]["v"]
                oscr[b * Rt + ci * C:b * Rt + (ci + 1) * C, LANES * p:LANES * (p + 1)] = yn + bonus
            yield

    _run_pipelined(part_a, part_b, batches)

    row_h = lax.broadcasted_iota(jnp.int32, (LANES, RW_N), 0) < RW_N

    @pl.when(tb == pl.num_programs(1) - 1)
    def _final_state():
        def fin_b(b, carry):
            for p in range(npair):
                S = sscr[b * npair + p]
                sT_ref[b, p] = jnp.where(row_h, S[:, :RW_N], S[:, RW_N:])
            return carry
        lax.fori_loop(0, bb, fin_b, 0)

    _gated_out_ln(oscr[...], g_ref, x_ref, wout_ref, lng_ref, lnb_ref, out_ref)


def _rwkv_rec(r4, k4, v4, lw4, a4, g4, x, s0, k_k, k_a, r_k, gn_g, gn_b, wout, lng, lnb, *, C, Rt, bb, gb):
    B, npair, T, _ = r4.shape
    D = x.shape[-1]
    kern = functools.partial(_rwkv_rec_kernel, C=C, batches=_rec_batches(bb, Rt // C, gb, RW_CHUNKS_PER_BATCH))
    tspec = pl.BlockSpec((bb, npair, Rt, LANES), lambda b, t: (b, 0, t, 0))
    xspec = pl.BlockSpec((bb, Rt, D), lambda b, t: (b, t, 0))
    sspec = pl.BlockSpec((bb, npair, LANES, RW_N), lambda b, t: (b, 0, 0, 0))
    wspec = _const_spec((npair, 1, LANES))
    return pl.pallas_call(
        kern,
        grid=(B // bb, T // Rt),
        in_specs=[tspec] * 6 + [xspec, sspec] + [wspec] * 5 + [_const_spec(wout.shape), _const_spec(lng.shape),
                                                              _const_spec(lnb.shape)],
        out_specs=[xspec, sspec],
        out_shape=[jax.ShapeDtypeStruct(x.shape, F32), jax.ShapeDtypeStruct(s0.shape, F32)],
        scratch_shapes=[pltpu.VMEM((bb * npair, LANES, LANES), F32), pltpu.VMEM((bb * Rt, RW_W), F32)],
        compiler_params=_cparams(2),
        name="rwkv_rec",
    )(r4, k4, v4, lw4, a4, g4, x, s0, k_k, k_a, r_k, gn_g, gn_b, wout, lng, lnb)


def _rwkv_step_kernel(r_ref, k_ref, v_ref, lw_ref, a_ref, s0_ref, kk_ref, ka_ref, rk_ref, gg_ref, gb_ref,
                      o_ref, sT_ref, vscr, yscr, *, T, B):
    N = RW_N
    IB = SUBLANES
    for t in range(T):
        rows = pl.ds(t * B, B)
        rT, kT, vT, aT = r_ref[rows, :].T, k_ref[rows, :].T, v_ref[rows, :].T, a_ref[rows, :].T
        w = jnp.exp(lw_ref[rows, :].T)
        kn = kT * kk_ref[...]
        kh = kT * (1.0 + (aT - 1.0) * ka_ref[...])
        vscr[...] = vT
        src = s0_ref if t == 0 else sT_ref
        bonus = []
        for h in range(2):
            hs = slice(N * h, N * (h + 1))
            kk = kn[hs] * lax.rsqrt(jnp.sum(kn[hs] * kn[hs], axis=0, keepdims=True) + RW_NORM_EPS)
            a_h, b_h, k_h, w_h, r_h = -kk, kk * aT[hs], kh[hs], w[hs], rT[hs]

            def step(ib, carry, h=h, a_h=a_h, b_h=b_h, k_h=k_h, w_h=w_h, r_h=r_h, src=src):
                i0 = pl.multiple_of(ib * IB, IB)
                S = src[h, pl.ds(i0, IB), :, :]
                sa = jnp.sum(S * a_h[None], axis=1)
                vb = vscr[pl.ds(N * h + i0, IB), :]
                Sn = S * w_h[None] + sa[:, None, :] * b_h[None] + vb[:, None, :] * k_h[None]
                sT_ref[h, pl.ds(i0, IB), :, :] = Sn
                yscr[pl.ds(N * h + i0, IB), :] = jnp.sum(Sn * r_h[None], axis=1)
                return carry

            lax.fori_loop(0, N // IB, step, 0)
            bonus.append(jnp.sum(r_h * k_h * rk_ref[hs, :], axis=0, keepdims=True) * vT[hs])
        y = yscr[...]
        outs = []
        for h in range(2):
            hs = slice(N * h, N * (h + 1))
            m = jnp.mean(y[hs], axis=0, keepdims=True)
            yc = y[hs] - m
            var = jnp.mean(yc * yc, axis=0, keepdims=True)
            outs.append(yc * lax.rsqrt(var + RW_GN_EPS) * gg_ref[hs, :] + gb_ref[hs, :] + bonus[h])
        o_ref[rows, :] = jnp.concatenate(outs, axis=0).T


def _rwkv_step(r4, k4, v4, lw4, a4, s0, k_k, k_a, r_k, gn_g, gn_b, *, T, B):
    npair = r4.shape[1]
    kern = functools.partial(_rwkv_step_kernel, T=T, B=B)
    tspec = pl.BlockSpec((None, None, T * B, LANES), lambda p: (0, p, 0, 0))
    sspec = pl.BlockSpec((None, 2, RW_N, RW_N, B), lambda p: (p, 0, 0, 0, 0))
    wspec = pl.BlockSpec((None, LANES, 1), lambda p: (p, 0, 0))
    return pl.pallas_call(
        kern,
        grid=(npair,),
        in_specs=[tspec] * 5 + [sspec] + [wspec] * 5,
        out_specs=[tspec, sspec],
        out_shape=[jax.ShapeDtypeStruct(r4.shape, F32), jax.ShapeDtypeStruct(s0.shape, F32)],
        scratch_shapes=[pltpu.VMEM((LANES, B), F32), pltpu.VMEM((LANES, B), F32)],
        compiler_params=_cparams(1),
        name="rwkv_step",
    )(r4, k4, v4, lw4, a4, s0, k_k, k_a, r_k, gn_g, gn_b)


def _gdn_pre_kernel(x_ref, halo_ref, win_ref, cw_ref, alog_ref, dtb_ref,
                    q_ref, k_ref, v_ref, z_ref, bg_ref, tail_ref, xscr, gscr, *, s, R, H, Hg, C):
    c = pl.program_id(1)
    CH = GDN_CONV_CH
    tt = R // s

    @pl.when(c == 0)
    def _init():
        xscr[0:H, :] = halo_ref[...]
        gscr[0:Hg, :] = jnp.zeros((Hg, LANES), F32)

    x = x_ref[...]
    u = _mm(x, win_ref[...])
    xb = u[:, :CH]
    xscr[H:H + R, :] = xb
    y = xb * cw_ref[3]
    for d in (1, 2, 3):
        y = y + xscr[pl.ds(H - d * s, R), :] * cw_ref[3 - d]
    tail = xscr[pl.ds(R, H), :]
    xscr[0:H, :] = tail
    tail_ref[...] = tail
    qkv = _silu(y)

    def l2n(z, scale):
        return z * (lax.rsqrt(jnp.sum(z * z, axis=-1, keepdims=True) + GDN_EPS) * scale)

    for h in range(GDN_HK):
        q_ref[h] = l2n(qkv[:, LANES * h:LANES * (h + 1)], GDN_DK ** -0.5)
        k_ref[h] = l2n(qkv[:, GDN_KEY_W + LANES * h:GDN_KEY_W + LANES * (h + 1)], 1.0)
    for h in range(GDN_HV):
        v_ref[h] = qkv[:, 2 * GDN_KEY_W + LANES * h:2 * GDN_KEY_W + LANES * (h + 1)]
        z_ref[h] = u[:, CH + LANES * h:CH + LANES * (h + 1)]

    bg = u[:, CH + GDN_VAL_W:CH + GDN_VAL_W + LANES]
    beta = jax.nn.sigmoid(bg)
    g = -jnp.exp(alog_ref[...]) * jax.nn.softplus(bg + dtb_ref[...])
    t_in = (lax.broadcasted_iota(jnp.int32, (R, 1), 0) // s) % C
    k = 1
    while k < min(C, tt):
        gscr[Hg:Hg + R, :] = g
        g = g + jnp.where(t_in >= k, gscr[pl.ds(Hg - k * s, R), :], 0.0)
        k *= 2
    lane = lax.broadcasted_iota(jnp.int32, (R, LANES), 1)
    bg_ref[...] = jnp.where(lane < GDN_HV, beta, g)


def _gdn_pre(x3, halo, win, cw, alog, dtb, *, s, R, C):
    NB, TT, D = x3.shape
    H = halo.shape[1]
    tt = R // s
    Hg = max(SUBLANES, (min(C, tt) // 2) * s)
    kern = functools.partial(_gdn_pre_kernel, s=s, R=R, H=H, Hg=Hg, C=C)

    def ospec(nh):
        return pl.BlockSpec((None, nh, R, LANES), lambda i, c: (i, 0, c, 0))

    def oshape(nh):
        return jax.ShapeDtypeStruct((NB, nh, TT, LANES), F32)

    return pl.pallas_call(
        kern,
        grid=(NB, TT // R),
        in_specs=[
            pl.BlockSpec((None, R, D), lambda i, c: (i, c, 0)),
            pl.BlockSpec((None, H, GDN_CONV_CH), lambda i, c: (i, 0, 0)),
            _const_spec(win.shape), _const_spec(cw.shape), _const_spec(alog.shape), _const_spec(dtb.shape),
        ],
        out_specs=[ospec(GDN_HK), ospec(GDN_HK), ospec(GDN_HV), ospec(GDN_HV),
                   pl.BlockSpec((None, R, LANES), lambda i, c: (i, c, 0)),
                   pl.BlockSpec((None, H, GDN_CONV_CH), lambda i, c: (i, 0, 0))],
        out_shape=[oshape(GDN_HK), oshape(GDN_HK), oshape(GDN_HV), oshape(GDN_HV),
                   jax.ShapeDtypeStruct((NB, TT, LANES), F32),
                   jax.ShapeDtypeStruct((NB, H, GDN_CONV_CH), F32)],
        scratch_shapes=[pltpu.VMEM((H + R, GDN_CONV_CH), F32), pltpu.VMEM((Hg + R, LANES), F32)],
        compiler_params=_cparams(2),
        name="gdn_pre",
    )(x3, halo, win, cw, alog, dtb)


def _gdn_rec_kernel(q_ref, k_ref, v_ref, z_ref, bg_ref, x_ref, s0_ref, ng_ref, wout_ref, lng_ref, lnb_ref,
                    out_ref, sT_ref, sscr, oscr, *, C, batches):
    tb = pl.program_id(1)
    bb, nhv, Rt, _ = v_ref.shape
    rep = GDN_HV // GDN_HK

    @pl.when(tb == 0)
    def _init():
        sscr[...] = s0_ref[...].reshape(sscr.shape)

    strict, incl = _packed_masks(C)
    lo = lax.broadcasted_iota(jnp.int32, (C, 2 * C), 1) < C

    def part_a(items, ctx):
        CH = [(b, h, ci) for b, ci in items for h in range(nhv)]
        KH = [(b, m, ci) for b, ci in items for m in range(GDN_HK)]
        rs = lambda ci: pl.ds(ci * C, C)
        bgs = {(b, ci): bg_ref[b, rs(ci), :] for b, ci in items}
        bgT = {key: x.T for key, x in bgs.items()}

        kq = [jnp.concatenate([k_ref[b, m, rs(ci), :], q_ref[b, m, rs(ci), :]], axis=0) for b, m, ci in KH]
        sc = _each(lambda x: _dot(x, jnp.concatenate([x[:C, :], x[:C, :]], axis=0), _NT), kq)
        yield

        def col2(x, lane):
            return jnp.where(lo, jnp.broadcast_to(x[:, lane:lane + 1], (C, 2 * C)),
                             jnp.broadcast_to(x[:, lane + 1:lane + 2], (C, 2 * C)))

        bcol = [col2(bgs[b, ci], rep * m) for b, m, ci in KH]
        gcol = [col2(bgs[b, ci], GDN_HV + rep * m) for b, m, ci in KH]
        grow = [jnp.concatenate([bgT[b, ci][GDN_HV + rep * m:GDN_HV + rep * m + 1, :],
                                 bgT[b, ci][GDN_HV + rep * m + 1:GDN_HV + rep * m + 2, :]], axis=1) for b, m, ci in KH]
        diff = _each(lambda c_, r_: c_ - r_, gcol, grow)
        Lp = _each(lambda z, bc, d: jnp.where(strict, z[:C, :] * bc * jnp.exp(jnp.where(strict, d, 0.0)), 0.0),
                   sc, bcol, diff)
        Ap = _each(lambda z, d: jnp.where(incl, z[C:, :] * jnp.exp(jnp.where(incl, d, 0.0)), 0.0), sc, diff)
        Tinv_p = yield from _unit_lower_inverse([-l_ for l_ in Lp])

        half = lambda xs, n, h: xs[n // rep][:, (h % rep) * C:(h % rep + 1) * C]
        Tinv = [half(Tinv_p, n, h) for n, (b, h, ci) in enumerate(CH)]
        A = [half(Ap, n, h) for n, (b, h, ci) in enumerate(CH)]
        k = [kq[n // rep][:C, :] for n in range(len(CH))]
        q = [kq[n // rep][C:, :] for n in range(len(CH))]
        v = [v_ref[b, h, rs(ci), :] for b, h, ci in CH]
        beta = [jnp.broadcast_to(bgs[b, ci][:, h:h + 1], (C, LANES)) for b, h, ci in CH]
        gc = [jnp.broadcast_to(bgs[b, ci][:, GDN_HV + h:GDN_HV + h + 1], (C, LANES)) for b, h, ci in CH]
        kb = _each(lambda x, y: x * y, k, beta)
        eg = _each(jnp.exp, gc)
        UW = _each(lambda t, v_, b_, kb_, e: _dot(t, jnp.concatenate([v_ * b_, kb_ * e], axis=1)),
                   Tinv, v, beta, kb, eg)
        yield
        for n, key in enumerate(CH):
            g_last = gc[n][C - 1:C, :]
            ctx[key] = dict(U=UW[n][:, :GDN_DV], WQl=jnp.concatenate([UW[n][:, GDN_DV:], q[n] * eg[n]], axis=0),
                            A=A[n], kdec=k[n] * jnp.exp(g_last - gc[n]), sdec=jnp.exp(g_last))

    def part_b(items, ctx):
        for ci in sorted({ci for _, ci in items}):
            CH = [(b, h, ci) for b, c_ in items if c_ == ci for h in range(nhv)]
            X = [ctx.pop(key) for key in CH]
            S = [sscr[b * nhv + h] for b, h, _ in CH]
            WQ = _each(lambda x, s_: _dot(x["WQl"], s_), X, S)
            yield
            v_new = _each(lambda x, wq: x["U"] - wq[:C, :], X, WQ)
            o = _each(lambda x, wq, vn: wq[C:, :] + _dot(x["A"], vn), X, WQ, v_new)
            Snew = _each(lambda x, s_, vn: s_ * x["sdec"] + _dot(x["kdec"], vn, _TN), X, S, v_new)
            yield
            for n, (b, h, _) in enumerate(CH):
                sscr[b * nhv + h] = Snew[n]
                oh = o[n]
                oscr[b * Rt + ci * C:b * Rt + (ci + 1) * C, LANES * h:LANES * (h + 1)] = (
                    oh * lax.rsqrt(jnp.mean(oh * oh, axis=-1, keepdims=True) + GDN_EPS) * ng_ref[...])
            yield

    _run_pipelined(part_a, part_b, batches)

    @pl.when(tb == pl.num_programs(1) - 1)
    def _final_state():
        sT_ref[...] = sscr[...].reshape(sT_ref.shape)

    _gated_out_ln(oscr[...], z_ref, x_ref, wout_ref, lng_ref, lnb_ref, out_ref)


def _gdn_rec(q4, k4, v4, z4, bg, x, s0, norm_g, wout, lng, lnb, *, C, Rt, bb, gb):
    B, _, T, _ = v4.shape
    D = x.shape[-1]
    kern = functools.partial(_gdn_rec_kernel, C=C, batches=_rec_batches(bb, Rt // C, gb, GDN_CHUNKS_PER_BATCH))

    def tspec(nh):
        return pl.BlockSpec((bb, nh, Rt, LANES), lambda b, t: (b, 0, t, 0))

    xspec = pl.BlockSpec((bb, Rt, D), lambda b, t: (b, t, 0))
    sspec = pl.BlockSpec((bb, GDN_HV, GDN_DK, GDN_DV), lambda b, t: (b, 0, 0, 0))
    return pl.pallas_call(
        kern,
        grid=(B // bb, T // Rt),
        in_specs=[tspec(GDN_HK), tspec(GDN_HK), tspec(GDN_HV), tspec(GDN_HV),
                  pl.BlockSpec((bb, Rt, LANES), lambda b, t: (b, t, 0)), xspec, sspec,
                  _const_spec(norm_g.shape), _const_spec(wout.shape), _const_spec(lng.shape), _const_spec(lnb.shape)],
        out_specs=[xspec, sspec],
        out_shape=[jax.ShapeDtypeStruct(x.shape, F32), jax.ShapeDtypeStruct(s0.shape, F32)],
        scratch_shapes=[pltpu.VMEM((bb * GDN_HV, GDN_DK, GDN_DV), F32), pltpu.VMEM((bb * Rt, GDN_VAL_W), F32)],
        compiler_params=_cparams(2),
        name="gdn_rec",
    )(q4, k4, v4, z4, bg, x, s0, norm_g, wout, lng, lnb)


class _Group:
    def __init__(self, B, T, time_major):
        self.B, self.T, self.time_major = B, T, time_major
        if time_major:
            self.s, self.NB, self.TT, self.R = B, 1, T * B, T * B
        else:
            self.s, self.NB, self.TT, self.R = 1, B, T, min(ROW_BLOCK, T)

    def to_rows(self, x):
        if self.time_major:
            return jnp.swapaxes(x, 0, 1).reshape(1, self.TT, x.shape[-1])
        return x

    def from_rows(self, x3):
        if self.time_major:
            return jnp.swapaxes(x3.reshape(self.T, self.B, x3.shape[-1]), 0, 1)
        return x3

    def halo(self, st, n_steps):
        if self.time_major:
            return jnp.swapaxes(st, 0, 1).reshape(1, n_steps * self.B, st.shape[-1])
        H = _halo_rows(n_steps, 1)
        return jnp.pad(st, ((0, 0), (H - n_steps, 0), (0, 0)))

    def unhalo(self, tail, n_steps):
        if self.time_major:
            return jnp.swapaxes(tail.reshape(n_steps, self.B, tail.shape[-1]), 0, 1)
        return tail[:, tail.shape[1] - n_steps:]

    def vec(self, st):
        return st[None] if self.time_major else st[:, None, :]

    def unvec(self, v):
        return v[0] if self.time_major else v[:, 0, :]

    def heads_to_batch(self, a4, Tpad):
        if not self.time_major:
            return a4
        nh = a4.shape[1]
        a = a4.reshape(nh, self.T, self.B, LANES).transpose(2, 0, 1, 3)
        return jnp.pad(a, ((0, 0), (0, 0), (0, Tpad - self.T), (0, 0)))

    def rows_to_batch(self, a3, Tpad):
        if not self.time_major:
            return a3
        return jnp.pad(self.from_rows(a3), ((0, 0), (0, Tpad - self.T), (0, 0)))

    def rows_from_batch(self, a3):
        if not self.time_major:
            return a3
        return self.to_rows(a3[:, :self.T])

    def rec_tiling(self, chunk, block):
        if self.time_major:
            Tp = -(-self.T // SUBLANES) * SUBLANES
            return Tp, Tp, Tp, min(self.B, SAMPLE_SEQ_BLOCK), min(self.B, SAMPLE_SEQ_GROUP)
        return min(chunk, self.T), self.T, min(block, self.T), PROMPT_SEQ_BLOCK if self.B % PROMPT_SEQ_BLOCK == 0 else 1, 1


def _lru_apply(g, x3, conv_st, h_st, p, lng, lnb):
    swap = not g.time_major
    assert g.B % SUBLANES == 0, "RG-LRU kernel needs the batch to fill whole sublane tiles"
    lay = _Group(g.B, g.T, True) if swap else g
    out, tail, hT = _lru_layer(x3, lay.halo(conv_st, CONV_W - 1), lay.vec(h_st), p["win"], p["cw"], p["cb"],
                               p["wg"], p["bg"], p["lam"], p["wout"], lng, lnb, s=lay.s,
                               R=min(LRU_BLOCK, g.T * g.B) if swap else g.R, swap=swap)
    return out, lay.unhalo(tail, CONV_W - 1), lay.unvec(hT)


def _rwkv_apply(g, x3, shift_st, wkv_st, p, lng, lnb):
    B = g.B
    pre = _rwkv_pre(x3, g.halo(shift_st[:, None, :], 1), p["mu"], p["win"], p["w0"], p["w1"], p["w2"], p["a0"],
                    p["a1"], p["a2"], s=g.s, R=g.R)
    new_shift = g.from_rows(x3)[:, -1]
    if g.time_major:
        r4, k4, v4, g4, lw4, a4 = pre
        s0 = jnp.transpose(wkv_st, (1, 2, 3, 0)).reshape(RW_H // 2, 2, RW_N, RW_N, B)
        col = lambda v: jnp.swapaxes(v, 1, 2)
        o4, sT = _rwkv_step(r4, k4, v4, lw4, a4, s0, col(p["k_k"]), col(p["k_a"]), col(p["r_k"]), col(p["gn_g"]),
                            col(p["gn_b"]), T=g.T, B=B)
        out = _post_layer(x3, o4, g4, p["wout"], lng, lnb)
        return out, new_shift, jnp.transpose(sT.reshape(RW_H, RW_N, RW_N, B), (3, 0, 1, 2))
    C, Tp, Rt, bb, gb = g.rec_tiling(RW_CHUNK, RW_BLOCK)
    r4, k4, v4, g4, lw4, a4 = [g.heads_to_batch(a, Tp) for a in pre]
    s0 = wkv_st.reshape(B, RW_H // 2, 2 * RW_N, RW_N)
    out, sT = _rwkv_rec(r4, k4, v4, lw4, a4, g4, g.rows_to_batch(x3, Tp), s0, p["k_k"], p["k_a"], p["r_k"],
                        p["gn_g"], p["gn_b"], p["wout"], lng, lnb, C=C, Rt=Rt, bb=bb, gb=gb)
    new_shift = g.from_rows(x3)[:, -1]
    return g.rows_from_batch(out), new_shift, sT.reshape(B, RW_H, RW_N, RW_N)


def _gdn_apply(g, x3, conv_st, S_st, p, lng, lnb):
    T = g.T
    C, Tp, Rt, bb, gb = g.rec_tiling(GDN_CHUNK, GDN_BLOCK)
    q4, k4, v4, z4, bg, tail = _gdn_pre(x3, g.halo(conv_st, CONV_W - 1), p["win"], p["cw"], p["alog"], p["dtb"],
                                        s=g.s, R=g.R, C=min(GDN_CHUNK, T))
    bgb = g.rows_to_batch(bg, T)
    if Tp != T:
        held = jnp.where(jnp.arange(LANES) < GDN_HV, 0.0, bgb[:, T - 1:T, :])
        bgb = jnp.concatenate([bgb, jnp.broadcast_to(held, (g.B, Tp - T, LANES))], axis=1)
    tb = lambda a: g.heads_to_batch(a, Tp)
    out, sT = _gdn_rec(tb(q4), tb(k4), tb(v4), tb(z4), bgb, g.rows_to_batch(x3, Tp), S_st, p["norm_g"], p["wout"],
                       lng, lnb, C=C, Rt=Rt, bb=bb, gb=gb)
    return g.rows_from_batch(out), g.unhalo(tail, CONV_W - 1), sT


def _trunk(g, x, st, params, ln_g, ln_b):
    lru_conv, lru_h, rw_shift, rw_S, gdn_conv, gdn_S = st
    new = ([], [], [], [], [], [])
    x3 = g.to_rows(x)
    ia = ib = ic = 0
    for layer in range(DEPTH):
        kind = layer % N_MIXERS
        lng, lnb = ln_g[layer][None, :], ln_b[layer][None, :]
        if kind == 0:
            x3, c, h = _lru_apply(g, x3, lru_conv[ia], lru_h[ia], params["lru"][ia], lng, lnb)
            new[0].append(c)
            new[1].append(h)
            ia += 1
        elif kind == 1:
            x3, sh, S = _rwkv_apply(g, x3, rw_shift[ib], rw_S[ib], params["rwkv"][ib], lng, lnb)
            new[2].append(sh)
            new[3].append(S)
            ib += 1
        else:
            x3, c, S = _gdn_apply(g, x3, gdn_conv[ic], gdn_S[ic], params["gdn"][ic], lng, lnb)
            new[4].append(c)
            new[5].append(S)
            ic += 1
    return g.from_rows(x3), tuple(s[0][None] if len(s) == 1 else jnp.stack(s) for s in new)


def _prep_params(lru_w_in, lru_conv_w, lru_conv_b, lru_wa, lru_ba, lru_wx, lru_bx, lru_lambda, lru_w_out, rw_mu,
                 rw_w_in, rw_w0, rw_w1, rw_w2, rw_a0, rw_a1, rw_a2, rw_k_k, rw_k_a, rw_r_k, rw_gn_g, rw_gn_b,
                 rw_w_out, gdn_w_in, gdn_conv_w, gdn_a_log, gdn_dt_bias, gdn_norm_g, gdn_w_out):
    row = lambda v: v[None, :]
    lru = []
    for n in range(lru_w_in.shape[0]):
        lru.append(dict(win=lru_w_in[n].astype(BF), cw=lru_conv_w[n][:, None, :], cb=row(lru_conv_b[n]),
                        wg=_lru_gate_weights(lru_wa[n], lru_wx[n]), bg=jnp.stack([lru_ba[n], lru_bx[n]])[:, None, :],
                        lam=row(lru_lambda[n]), wout=lru_w_out[n].astype(BF)))
    pairw = lambda v: v.reshape(RW_W // LANES, 1, LANES)
    rwkv = []
    for n in range(rw_w_in.shape[0]):
        rwkv.append(dict(mu=rw_mu[n][:, None, :],win=rw_w_in[n].astype(BF), w0=row(rw_w0[n]), w1=rw_w1[n].astype(BF),
                         w2=rw_w2[n].astype(BF), a0=row(rw_a0[n]), a1=rw_a1[n].astype(BF), a2=rw_a2[n].astype(BF),
                         k_k=pairw(rw_k_k[n]), k_a=pairw(rw_k_a[n]), r_k=pairw(rw_r_k[n]), gn_g=pairw(rw_gn_g[n]),
                         gn_b=pairw(rw_gn_b[n]), wout=rw_w_out[n].astype(BF)))
    gdn = []
    for n in range(gdn_w_in.shape[0]):
        w = gdn_w_in[n]
        o2 = GDN_CONV_CH + GDN_VAL_W
        wpad = jnp.pad(w[:, o2:], ((0, 0), (0, LANES - 2 * GDN_HV)))
        lanes = lambda v: jnp.pad(v, (GDN_HV, LANES - 2 * GDN_HV))[None, :]
        gdn.append(dict(win=jnp.concatenate([w[:, :o2], wpad], axis=1).astype(BF), cw=gdn_conv_w[n][:, None, :],
                        alog=lanes(gdn_a_log[n]), dtb=lanes(gdn_dt_bias[n]), norm_g=row(gdn_norm_g[n]),
                        wout=gdn_w_out[n].astype(BF)))
    return dict(lru=lru, rwkv=rwkv, gdn=gdn)


def kernel(x_prompt, x_sample, state_lru_conv, state_lru_h, state_rwkv_shift, state_rwkv_wkv, state_gdn_conv, state_gdn_S, ln_g, ln_b, lru_w_in, lru_conv_w, lru_conv_b, lru_wa, lru_ba, lru_wx, lru_bx, lru_lambda, lru_w_out, rw_mu, rw_w_in, rw_w0, rw_w1, rw_w2, rw_a0, rw_a1, rw_a2, rw_k_k, rw_k_a, rw_r_k, rw_gn_g, rw_gn_b, rw_w_out, gdn_w_in, gdn_conv_w, gdn_a_log, gdn_dt_bias, gdn_norm_g, gdn_w_out):
    params = _prep_params(lru_w_in, lru_conv_w, lru_conv_b, lru_wa, lru_ba, lru_wx, lru_bx, lru_lambda, lru_w_out,
                          rw_mu, rw_w_in, rw_w0, rw_w1, rw_w2, rw_a0, rw_a1, rw_a2, rw_k_k, rw_k_a, rw_r_k, rw_gn_g,
                          rw_gn_b, rw_w_out, gdn_w_in, gdn_conv_w, gdn_a_log, gdn_dt_bias, gdn_norm_g, gdn_w_out)
    bp, tp, _ = x_prompt.shape
    bs, ts, _ = x_sample.shape
    n_a, n_b, n_c = state_lru_conv.shape[0], state_rwkv_shift.shape[0], state_gdn_conv.shape[0]
    zero_state = (jnp.zeros((n_a, bp, CONV_W - 1, LRU_W), F32),
                  jnp.zeros((n_a, bp, LRU_W), F32),
                  jnp.zeros((n_b, bp, D_MODEL), F32),
                  jnp.zeros((n_b, bp, RW_H, RW_N, RW_N), F32),
                  jnp.zeros((n_c, bp, CONV_W - 1, GDN_CONV_CH), F32),
                  jnp.zeros((n_c, bp, GDN_HV, GDN_DK, GDN_DV), F32))
    y_prompt, sp = _trunk(_Group(bp, tp, False), x_prompt, zero_state, params, ln_g, ln_b)
    y_sample, ss = _trunk(_Group(bs, ts, True), x_sample,
                          (state_lru_conv, state_lru_h, state_rwkv_shift, state_rwkv_wkv, state_gdn_conv,
                           state_gdn_S), params, ln_g, ln_b)
    return (y_prompt, y_sample, sp[0], ss[0], sp[1], ss[1], sp[2], ss[2], sp[3], ss[3], sp[4], ss[4], sp[5], ss[5])
```

```python
import functools
import math

import jax
import jax.numpy as jnp
from jax import lax
from jax.experimental import pallas as pl
from jax.experimental.pallas import tpu as pltpu

F32 = jnp.float32
BF = jnp.bfloat16

D_MODEL = 1024
DEPTH = 4
N_MIXERS = 3
DN_ALPHA = (2.0 * DEPTH) ** 0.25
LN_EPS = 1e-5
CONV_W = 4

LRU_W = D_MODEL
LRU_BLOCKS = 16
LRU_BS = LRU_W // LRU_BLOCKS
LRU_C = 8.0

RW_W = D_MODEL
RW_N = 64
RW_H = RW_W // RW_N
RW_GN_EPS = 64e-5
RW_NORM_EPS = 1e-12

GDN_HK = 4
GDN_HV = 8
GDN_DK = 128
GDN_DV = 128
GDN_KEY_W = GDN_HK * GDN_DK
GDN_VAL_W = GDN_HV * GDN_DV
GDN_CONV_CH = 2 * GDN_KEY_W + GDN_VAL_W
GDN_CHUNK = 64
GDN_EPS = 1e-6

LANES = 128
SUBLANES = 8
MXU_WIDTH = 256
VMEM_LIMIT = 56 * 1024 * 1024
ROW_BLOCK = 512
LRU_BLOCK = 1024
RW_BLOCK = 256
GDN_BLOCK = 256
RW_CHUNKS_PER_BATCH = 1
GDN_CHUNKS_PER_BATCH = 2
PROMPT_SEQ_BLOCK = 2
SAMPLE_SEQ_BLOCK = 8
RW_CHUNK = 64

_NN = (((1,), (0,)), ((), ()))
_NT = (((1,), (1,)), ((), ()))
_TN = (((0,), (0,)), ((), ()))


def _mm(a, b):
    return jnp.dot(a.astype(BF), b.astype(BF), preferred_element_type=F32)


def _dot(a, b, dims=_NN):
    return lax.dot_general(a.astype(BF), b.astype(BF), dims, preferred_element_type=F32)


def _cumsum_rows(x, tri3):
    b1 = x.astype(BF)
    r1 = x - b1.astype(F32)
    b2 = r1.astype(BF)
    b3 = (r1 - b2.astype(F32)).astype(BF)
    return jnp.dot(tri3, jnp.concatenate([b1, b2, b3], axis=0), preferred_element_type=F32)


def _layer_norm(z, g, b):
    mu = jnp.mean(z, axis=-1, keepdims=True)
    zc = z - mu
    var = jnp.mean(zc * zc, axis=-1, keepdims=True)
    return zc * lax.rsqrt(var + LN_EPS) * g + b


def _silu(x):
    return x * jax.nn.sigmoid(x)


def _cparams(n_axes):
    return pltpu.CompilerParams(dimension_semantics=("arbitrary",) * n_axes, vmem_limit_bytes=VMEM_LIMIT)


def _const_spec(shape):
    nd = len(shape)
    return pl.BlockSpec(shape, lambda *_: (0,) * nd, pipeline_mode=pl.Buffered(1))


def _halo_rows(n_steps, s):
    rows = n_steps * s
    return rows if rows % SUBLANES == 0 else SUBLANES


def _lru_kernel(x_ref, halo_ref, h0_ref, win_ref, cw_ref, cb_ref, wg_ref, bg_ref, lam_ref, wout_ref, lng_ref,
                lnb_ref, out_ref, tail_ref, hT_ref, xscr, hscr, *, s, R, H, swap):
    c = pl.program_id(1)
    C = LRU_W
    tt = R // s

    @pl.when(c == 0)
    def _init():
        xscr[0:H, :] = halo_ref[...]
        hscr[...] = h0_ref[...]

    x = x_ref[...]
    if swap:
        x = jnp.swapaxes(x, 0, 1).reshape(R, D_MODEL)
    u = _mm(x, win_ref[...])
    xb = u[:, :C]
    gate = u[:, C:]

    xscr[H:H + R, :] = xb
    xc = xb * cw_ref[3] + cb_ref[...]
    for d in (1, 2, 3):
        xc = xc + xscr[pl.ds(H - d * s, R), :] * cw_ref[3 - d]
    tail = xscr[pl.ds(R, H), :]
    xscr[0:H, :] = tail
    tail_ref[...] = tail

    G = MXU_WIDTH
    xcb = xc.astype(BF)
    ra, ix = [], []
    for g in range(C // G):
        gt = jnp.dot(xcb[:, G * g:G * (g + 1)], wg_ref[g], preferred_element_type=F32)
        ra.append(gt[:, :G])
        ix.append(gt[:, G:])
    r = jax.nn.sigmoid(jnp.concatenate(ra, axis=1) + bg_ref[0])
    i = jax.nn.sigmoid(jnp.concatenate(ix, axis=1) + bg_ref[1])
    log_a = r * ((-LRU_C) * jax.nn.softplus(-lam_ref[...]))
    a = jnp.exp(log_a)
    b = jnp.sqrt(1.0 - a * a) * i * xc

    hlast = hscr[...]
    hs = []
    for t in range(tt):
        hlast = a[t * s:(t + 1) * s, :] * hlast + b[t * s:(t + 1) * s, :]
        hs.append(hlast)
    h = jnp.concatenate(hs, axis=0)
    hscr[...] = hlast
    hT_ref[...] = hlast

    y = _mm(h * _silu(gate), wout_ref[...])
    out = _layer_norm(DN_ALPHA * x + y, lng_ref[...], lnb_ref[...])
    if swap:
        out = jnp.swapaxes(out.reshape(tt, s, D_MODEL), 0, 1)
    out_ref[...] = out


def _lru_layer(x3, halo, h0, win, cw, cb, wg, bg, lam, wout, lng, lnb, *, s, R, swap):
    C = LRU_W
    D = x3.shape[-1]
    H = halo.shape[1]
    tt = R // s
    assert s % SUBLANES == 0
    if swap:
        NB, TT = 1, x3.shape[1] * s
        xspec = pl.BlockSpec((s, tt, D), lambda i, c: (0, c, 0))
    else:
        NB, TT = x3.shape[:2]
        xspec = pl.BlockSpec((None, R, D), lambda i, c: (i, c, 0))
    kern = functools.partial(_lru_kernel, s=s, R=R, H=H, swap=swap)
    return pl.pallas_call(
        kern,
        grid=(NB, TT // R),
        in_specs=[
            xspec,
            pl.BlockSpec((None, H, C), lambda i, c: (i, 0, 0)),
            pl.BlockSpec((None, s, C), lambda i, c: (i, 0, 0)),
            _const_spec(win.shape), _const_spec(cw.shape), _const_spec(cb.shape), _const_spec(wg.shape),
            _const_spec(bg.shape), _const_spec(lam.shape), _const_spec(wout.shape), _const_spec(lng.shape),
            _const_spec(lnb.shape),
        ],
        out_specs=[
            xspec,
            pl.BlockSpec((None, H, C), lambda i, c: (i, 0, 0)),
            pl.BlockSpec((None, s, C), lambda i, c: (i, 0, 0)),
        ],
        out_shape=[
            jax.ShapeDtypeStruct(x3.shape, F32),
            jax.ShapeDtypeStruct((NB, H, C), F32),
            jax.ShapeDtypeStruct((NB, s, C), F32),
        ],
        scratch_shapes=[
            pltpu.VMEM((H + R, C), F32),
            pltpu.VMEM((s, C), F32),
        ],
        compiler_params=_cparams(2),
        name="lru_layer",
    )(x3, halo, h0, win, cw, cb, wg, bg, lam, wout, lng, lnb)


def _lru_gate_weights(wa, wx):
    G = MXU_WIDTH
    nb = G // LRU_BS

    def bd(w):
        w4 = w.reshape(LRU_BLOCKS // nb, nb, LRU_BS, LRU_BS)
        eye = jnp.eye(nb, dtype=w.dtype)
        return jnp.einsum("gaij,ab->gaibj", w4, eye).reshape(LRU_BLOCKS // nb, G, G)
    return jnp.concatenate([bd(wa), bd(wx)], axis=2).astype(BF)


def _gated_out_ln(o, g_ref, x_ref, w_ref, lng_ref, lnb_ref, out_ref):
    bb, nh, Rt, _ = g_ref.shape
    rows = bb * Rt
    g = jnp.concatenate([g_ref[:, p, :, :].reshape(rows, LANES) for p in range(nh)], axis=1)
    y = _mm(o * _silu(g), w_ref[...])
    x = x_ref[...].reshape(rows, D_MODEL)
    out_ref[...] = _layer_norm(DN_ALPHA * x + y, lng_ref[...], lnb_ref[...]).reshape(bb, Rt, D_MODEL)


def _post_kernel(x_ref, o_ref, g_ref, w_ref, lng_ref, lnb_ref, out_ref):
    o = jnp.concatenate([o_ref[0, p] for p in range(o_ref.shape[1])], axis=1)
    _gated_out_ln(o, g_ref, x_ref, w_ref, lng_ref, lnb_ref, out_ref)


def _post_layer(x3, o4, g4, w, lng, lnb):
    return pl.pallas_call(
        _post_kernel,
        out_shape=jax.ShapeDtypeStruct(x3.shape, F32),
        compiler_params=pltpu.CompilerParams(vmem_limit_bytes=VMEM_LIMIT),
        name="post_layer",
    )(x3, o4, g4, w, lng, lnb)


def _rwkv_pre_kernel(x_ref, halo_ref, mu_ref, win_ref, w0_ref, w1_ref, w2_ref, a0_ref, a1_ref, a2_ref,
                     r_ref, k_ref, v_ref, g_ref, lw_ref, a_ref, xscr, xxscr, *, s, R, H):
    c = pl.program_id(1)

    @pl.when(c == 0)
    def _init():
        xscr[0:H, :] = halo_ref[...]

    x = x_ref[...]
    xscr[H:H + R, :] = x
    xxscr[...] = xscr[pl.ds(H - s, R), :] - x
    xscr[0:H, :] = xscr[pl.ds(R, H), :]
    xx = xxscr[...]
    xm = lambda n: x + xx * mu_ref[n]

    def put(ref, val):
        for p in range(RW_W // LANES):
            ref[p] = val[:, LANES * p:LANES * (p + 1)]

    put(r_ref, _mm(xm(0), win_ref[0]))
    put(k_ref, _mm(xm(1), win_ref[1]))
    put(v_ref, _mm(xm(2), win_ref[2]))
    put(g_ref, _mm(xm(3), win_ref[3]))
    w_raw = w0_ref[...] + _mm(jnp.tanh(_mm(xm(4), w1_ref[...])), w2_ref[...])
    put(lw_ref, (-math.exp(-0.5)) * jax.nn.sigmoid(w_raw))
    put(a_ref, jax.nn.sigmoid(a0_ref[...] + _mm(_mm(xm(5), a1_ref[...]), a2_ref[...])))


def _rwkv_pre(x3, halo, mu, win, w0, w1, w2, a0, a1, a2, *, s, R):
    NB, TT, D = x3.shape
    H = halo.shape[1]
    nh = RW_W // LANES
    kern = functools.partial(_rwkv_pre_kernel, s=s, R=R, H=H)
    ospec = pl.BlockSpec((None, nh, R, LANES), lambda i, c: (i, 0, c, 0))
    oshape = jax.ShapeDtypeStruct((NB, nh, TT, LANES), F32)
    return pl.pallas_call(
        kern,
        grid=(NB, TT // R),
        in_specs=[
            pl.BlockSpec((None, R, D), lambda i, c: (i, c, 0)),
            pl.BlockSpec((None, H, D), lambda i, c: (i, 0, 0)),
            _const_spec(mu.shape), _const_spec(win.shape), _const_spec(w0.shape), _const_spec(w1.shape),
            _const_spec(w2.shape), _const_spec(a0.shape), _const_spec(a1.shape), _const_spec(a2.shape),
        ],
        out_specs=[ospec] * 6,
        out_shape=[oshape] * 6,
        scratch_shapes=[pltpu.VMEM((H + R, D), F32), pltpu.VMEM((R, D), F32)],
        compiler_params=_cparams(2),
        name="rwkv_pre",
    )(x3, halo, mu, win, w0, w1, w2, a0, a1, a2)


def _seg_sum(x):
    lane = lax.broadcasted_iota(jnp.int32, x.shape, 1)
    lo = lane < RW_N
    s0 = jnp.sum(jnp.where(lo, x, 0.0), axis=-1, keepdims=True)
    s1 = jnp.sum(jnp.where(lo, 0.0, x), axis=-1, keepdims=True)
    return jnp.where(lo, s0, s1)


def _stack2(x):
    lane = lax.broadcasted_iota(jnp.int32, x.shape, 1)
    lo = lane < x.shape[1] // 2
    return jnp.concatenate([jnp.where(lo, x, 0.0), jnp.where(lo, 0.0, x)], axis=0)


def _each(f, *lists):
    return [f(*t) for t in zip(*lists)]


def _packed_masks(C):
    t = lax.broadcasted_iota(jnp.int32, (C, 2 * C), 0)
    s = lax.broadcasted_iota(jnp.int32, (C, 2 * C), 1) % C
    return s < t, s <= t


def _stackn(x, n):
    grp = lax.broadcasted_iota(jnp.int32, x.shape, 1) // (x.shape[1] // n)
    return jnp.concatenate([jnp.where(grp == k, x, 0.0) for k in range(n)], axis=0)


def _neumann_inverse(Ps, nblk):
    m = Ps[0].shape[0]
    eye = (lax.broadcasted_iota(jnp.int32, Ps[0].shape, 1) % m) == lax.broadcasted_iota(jnp.int32, Ps[0].shape, 0)
    invs = [jnp.where(eye, 1.0, P) for P in Ps]
    pw = Ps
    span = 2
    while span < m:
        pw = _each(lambda p: _dot(p, _stackn(p, nblk)), pw)
        yield
        invs = _each(lambda inv, p: inv + _dot(inv, _stackn(p, nblk)), invs, pw)
        yield
        span *= 2
    return invs


def _unit_lower_inverse(Ls):
    C = Ls[0].shape[0]
    h = C // 2
    if h % SUBLANES:
        return (yield from _neumann_inverse(Ls, 2))
    grp = lax.broadcasted_iota(jnp.int32, (h, 2 * C), 1) // h
    diag0 = (grp % 2) == 0
    zero = jnp.zeros((h, 2 * C), F32)
    D = yield from _neumann_inverse([jnp.where(diag0, L[:h, :], L[h:, :]) for L in Ls], 4)
    L21 = [jnp.where(diag0, L[h:, :], 0.0) for L in Ls]
    M1 = _each(lambda d, q: _dot(d, jnp.concatenate(
        [zero, jnp.where(grp == 0, q, 0.0), zero, jnp.where(grp == 2, q, 0.0)], axis=0)), D, L21)
    yield
    X21 = _each(lambda m1, d: _dot(m1, jnp.concatenate(
        [jnp.where(grp == 0, d, 0.0), zero, jnp.where(grp == 2, d, 0.0), zero], axis=0)), M1, D)
    yield
    return _each(lambda d, x21: jnp.concatenate([jnp.where(diag0, d, 0.0), x21 + jnp.where(diag0, 0.0, d)], axis=0),
                 D, X21)


def _run_pipelined(part_a, part_b, batches):
    ctx = {}
    prev = None
    for batch in batches:
        gens = [part_a(batch, ctx)] + ([part_b(prev, ctx)] if prev is not None else [])
        while gens:
            for g in list(gens):
                try:
                    next(g)
                except StopIteration:
                    gens.remove(g)
        prev = batch
    for _ in part_b(prev, ctx):
        pass


def _rec_batches(bb, nchunk, gb, cpb):
    if nchunk > 1:
        return tuple(tuple((b, c) for b in range(bb) for c in range(c0, min(c0 + cpb, nchunk)))
                     for c0 in range(0, nchunk, cpb))
    return tuple(tuple((b, 0) for b in range(b0, b0 + gb)) for b0 in range(0, bb, gb))


def _rwkv_rec_kernel(r_ref, k_ref, v_ref, lw_ref, a_ref, g_ref, x_ref, s0_ref, kk_ref, ka_ref, rk_ref, gg_ref,
                     gb_ref, wout_ref, lng_ref, lnb_ref, out_ref, sT_ref, sscr, oscr, *, C, batches):
    tb = pl.program_id(1)
    bb, npair, Rt, _ = r_ref.shape
    C2 = 2 * C
    lane_sq = lax.broadcasted_iota(jnp.int32, (LANES, LANES), 1)
    row_sq = lax.broadcasted_iota(jnp.int32, (LANES, LANES), 0)
    same_head = (lane_sq < RW_N) == (row_sq < RW_N)

    @pl.when(tb == 0)
    def _init():
        def init_b(b, carry):
            for p in range(npair):
                s2 = s0_ref[b, p]
                sscr[b * npair + p] = jnp.where(same_head, jnp.concatenate([s2, s2], axis=1), 0.0)
            return carry
        lax.fori_loop(0, bb, init_b, 0)

    strict, incl = _packed_masks(C)
    tri3 = ((lax.broadcasted_iota(jnp.int32, (C, 3 * C), 1) % C)
            <= lax.broadcasted_iota(jnp.int32, (C, 3 * C), 0)).astype(BF)

    def part_a(items, ctx):
        CH = [(b, p, ci) for b, ci in items for p in range(npair)]
        rs = lambda ci: pl.ds(ci * C, C)
        r = [r_ref[b, p, rs(ci), :] for b, p, ci in CH]
        k = [k_ref[b, p, rs(ci), :] for b, p, ci in CH]
        v = [v_ref[b, p, rs(ci), :] for b, p, ci in CH]
        lw = [lw_ref[b, p, rs(ci), :] for b, p, ci in CH]
        a = [a_ref[b, p, rs(ci), :] for b, p, ci in CH]
        kn = [k_ * kk_ref[p] for k_, (b, p, ci) in zip(k, CH)]
        kk = _each(lambda z: z * lax.rsqrt(_seg_sum(z * z) + RW_NORM_EPS), kn)
        kh = [k_ * (1.0 + (a_ - 1.0) * ka_ref[p]) for k_, a_, (b, p, ci) in zip(k, a, CH)]
        bvec = _each(lambda x, y: x * y, kk, a)
        cum = _each(lambda x: _cumsum_rows(x, tri3), lw)
        yield
        cum_last = [c_[C - 1:C, :] for c_ in cum]
        e_neg = _each(lambda c_: jnp.exp(-c_), cum)
        e_dec = _each(lambda cl, c_: jnp.exp(cl - c_), cum_last, cum)
        At = _each(lambda kk_, c_, lw_: -kk_ * jnp.exp(c_ - lw_), kk, cum, lw)
        Rt_ = _each(lambda r_, c_: r_ * jnp.exp(c_), r, cum)
        AR = _each(lambda x, y: jnp.concatenate([x, y], axis=0), At, Rt_)
        BKs = _each(lambda b_, kh_, e: jnp.concatenate([_stack2(b_ * e), _stack2(kh_ * e)], axis=0), bvec, kh, e_neg)
        V2 = _each(_stack2, v)
        sc = _each(lambda x, y: _dot(x, y, _NT), AR, BKs)
        yield
        Lab = [jnp.where(strict, z[:C, :C2], 0.0) for z in sc]
        Lak = [jnp.where(strict, z[:C, C2:], 0.0) for z in sc]
        Mrbk = [jnp.concatenate([jnp.where(incl, z[C:, :C2], 0.0), jnp.where(incl, z[C:, C2:], 0.0)], axis=1)
                for z in sc]
        LV = _each(_dot, Lak, V2)
        Tinv = yield from _unit_lower_inverse(Lab)
        for n, key in enumerate(CH):
            ctx[key] = dict(AR=AR[n], V2=V2[n], LV=LV[n], Tinv=Tinv[n], Mrbk=Mrbk[n], v=v[n], r=r[n], kh=kh[n],
                            sdec=jnp.exp(cum_last[n]), bkdec=jnp.concatenate([bvec[n] * e_dec[n], kh[n] * e_dec[n]], 0))

    def part_b(items, ctx):
        for ci in sorted({ci for _, ci in items}):
            CH = [(b, p, ci) for b, c_ in items if c_ == ci for p in range(npair)]
            X = [ctx.pop(key) for key in CH]
            S = [sscr[b * npair + p] for b, p, _ in CH]
            UY0 = _each(lambda x, s_: _dot(x["AR"], s_, _NT), X, S)
            yield
            U = _each(lambda x, u: _dot(x["Tinv"], _stack2(u[:C, :] + x["LV"])), X, UY0)
            yield
            y = _each(lambda x, u0, u: u0[C:, :] + _dot(x["Mrbk"], jnp.concatenate([_stack2(u), x["V2"]], axis=0)),
                      X, UY0, U)
            Snew = _each(lambda x, s_, u: s_ * x["sdec"] + jnp.where(
                same_head, _dot(jnp.concatenate([u, x["v"]], axis=0), x["bkdec"], _TN), 0.0), X, S, U)
            yield
            for n, (b, p, _) in enumerate(CH):
                sscr[b * npair + p] = Snew[n]
                m = _seg_sum(y[n]) * (1.0 / RW_N)
                yc = y[n] - m
                var = _seg_sum(yc * yc) * (1.0 / RW_N)
                yn = yc * lax.rsqrt(var + RW_GN_EPS) * gg_ref[p] + gb_ref[p]
                bonus = _seg_sum(X[n]["r"] * X[n]["kh"] * rk_ref[p]) * X[n]["v"]
                oscr[b * Rt + ci * C:b * Rt + (ci + 1) * C, LANES * p:LANES * (p + 1)] = yn + bonus
            yield

    _run_pipelined(part_a, part_b, batches)

    row_h = lax.broadcasted_iota(jnp.int32, (LANES, RW_N), 0) < RW_N

    @pl.when(tb == pl.num_programs(1) - 1)
    def _final_state():
        def fin_b(b, carry):
            for p in range(npair):
                S = sscr[b * npair + p]
                sT_ref[b, p] = jnp.where(row_h, S[:, :RW_N], S[:, RW_N:])
            return carry
        lax.fori_loop(0, bb, fin_b, 0)

    _gated_out_ln(oscr[...], g_ref, x_ref, wout_ref, lng_ref, lnb_ref, out_ref)


def _rwkv_rec(r4, k4, v4, lw4, a4, g4, x, s0, k_k, k_a, r_k, gn_g, gn_b, wout, lng, lnb, *, C, Rt, bb, gb):
    B, npair, T, _ = r4.shape
    D = x.shape[-1]
    kern = functools.partial(_rwkv_rec_kernel, C=C, batches=_rec_batches(bb, Rt // C, gb, RW_CHUNKS_PER_BATCH))
    tspec = pl.BlockSpec((bb, npair, Rt, LANES), lambda b, t: (b, 0, t, 0))
    xspec = pl.BlockSpec((bb, Rt, D), lambda b, t: (b, t, 0))
    sspec = pl.BlockSpec((bb, npair, LANES, RW_N), lambda b, t: (b, 0, 0, 0))
    wspec = _const_spec((npair, 1, LANES))
    return pl.pallas_call(
        kern,
        grid=(B // bb, T // Rt),
        in_specs=[tspec] * 6 + [xspec, sspec] + [wspec] * 5 + [_const_spec(wout.shape), _const_spec(lng.shape),
                                                              _const_spec(lnb.shape)],
        out_specs=[xspec, sspec],
        out_shape=[jax.ShapeDtypeStruct(x.shape, F32), jax.ShapeDtypeStruct(s0.shape, F32)],
        scratch_shapes=[pltpu.VMEM((bb * npair, LANES, LANES), F32), pltpu.VMEM((bb * Rt, RW_W), F32)],
        compiler_params=_cparams(2),
        name="rwkv_rec",
    )(r4, k4, v4, lw4, a4, g4, x, s0, k_k, k_a, r_k, gn_g, gn_b, wout, lng, lnb)


def _rwkv_step_kernel(r_ref, k_ref, v_ref, lw_ref, a_ref, s0_ref, kk_ref, ka_ref, rk_ref, gg_ref, gb_ref,
                      o_ref, sT_ref, vscr, yscr, *, T, B):
    N = RW_N
    IB = SUBLANES
    for t in range(T):
        rows = pl.ds(t * B, B)
        rT, kT, vT, aT = r_ref[rows, :].T, k_ref[rows, :].T, v_ref[rows, :].T, a_ref[rows, :].T
        w = jnp.exp(lw_ref[rows, :].T)
        kn = kT * kk_ref[...]
        kh = kT * (1.0 + (aT - 1.0) * ka_ref[...])
        vscr[...] = vT
        src = s0_ref if t == 0 else sT_ref
        bonus = []
        for h in range(2):
            hs = slice(N * h, N * (h + 1))
            kk = kn[hs] * lax.rsqrt(jnp.sum(kn[hs] * kn[hs], axis=0, keepdims=True) + RW_NORM_EPS)
            a_h, b_h, k_h, w_h, r_h = -kk, kk * aT[hs], kh[hs], w[hs], rT[hs]

            def step(ib, carry, h=h, a_h=a_h, b_h=b_h, k_h=k_h, w_h=w_h, r_h=r_h, src=src):
                i0 = pl.multiple_of(ib * IB, IB)
                S = src[h, pl.ds(i0, IB), :, :]
                sa = jnp.sum(S * a_h[None], axis=1)
                vb = vscr[pl.ds(N * h + i0, IB), :]
                Sn = S * w_h[None] + sa[:, None, :] * b_h[None] + vb[:, None, :] * k_h[None]
                sT_ref[h, pl.ds(i0, IB), :, :] = Sn
                yscr[pl.ds(N * h + i0, IB), :] = jnp.sum(Sn * r_h[None], axis=1)
                return carry

            lax.fori_loop(0, N // IB, step, 0)
            bonus.append(jnp.sum(r_h * k_h * rk_ref[hs, :], axis=0, keepdims=True) * vT[hs])
        y = yscr[...]
        outs = []
        for h in range(2):
            hs = slice(N * h, N * (h + 1))
            m = jnp.mean(y[hs], axis=0, keepdims=True)
            yc = y[hs] - m
            var = jnp.mean(yc * yc, axis=0, keepdims=True)
            outs.append(yc * lax.rsqrt(var + RW_GN_EPS) * gg_ref[hs, :] + gb_ref[hs, :] + bonus[h])
        o_ref[rows, :] = jnp.concatenate(outs, axis=0).T


def _rwkv_step(r4, k4, v4, lw4, a4, s0, k_k, k_a, r_k, gn_g, gn_b, *, T, B):
    npair = r4.shape[1]
    kern = functools.partial(_rwkv_step_kernel, T=T, B=B)
    tspec = pl.BlockSpec((None, None, T * B, LANES), lambda p: (0, p, 0, 0))
    sspec = pl.BlockSpec((None, 2, RW_N, RW_N, B), lambda p: (p, 0, 0, 0, 0))
    wspec = pl.BlockSpec((None, LANES, 1), lambda p: (p, 0, 0))
    return pl.pallas_call(
        kern,
        grid=(npair,),
        in_specs=[tspec] * 5 + [sspec] + [wspec] * 5,
        out_specs=[tspec, sspec],
        out_shape=[jax.ShapeDtypeStruct(r4.shape, F32), jax.ShapeDtypeStruct(s0.shape, F32)],
        scratch_shapes=[pltpu.VMEM((LANES, B), F32), pltpu.VMEM((LANES, B), F32)],
        compiler_params=_cparams(1),
        name="rwkv_step",
    )(r4, k4, v4, lw4, a4, s0, k_k, k_a, r_k, gn_g, gn_b)


def _gdn_pre_kernel(x_ref, halo_ref, win_ref, cw_ref, alog_ref, dtb_ref,
                    q_ref, k_ref, v_ref, z_ref, bg_ref, tail_ref, xscr, gscr, *, s, R, H, Hg, C):
    c = pl.program_id(1)
    CH = GDN_CONV_CH
    tt = R // s

    @pl.when(c == 0)
    def _init():
        xscr[0:H, :] = halo_ref[...]
        gscr[0:Hg, :] = jnp.zeros((Hg, LANES), F32)

    x = x_ref[...]
    u = _mm(x, win_ref[...])
    xb = u[:, :CH]
    xscr[H:H + R, :] = xb
    y = xb * cw_ref[3]
    for d in (1, 2, 3):
        y = y + xscr[pl.ds(H - d * s, R), :] * cw_ref[3 - d]
    tail = xscr[pl.ds(R, H), :]
    xscr[0:H, :] = tail
    tail_ref[...] = tail
    qkv = _silu(y)

    def l2n(z, scale):
        return z * (lax.rsqrt(jnp.sum(z * z, axis=-1, keepdims=True) + GDN_EPS) * scale)

    for h in range(GDN_HK):
        q_ref[h] = l2n(qkv[:, LANES * h:LANES * (h + 1)], GDN_DK ** -0.5)
        k_ref[h] = l2n(qkv[:, GDN_KEY_W + LANES * h:GDN_KEY_W + LANES * (h + 1)], 1.0)
    for h in range(GDN_HV):
        v_ref[h] = qkv[:, 2 * GDN_KEY_W + LANES * h:2 * GDN_KEY_W + LANES * (h + 1)]
        z_ref[h] = u[:, CH + LANES * h:CH + LANES * (h + 1)]

    bg = u[:, CH + GDN_VAL_W:CH + GDN_VAL_W + LANES]
    beta = jax.nn.sigmoid(bg)
    g = -jnp.exp(alog_ref[...]) * jax.nn.softplus(bg + dtb_ref[...])
    t_in = (lax.broadcasted_iota(jnp.int32, (R, 1), 0) // s) % C
    k = 1
    while k < min(C, tt):
        gscr[Hg:Hg + R, :] = g
        g = g + jnp.where(t_in >= k, gscr[pl.ds(Hg - k * s, R), :], 0.0)
        k *= 2
    lane = lax.broadcasted_iota(jnp.int32, (R, LANES), 1)
    bg_ref[...] = jnp.where(lane < GDN_HV, beta, g)


def _gdn_pre(x3, halo, win, cw, alog, dtb, *, s, R, C):
    NB, TT, D = x3.shape
    H = halo.shape[1]
    tt = R // s
    Hg = max(SUBLANES, (min(C, tt) // 2) * s)
    kern = functools.partial(_gdn_pre_kernel, s=s, R=R, H=H, Hg=Hg, C=C)

    def ospec(nh):
        return pl.BlockSpec((None, nh, R, LANES), lambda i, c: (i, 0, c, 0))

    def oshape(nh):
        return jax.ShapeDtypeStruct((NB, nh, TT, LANES), F32)

    return pl.pallas_call(
        kern,
        grid=(NB, TT // R),
        in_specs=[
            pl.BlockSpec((None, R, D), lambda i, c: (i, c, 0)),
            pl.BlockSpec((None, H, GDN_CONV_CH), lambda i, c: (i, 0, 0)),
            _const_spec(win.shape), _const_spec(cw.shape), _const_spec(alog.shape), _const_spec(dtb.shape),
        ],
        out_specs=[ospec(GDN_HK), ospec(GDN_HK), ospec(GDN_HV), ospec(GDN_HV),
                   pl.BlockSpec((None, R, LANES), lambda i, c: (i, c, 0)),
                   pl.BlockSpec((None, H, GDN_CONV_CH), lambda i, c: (i, 0, 0))],
        out_shape=[oshape(GDN_HK), oshape(GDN_HK), oshape(GDN_HV), oshape(GDN_HV),
                   jax.ShapeDtypeStruct((NB, TT, LANES), F32),
                   jax.ShapeDtypeStruct((NB, H, GDN_CONV_CH), F32)],
        scratch_shapes=[pltpu.VMEM((H + R, GDN_CONV_CH), F32), pltpu.VMEM((Hg + R, LANES), F32)],
        compiler_params=_cparams(2),
        name="gdn_pre",
    )(x3, halo, win, cw, alog, dtb)


def _gdn_rec_kernel(q_ref, k_ref, v_ref, z_ref, bg_ref, x_ref, s0_ref, ng_ref, wout_ref, lng_ref, lnb_ref,
                    out_ref, sT_ref, sscr, oscr, *, C, batches):
    tb = pl.program_id(1)
    bb, nhv, Rt, _ = v_ref.shape
    rep = GDN_HV // GDN_HK

    @pl.when(tb == 0)
    def _init():
        sscr[...] = s0_ref[...].reshape(sscr.shape)

    strict, incl = _packed_masks(C)
    lo = lax.broadcasted_iota(jnp.int32, (C, 2 * C), 1) < C

    def part_a(items, ctx):
        CH = [(b, h, ci) for b, ci in items for h in range(nhv)]
        KH = [(b, m, ci) for b, ci in items for m in range(GDN_HK)]
        rs = lambda ci: pl.ds(ci * C, C)
        bgs = {(b, ci): bg_ref[b, rs(ci), :] for b, ci in items}
        bgT = {key: x.T for key, x in bgs.items()}

        kq = [jnp.concatenate([k_ref[b, m, rs(ci), :], q_ref[b, m, rs(ci), :]], axis=0) for b, m, ci in KH]
        sc = _each(lambda x: _dot(x, jnp.concatenate([x[:C, :], x[:C, :]], axis=0), _NT), kq)
        yield

        def col2(x, lane):
            return jnp.where(lo, jnp.broadcast_to(x[:, lane:lane + 1], (C, 2 * C)),
                             jnp.broadcast_to(x[:, lane + 1:lane + 2], (C, 2 * C)))

        bcol = [col2(bgs[b, ci], rep * m) for b, m, ci in KH]
        gcol = [col2(bgs[b, ci], GDN_HV + rep * m) for b, m, ci in KH]
        grow = [jnp.concatenate([bgT[b, ci][GDN_HV + rep * m:GDN_HV + rep * m + 1, :],
                                 bgT[b, ci][GDN_HV + rep * m + 1:GDN_HV + rep * m + 2, :]], axis=1) for b, m, ci in KH]
        diff = _each(lambda c_, r_: c_ - r_, gcol, grow)
        Lp = _each(lambda z, bc, d: jnp.where(strict, z[:C, :] * bc * jnp.exp(jnp.where(strict, d, 0.0)), 0.0),
                   sc, bcol, diff)
        Ap = _each(lambda z, d: jnp.where(incl, z[C:, :] * jnp.exp(jnp.where(incl, d, 0.0)), 0.0), sc, diff)
        Tinv_p = yield from _unit_lower_inverse([-l_ for l_ in Lp])

        half = lambda xs, n, h: xs[n // rep][:, (h % rep) * C:(h % rep + 1) * C]
        Tinv = [half(Tinv_p, n, h) for n, (b, h, ci) in enumerate(CH)]
        A = [half(Ap, n, h) for n, (b, h, ci) in enumerate(CH)]
        k = [kq[n // rep][:C, :] for n in range(len(CH))]
        q = [kq[n // rep][C:, :] for n in range(len(CH))]
        v = [v_ref[b, h, rs(ci), :] for b, h, ci in CH]
        beta = [jnp.broadcast_to(bgs[b, ci][:, h:h + 1], (C, LANES)) for b, h, ci in CH]
        gc = [jnp.broadcast_to(bgs[b, ci][:, GDN_HV + h:GDN_HV + h + 1], (C, LANES)) for b, h, ci in CH]
        kb = _each(lambda x, y: x * y, k, beta)
        eg = _each(jnp.exp, gc)
        UW = _each(lambda t, v_, b_, kb_, e: _dot(t, jnp.concatenate([v_ * b_, kb_ * e], axis=1)),
                   Tinv, v, beta, kb, eg)
        yield
        for n, key in enumerate(CH):
            g_last = gc[n][C - 1:C, :]
            ctx[key] = dict(U=UW[n][:, :GDN_DV], WQl=jnp.concatenate([UW[n][:, GDN_DV:], q[n] * eg[n]], axis=0),
                            A=A[n], kdec=k[n] * jnp.exp(g_last - gc[n]), sdec=jnp.exp(g_last))

    def part_b(items, ctx):
        for ci in sorted({ci for _, ci in items}):
            CH = [(b, h, ci) for b, c_ in items if c_ == ci for h in range(nhv)]
            X = [ctx.pop(key) for key in CH]
            S = [sscr[b * nhv + h] for b, h, _ in CH]
            WQ = _each(lambda x, s_: _dot(x["WQl"], s_), X, S)
            yield
            v_new = _each(lambda x, wq: x["U"] - wq[:C, :], X, WQ)
            o = _each(lambda x, wq, vn: wq[C:, :] + _dot(x["A"], vn), X, WQ, v_new)
            Snew = _each(lambda x, s_, vn: s_ * x["sdec"] + _dot(x["kdec"], vn, _TN), X, S, v_new)
            yield
            for n, (b, h, _) in enumerate(CH):
                sscr[b * nhv + h] = Snew[n]
                oh = o[n]
                oscr[b * Rt + ci * C:b * Rt + (ci + 1) * C, LANES * h:LANES * (h + 1)] = (
                    oh * lax.rsqrt(jnp.mean(oh * oh, axis=-1, keepdims=True) + GDN_EPS) * ng_ref[...])
            yield

    _run_pipelined(part_a, part_b, batches)

    @pl.when(tb == pl.num_programs(1) - 1)
    def _final_state():
        sT_ref[...] = sscr[...].reshape(sT_ref.shape)

    _gated_out_ln(oscr[...], z_ref, x_ref, wout_ref, lng_ref, lnb_ref, out_ref)


def _gdn_rec(q4, k4, v4, z4, bg, x, s0, norm_g, wout, lng, lnb, *, C, Rt, bb, gb):
    B, _, T, _ = v4.shape
    D = x.shape[-1]
    kern = functools.partial(_gdn_rec_kernel, C=C, batches=_rec_batches(bb, Rt // C, gb, GDN_CHUNKS_PER_BATCH))

    def tspec(nh):
        return pl.BlockSpec((bb, nh, Rt, LANES), lambda b, t: (b, 0, t, 0))

    xspec = pl.BlockSpec((bb, Rt, D), lambda b, t: (b, t, 0))
    sspec = pl.BlockSpec((bb, GDN_HV, GDN_DK, GDN_DV), lambda b, t: (b, 0, 0, 0))
    return pl.pallas_call(
        kern,
        grid=(B // bb, T // Rt),
        in_specs=[tspec(GDN_HK), tspec(GDN_HK), tspec(GDN_HV), tspec(GDN_HV),
                  pl.BlockSpec((bb, Rt, LANES), lambda b, t: (b, t, 0)), xspec, sspec,
                  _const_spec(norm_g.shape), _const_spec(wout.shape), _const_spec(lng.shape), _const_spec(lnb.shape)],
        out_specs=[xspec, sspec],
        out_shape=[jax.ShapeDtypeStruct(x.shape, F32), jax.ShapeDtypeStruct(s0.shape, F32)],
        scratch_shapes=[pltpu.VMEM((bb * GDN_HV, GDN_DK, GDN_DV), F32), pltpu.VMEM((bb * Rt, GDN_VAL_W), F32)],
        compiler_params=_cparams(2),
        name="gdn_rec",
    )(q4, k4, v4, z4, bg, x, s0, norm_g, wout, lng, lnb)


class _Group:
    def __init__(self, B, T, time_major):
        self.B, self.T, self.time_major = B, T, time_major
        if time_major:
            self.s, self.NB, self.TT, self.R = B, 1, T * B, T * B
        else:
            self.s, self.NB, self.TT, self.R = 1, B, T, min(ROW_BLOCK, T)

    def to_rows(self, x):
        if self.time_major:
            return jnp.swapaxes(x, 0, 1).reshape(1, self.TT, x.shape[-1])
        return x

    def from_rows(self, x3):
        if self.time_major:
            return jnp.swapaxes(x3.reshape(self.T, self.B, x3.shape[-1]), 0, 1)
        return x3

    def halo(self, st, n_steps):
        if self.time_major:
            return jnp.swapaxes(st, 0, 1).reshape(1, n_steps * self.B, st.shape[-1])
        H = _halo_rows(n_steps, 1)
        return jnp.pad(st, ((0, 0), (H - n_steps, 0), (0, 0)))

    def unhalo(self, tail, n_steps):
        if self.time_major:
            return jnp.swapaxes(tail.reshape(n_steps, self.B, tail.shape[-1]), 0, 1)
        return tail[:, tail.shape[1] - n_steps:]

    def vec(self, st):
        return st[None] if self.time_major else st[:, None, :]

    def unvec(self, v):
        return v[0] if self.time_major else v[:, 0, :]

    def heads_to_batch(self, a4, Tpad):
        if not self.time_major:
            return a4
        nh = a4.shape[1]
        a = a4.reshape(nh, self.T, self.B, LANES).transpose(2, 0, 1, 3)
        return jnp.pad(a, ((0, 0), (0, 0), (0, Tpad - self.T), (0, 0)))

    def rows_to_batch(self, a3, Tpad):
        if not self.time_major:
            return a3
        return jnp.pad(self.from_rows(a3), ((0, 0), (0, Tpad - self.T), (0, 0)))

    def rows_from_batch(self, a3):
        if not self.time_major:
            return a3
        return self.to_rows(a3[:, :self.T])

    def rec_tiling(self, chunk, block):
        if self.time_major:
            Tp = -(-self.T // SUBLANES) * SUBLANES
            return Tp, Tp, Tp, min(self.B, SAMPLE_SEQ_BLOCK), min(self.B, SAMPLE_SEQ_BLOCK)
        return min(chunk, self.T), self.T, min(block, self.T), PROMPT_SEQ_BLOCK if self.B % PROMPT_SEQ_BLOCK == 0 else 1, 1


def _lru_apply(g, x3, conv_st, h_st, p, lng, lnb):
    swap = not g.time_major
    assert g.B % SUBLANES == 0, "RG-LRU kernel needs the batch to fill whole sublane tiles"
    lay = _Group(g.B, g.T, True) if swap else g
    out, tail, hT = _lru_layer(x3, lay.halo(conv_st, CONV_W - 1), lay.vec(h_st), p["win"], p["cw"], p["cb"],
                               p["wg"], p["bg"], p["lam"], p["wout"], lng, lnb, s=lay.s,
                               R=min(LRU_BLOCK, g.T * g.B) if swap else g.R, swap=swap)
    return out, lay.unhalo(tail, CONV_W - 1), lay.unvec(hT)


def _rwkv_apply(g, x3, shift_st, wkv_st, p, lng, lnb):
    B = g.B
    pre = _rwkv_pre(x3, g.halo(shift_st[:, None, :], 1), p["mu"], p["win"], p["w0"], p["w1"], p["w2"], p["a0"],
                    p["a1"], p["a2"], s=g.s, R=g.R)
    new_shift = g.from_rows(x3)[:, -1]
    if g.time_major:
        r4, k4, v4, g4, lw4, a4 = pre
        s0 = jnp.transpose(wkv_st, (1, 2, 3, 0)).reshape(RW_H // 2, 2, RW_N, RW_N, B)
        col = lambda v: jnp.swapaxes(v, 1, 2)
        o4, sT = _rwkv_step(r4, k4, v4, lw4, a4, s0, col(p["k_k"]), col(p["k_a"]), col(p["r_k"]), col(p["gn_g"]),
                            col(p["gn_b"]), T=g.T, B=B)
        out = _post_layer(x3, o4, g4, p["wout"], lng, lnb)
        return out, new_shift, jnp.transpose(sT.reshape(RW_H, RW_N, RW_N, B), (3, 0, 1, 2))
    C, Tp, Rt, bb, gb = g.rec_tiling(RW_CHUNK, RW_BLOCK)
    r4, k4, v4, g4, lw4, a4 = [g.heads_to_batch(a, Tp) for a in pre]
    s0 = wkv_st.reshape(B, RW_H // 2, 2 * RW_N, RW_N)
    out, sT = _rwkv_rec(r4, k4, v4, lw4, a4, g4, g.rows_to_batch(x3, Tp), s0, p["k_k"], p["k_a"], p["r_k"],
                        p["gn_g"], p["gn_b"], p["wout"], lng, lnb, C=C, Rt=Rt, bb=bb, gb=gb)
    new_shift = g.from_rows(x3)[:, -1]
    return g.rows_from_batch(out), new_shift, sT.reshape(B, RW_H, RW_N, RW_N)


def _gdn_apply(g, x3, conv_st, S_st, p, lng, lnb):
    T = g.T
    C, Tp, Rt, bb, gb = g.rec_tiling(GDN_CHUNK, GDN_BLOCK)
    q4, k4, v4, z4, bg, tail = _gdn_pre(x3, g.halo(conv_st, CONV_W - 1), p["win"], p["cw"], p["alog"], p["dtb"],
                                        s=g.s, R=g.R, C=min(GDN_CHUNK, T))
    bgb = g.rows_to_batch(bg, T)
    if Tp != T:
        held = jnp.where(jnp.arange(LANES) < GDN_HV, 0.0, bgb[:, T - 1:T, :])
        bgb = jnp.concatenate([bgb, jnp.broadcast_to(held, (g.B, Tp - T, LANES))], axis=1)
    tb = lambda a: g.heads_to_batch(a, Tp)
    out, sT = _gdn_rec(tb(q4), tb(k4), tb(v4), tb(z4), bgb, g.rows_to_batch(x3, Tp), S_st, p["norm_g"], p["wout"],
                       lng, lnb, C=C, Rt=Rt, bb=bb, gb=gb)
    return g.rows_from_batch(out), g.unhalo(tail, CONV_W - 1), sT


def _trunk(g, x, st, params, ln_g, ln_b):
    lru_conv, lru_h, rw_shift, rw_S, gdn_conv, gdn_S = st
    new = ([], [], [], [], [], [])
    x3 = g.to_rows(x)
    ia = ib = ic = 0
    for layer in range(DEPTH):
        kind = layer % N_MIXERS
        lng, lnb = ln_g[layer][None, :], ln_b[layer][None, :]
        if kind == 0:
            x3, c, h = _lru_apply(g, x3, lru_conv[ia], lru_h[ia], params["lru"][ia], lng, lnb)
            new[0].append(c)
            new[1].append(h)
            ia += 1
        elif kind == 1:
            x3, sh, S = _rwkv_apply(g, x3, rw_shift[ib], rw_S[ib], params["rwkv"][ib], lng, lnb)
            new[2].append(sh)
            new[3].append(S)
            ib += 1
        else:
            x3, c, S = _gdn_apply(g, x3, gdn_conv[ic], gdn_S[ic], params["gdn"][ic], lng, lnb)
            new[4].append(c)
            new[5].append(S)
            ic += 1
    return g.from_rows(x3), tuple(s[0][None] if len(s) == 1 else jnp.stack(s) for s in new)


def _prep_params(lru_w_in, lru_conv_w, lru_conv_b, lru_wa, lru_ba, lru_wx, lru_bx, lru_lambda, lru_w_out, rw_mu,
                 rw_w_in, rw_w0, rw_w1, rw_w2, rw_a0, rw_a1, rw_a2, rw_k_k, rw_k_a, rw_r_k, rw_gn_g, rw_gn_b,
                 rw_w_out, gdn_w_in, gdn_conv_w, gdn_a_log, gdn_dt_bias, gdn_norm_g, gdn_w_out):
    row = lambda v: v[None, :]
    lru = []
    for n in range(lru_w_in.shape[0]):
        lru.append(dict(win=lru_w_in[n].astype(BF), cw=lru_conv_w[n][:, None, :], cb=row(lru_conv_b[n]),
                        wg=_lru_gate_weights(lru_wa[n], lru_wx[n]), bg=jnp.stack([lru_ba[n], lru_bx[n]])[:, None, :],
                        lam=row(lru_lambda[n]), wout=lru_w_out[n].astype(BF)))
    pairw = lambda v: v.reshape(RW_W // LANES, 1, LANES)
    rwkv = []
    for n in range(rw_w_in.shape[0]):
        rwkv.append(dict(mu=rw_mu[n][:, None, :],win=rw_w_in[n].astype(BF), w0=row(rw_w0[n]), w1=rw_w1[n].astype(BF),
                         w2=rw_w2[n].astype(BF), a0=row(rw_a0[n]), a1=rw_a1[n].astype(BF), a2=rw_a2[n].astype(BF),
                         k_k=pairw(rw_k_k[n]), k_a=pairw(rw_k_a[n]), r_k=pairw(rw_r_k[n]), gn_g=pairw(rw_gn_g[n]),
                         gn_b=pairw(rw_gn_b[n]), wout=rw_w_out[n].astype(BF)))
    gdn = []
    for n in range(gdn_w_in.shape[0]):
        w = gdn_w_in[n]
        o2 = GDN_CONV_CH + GDN_VAL_W
        wpad = jnp.pad(w[:, o2:], ((0, 0), (0, LANES - 2 * GDN_HV)))
        lanes = lambda v: jnp.pad(v, (GDN_HV, LANES - 2 * GDN_HV))[None, :]
        gdn.append(dict(win=jnp.concatenate([w[:, :o2], wpad], axis=1).astype(BF), cw=gdn_conv_w[n][:, None, :],
                        alog=lanes(gdn_a_log[n]), dtb=lanes(gdn_dt_bias[n]), norm_g=row(gdn_norm_g[n]),
                        wout=gdn_w_out[n].astype(BF)))
    return dict(lru=lru, rwkv=rwkv, gdn=gdn)


def kernel(x_prompt, x_sample, state_lru_conv, state_lru_h, state_rwkv_shift, state_rwkv_wkv, state_gdn_conv, state_gdn_S, ln_g, ln_b, lru_w_in, lru_conv_w, lru_conv_b, lru_wa, lru_ba, lru_wx, lru_bx, lru_lambda, lru_w_out, rw_mu, rw_w_in, rw_w0, rw_w1, rw_w2, rw_a0, rw_a1, rw_a2, rw_k_k, rw_k_a, rw_r_k, rw_gn_g, rw_gn_b, rw_w_out, gdn_w_in, gdn_conv_w, gdn_a_log, gdn_dt_bias, gdn_norm_g, gdn_w_out):
    params = _prep_params(lru_w_in, lru_conv_w, lru_conv_b, lru_wa, lru_ba, lru_wx, lru_bx, lru_lambda, lru_w_out,
                          rw_mu, rw_w_in, rw_w0, rw_w1, rw_w2, rw_a0, rw_a1, rw_a2, rw_k_k, rw_k_a, rw_r_k, rw_gn_g,
                          rw_gn_b, rw_w_out, gdn_w_in, gdn_conv_w, gdn_a_log, gdn_dt_bias, gdn_norm_g, gdn_w_out)
    bp, tp, _ = x_prompt.shape
    bs, ts, _ = x_sample.shape
    n_a, n_b, n_c = state_lru_conv.shape[0], state_rwkv_shift.shape[0], state_gdn_conv.shape[0]
    zero_state = (jnp.zeros((n_a, bp, CONV_W - 1, LRU_W), F32),
                  jnp.zeros((n_a, bp, LRU_W), F32),
                  jnp.zeros((n_b, bp, D_MODEL), F32),
                  jnp.zeros((n_b, bp, RW_H, RW_N, RW_N), F32),
                  jnp.zeros((n_c, bp, CONV_W - 1, GDN_CONV_CH), F32),
                  jnp.zeros((n_c, bp, GDN_HV, GDN_DK, GDN_DV), F32))
    y_prompt, sp = _trunk(_Group(bp, tp, False), x_prompt, zero_state, params, ln_g, ln_b)
    y_sample, ss = _trunk(_Group(bs, ts, True), x_sample,
                          (state_lru_conv, state_lru_h, state_rwkv_shift, state_rwkv_wkv, state_gdn_conv,
                           state_gdn_S), params, ln_g, ln_b)
    return (y_prompt, y_sample, sp[0], ss[0], sp[1], ss[1], sp[2], ss[2], sp[3], ss[3], sp[4], ss[4], sp[5], ss[5])
```

```python
import functools
import math

import jax
import jax.numpy as jnp
from jax import lax
from jax.experimental import pallas as pl
from jax.experimental.pallas import tpu as pltpu

F32 = jnp.float32
BF = jnp.bfloat16

D_MODEL = 1024
DEPTH = 4
N_MIXERS = 3
DN_ALPHA = (2.0 * DEPTH) ** 0.25
LN_EPS = 1e-5
CONV_W = 4

LRU_W = D_MODEL
LRU_BLOCKS = 16
LRU_BS = LRU_W // LRU_BLOCKS
LRU_C = 8.0

RW_W = D_MODEL
RW_N = 64
RW_H = RW_W // RW_N
RW_GN_EPS = 64e-5
RW_NORM_EPS = 1e-12

GDN_HK = 4
GDN_HV = 8
GDN_DK = 128
GDN_DV = 128
GDN_KEY_W = GDN_HK * GDN_DK
GDN_VAL_W = GDN_HV * GDN_DV
GDN_CONV_CH = 2 * GDN_KEY_W + GDN_VAL_W
GDN_CHUNK = 64
GDN_EPS = 1e-6

LANES = 128
SUBLANES = 8
MXU_WIDTH = 256
VMEM_LIMIT = 56 * 1024 * 1024
ROW_BLOCK = 512
LRU_BLOCK = 1024
RW_BLOCK = 256
GDN_BLOCK = 256
RW_CHUNKS_PER_BATCH = 1
GDN_CHUNKS_PER_BATCH = 2
PROMPT_SEQ_BLOCK = 2
SAMPLE_SEQ_BLOCK = 8
RW_CHUNK = 64

_NN = (((1,), (0,)), ((), ()))
_NT = (((1,), (1,)), ((), ()))
_TN = (((0,), (0,)), ((), ()))


def _mm(a, b):
    return jnp.dot(a.astype(BF), b.astype(BF), preferred_element_type=F32)


def _dot(a, b, dims=_NN):
    return lax.dot_general(a.astype(BF), b.astype(BF), dims, preferred_element_type=F32)


def _cumsum_rows(x, tri3):
    b1 = x.astype(BF)
    r1 = x - b1.astype(F32)
    b2 = r1.astype(BF)
    b3 = (r1 - b2.astype(F32)).astype(BF)
    return jnp.dot(tri3, jnp.concatenate([b1, b2, b3], axis=0), preferred_element_type=F32)


def _layer_norm(z, g, b):
    mu = jnp.mean(z, axis=-1, keepdims=True)
    zc = z - mu
    var = jnp.mean(zc * zc, axis=-1, keepdims=True)
    return zc * lax.rsqrt(var + LN_EPS) * g + b


def _silu(x):
    return x * jax.nn.sigmoid(x)


def _cparams(n_axes):
    return pltpu.CompilerParams(dimension_semantics=("arbitrary",) * n_axes, vmem_limit_bytes=VMEM_LIMIT)


def _const_spec(shape):
    nd = len(shape)
    return pl.BlockSpec(shape, lambda *_: (0,) * nd, pipeline_mode=pl.Buffered(1))


def _halo_rows(n_steps, s):
    rows = n_steps * s
    return rows if rows % SUBLANES == 0 else SUBLANES


def _run_skewed(gens):
    pending, active = list(gens), []
    while pending or active:
        if pending:
            active.append(pending.pop(0))
        for g in list(active):
            try:
                next(g)
            except StopIteration:
                active.remove(g)


def _lru_kernel(x_ref, halo_ref, h0_ref, win_ref, cw_ref, cb_ref, wg_ref, bg_ref, lam_ref, wout_ref, lng_ref,
                lnb_ref, out_ref, tail_ref, hT_ref, xscr, hscr, *, s, R, H, swap):
    c = pl.program_id(1)
    C = LRU_W
    tt = R // s

    @pl.when(c == 0)
    def _init():
        xscr[0:H, :] = halo_ref[...]
        hscr[...] = h0_ref[...]

    x = x_ref[...]
    if swap:
        x = jnp.swapaxes(x, 0, 1).reshape(R, D_MODEL)
    xb16 = x.astype(BF)
    G = MXU_WIDTH
    ysum = []

    def lane_group(g):
        ls = pl.ds(G * g, G)
        xb = jnp.dot(xb16, win_ref[:, ls], preferred_element_type=F32)
        gate = jnp.dot(xb16, win_ref[:, pl.ds(C + G * g, G)], preferred_element_type=F32)
        xscr[H:H + R, ls] = xb
        yield

        xc = xb * cw_ref[3, :, ls] + cb_ref[:, ls]
        for d in (1, 2, 3):
            xc = xc + xscr[pl.ds(H - d * s, R), ls] * cw_ref[3 - d, :, ls]
        gt = jnp.dot(xc.astype(BF), wg_ref[g], preferred_element_type=F32)
        r = jax.nn.sigmoid(gt[:, :G] + bg_ref[0, :, ls])
        i = jax.nn.sigmoid(gt[:, G:] + bg_ref[1, :, ls])
        log_a = r * ((-LRU_C) * jax.nn.softplus(-lam_ref[:, ls]))
        a = jnp.exp(log_a)
        b = jnp.sqrt(1.0 - a * a) * i * xc
        hlast = hscr[:, ls]
        hs = []
        for t in range(tt):
            hlast = a[t * s:(t + 1) * s, :] * hlast + b[t * s:(t + 1) * s, :]
            hs.append(hlast)
        hscr[:, ls] = hlast
        hg = jnp.concatenate(hs, axis=0) * _silu(gate)
        yield

        ysum.append(jnp.dot(hg.astype(BF), wout_ref[ls, :], preferred_element_type=F32))
        yield

    _run_skewed([lane_group(g) for g in range(C // G)])
    tail = xscr[pl.ds(R, H), :]
    xscr[0:H, :] = tail
    tail_ref[...] = tail
    hT_ref[...] = hscr[...]
    y = functools.reduce(lambda p, q: p + q, ysum)
    out = _layer_norm(DN_ALPHA * x + y, lng_ref[...], lnb_ref[...])
    if swap:
        out = jnp.swapaxes(out.reshape(tt, s, D_MODEL), 0, 1)
    out_ref[...] = out


def _lru_layer(x3, halo, h0, win, cw, cb, wg, bg, lam, wout, lng, lnb, *, s, R, swap):
    C = LRU_W
    D = x3.shape[-1]
    H = halo.shape[1]
    tt = R // s
    assert s % SUBLANES == 0
    if swap:
        NB, TT = 1, x3.shape[1] * s
        xspec = pl.BlockSpec((s, tt, D), lambda i, c: (0, c, 0))
    else:
        NB, TT = x3.shape[:2]
        xspec = pl.BlockSpec((None, R, D), lambda i, c: (i, c, 0))
    kern = functools.partial(_lru_kernel, s=s, R=R, H=H, swap=swap)
    return pl.pallas_call(
        kern,
        grid=(NB, TT // R),
        in_specs=[
            xspec,
            pl.BlockSpec((None, H, C), lambda i, c: (i, 0, 0)),
            pl.BlockSpec((None, s, C), lambda i, c: (i, 0, 0)),
            _const_spec(win.shape), _const_spec(cw.shape), _const_spec(cb.shape), _const_spec(wg.shape),
            _const_spec(bg.shape), _const_spec(lam.shape), _const_spec(wout.shape), _const_spec(lng.shape),
            _const_spec(lnb.shape),
        ],
        out_specs=[
            xspec,
            pl.BlockSpec((None, H, C), lambda i, c: (i, 0, 0)),
            pl.BlockSpec((None, s, C), lambda i, c: (i, 0, 0)),
        ],
        out_shape=[
            jax.ShapeDtypeStruct(x3.shape, F32),
            jax.ShapeDtypeStruct((NB, H, C), F32),
            jax.ShapeDtypeStruct((NB, s, C), F32),
        ],
        scratch_shapes=[
            pltpu.VMEM((H + R, C), F32),
            pltpu.VMEM((s, C), F32),
        ],
        compiler_params=_cparams(2),
        name="lru_layer",
    )(x3, halo, h0, win, cw, cb, wg, bg, lam, wout, lng, lnb)


def _lru_gate_weights(wa, wx):
    G = MXU_WIDTH
    nb = G // LRU_BS

    def bd(w):
        w4 = w.reshape(LRU_BLOCKS // nb, nb, LRU_BS, LRU_BS)
        eye = jnp.eye(nb, dtype=w.dtype)
        return jnp.einsum("gaij,ab->gaibj", w4, eye).reshape(LRU_BLOCKS // nb, G, G)
    return jnp.concatenate([bd(wa), bd(wx)], axis=2).astype(BF)


def _gated_out_ln(o, g_ref, x_ref, w_ref, lng_ref, lnb_ref, out_ref):
    bb, nh, Rt, _ = g_ref.shape
    rows = bb * Rt
    g = jnp.concatenate([g_ref[:, p, :, :].reshape(rows, LANES) for p in range(nh)], axis=1)
    y = _mm(o * _silu(g), w_ref[...])
    x = x_ref[...].reshape(rows, D_MODEL)
    out_ref[...] = _layer_norm(DN_ALPHA * x + y, lng_ref[...], lnb_ref[...]).reshape(bb, Rt, D_MODEL)


def _post_kernel(x_ref, o_ref, g_ref, w_ref, lng_ref, lnb_ref, out_ref):
    o = jnp.concatenate([o_ref[0, p] for p in range(o_ref.shape[1])], axis=1)
    _gated_out_ln(o, g_ref, x_ref, w_ref, lng_ref, lnb_ref, out_ref)


def _post_layer(x3, o4, g4, w, lng, lnb):
    return pl.pallas_call(
        _post_kernel,
        out_shape=jax.ShapeDtypeStruct(x3.shape, F32),
        compiler_params=pltpu.CompilerParams(vmem_limit_bytes=VMEM_LIMIT),
        name="post_layer",
    )(x3, o4, g4, w, lng, lnb)


def _rwkv_pre_kernel(x_ref, halo_ref, mu_ref, win_ref, w0_ref, w1_ref, w2_ref, a0_ref, a1_ref, a2_ref,
                     r_ref, k_ref, v_ref, g_ref, lw_ref, a_ref, xscr, xxscr, *, s, R, H):
    c = pl.program_id(1)

    @pl.when(c == 0)
    def _init():
        xscr[0:H, :] = halo_ref[...]

    x = x_ref[...]
    xscr[H:H + R, :] = x
    xxscr[...] = xscr[pl.ds(H - s, R), :] - x
    xscr[0:H, :] = xscr[pl.ds(R, H), :]
    xx = xxscr[...]
    xm = lambda n: x + xx * mu_ref[n]

    def put(ref, val):
        for p in range(RW_W // LANES):
            ref[p] = val[:, LANES * p:LANES * (p + 1)]

    put(r_ref, _mm(xm(0), win_ref[0]))
    put(k_ref, _mm(xm(1), win_ref[1]))
    put(v_ref, _mm(xm(2), win_ref[2]))
    put(g_ref, _mm(xm(3), win_ref[3]))
    w_raw = w0_ref[...] + _mm(jnp.tanh(_mm(xm(4), w1_ref[...])), w2_ref[...])
    put(lw_ref, (-math.exp(-0.5)) * jax.nn.sigmoid(w_raw))
    put(a_ref, jax.nn.sigmoid(a0_ref[...] + _mm(_mm(xm(5), a1_ref[...]), a2_ref[...])))


def _rwkv_pre(x3, halo, mu, win, w0, w1, w2, a0, a1, a2, *, s, R):
    NB, TT, D = x3.shape
    H = halo.shape[1]
    nh = RW_W // LANES
    kern = functools.partial(_rwkv_pre_kernel, s=s, R=R, H=H)
    ospec = pl.BlockSpec((None, nh, R, LANES), lambda i, c: (i, 0, c, 0))
    oshape = jax.ShapeDtypeStruct((NB, nh, TT, LANES), F32)
    return pl.pallas_call(
        kern,
        grid=(NB, TT // R),
        in_specs=[
            pl.BlockSpec((None, R, D), lambda i, c: (i, c, 0)),
            pl.BlockSpec((None, H, D), lambda i, c: (i, 0, 0)),
            _const_spec(mu.shape), _const_spec(win.shape), _const_spec(w0.shape), _const_spec(w1.shape),
            _const_spec(w2.shape), _const_spec(a0.shape), _const_spec(a1.shape), _const_spec(a2.shape),
        ],
        out_specs=[ospec] * 6,
        out_shape=[oshape] * 6,
        scratch_shapes=[pltpu.VMEM((H + R, D), F32), pltpu.VMEM((R, D), F32)],
        compiler_params=_cparams(2),
        name="rwkv_pre",
    )(x3, halo, mu, win, w0, w1, w2, a0, a1, a2)


def _seg_sum(x):
    lane = lax.broadcasted_iota(jnp.int32, x.shape, 1)
    lo = lane < RW_N
    s0 = jnp.sum(jnp.where(lo, x, 0.0), axis=-1, keepdims=True)
    s1 = jnp.sum(jnp.where(lo, 0.0, x), axis=-1, keepdims=True)
    return jnp.where(lo, s0, s1)


def _stack2(x):
    lane = lax.broadcasted_iota(jnp.int32, x.shape, 1)
    lo = lane < x.shape[1] // 2
    return jnp.concatenate([jnp.where(lo, x, 0.0), jnp.where(lo, 0.0, x)], axis=0)


def _each(f, *lists):
    return [f(*t) for t in zip(*lists)]


def _packed_masks(C):
    t = lax.broadcasted_iota(jnp.int32, (C, 2 * C), 0)
    s = lax.broadcasted_iota(jnp.int32, (C, 2 * C), 1) % C
    return s < t, s <= t


def _stackn(x, n):
    grp = lax.broadcasted_iota(jnp.int32, x.shape, 1) // (x.shape[1] // n)
    return jnp.concatenate([jnp.where(grp == k, x, 0.0) for k in range(n)], axis=0)


def _neumann_inverse(Ps, nblk):
    m = Ps[0].shape[0]
    eye = (lax.broadcasted_iota(jnp.int32, Ps[0].shape, 1) % m) == lax.broadcasted_iota(jnp.int32, Ps[0].shape, 0)
    invs = [jnp.where(eye, 1.0, P) for P in Ps]
    pw = Ps
    span = 2
    while span < m:
        pw = _each(lambda p: _dot(p, _stackn(p, nblk)), pw)
        yield
        invs = _each(lambda inv, p: inv + _dot(inv, _stackn(p, nblk)), invs, pw)
        yield
        span *= 2
    return invs


def _unit_lower_inverse(Ls):
    C = Ls[0].shape[0]
    h = C // 2
    if h % SUBLANES:
        return (yield from _neumann_inverse(Ls, 2))
    grp = lax.broadcasted_iota(jnp.int32, (h, 2 * C), 1) // h
    diag0 = (grp % 2) == 0
    zero = jnp.zeros((h, 2 * C), F32)
    D = yield from _neumann_inverse([jnp.where(diag0, L[:h, :], L[h:, :]) for L in Ls], 4)
    L21 = [jnp.where(diag0, L[h:, :], 0.0) for L in Ls]
    M1 = _each(lambda d, q: _dot(d, jnp.concatenate(
        [zero, jnp.where(grp == 0, q, 0.0), zero, jnp.where(grp == 2, q, 0.0)], axis=0)), D, L21)
    yield
    X21 = _each(lambda m1, d: _dot(m1, jnp.concatenate(
        [jnp.where(grp == 0, d, 0.0), zero, jnp.where(grp == 2, d, 0.0), zero], axis=0)), M1, D)
    yield
    return _each(lambda d, x21: jnp.concatenate([jnp.where(diag0, d, 0.0), x21 + jnp.where(diag0, 0.0, d)], axis=0),
                 D, X21)


def _run_pipelined(part_a, part_b, batches):
    ctx = {}
    prev = None
    for batch in batches:
        gens = [part_a(batch, ctx)] + ([part_b(prev, ctx)] if prev is not None else [])
        while gens:
            for g in list(gens):
                try:
                    next(g)
                except StopIteration:
                    gens.remove(g)
        prev = batch
    for _ in part_b(prev, ctx):
        pass


def _rec_batches(bb, nchunk, gb, cpb):
    if nchunk > 1:
        return tuple(tuple((b, c) for b in range(bb) for c in range(c0, min(c0 + cpb, nchunk)))
                     for c0 in range(0, nchunk, cpb))
    return tuple(tuple((b, 0) for b in range(b0, b0 + gb)) for b0 in range(0, bb, gb))


def _rwkv_rec_kernel(r_ref, k_ref, v_ref, lw_ref, a_ref, g_ref, x_ref, s0_ref, kk_ref, ka_ref, rk_ref, gg_ref,
                     gb_ref, wout_ref, lng_ref, lnb_ref, out_ref, sT_ref, sscr, oscr, *, C, batches):
    tb = pl.program_id(1)
    bb, npair, Rt, _ = r_ref.shape
    C2 = 2 * C
    lane_sq = lax.broadcasted_iota(jnp.int32, (LANES, LANES), 1)
    row_sq = lax.broadcasted_iota(jnp.int32, (LANES, LANES), 0)
    same_head = (lane_sq < RW_N) == (row_sq < RW_N)

    @pl.when(tb == 0)
    def _init():
        def init_b(b, carry):
            for p in range(npair):
                s2 = s0_ref[b, p]
                sscr[b * npair + p] = jnp.where(same_head, jnp.concatenate([s2, s2], axis=1), 0.0)
            return carry
        lax.fori_loop(0, bb, init_b, 0)

    strict, incl = _packed_masks(C)
    tri3 = ((lax.broadcasted_iota(jnp.int32, (C, 3 * C), 1) % C)
            <= lax.broadcasted_iota(jnp.int32, (C, 3 * C), 0)).astype(BF)

    def part_a(items, ctx):
        CH = [(b, p, ci) for b, ci in items for p in range(npair)]
        rs = lambda ci: pl.ds(ci * C, C)
        r = [r_ref[b, p, rs(ci), :] for b, p, ci in CH]
        k = [k_ref[b, p, rs(ci), :] for b, p, ci in CH]
        v = [v_ref[b, p, rs(ci), :] for b, p, ci in CH]
        lw = [lw_ref[b, p, rs(ci), :] for b, p, ci in CH]
        a = [a_ref[b, p, rs(ci), :] for b, p, ci in CH]
        kn = [k_ * kk_ref[p] for k_, (b, p, ci) in zip(k, CH)]
        kk = _each(lambda z: z * lax.rsqrt(_seg_sum(z * z) + RW_NORM_EPS), kn)
        kh = [k_ * (1.0 + (a_ - 1.0) * ka_ref[p]) for k_, a_, (b, p, ci) in zip(k, a, CH)]
        bvec = _each(lambda x, y: x * y, kk, a)
        cum = _each(lambda x: _cumsum_rows(x, tri3), lw)
        yield
        cum_last = [c_[C - 1:C, :] for c_ in cum]
        e_neg = _each(lambda c_: jnp.exp(-c_), cum)
        e_dec = _each(lambda cl, c_: jnp.exp(cl - c_), cum_last, cum)
        At = _each(lambda kk_, c_, lw_: -kk_ * jnp.exp(c_ - lw_), kk, cum, lw)
        Rt_ = _each(lambda r_, c_: r_ * jnp.exp(c_), r, cum)
        AR = _each(lambda x, y: jnp.concatenate([x, y], axis=0), At, Rt_)
        BKs = _each(lambda b_, kh_, e: jnp.concatenate([_stack2(b_ * e), _stack2(kh_ * e)], axis=0), bvec, kh, e_neg)
        V2 = _each(_stack2, v)
        sc = _each(lambda x, y: _dot(x, y, _NT), AR, BKs)
        yield
        Lab = [jnp.where(strict, z[:C, :C2], 0.0) for z in sc]
        Lak = [jnp.where(strict, z[:C, C2:], 0.0) for z in sc]
        Mrbk = [jnp.concatenate([jnp.where(incl, z[C:, :C2], 0.0), jnp.where(incl, z[C:, C2:], 0.0)], axis=1)
                for z in sc]
        LV = _each(_dot, Lak, V2)
        Tinv = yield from _unit_lower_inverse(Lab)
        for n, key in enumerate(CH):
            ctx[key] = dict(AR=AR[n], V2=V2[n], LV=LV[n], Tinv=Tinv[n], Mrbk=Mrbk[n], v=v[n], r=r[n], kh=kh[n],
                            sdec=jnp.exp(cum_last[n]), bkdec=jnp.concatenate([bvec[n] * e_dec[n], kh[n] * e_dec[n]], 0))

    def part_b(items, ctx):
        for ci in sorted({ci for _, ci in items}):
            CH = [(b, p, ci) for b, c_ in items if c_ == ci for p in range(npair)]
            X = [ctx.pop(key) for key in CH]
            S = [sscr[b * npair + p] for b, p, _ in CH]
            UY0 = _each(lambda x, s_: _dot(x["AR"], s_, _NT), X, S)
            yield
            U = _each(lambda x, u: _dot(x["Tinv"], _stack2(u[:C, :] + x["LV"])), X, UY0)
            yield
            y = _each(lambda x, u0, u: u0[C:, :] + _dot(x["Mrbk"], jnp.concatenate([_stack2(u), x["V2"]], axis=0)),
                      X, UY0, U)
            Snew = _each(lambda x, s_, u: s_ * x["sdec"] + jnp.where(
                same_head, _dot(jnp.concatenate([u, x["v"]], axis=0), x["bkdec"], _TN), 0.0), X, S, U)
            yield
            for n, (b, p, _) in enumerate(CH):
                sscr[b * npair + p] = Snew[n]
                m = _seg_sum(y[n]) * (1.0 / RW_N)
                yc = y[n] - m
                var = _seg_sum(yc * yc) * (1.0 / RW_N)
                yn = yc * lax.rsqrt(var + RW_GN_EPS) * gg_ref[p] + gb_ref[p]
                bonus = _seg_sum(X[n]["r"] * X[n]["kh"] * rk_ref[p]) * X[n]["v"]
                oscr[b * Rt + ci * C:b * Rt + (ci + 1) * C, LANES * p:LANES * (p + 1)] = yn + bonus
            yield

    _run_pipelined(part_a, part_b, batches)

    row_h = lax.broadcasted_iota(jnp.int32, (LANES, RW_N), 0) < RW_N

    @pl.when(tb == pl.num_programs(1) - 1)
    def _final_state():
        def fin_b(b, carry):
            for p in range(npair):
                S = sscr[b * npair + p]
                sT_ref[b, p] = jnp.where(row_h, S[:, :RW_N], S[:, RW_N:])
            return carry
        lax.fori_loop(0, bb, fin_b, 0)

    _gated_out_ln(oscr[...], g_ref, x_ref, wout_ref, lng_ref, lnb_ref, out_ref)


def _rwkv_rec(r4, k4, v4, lw4, a4, g4, x, s0, k_k, k_a, r_k, gn_g, gn_b, wout, lng, lnb, *, C, Rt, bb, gb):
    B, npair, T, _ = r4.shape
    D = x.shape[-1]
    kern = functools.partial(_rwkv_rec_kernel, C=C, batches=_rec_batches(bb, Rt // C, gb, RW_CHUNKS_PER_BATCH))
    tspec = pl.BlockSpec((bb, npair, Rt, LANES), lambda b, t: (b, 0, t, 0))
    xspec = pl.BlockSpec((bb, Rt, D), lambda b, t: (b, t, 0))
    sspec = pl.BlockSpec((bb, npair, LANES, RW_N), lambda b, t: (b, 0, 0, 0))
    wspec = _const_spec((npair, 1, LANES))
    return pl.pallas_call(
        kern,
        grid=(B // bb, T // Rt),
        in_specs=[tspec] * 6 + [xspec, sspec] + [wspec] * 5 + [_const_spec(wout.shape), _const_spec(lng.shape),
                                                              _const_spec(lnb.shape)],
        out_specs=[xspec, sspec],
        out_shape=[jax.ShapeDtypeStruct(x.shape, F32), jax.ShapeDtypeStruct(s0.shape, F32)],
        scratch_shapes=[pltpu.VMEM((bb * npair, LANES, LANES), F32), pltpu.VMEM((bb * Rt, RW_W), F32)],
        compiler_params=_cparams(2),
        name="rwkv_rec",
    )(r4, k4, v4, lw4, a4, g4, x, s0, k_k, k_a, r_k, gn_g, gn_b, wout, lng, lnb)


def _rwkv_step_kernel(r_ref, k_ref, v_ref, lw_ref, a_ref, s0_ref, kk_ref, ka_ref, rk_ref, gg_ref, gb_ref,
                      o_ref, sT_ref, vscr, yscr, *, T, B):
    N = RW_N
    IB = SUBLANES
    for t in range(T):
        rows = pl.ds(t * B, B)
        rT, kT, vT, aT = r_ref[rows, :].T, k_ref[rows, :].T, v_ref[rows, :].T, a_ref[rows, :].T
        w = jnp.exp(lw_ref[rows, :].T)
        kn = kT * kk_ref[...]
        kh = kT * (1.0 + (aT - 1.0) * ka_ref[...])
        vscr[...] = vT
        src = s0_ref if t == 0 else sT_ref
        bonus = []
        for h in range(2):
            hs = slice(N * h, N * (h + 1))
            kk = kn[hs] * lax.rsqrt(jnp.sum(kn[hs] * kn[hs], axis=0, keepdims=True) + RW_NORM_EPS)
            a_h, b_h, k_h, w_h, r_h = -kk, kk * aT[hs], kh[hs], w[hs], rT[hs]

            def step(ib, carry, h=h, a_h=a_h, b_h=b_h, k_h=k_h, w_h=w_h, r_h=r_h, src=src):
                i0 = pl.multiple_of(ib * IB, IB)
                S = src[h, pl.ds(i0, IB), :, :]
                sa = jnp.sum(S * a_h[None], axis=1)
                vb = vscr[pl.ds(N * h + i0, IB), :]
                Sn = S * w_h[None] + sa[:, None, :] * b_h[None] + vb[:, None, :] * k_h[None]
                sT_ref[h, pl.ds(i0, IB), :, :] = Sn
                yscr[pl.ds(N * h + i0, IB), :] = jnp.sum(Sn * r_h[None], axis=1)
                return carry

            lax.fori_loop(0, N // IB, step, 0)
            bonus.append(jnp.sum(r_h * k_h * rk_ref[hs, :], axis=0, keepdims=True) * vT[hs])
        y = yscr[...]
        outs = []
        for h in range(2):
            hs = slice(N * h, N * (h + 1))
            m = jnp.mean(y[hs], axis=0, keepdims=True)
            yc = y[hs] - m
            var = jnp.mean(yc * yc, axis=0, keepdims=True)
            outs.append(yc * lax.rsqrt(var + RW_GN_EPS) * gg_ref[hs, :] + gb_ref[hs, :] + bonus[h])
        o_ref[rows, :] = jnp.concatenate(outs, axis=0).T


def _rwkv_step(r4, k4, v4, lw4, a4, s0, k_k, k_a, r_k, gn_g, gn_b, *, T, B):
    npair = r4.shape[1]
    kern = functools.partial(_rwkv_step_kernel, T=T, B=B)
    tspec = pl.BlockSpec((None, None, T * B, LANES), lambda p: (0, p, 0, 0))
    sspec = pl.BlockSpec((None, 2, RW_N, RW_N, B), lambda p: (p, 0, 0, 0, 0))
    wspec = pl.BlockSpec((None, LANES, 1), lambda p: (p, 0, 0))
    return pl.pallas_call(
        kern,
        grid=(npair,),
        in_specs=[tspec] * 5 + [sspec] + [wspec] * 5,
        out_specs=[tspec, sspec],
        out_shape=[jax.ShapeDtypeStruct(r4.shape, F32), jax.ShapeDtypeStruct(s0.shape, F32)],
        scratch_shapes=[pltpu.VMEM((LANES, B), F32), pltpu.VMEM((LANES, B), F32)],
        compiler_params=_cparams(1),
        name="rwkv_step",
    )(r4, k4, v4, lw4, a4, s0, k_k, k_a, r_k, gn_g, gn_b)


def _gdn_pre_kernel(x_ref, halo_ref, win_ref, cw_ref, alog_ref, dtb_ref,
                    q_ref, k_ref, v_ref, z_ref, bg_ref, tail_ref, xscr, gscr, *, s, R, H, Hg, C):
    c = pl.program_id(1)
    CH = GDN_CONV_CH
    tt = R // s

    @pl.when(c == 0)
    def _init():
        xscr[0:H, :] = halo_ref[...]
        gscr[0:Hg, :] = jnp.zeros((Hg, LANES), F32)

    x = x_ref[...]
    u = _mm(x, win_ref[...])
    xb = u[:, :CH]
    xscr[H:H + R, :] = xb
    y = xb * cw_ref[3]
    for d in (1, 2, 3):
        y = y + xscr[pl.ds(H - d * s, R), :] * cw_ref[3 - d]
    tail = xscr[pl.ds(R, H), :]
    xscr[0:H, :] = tail
    tail_ref[...] = tail
    qkv = _silu(y)

    def l2n(z, scale):
        return z * (lax.rsqrt(jnp.sum(z * z, axis=-1, keepdims=True) + GDN_EPS) * scale)

    for h in range(GDN_HK):
        q_ref[h] = l2n(qkv[:, LANES * h:LANES * (h + 1)], GDN_DK ** -0.5)
        k_ref[h] = l2n(qkv[:, GDN_KEY_W + LANES * h:GDN_KEY_W + LANES * (h + 1)], 1.0)
    for h in range(GDN_HV):
        v_ref[h] = qkv[:, 2 * GDN_KEY_W + LANES * h:2 * GDN_KEY_W + LANES * (h + 1)]
        z_ref[h] = u[:, CH + LANES * h:CH + LANES * (h + 1)]

    bg = u[:, CH + GDN_VAL_W:CH + GDN_VAL_W + LANES]
    beta = jax.nn.sigmoid(bg)
    g = -jnp.exp(alog_ref[...]) * jax.nn.softplus(bg + dtb_ref[...])
    t_in = (lax.broadcasted_iota(jnp.int32, (R, 1), 0) // s) % C
    k = 1
    while k < min(C, tt):
        gscr[Hg:Hg + R, :] = g
        g = g + jnp.where(t_in >= k, gscr[pl.ds(Hg - k * s, R), :], 0.0)
        k *= 2
    lane = lax.broadcasted_iota(jnp.int32, (R, LANES), 1)
    bg_ref[...] = jnp.where(lane < GDN_HV, beta, g)


def _gdn_pre(x3, halo, win, cw, alog, dtb, *, s, R, C):
    NB, TT, D = x3.shape
    H = halo.shape[1]
    tt = R // s
    Hg = max(SUBLANES, (min(C, tt) // 2) * s)
    kern = functools.partial(_gdn_pre_kernel, s=s, R=R, H=H, Hg=Hg, C=C)

    def ospec(nh):
        return pl.BlockSpec((None, nh, R, LANES), lambda i, c: (i, 0, c, 0))

    def oshape(nh):
        return jax.ShapeDtypeStruct((NB, nh, TT, LANES), F32)

    return pl.pallas_call(
        kern,
        grid=(NB, TT // R),
        in_specs=[
            pl.BlockSpec((None, R, D), lambda i, c: (i, c, 0)),
            pl.BlockSpec((None, H, GDN_CONV_CH), lambda i, c: (i, 0, 0)),
            _const_spec(win.shape), _const_spec(cw.shape), _const_spec(alog.shape), _const_spec(dtb.shape),
        ],
        out_specs=[ospec(GDN_HK), ospec(GDN_HK), ospec(GDN_HV), ospec(GDN_HV),
                   pl.BlockSpec((None, R, LANES), lambda i, c: (i, c, 0)),
                   pl.BlockSpec((None, H, GDN_CONV_CH), lambda i, c: (i, 0, 0))],
        out_shape=[oshape(GDN_HK), oshape(GDN_HK), oshape(GDN_HV), oshape(GDN_HV),
                   jax.ShapeDtypeStruct((NB, TT, LANES), F32),
                   jax.ShapeDtypeStruct((NB, H, GDN_CONV_CH), F32)],
        scratch_shapes=[pltpu.VMEM((H + R, GDN_CONV_CH), F32), pltpu.VMEM((Hg + R, LANES), F32)],
        compiler_params=_cparams(2),
        name="gdn_pre",
    )(x3, halo, win, cw, alog, dtb)


def _gdn_rec_kernel(q_ref, k_ref, v_ref, z_ref, bg_ref, x_ref, s0_ref, ng_ref, wout_ref, lng_ref, lnb_ref,
                    out_ref, sT_ref, sscr, oscr, *, C, batches):
    tb = pl.program_id(1)
    bb, nhv, Rt, _ = v_ref.shape
    rep = GDN_HV // GDN_HK

    @pl.when(tb == 0)
    def _init():
        sscr[...] = s0_ref[...].reshape(sscr.shape)

    strict, incl = _packed_masks(C)
    lo = lax.broadcasted_iota(jnp.int32, (C, 2 * C), 1) < C

    def part_a(items, ctx):
        CH = [(b, h, ci) for b, ci in items for h in range(nhv)]
        KH = [(b, m, ci) for b, ci in items for m in range(GDN_HK)]
        rs = lambda ci: pl.ds(ci * C, C)
        bgs = {(b, ci): bg_ref[b, rs(ci), :] for b, ci in items}
        bgT = {key: x.T for key, x in bgs.items()}

        kq = [jnp.concatenate([k_ref[b, m, rs(ci), :], q_ref[b, m, rs(ci), :]], axis=0) for b, m, ci in KH]
        sc = _each(lambda x: _dot(x, jnp.concatenate([x[:C, :], x[:C, :]], axis=0), _NT), kq)
        yield

        def col2(x, lane):
            return jnp.where(lo, jnp.broadcast_to(x[:, lane:lane + 1], (C, 2 * C)),
                             jnp.broadcast_to(x[:, lane + 1:lane + 2], (C, 2 * C)))

        bcol = [col2(bgs[b, ci], rep * m) for b, m, ci in KH]
        gcol = [col2(bgs[b, ci], GDN_HV + rep * m) for b, m, ci in KH]
        grow = [jnp.concatenate([bgT[b, ci][GDN_HV + rep * m:GDN_HV + rep * m + 1, :],
                                 bgT[b, ci][GDN_HV + rep * m + 1:GDN_HV + rep * m + 2, :]], axis=1) for b, m, ci in KH]
        diff = _each(lambda c_, r_: c_ - r_, gcol, grow)
        Lp = _each(lambda z, bc, d: jnp.where(strict, z[:C, :] * bc * jnp.exp(jnp.where(strict, d, 0.0)), 0.0),
                   sc, bcol, diff)
        Ap = _each(lambda z, d: jnp.where(incl, z[C:, :] * jnp.exp(jnp.where(incl, d, 0.0)), 0.0), sc, diff)
        Tinv_p = yield from _unit_lower_inverse([-l_ for l_ in Lp])

        half = lambda xs, n, h: xs[n // rep][:, (h % rep) * C:(h % rep + 1) * C]
        Tinv = [half(Tinv_p, n, h) for n, (b, h, ci) in enumerate(CH)]
        A = [half(Ap, n, h) for n, (b, h, ci) in enumerate(CH)]
        k = [kq[n // rep][:C, :] for n in range(len(CH))]
        q = [kq[n // rep][C:, :] for n in range(len(CH))]
        v = [v_ref[b, h, rs(ci), :] for b, h, ci in CH]
        beta = [jnp.broadcast_to(bgs[b, ci][:, h:h + 1], (C, LANES)) for b, h, ci in CH]
        gc = [jnp.broadcast_to(bgs[b, ci][:, GDN_HV + h:GDN_HV + h + 1], (C, LANES)) for b, h, ci in CH]
        kb = _each(lambda x, y: x * y, k, beta)
        eg = _each(jnp.exp, gc)
        UW = _each(lambda t, v_, b_, kb_, e: _dot(t, jnp.concatenate([v_ * b_, kb_ * e], axis=1)),
                   Tinv, v, beta, kb, eg)
        yield
        for n, key in enumerate(CH):
            g_last = gc[n][C - 1:C, :]
            ctx[key] = dict(U=UW[n][:, :GDN_DV], WQl=jnp.concatenate([UW[n][:, GDN_DV:], q[n] * eg[n]], axis=0),
                            A=A[n], kdec=k[n] * jnp.exp(g_last - gc[n]), sdec=jnp.exp(g_last))

    def part_b(items, ctx):
        for ci in sorted({ci for _, ci in items}):
            CH = [(b, h, ci) for b, c_ in items if c_ == ci for h in range(nhv)]
            X = [ctx.pop(key) for key in CH]
            S = [sscr[b * nhv + h] for b, h, _ in CH]
            WQ = _each(lambda x, s_: _dot(x["WQl"], s_), X, S)
            yield
            v_new = _each(lambda x, wq: x["U"] - wq[:C, :], X, WQ)
            o = _each(lambda x, wq, vn: wq[C:, :] + _dot(x["A"], vn), X, WQ, v_new)
            Snew = _each(lambda x, s_, vn: s_ * x["sdec"] + _dot(x["kdec"], vn, _TN), X, S, v_new)
            yield
            for n, (b, h, _) in enumerate(CH):
                sscr[b * nhv + h] = Snew[n]
                oh = o[n]
                oscr[b * Rt + ci * C:b * Rt + (ci + 1) * C, LANES * h:LANES * (h + 1)] = (
                    oh * lax.rsqrt(jnp.mean(oh * oh, axis=-1, keepdims=True) + GDN_EPS) * ng_ref[...])
            yield

    _run_pipelined(part_a, part_b, batches)

    @pl.when(tb == pl.num_programs(1) - 1)
    def _final_state():
        sT_ref[...] = sscr[...].reshape(sT_ref.shape)

    _gated_out_ln(oscr[...], z_ref, x_ref, wout_ref, lng_ref, lnb_ref, out_ref)


def _gdn_rec(q4, k4, v4, z4, bg, x, s0, norm_g, wout, lng, lnb, *, C, Rt, bb, gb):
    B, _, T, _ = v4.shape
    D = x.shape[-1]
    kern = functools.partial(_gdn_rec_kernel, C=C, batches=_rec_batches(bb, Rt // C, gb, GDN_CHUNKS_PER_BATCH))

    def tspec(nh):
        return pl.BlockSpec((bb, nh, Rt, LANES), lambda b, t: (b, 0, t, 0))

    xspec = pl.BlockSpec((bb, Rt, D), lambda b, t: (b, t, 0))
    sspec = pl.BlockSpec((bb, GDN_HV, GDN_DK, GDN_DV), lambda b, t: (b, 0, 0, 0))
    return pl.pallas_call(
        kern,
        grid=(B // bb, T // Rt),
        in_specs=[tspec(GDN_HK), tspec(GDN_HK), tspec(GDN_HV), tspec(GDN_HV),
                  pl.BlockSpec((bb, Rt, LANES), lambda b, t: (b, t, 0)), xspec, sspec,
                  _const_spec(norm_g.shape), _const_spec(wout.shape), _const_spec(lng.shape), _const_spec(lnb.shape)],
        out_specs=[xspec, sspec],
        out_shape=[jax.ShapeDtypeStruct(x.shape, F32), jax.ShapeDtypeStruct(s0.shape, F32)],
        scratch_shapes=[pltpu.VMEM((bb * GDN_HV, GDN_DK, GDN_DV), F32), pltpu.VMEM((bb * Rt, GDN_VAL_W), F32)],
        compiler_params=_cparams(2),
        name="gdn_rec",
    )(q4, k4, v4, z4, bg, x, s0, norm_g, wout, lng, lnb)


class _Group:
    def __init__(self, B, T, time_major):
        self.B, self.T, self.time_major = B, T, time_major
        if time_major:
            self.s, self.NB, self.TT, self.R = B, 1, T * B, T * B
        else:
            self.s, self.NB, self.TT, self.R = 1, B, T, min(ROW_BLOCK, T)

    def to_rows(self, x):
        if self.time_major:
            return jnp.swapaxes(x, 0, 1).reshape(1, self.TT, x.shape[-1])
        return x

    def from_rows(self, x3):
        if self.time_major:
            return jnp.swapaxes(x3.reshape(self.T, self.B, x3.shape[-1]), 0, 1)
        return x3

    def halo(self, st, n_steps):
        if self.time_major:
            return jnp.swapaxes(st, 0, 1).reshape(1, n_steps * self.B, st.shape[-1])
        H = _halo_rows(n_steps, 1)
        return jnp.pad(st, ((0, 0), (H - n_steps, 0), (0, 0)))

    def unhalo(self, tail, n_steps):
        if self.time_major:
            return jnp.swapaxes(tail.reshape(n_steps, self.B, tail.shape[-1]), 0, 1)
        return tail[:, tail.shape[1] - n_steps:]

    def vec(self, st):
        return st[None] if self.time_major else st[:, None, :]

    def unvec(self, v):
        return v[0] if self.time_major else v[:, 0, :]

    def heads_to_batch(self, a4, Tpad):
        if not self.time_major:
            return a4
        nh = a4.shape[1]
        a = a4.reshape(nh, self.T, self.B, LANES).transpose(2, 0, 1, 3)
        return jnp.pad(a, ((0, 0), (0, 0), (0, Tpad - self.T), (0, 0)))

    def rows_to_batch(self, a3, Tpad):
        if not self.time_major:
            return a3
        return jnp.pad(self.from_rows(a3), ((0, 0), (0, Tpad - self.T), (0, 0)))

    def rows_from_batch(self, a3):
        if not self.time_major:
            return a3
        return self.to_rows(a3[:, :self.T])

    def rec_tiling(self, chunk, block):
        if self.time_major:
            Tp = -(-self.T // SUBLANES) * SUBLANES
            return Tp, Tp, Tp, min(self.B, SAMPLE_SEQ_BLOCK), min(self.B, SAMPLE_SEQ_BLOCK)
        return min(chunk, self.T), self.T, min(block, self.T), PROMPT_SEQ_BLOCK if self.B % PROMPT_SEQ_BLOCK == 0 else 1, 1


def _lru_apply(g, x3, conv_st, h_st, p, lng, lnb):
    swap = not g.time_major
    assert g.B % SUBLANES == 0, "RG-LRU kernel needs the batch to fill whole sublane tiles"
    lay = _Group(g.B, g.T, True) if swap else g
    out, tail, hT = _lru_layer(x3, lay.halo(conv_st, CONV_W - 1), lay.vec(h_st), p["win"], p["cw"], p["cb"],
                               p["wg"], p["bg"], p["lam"], p["wout"], lng, lnb, s=lay.s,
                               R=min(LRU_BLOCK, g.T * g.B) if swap else g.R, swap=swap)
    return out, lay.unhalo(tail, CONV_W - 1), lay.unvec(hT)


def _rwkv_apply(g, x3, shift_st, wkv_st, p, lng, lnb):
    B = g.B
    pre = _rwkv_pre(x3, g.halo(shift_st[:, None, :], 1), p["mu"], p["win"], p["w0"], p["w1"], p["w2"], p["a0"],
                    p["a1"], p["a2"], s=g.s, R=g.R)
    new_shift = g.from_rows(x3)[:, -1]
    if g.time_major:
        r4, k4, v4, g4, lw4, a4 = pre
        s0 = jnp.transpose(wkv_st, (1, 2, 3, 0)).reshape(RW_H // 2, 2, RW_N, RW_N, B)
        col = lambda v: jnp.swapaxes(v, 1, 2)
        o4, sT = _rwkv_step(r4, k4, v4, lw4, a4, s0, col(p["k_k"]), col(p["k_a"]), col(p["r_k"]), col(p["gn_g"]),
                            col(p["gn_b"]), T=g.T, B=B)
        out = _post_layer(x3, o4, g4, p["wout"], lng, lnb)
        return out, new_shift, jnp.transpose(sT.reshape(RW_H, RW_N, RW_N, B), (3, 0, 1, 2))
    C, Tp, Rt, bb, gb = g.rec_tiling(RW_CHUNK, RW_BLOCK)
    r4, k4, v4, g4, lw4, a4 = [g.heads_to_batch(a, Tp) for a in pre]
    s0 = wkv_st.reshape(B, RW_H // 2, 2 * RW_N, RW_N)
    out, sT = _rwkv_rec(r4, k4, v4, lw4, a4, g4, g.rows_to_batch(x3, Tp), s0, p["k_k"], p["k_a"], p["r_k"],
                        p["gn_g"], p["gn_b"], p["wout"], lng, lnb, C=C, Rt=Rt, bb=bb, gb=gb)
    new_shift = g.from_rows(x3)[:, -1]
    return g.rows_from_batch(out), new_shift, sT.reshape(B, RW_H, RW_N, RW_N)


def _gdn_apply(g, x3, conv_st, S_st, p, lng, lnb):
    T = g.T
    C, Tp, Rt, bb, gb = g.rec_tiling(GDN_CHUNK, GDN_BLOCK)
    q4, k4, v4, z4, bg, tail = _gdn_pre(x3, g.halo(conv_st, CONV_W - 1), p["win"], p["cw"], p["alog"], p["dtb"],
                                        s=g.s, R=g.R, C=min(GDN_CHUNK, T))
    bgb = g.rows_to_batch(bg, T)
    if Tp != T:
        held = jnp.where(jnp.arange(LANES) < GDN_HV, 0.0, bgb[:, T - 1:T, :])
        bgb = jnp.concatenate([bgb, jnp.broadcast_to(held, (g.B, Tp - T, LANES))], axis=1)
    tb = lambda a: g.heads_to_batch(a, Tp)
    out, sT = _gdn_rec(tb(q4), tb(k4), tb(v4), tb(z4), bgb, g.rows_to_batch(x3, Tp), S_st, p["norm_g"], p["wout"],
                       lng, lnb, C=C, Rt=Rt, bb=bb, gb=gb)
    return g.rows_from_batch(out), g.unhalo(tail, CONV_W - 1), sT


def _trunk(g, x, st, params, ln_g, ln_b):
    lru_conv, lru_h, rw_shift, rw_S, gdn_conv, gdn_S = st
    new = ([], [], [], [], [], [])
    x3 = g.to_rows(x)
    ia = ib = ic = 0
    for layer in range(DEPTH):
        kind = layer % N_MIXERS
        lng, lnb = ln_g[layer][None, :], ln_b[layer][None, :]
        if kind == 0:
            x3, c, h = _lru_apply(g, x3, lru_conv[ia], lru_h[ia], params["lru"][ia], lng, lnb)
            new[0].append(c)
            new[1].append(h)
            ia += 1
        elif kind == 1:
            x3, sh, S = _rwkv_apply(g, x3, rw_shift[ib], rw_S[ib], params["rwkv"][ib], lng, lnb)
            new[2].append(sh)
            new[3].append(S)
            ib += 1
        else:
            x3, c, S = _gdn_apply(g, x3, gdn_conv[ic], gdn_S[ic], params["gdn"][ic], lng, lnb)
            new[4].append(c)
            new[5].append(S)
            ic += 1
    return g.from_rows(x3), tuple(s[0][None] if len(s) == 1 else jnp.stack(s) for s in new)


def _prep_params(lru_w_in, lru_conv_w, lru_conv_b, lru_wa, lru_ba, lru_wx, lru_bx, lru_lambda, lru_w_out, rw_mu,
                 rw_w_in, rw_w0, rw_w1, rw_w2, rw_a0, rw_a1, rw_a2, rw_k_k, rw_k_a, rw_r_k, rw_gn_g, rw_gn_b,
                 rw_w_out, gdn_w_in, gdn_conv_w, gdn_a_log, gdn_dt_bias, gdn_norm_g, gdn_w_out):
    row = lambda v: v[None, :]
    lru = []
    for n in range(lru_w_in.shape[0]):
        lru.append(dict(win=lru_w_in[n].astype(BF), cw=lru_conv_w[n][:, None, :], cb=row(lru_conv_b[n]),
                        wg=_lru_gate_weights(lru_wa[n], lru_wx[n]), bg=jnp.stack([lru_ba[n], lru_bx[n]])[:, None, :],
                        lam=row(lru_lambda[n]), wout=lru_w_out[n].astype(BF)))
    pairw = lambda v: v.reshape(RW_W // LANES, 1, LANES)
    rwkv = []
    for n in range(rw_w_in.shape[0]):
        rwkv.append(dict(mu=rw_mu[n][:, None, :],win=rw_w_in[n].astype(BF), w0=row(rw_w0[n]), w1=rw_w1[n].astype(BF),
                         w2=rw_w2[n].astype(BF), a0=row(rw_a0[n]), a1=rw_a1[n].astype(BF), a2=rw_a2[n].astype(BF),
                         k_k=pairw(rw_k_k[n]), k_a=pairw(rw_k_a[n]), r_k=pairw(rw_r_k[n]), gn_g=pairw(rw_gn_g[n]),
                         gn_b=pairw(rw_gn_b[n]), wout=rw_w_out[n].astype(BF)))
    gdn = []
    for n in range(gdn_w_in.shape[0]):
        w = gdn_w_in[n]
        o2 = GDN_CONV_CH + GDN_VAL_W
        wpad = jnp.pad(w[:, o2:], ((0, 0), (0, LANES - 2 * GDN_HV)))
        lanes = lambda v: jnp.pad(v, (GDN_HV, LANES - 2 * GDN_HV))[None, :]
        gdn.append(dict(win=jnp.concatenate([w[:, :o2], wpad], axis=1).astype(BF), cw=gdn_conv_w[n][:, None, :],
                        alog=lanes(gdn_a_log[n]), dtb=lanes(gdn_dt_bias[n]), norm_g=row(gdn_norm_g[n]),
                        wout=gdn_w_out[n].astype(BF)))
    return dict(lru=lru, rwkv=rwkv, gdn=gdn)


def kernel(x_prompt, x_sample, state_lru_conv, state_lru_h, state_rwkv_shift, state_rwkv_wkv, state_gdn_conv, state_gdn_S, ln_g, ln_b, lru_w_in, lru_conv_w, lru_conv_b, lru_wa, lru_ba, lru_wx, lru_bx, lru_lambda, lru_w_out, rw_mu, rw_w_in, rw_w0, rw_w1, rw_w2, rw_a0, rw_a1, rw_a2, rw_k_k, rw_k_a, rw_r_k, rw_gn_g, rw_gn_b, rw_w_out, gdn_w_in, gdn_conv_w, gdn_a_log, gdn_dt_bias, gdn_norm_g, gdn_w_out):
    params = _prep_params(lru_w_in, lru_conv_w, lru_conv_b, lru_wa, lru_ba, lru_wx, lru_bx, lru_lambda, lru_w_out,
                          rw_mu, rw_w_in, rw_w0, rw_w1, rw_w2, rw_a0, rw_a1, rw_a2, rw_k_k, rw_k_a, rw_r_k, rw_gn_g,
                          rw_gn_b, rw_w_out, gdn_w_in, gdn_conv_w, gdn_a_log, gdn_dt_bias, gdn_norm_g, gdn_w_out)
    bp, tp, _ = x_prompt.shape
    bs, ts, _ = x_sample.shape
    n_a, n_b, n_c = state_lru_conv.shape[0], state_rwkv_shift.shape[0], state_gdn_conv.shape[0]
    zero_state = (jnp.zeros((n_a, bp, CONV_W - 1, LRU_W), F32),
                  jnp.zeros((n_a, bp, LRU_W), F32),
                  jnp.zeros((n_b, bp, D_MODEL), F32),
                  jnp.zeros((n_b, bp, RW_H, RW_N, RW_N), F32),
                  jnp.zeros((n_c, bp, CONV_W - 1, GDN_CONV_CH), F32),
                  jnp.zeros((n_c, bp, GDN_HV, GDN_DK, GDN_DV), F32))
    y_prompt, sp = _trunk(_Group(bp, tp, False), x_prompt, zero_state, params, ln_g, ln_b)
    y_sample, ss = _trunk(_Group(bs, ts, True), x_sample,
                          (state_lru_conv, state_lru_h, state_rwkv_shift, state_rwkv_wkv, state_gdn_conv,
                           state_gdn_S), params, ln_g, ln_b)
    return (y_prompt, y_sample, sp[0], ss[0], sp[1], ss[1], sp[2], ss[2], sp[3], ss[3], sp[4], ss[4], sp[5], ss[5])
```

```python
import functools
import math

import jax
import jax.numpy as jnp
from jax import lax
from jax.experimental import pallas as pl
from jax.experimental.pallas import tpu as pltpu

F32 = jnp.float32
BF = jnp.bfloat16

D_MODEL = 1024
DEPTH = 4
N_MIXERS = 3
DN_ALPHA = (2.0 * DEPTH) ** 0.25
LN_EPS = 1e-5
CONV_W = 4

LRU_W = D_MODEL
LRU_BLOCKS = 16
LRU_BS = LRU_W // LRU_BLOCKS
LRU_C = 8.0

RW_W = D_MODEL
RW_N = 64
RW_H = RW_W // RW_N
RW_GN_EPS = 64e-5
RW_NORM_EPS = 1e-12

GDN_HK = 4
GDN_HV = 8
GDN_DK = 128
GDN_DV = 128
GDN_KEY_W = GDN_HK * GDN_DK
GDN_VAL_W = GDN_HV * GDN_DV
GDN_CONV_CH = 2 * GDN_KEY_W + GDN_VAL_W
GDN_CHUNK = 64
GDN_EPS = 1e-6

LANES = 128
SUBLANES = 8
MXU_WIDTH = 256
VMEM_LIMIT = 56 * 1024 * 1024
ROW_BLOCK = 512
LRU_BLOCK = 1024
RW_BLOCK = 256
GDN_BLOCK = 256
RW_CHUNKS_PER_BATCH = 1
GDN_CHUNKS_PER_BATCH = 2
PROMPT_SEQ_BLOCK = 2
SAMPLE_SEQ_BLOCK = 8
RW_CHUNK = 64

_NN = (((1,), (0,)), ((), ()))
_NT = (((1,), (1,)), ((), ()))
_TN = (((0,), (0,)), ((), ()))


def _mm(a, b):
    return jnp.dot(a.astype(BF), b.astype(BF), preferred_element_type=F32)


def _dot(a, b, dims=_NN):
    return lax.dot_general(a.astype(BF), b.astype(BF), dims, preferred_element_type=F32)


def _cumsum_rows(x, tri3):
    b1 = x.astype(BF)
    r1 = x - b1.astype(F32)
    b2 = r1.astype(BF)
    b3 = (r1 - b2.astype(F32)).astype(BF)
    return jnp.dot(tri3, jnp.concatenate([b1, b2, b3], axis=0), preferred_element_type=F32)


def _layer_norm(z, g, b):
    mu = jnp.mean(z, axis=-1, keepdims=True)
    zc = z - mu
    var = jnp.mean(zc * zc, axis=-1, keepdims=True)
    return zc * lax.rsqrt(var + LN_EPS) * g + b


def _silu(x):
    return x * jax.nn.sigmoid(x)


def _cparams(n_axes):
    return pltpu.CompilerParams(dimension_semantics=("arbitrary",) * n_axes, vmem_limit_bytes=VMEM_LIMIT)


def _const_spec(shape):
    nd = len(shape)
    return pl.BlockSpec(shape, lambda *_: (0,) * nd, pipeline_mode=pl.Buffered(1))


def _halo_rows(n_steps, s):
    rows = n_steps * s
    return rows if rows % SUBLANES == 0 else SUBLANES


def _run_skewed(gens):
    pending, active = list(gens), []
    while pending or active:
        if pending:
            active.append(pending.pop(0))
        for g in list(active):
            try:
                next(g)
            except StopIteration:
                active.remove(g)


def _lru_kernel(x_ref, halo_ref, h0_ref, win_ref, cw_ref, cb_ref, wg_ref, bg_ref, lam_ref, wout_ref, lng_ref,
                lnb_ref, out_ref, tail_ref, hT_ref, xscr, hscr, *, s, R, H, swap):
    c = pl.program_id(1)
    C = LRU_W
    tt = R // s

    @pl.when(c == 0)
    def _init():
        xscr[0:H, :] = halo_ref[...]
        hscr[...] = h0_ref[...]

    x = x_ref[...]
    if swap:
        x = jnp.swapaxes(x, 0, 1).reshape(R, D_MODEL)
    xb16 = x.astype(BF)
    G = MXU_WIDTH
    ysum = []

    def lane_group(g):
        ls = pl.ds(G * g, G)
        xb = jnp.dot(xb16, win_ref[:, ls], preferred_element_type=F32)
        gate = jnp.dot(xb16, win_ref[:, pl.ds(C + G * g, G)], preferred_element_type=F32)
        xscr[H:H + R, ls] = xb
        yield

        xc = xb * cw_ref[3, :, ls] + cb_ref[:, ls]
        for d in (1, 2, 3):
            xc = xc + xscr[pl.ds(H - d * s, R), ls] * cw_ref[3 - d, :, ls]
        gt = jnp.dot(xc.astype(BF), wg_ref[g], preferred_element_type=F32)
        r = jax.nn.sigmoid(gt[:, :G] + bg_ref[0, :, ls])
        i = jax.nn.sigmoid(gt[:, G:] + bg_ref[1, :, ls])
        log_a = r * ((-LRU_C) * jax.nn.softplus(-lam_ref[:, ls]))
        a = jnp.exp(log_a)
        b = jnp.sqrt(1.0 - a * a) * i * xc
        hlast = hscr[:, ls]
        hs = []
        for t in range(tt):
            hlast = a[t * s:(t + 1) * s, :] * hlast + b[t * s:(t + 1) * s, :]
            hs.append(hlast)
        hscr[:, ls] = hlast
        hg = jnp.concatenate(hs, axis=0) * _silu(gate)
        yield

        ysum.append(jnp.dot(hg.astype(BF), wout_ref[ls, :], preferred_element_type=F32))
        yield

    _run_skewed([lane_group(g) for g in range(C // G)])
    tail = xscr[pl.ds(R, H), :]
    xscr[0:H, :] = tail
    tail_ref[...] = tail
    hT_ref[...] = hscr[...]
    y = functools.reduce(lambda p, q: p + q, ysum)
    out = _layer_norm(DN_ALPHA * x + y, lng_ref[...], lnb_ref[...])
    if swap:
        out = jnp.swapaxes(out.reshape(tt, s, D_MODEL), 0, 1)
    out_ref[...] = out


def _lru_layer(x3, halo, h0, win, cw, cb, wg, bg, lam, wout, lng, lnb, *, s, R, swap):
    C = LRU_W
    D = x3.shape[-1]
    H = halo.shape[1]
    tt = R // s
    assert s % SUBLANES == 0
    if swap:
        NB, TT = 1, x3.shape[1] * s
        xspec = pl.BlockSpec((s, tt, D), lambda i, c: (0, c, 0))
    else:
        NB, TT = x3.shape[:2]
        xspec = pl.BlockSpec((None, R, D), lambda i, c: (i, c, 0))
    kern = functools.partial(_lru_kernel, s=s, R=R, H=H, swap=swap)
    return pl.pallas_call(
        kern,
        grid=(NB, TT // R),
        in_specs=[
            xspec,
            pl.BlockSpec((None, H, C), lambda i, c: (i, 0, 0)),
            pl.BlockSpec((None, s, C), lambda i, c: (i, 0, 0)),
            _const_spec(win.shape), _const_spec(cw.shape), _const_spec(cb.shape), _const_spec(wg.shape),
            _const_spec(bg.shape), _const_spec(lam.shape), _const_spec(wout.shape), _const_spec(lng.shape),
            _const_spec(lnb.shape),
        ],
        out_specs=[
            xspec,
            pl.BlockSpec((None, H, C), lambda i, c: (i, 0, 0)),
            pl.BlockSpec((None, s, C), lambda i, c: (i, 0, 0)),
        ],
        out_shape=[
            jax.ShapeDtypeStruct(x3.shape, F32),
            jax.ShapeDtypeStruct((NB, H, C), F32),
            jax.ShapeDtypeStruct((NB, s, C), F32),
        ],
        scratch_shapes=[
            pltpu.VMEM((H + R, C), F32),
            pltpu.VMEM((s, C), F32),
        ],
        compiler_params=_cparams(2),
        name="lru_layer",
    )(x3, halo, h0, win, cw, cb, wg, bg, lam, wout, lng, lnb)


def _lru_gate_weights(wa, wx):
    G = MXU_WIDTH
    nb = G // LRU_BS

    def bd(w):
        w4 = w.reshape(LRU_BLOCKS // nb, nb, LRU_BS, LRU_BS)
        eye = jnp.eye(nb, dtype=w.dtype)
        return jnp.einsum("gaij,ab->gaibj", w4, eye).reshape(LRU_BLOCKS // nb, G, G)
    return jnp.concatenate([bd(wa), bd(wx)], axis=2).astype(BF)


def _gated_out_ln(o, g_ref, x_ref, w_ref, lng_ref, lnb_ref, out_ref):
    bb, nh, Rt, _ = g_ref.shape
    rows = bb * Rt
    g = jnp.concatenate([g_ref[:, p, :, :].reshape(rows, LANES) for p in range(nh)], axis=1)
    y = _mm(o * _silu(g), w_ref[...])
    x = x_ref[...].reshape(rows, D_MODEL)
    out_ref[...] = _layer_norm(DN_ALPHA * x + y, lng_ref[...], lnb_ref[...]).reshape(bb, Rt, D_MODEL)


def _post_kernel(x_ref, o_ref, g_ref, w_ref, lng_ref, lnb_ref, out_ref):
    o = jnp.concatenate([o_ref[0, p] for p in range(o_ref.shape[1])], axis=1)
    _gated_out_ln(o, g_ref, x_ref, w_ref, lng_ref, lnb_ref, out_ref)


def _post_layer(x3, o4, g4, w, lng, lnb):
    return pl.pallas_call(
        _post_kernel,
        out_shape=jax.ShapeDtypeStruct(x3.shape, F32),
        compiler_params=pltpu.CompilerParams(vmem_limit_bytes=VMEM_LIMIT),
        name="post_layer",
    )(x3, o4, g4, w, lng, lnb)


def _rwkv_pre_kernel(x_ref, halo_ref, mu_ref, win_ref, w0_ref, w1_ref, w2_ref, a0_ref, a1_ref, a2_ref,
                     r_ref, k_ref, v_ref, g_ref, lw_ref, a_ref, xscr, xxscr, *, s, R, H):
    c = pl.program_id(1)

    @pl.when(c == 0)
    def _init():
        xscr[0:H, :] = halo_ref[...]

    x = x_ref[...]
    xscr[H:H + R, :] = x
    xxscr[...] = xscr[pl.ds(H - s, R), :] - x
    xscr[0:H, :] = xscr[pl.ds(R, H), :]
    xx = xxscr[...]
    xm = lambda n: x + xx * mu_ref[n]

    def put(ref, val):
        for p in range(RW_W // LANES):
            ref[p] = val[:, LANES * p:LANES * (p + 1)]

    put(r_ref, _mm(xm(0), win_ref[0]))
    put(k_ref, _mm(xm(1), win_ref[1]))
    put(v_ref, _mm(xm(2), win_ref[2]))
    put(g_ref, _mm(xm(3), win_ref[3]))
    w_raw = w0_ref[...] + _mm(jnp.tanh(_mm(xm(4), w1_ref[...])), w2_ref[...])
    put(lw_ref, (-math.exp(-0.5)) * jax.nn.sigmoid(w_raw))
    put(a_ref, jax.nn.sigmoid(a0_ref[...] + _mm(_mm(xm(5), a1_ref[...]), a2_ref[...])))


def _rwkv_pre(x3, halo, mu, win, w0, w1, w2, a0, a1, a2, *, s, R):
    NB, TT, D = x3.shape
    H = halo.shape[1]
    nh = RW_W // LANES
    kern = functools.partial(_rwkv_pre_kernel, s=s, R=R, H=H)
    ospec = pl.BlockSpec((None, nh, R, LANES), lambda i, c: (i, 0, c, 0))
    oshape = jax.ShapeDtypeStruct((NB, nh, TT, LANES), F32)
    return pl.pallas_call(
        kern,
        grid=(NB, TT // R),
        in_specs=[
            pl.BlockSpec((None, R, D), lambda i, c: (i, c, 0)),
            pl.BlockSpec((None, H, D), lambda i, c: (i, 0, 0)),
            _const_spec(mu.shape), _const_spec(win.shape), _const_spec(w0.shape), _const_spec(w1.shape),
            _const_spec(w2.shape), _const_spec(a0.shape), _const_spec(a1.shape), _const_spec(a2.shape),
        ],
        out_specs=[ospec] * 6,
        out_shape=[oshape] * 6,
        scratch_shapes=[pltpu.VMEM((H + R, D), F32), pltpu.VMEM((R, D), F32)],
        compiler_params=_cparams(2),
        name="rwkv_pre",
    )(x3, halo, mu, win, w0, w1, w2, a0, a1, a2)


def _seg_sum(x):
    lane = lax.broadcasted_iota(jnp.int32, x.shape, 1)
    lo = lane < RW_N
    s0 = jnp.sum(jnp.where(lo, x, 0.0), axis=-1, keepdims=True)
    s1 = jnp.sum(jnp.where(lo, 0.0, x), axis=-1, keepdims=True)
    return jnp.where(lo, s0, s1)


def _stack2(x):
    lane = lax.broadcasted_iota(jnp.int32, x.shape, 1)
    lo = lane < x.shape[1] // 2
    return jnp.concatenate([jnp.where(lo, x, 0.0), jnp.where(lo, 0.0, x)], axis=0)


def _each(f, *lists):
    return [f(*t) for t in zip(*lists)]


def _packed_masks(C):
    t = lax.broadcasted_iota(jnp.int32, (C, 2 * C), 0)
    s = lax.broadcasted_iota(jnp.int32, (C, 2 * C), 1) % C
    return s < t, s <= t


def _stackn(x, n):
    grp = lax.broadcasted_iota(jnp.int32, x.shape, 1) // (x.shape[1] // n)
    return jnp.concatenate([jnp.where(grp == k, x, 0.0) for k in range(n)], axis=0)


def _neumann_inverse(Ps, nblk):
    m = Ps[0].shape[0]
    eye = (lax.broadcasted_iota(jnp.int32, Ps[0].shape, 1) % m) == lax.broadcasted_iota(jnp.int32, Ps[0].shape, 0)
    invs = [jnp.where(eye, 1.0, P) for P in Ps]
    pw = Ps
    span = 2
    while span < m:
        pw = _each(lambda p: _dot(p, _stackn(p, nblk)), pw)
        yield
        invs = _each(lambda inv, p: inv + _dot(inv, _stackn(p, nblk)), invs, pw)
        yield
        span *= 2
    return invs


def _unit_lower_inverse(Ls):
    C = Ls[0].shape[0]
    h = C // 2
    if h % SUBLANES:
        return (yield from _neumann_inverse(Ls, 2))
    grp = lax.broadcasted_iota(jnp.int32, (h, 2 * C), 1) // h
    diag0 = (grp % 2) == 0
    zero = jnp.zeros((h, 2 * C), F32)
    D = yield from _neumann_inverse([jnp.where(diag0, L[:h, :], L[h:, :]) for L in Ls], 4)
    L21 = [jnp.where(diag0, L[h:, :], 0.0) for L in Ls]
    M1 = _each(lambda d, q: _dot(d, jnp.concatenate(
        [zero, jnp.where(grp == 0, q, 0.0), zero, jnp.where(grp == 2, q, 0.0)], axis=0)), D, L21)
    yield
    X21 = _each(lambda m1, d: _dot(m1, jnp.concatenate(
        [jnp.where(grp == 0, d, 0.0), zero, jnp.where(grp == 2, d, 0.0), zero], axis=0)), M1, D)
    yield
    return _each(lambda d, x21: jnp.concatenate([jnp.where(diag0, d, 0.0), x21 + jnp.where(diag0, 0.0, d)], axis=0),
                 D, X21)


def _run_pipelined(part_a, part_b, batches):
    ctx = {}
    prev = None
    for batch in batches:
        gens = [part_a(batch, ctx)] + ([part_b(prev, ctx)] if prev is not None else [])
        while gens:
            for g in list(gens):
                try:
                    next(g)
                except StopIteration:
                    gens.remove(g)
        prev = batch
    for _ in part_b(prev, ctx):
        pass


def _rec_batches(bb, nchunk, gb, cpb):
    if nchunk > 1:
        return tuple(tuple((b, c) for b in range(bb) for c in range(c0, min(c0 + cpb, nchunk)))
                     for c0 in range(0, nchunk, cpb))
    return tuple(tuple((b, 0) for b in range(b0, b0 + gb)) for b0 in range(0, bb, gb))


def _rwkv_rec_kernel(r_ref, k_ref, v_ref, lw_ref, a_ref, g_ref, x_ref, s0_ref, kk_ref, ka_ref, rk_ref, gg_ref,
                     gb_ref, wout_ref, lng_ref, lnb_ref, out_ref, sT_ref, sscr, oscr, *, C, batches):
    tb = pl.program_id(1)
    bb, npair, Rt, _ = r_ref.shape
    C2 = 2 * C
    lane_sq = lax.broadcasted_iota(jnp.int32, (LANES, LANES), 1)
    row_sq = lax.broadcasted_iota(jnp.int32, (LANES, LANES), 0)
    same_head = (lane_sq < RW_N) == (row_sq < RW_N)

    @pl.when(tb == 0)
    def _init():
        def init_b(b, carry):
            for p in range(npair):
                s2 = s0_ref[b, p]
                sscr[b * npair + p] = jnp.where(same_head, jnp.concatenate([s2, s2], axis=1), 0.0)
            return carry
        lax.fori_loop(0, bb, init_b, 0)

    strict, incl = _packed_masks(C)
    tri3 = ((lax.broadcasted_iota(jnp.int32, (C, 3 * C), 1) % C)
            <= lax.broadcasted_iota(jnp.int32, (C, 3 * C), 0)).astype(BF)

    def part_a(items, ctx):
        CH = [(b, p, ci) for b, ci in items for p in range(npair)]
        rs = lambda ci: pl.ds(ci * C, C)
        r = [r_ref[b, p, rs(ci), :] for b, p, ci in CH]
        k = [k_ref[b, p, rs(ci), :] for b, p, ci in CH]
        v = [v_ref[b, p, rs(ci), :] for b, p, ci in CH]
        lw = [lw_ref[b, p, rs(ci), :] for b, p, ci in CH]
        a = [a_ref[b, p, rs(ci), :] for b, p, ci in CH]
        kn = [k_ * kk_ref[p] for k_, (b, p, ci) in zip(k, CH)]
        kk = _each(lambda z: z * lax.rsqrt(_seg_sum(z * z) + RW_NORM_EPS), kn)
        kh = [k_ * (1.0 + (a_ - 1.0) * ka_ref[p]) for k_, a_, (b, p, ci) in zip(k, a, CH)]
        bvec = _each(lambda x, y: x * y, kk, a)
        cum = _each(lambda x: _cumsum_rows(x, tri3), lw)
        yield
        cum_last = [c_[C - 1:C, :] for c_ in cum]
        e_neg = _each(lambda c_: jnp.exp(-c_), cum)
        e_dec = _each(lambda cl, c_: jnp.exp(cl - c_), cum_last, cum)
        At = _each(lambda kk_, c_, lw_: -kk_ * jnp.exp(c_ - lw_), kk, cum, lw)
        Rt_ = _each(lambda r_, c_: r_ * jnp.exp(c_), r, cum)
        AR = _each(lambda x, y: jnp.concatenate([x, y], axis=0), At, Rt_)
        BKs = _each(lambda b_, kh_, e: jnp.concatenate([_stack2(b_ * e), _stack2(kh_ * e)], axis=0), bvec, kh, e_neg)
        V2 = _each(_stack2, v)
        sc = _each(lambda x, y: _dot(x, y, _NT), AR, BKs)
        yield
        Lab = [jnp.where(strict, z[:C, :C2], 0.0) for z in sc]
        Lak = [jnp.where(strict, z[:C, C2:], 0.0) for z in sc]
        Mrbk = [jnp.concatenate([jnp.where(incl, z[C:, :C2], 0.0), jnp.where(incl, z[C:, C2:], 0.0)], axis=1)
                for z in sc]
        LV = _each(_dot, Lak, V2)
        Tinv = yield from _unit_lower_inverse(Lab)
        for n, key in enumerate(CH):
            ctx[key] = dict(AR=AR[n], V2=V2[n], LV=LV[n], Tinv=Tinv[n], Mrbk=Mrbk[n], v=v[n], r=r[n], kh=kh[n],
                            sdec=jnp.exp(cum_last[n]), bkdec=jnp.concatenate([bvec[n] * e_dec[n], kh[n] * e_dec[n]], 0))

    def part_b(items, ctx):
        for ci in sorted({ci for _, ci in items}):
            CH = [(b, p, ci) for b, c_ in items if c_ == ci for p in range(npair)]
            X = [ctx.pop(key) for key in CH]
            S = [sscr[b * npair + p] for b, p, _ in CH]
            UY0 = _each(lambda x, s_: _dot(x["AR"], s_, _NT), X, S)
            yield
            U = _each(lambda x, u: _dot(x["Tinv"], _stack2(u[:C, :] + x["LV"])), X, UY0)
            yield
            y = _each(lambda x, u0, u: u0[C:, :] + _dot(x["Mrbk"], jnp.concatenate([_stack2(u), x["V2"]], axis=0)),
                      X, UY0, U)
            Snew = _each(lambda x, s_, u: s_ * x["sdec"] + jnp.where(
                same_head, _dot(jnp.concatenate([u, x["v"]], axis=0), x["bkdec"], _TN), 0.0), X, S, U)
            yield
            for n, (b, p, _) in enumerate(CH):
                sscr[b * npair + p] = Snew[n]
                m = _seg_sum(y[n]) * (1.0 / RW_N)
                yc = y[n] - m
                var = _seg_sum(yc * yc) * (1.0 / RW_N)
                yn = yc * lax.rsqrt(var + RW_GN_EPS) * gg_ref[p] + gb_ref[p]
                bonus = _seg_sum(X[n]["r"] * X[n]["kh"] * rk_ref[p]) * X[n]["v"]
                oscr[b * Rt + ci * C:b * Rt + (ci + 1) * C, LANES * p:LANES * (p + 1)] = yn + bonus
            yield

    _run_pipelined(part_a, part_b, batches)

    row_h = lax.broadcasted_iota(jnp.int32, (LANES, RW_N), 0) < RW_N

    @pl.when(tb == pl.num_programs(1) - 1)
    def _final_state():
        def fin_b(b, carry):
            for p in range(npair):
                S = sscr[b * npair + p]
                sT_ref[b, p] = jnp.where(row_h, S[:, :RW_N], S[:, RW_N:])
            return carry
        lax.fori_loop(0, bb, fin_b, 0)

    _gated_out_ln(oscr[...], g_ref, x_ref, wout_ref, lng_ref, lnb_ref, out_ref)


def _rwkv_rec(r4, k4, v4, lw4, a4, g4, x, s0, k_k, k_a, r_k, gn_g, gn_b, wout, lng, lnb, *, C, Rt, bb, gb):
    B, npair, T, _ = r4.shape
    D = x.shape[-1]
    kern = functools.partial(_rwkv_rec_kernel, C=C, batches=_rec_batches(bb, Rt // C, gb, RW_CHUNKS_PER_BATCH))
    tspec = pl.BlockSpec((bb, npair, Rt, LANES), lambda b, t: (b, 0, t, 0))
    xspec = pl.BlockSpec((bb, Rt, D), lambda b, t: (b, t, 0))
    sspec = pl.BlockSpec((bb, npair, LANES, RW_N), lambda b, t: (b, 0, 0, 0))
    wspec = _const_spec((npair, 1, LANES))
    return pl.pallas_call(
        kern,
        grid=(B // bb, T // Rt),
        in_specs=[tspec] * 6 + [xspec, sspec] + [wspec] * 5 + [_const_spec(wout.shape), _const_spec(lng.shape),
                                                              _const_spec(lnb.shape)],
        out_specs=[xspec, sspec],
        out_shape=[jax.ShapeDtypeStruct(x.shape, F32), jax.ShapeDtypeStruct(s0.shape, F32)],
        scratch_shapes=[pltpu.VMEM((bb * npair, LANES, LANES), F32), pltpu.VMEM((bb * Rt, RW_W), F32)],
        compiler_params=_cparams(2),
        name="rwkv_rec",
    )(r4, k4, v4, lw4, a4, g4, x, s0, k_k, k_a, r_k, gn_g, gn_b, wout, lng, lnb)


def _rwkv_step_kernel(r_ref, k_ref, v_ref, lw_ref, a_ref, s0_ref, kk_ref, ka_ref, rk_ref, gg_ref, gb_ref,
                      o_ref, sT_ref, vscr, yscr, *, T, B):
    N = RW_N
    IB = 2 * SUBLANES
    for t in range(T):
        rows = pl.ds(t * B, B)
        rT, kT, vT, aT = r_ref[rows, :].T, k_ref[rows, :].T, v_ref[rows, :].T, a_ref[rows, :].T
        w = jnp.exp(lw_ref[rows, :].T)
        kn = kT * kk_ref[...]
        kh = kT * (1.0 + (aT - 1.0) * ka_ref[...])
        vscr[...] = vT
        src = s0_ref if t == 0 else sT_ref
        bonus = []
        for h in range(2):
            hs = slice(N * h, N * (h + 1))
            kk = kn[hs] * lax.rsqrt(jnp.sum(kn[hs] * kn[hs], axis=0, keepdims=True) + RW_NORM_EPS)
            a_h, b_h, k_h, w_h, r_h = -kk, kk * aT[hs], kh[hs], w[hs], rT[hs]

            def step(ib, carry, h=h, a_h=a_h, b_h=b_h, k_h=k_h, w_h=w_h, r_h=r_h, src=src):
                i0 = pl.multiple_of(ib * IB, IB)
                S = src[h, pl.ds(i0, IB), :, :]
                sa = jnp.sum(S * a_h[None], axis=1)
                vb = vscr[pl.ds(N * h + i0, IB), :]
                Sn = S * w_h[None] + sa[:, None, :] * b_h[None] + vb[:, None, :] * k_h[None]
                sT_ref[h, pl.ds(i0, IB), :, :] = Sn
                yscr[pl.ds(N * h + i0, IB), :] = jnp.sum(Sn * r_h[None], axis=1)
                return carry

            lax.fori_loop(0, N // IB, step, 0)
            bonus.append(jnp.sum(r_h * k_h * rk_ref[hs, :], axis=0, keepdims=True) * vT[hs])
        y = yscr[...]
        outs = []
        for h in range(2):
            hs = slice(N * h, N * (h + 1))
            m = jnp.mean(y[hs], axis=0, keepdims=True)
            yc = y[hs] - m
            var = jnp.mean(yc * yc, axis=0, keepdims=True)
            outs.append(yc * lax.rsqrt(var + RW_GN_EPS) * gg_ref[hs, :] + gb_ref[hs, :] + bonus[h])
        o_ref[rows, :] = jnp.concatenate(outs, axis=0).T


def _rwkv_step(r4, k4, v4, lw4, a4, s0, k_k, k_a, r_k, gn_g, gn_b, *, T, B):
    npair = r4.shape[1]
    kern = functools.partial(_rwkv_step_kernel, T=T, B=B)
    tspec = pl.BlockSpec((None, None, T * B, LANES), lambda p: (0, p, 0, 0))
    sspec = pl.BlockSpec((None, 2, RW_N, RW_N, B), lambda p: (p, 0, 0, 0, 0))
    wspec = pl.BlockSpec((None, LANES, 1), lambda p: (p, 0, 0))
    return pl.pallas_call(
        kern,
        grid=(npair,),
        in_specs=[tspec] * 5 + [sspec] + [wspec] * 5,
        out_specs=[tspec, sspec],
        out_shape=[jax.ShapeDtypeStruct(r4.shape, F32), jax.ShapeDtypeStruct(s0.shape, F32)],
        scratch_shapes=[pltpu.VMEM((LANES, B), F32), pltpu.VMEM((LANES, B), F32)],
        compiler_params=_cparams(1),
        name="rwkv_step",
    )(r4, k4, v4, lw4, a4, s0, k_k, k_a, r_k, gn_g, gn_b)


def _gdn_pre_kernel(x_ref, halo_ref, win_ref, cw_ref, alog_ref, dtb_ref,
                    q_ref, k_ref, v_ref, z_ref, bg_ref, tail_ref, xscr, gscr, *, s, R, H, Hg, C):
    c = pl.program_id(1)
    CH = GDN_CONV_CH
    tt = R // s

    @pl.when(c == 0)
    def _init():
        xscr[0:H, :] = halo_ref[...]
        gscr[0:Hg, :] = jnp.zeros((Hg, LANES), F32)

    x = x_ref[...]
    u = _mm(x, win_ref[...])
    xb = u[:, :CH]
    xscr[H:H + R, :] = xb
    y = xb * cw_ref[3]
    for d in (1, 2, 3):
        y = y + xscr[pl.ds(H - d * s, R), :] * cw_ref[3 - d]
    tail = xscr[pl.ds(R, H), :]
    xscr[0:H, :] = tail
    tail_ref[...] = tail
    qkv = _silu(y)

    def l2n(z, scale):
        return z * (lax.rsqrt(jnp.sum(z * z, axis=-1, keepdims=True) + GDN_EPS) * scale)

    for h in range(GDN_HK):
        q_ref[h] = l2n(qkv[:, LANES * h:LANES * (h + 1)], GDN_DK ** -0.5)
        k_ref[h] = l2n(qkv[:, GDN_KEY_W + LANES * h:GDN_KEY_W + LANES * (h + 1)], 1.0)
    for h in range(GDN_HV):
        v_ref[h] = qkv[:, 2 * GDN_KEY_W + LANES * h:2 * GDN_KEY_W + LANES * (h + 1)]
        z_ref[h] = u[:, CH + LANES * h:CH + LANES * (h + 1)]

    bg = u[:, CH + GDN_VAL_W:CH + GDN_VAL_W + LANES]
    beta = jax.nn.sigmoid(bg)
    g = -jnp.exp(alog_ref[...]) * jax.nn.softplus(bg + dtb_ref[...])
    t_in = (lax.broadcasted_iota(jnp.int32, (R, 1), 0) // s) % C
    k = 1
    while k < min(C, tt):
        gscr[Hg:Hg + R, :] = g
        g = g + jnp.where(t_in >= k, gscr[pl.ds(Hg - k * s, R), :], 0.0)
        k *= 2
    lane = lax.broadcasted_iota(jnp.int32, (R, LANES), 1)
    bg_ref[...] = jnp.where(lane < GDN_HV, beta, g)


def _gdn_pre(x3, halo, win, cw, alog, dtb, *, s, R, C):
    NB, TT, D = x3.shape
    H = halo.shape[1]
    tt = R // s
    Hg = max(SUBLANES, (min(C, tt) // 2) * s)
    kern = functools.partial(_gdn_pre_kernel, s=s, R=R, H=H, Hg=Hg, C=C)

    def ospec(nh):
        return pl.BlockSpec((None, nh, R, LANES), lambda i, c: (i, 0, c, 0))

    def oshape(nh):
        return jax.ShapeDtypeStruct((NB, nh, TT, LANES), F32)

    return pl.pallas_call(
        kern,
        grid=(NB, TT // R),
        in_specs=[
            pl.BlockSpec((None, R, D), lambda i, c: (i, c, 0)),
            pl.BlockSpec((None, H, GDN_CONV_CH), lambda i, c: (i, 0, 0)),
            _const_spec(win.shape), _const_spec(cw.shape), _const_spec(alog.shape), _const_spec(dtb.shape),
        ],
        out_specs=[ospec(GDN_HK), ospec(GDN_HK), ospec(GDN_HV), ospec(GDN_HV),
                   pl.BlockSpec((None, R, LANES), lambda i, c: (i, c, 0)),
                   pl.BlockSpec((None, H, GDN_CONV_CH), lambda i, c: (i, 0, 0))],
        out_shape=[oshape(GDN_HK), oshape(GDN_HK), oshape(GDN_HV), oshape(GDN_HV),
                   jax.ShapeDtypeStruct((NB, TT, LANES), F32),
                   jax.ShapeDtypeStruct((NB, H, GDN_CONV_CH), F32)],
        scratch_shapes=[pltpu.VMEM((H + R, GDN_CONV_CH), F32), pltpu.VMEM((Hg + R, LANES), F32)],
        compiler_params=_cparams(2),
        name="gdn_pre",
    )(x3, halo, win, cw, alog, dtb)


def _gdn_rec_kernel(q_ref, k_ref, v_ref, z_ref, bg_ref, x_ref, s0_ref, ng_ref, wout_ref, lng_ref, lnb_ref,
                    out_ref, sT_ref, sscr, oscr, *, C, batches):
    tb = pl.program_id(1)
    bb, nhv, Rt, _ = v_ref.shape
    rep = GDN_HV // GDN_HK

    @pl.when(tb == 0)
    def _init():
        sscr[...] = s0_ref[...].reshape(sscr.shape)

    strict, incl = _packed_masks(C)
    lo = lax.broadcasted_iota(jnp.int32, (C, 2 * C), 1) < C

    def part_a(items, ctx):
        CH = [(b, h, ci) for b, ci in items for h in range(nhv)]
        KH = [(b, m, ci) for b, ci in items for m in range(GDN_HK)]
        rs = lambda ci: pl.ds(ci * C, C)
        bgs = {(b, ci): bg_ref[b, rs(ci), :] for b, ci in items}
        bgT = {key: x.T for key, x in bgs.items()}

        kq = [jnp.concatenate([k_ref[b, m, rs(ci), :], q_ref[b, m, rs(ci), :]], axis=0) for b, m, ci in KH]
        sc = _each(lambda x: _dot(x, jnp.concatenate([x[:C, :], x[:C, :]], axis=0), _NT), kq)
        yield

        def col2(x, lane):
            return jnp.where(lo, jnp.broadcast_to(x[:, lane:lane + 1], (C, 2 * C)),
                             jnp.broadcast_to(x[:, lane + 1:lane + 2], (C, 2 * C)))

        bcol = [col2(bgs[b, ci], rep * m) for b, m, ci in KH]
        gcol = [col2(bgs[b, ci], GDN_HV + rep * m) for b, m, ci in KH]
        grow = [jnp.concatenate([bgT[b, ci][GDN_HV + rep * m:GDN_HV + rep * m + 1, :],
                                 bgT[b, ci][GDN_HV + rep * m + 1:GDN_HV + rep * m + 2, :]], axis=1) for b, m, ci in KH]
        diff = _each(lambda c_, r_: c_ - r_, gcol, grow)
        Lp = _each(lambda z, bc, d: jnp.where(strict, z[:C, :] * bc * jnp.exp(jnp.where(strict, d, 0.0)), 0.0),
                   sc, bcol, diff)
        Ap = _each(lambda z, d: jnp.where(incl, z[C:, :] * jnp.exp(jnp.where(incl, d, 0.0)), 0.0), sc, diff)
        Tinv_p = yield from _unit_lower_inverse([-l_ for l_ in Lp])

        half = lambda xs, n, h: xs[n // rep][:, (h % rep) * C:(h % rep + 1) * C]
        Tinv = [half(Tinv_p, n, h) for n, (b, h, ci) in enumerate(CH)]
        A = [half(Ap, n, h) for n, (b, h, ci) in enumerate(CH)]
        k = [kq[n // rep][:C, :] for n in range(len(CH))]
        q = [kq[n // rep][C:, :] for n in range(len(CH))]
        v = [v_ref[b, h, rs(ci), :] for b, h, ci in CH]
        beta = [jnp.broadcast_to(bgs[b, ci][:, h:h + 1], (C, LANES)) for b, h, ci in CH]
        gc = [jnp.broadcast_to(bgs[b, ci][:, GDN_HV + h:GDN_HV + h + 1], (C, LANES)) for b, h, ci in CH]
        kb = _each(lambda x, y: x * y, k, beta)
        eg = _each(jnp.exp, gc)
        UW = _each(lambda t, v_, b_, kb_, e: _dot(t, jnp.concatenate([v_ * b_, kb_ * e], axis=1)),
                   Tinv, v, beta, kb, eg)
        yield
        for n, key in enumerate(CH):
            g_last = gc[n][C - 1:C, :]
            ctx[key] = dict(U=UW[n][:, :GDN_DV], WQl=jnp.concatenate([UW[n][:, GDN_DV:], q[n] * eg[n]], axis=0),
                            A=A[n], kdec=k[n] * jnp.exp(g_last - gc[n]), sdec=jnp.exp(g_last))

    def part_b(items, ctx):
        for ci in sorted({ci for _, ci in items}):
            CH = [(b, h, ci) for b, c_ in items if c_ == ci for h in range(nhv)]
            X = [ctx.pop(key) for key in CH]
            S = [sscr[b * nhv + h] for b, h, _ in CH]
            WQ = _each(lambda x, s_: _dot(x["WQl"], s_), X, S)
            yield
            v_new = _each(lambda x, wq: x["U"] - wq[:C, :], X, WQ)
            o = _each(lambda x, wq, vn: wq[C:, :] + _dot(x["A"], vn), X, WQ, v_new)
            Snew = _each(lambda x, s_, vn: s_ * x["sdec"] + _dot(x["kdec"], vn, _TN), X, S, v_new)
            yield
            for n, (b, h, _) in enumerate(CH):
                sscr[b * nhv + h] = Snew[n]
                oh = o[n]
                oscr[b * Rt + ci * C:b * Rt + (ci + 1) * C, LANES * h:LANES * (h + 1)] = (
                    oh * lax.rsqrt(jnp.mean(oh * oh, axis=-1, keepdims=True) + GDN_EPS) * ng_ref[...])
            yield

    _run_pipelined(part_a, part_b, batches)

    @pl.when(tb == pl.num_programs(1) - 1)
    def _final_state():
        sT_ref[...] = sscr[...].reshape(sT_ref.shape)

    _gated_out_ln(oscr[...], z_ref, x_ref, wout_ref, lng_ref, lnb_ref, out_ref)


def _gdn_rec(q4, k4, v4, z4, bg, x, s0, norm_g, wout, lng, lnb, *, C, Rt, bb, gb):
    B, _, T, _ = v4.shape
    D = x.shape[-1]
    kern = functools.partial(_gdn_rec_kernel, C=C, batches=_rec_batches(bb, Rt // C, gb, GDN_CHUNKS_PER_BATCH))

    def tspec(nh):
        return pl.BlockSpec((bb, nh, Rt, LANES), lambda b, t: (b, 0, t, 0))

    xspec = pl.BlockSpec((bb, Rt, D), lambda b, t: (b, t, 0))
    sspec = pl.BlockSpec((bb, GDN_HV, GDN_DK, GDN_DV), lambda b, t: (b, 0, 0, 0))
    return pl.pallas_call(
        kern,
        grid=(B // bb, T // Rt),
        in_specs=[tspec(GDN_HK), tspec(GDN_HK), tspec(GDN_HV), tspec(GDN_HV),
                  pl.BlockSpec((bb, Rt, LANES), lambda b, t: (b, t, 0)), xspec, sspec,
                  _const_spec(norm_g.shape), _const_spec(wout.shape), _const_spec(lng.shape), _const_spec(lnb.shape)],
        out_specs=[xspec, sspec],
        out_shape=[jax.ShapeDtypeStruct(x.shape, F32), jax.ShapeDtypeStruct(s0.shape, F32)],
        scratch_shapes=[pltpu.VMEM((bb * GDN_HV, GDN_DK, GDN_DV), F32), pltpu.VMEM((bb * Rt, GDN_VAL_W), F32)],
        compiler_params=_cparams(2),
        name="gdn_rec",
    )(q4, k4, v4, z4, bg, x, s0, norm_g, wout, lng, lnb)


class _Group:
    def __init__(self, B, T, time_major):
        self.B, self.T, self.time_major = B, T, time_major
        if time_major:
            self.s, self.NB, self.TT, self.R = B, 1, T * B, T * B
        else:
            self.s, self.NB, self.TT, self.R = 1, B, T, min(ROW_BLOCK, T)

    def to_rows(self, x):
        if self.time_major:
            return jnp.swapaxes(x, 0, 1).reshape(1, self.TT, x.shape[-1])
        return x

    def from_rows(self, x3):
        if self.time_major:
            return jnp.swapaxes(x3.reshape(self.T, self.B, x3.shape[-1]), 0, 1)
        return x3

    def halo(self, st, n_steps):
        if self.time_major:
            return jnp.swapaxes(st, 0, 1).reshape(1, n_steps * self.B, st.shape[-1])
        H = _halo_rows(n_steps, 1)
        return jnp.pad(st, ((0, 0), (H - n_steps, 0), (0, 0)))

    def unhalo(self, tail, n_steps):
        if self.time_major:
            return jnp.swapaxes(tail.reshape(n_steps, self.B, tail.shape[-1]), 0, 1)
        return tail[:, tail.shape[1] - n_steps:]

    def vec(self, st):
        return st[None] if self.time_major else st[:, None, :]

    def unvec(self, v):
        return v[0] if self.time_major else v[:, 0, :]

    def heads_to_batch(self, a4, Tpad):
        if not self.time_major:
            return a4
        nh = a4.shape[1]
        a = a4.reshape(nh, self.T, self.B, LANES).transpose(2, 0, 1, 3)
        return jnp.pad(a, ((0, 0), (0, 0), (0, Tpad - self.T), (0, 0)))

    def rows_to_batch(self, a3, Tpad):
        if not self.time_major:
            return a3
        return jnp.pad(self.from_rows(a3), ((0, 0), (0, Tpad - self.T), (0, 0)))

    def rows_from_batch(self, a3):
        if not self.time_major:
            return a3
        return self.to_rows(a3[:, :self.T])

    def rec_tiling(self, chunk, block):
        if self.time_major:
            Tp = -(-self.T // SUBLANES) * SUBLANES
            return Tp, Tp, Tp, min(self.B, SAMPLE_SEQ_BLOCK), min(self.B, SAMPLE_SEQ_BLOCK)
        return min(chunk, self.T), self.T, min(block, self.T), PROMPT_SEQ_BLOCK if self.B % PROMPT_SEQ_BLOCK == 0 else 1, 1


def _lru_apply(g, x3, conv_st, h_st, p, lng, lnb):
    swap = not g.time_major
    assert g.B % SUBLANES == 0, "RG-LRU kernel needs the batch to fill whole sublane tiles"
    lay = _Group(g.B, g.T, True) if swap else g
    out, tail, hT = _lru_layer(x3, lay.halo(conv_st, CONV_W - 1), lay.vec(h_st), p["win"], p["cw"], p["cb"],
                               p["wg"], p["bg"], p["lam"], p["wout"], lng, lnb, s=lay.s,
                               R=min(LRU_BLOCK, g.T * g.B) if swap else g.R, swap=swap)
    return out, lay.unhalo(tail, CONV_W - 1), lay.unvec(hT)


def _rwkv_apply(g, x3, shift_st, wkv_st, p, lng, lnb):
    B = g.B
    pre = _rwkv_pre(x3, g.halo(shift_st[:, None, :], 1), p["mu"], p["win"], p["w0"], p["w1"], p["w2"], p["a0"],
                    p["a1"], p["a2"], s=g.s, R=g.R)
    new_shift = g.from_rows(x3)[:, -1]
    if g.time_major:
        r4, k4, v4, g4, lw4, a4 = pre
        s0 = jnp.transpose(wkv_st, (1, 2, 3, 0)).reshape(RW_H // 2, 2, RW_N, RW_N, B)
        col = lambda v: jnp.swapaxes(v, 1, 2)
        o4, sT = _rwkv_step(r4, k4, v4, lw4, a4, s0, col(p["k_k"]), col(p["k_a"]), col(p["r_k"]), col(p["gn_g"]),
                            col(p["gn_b"]), T=g.T, B=B)
        out = _post_layer(x3, o4, g4, p["wout"], lng, lnb)
        return out, new_shift, jnp.transpose(sT.reshape(RW_H, RW_N, RW_N, B), (3, 0, 1, 2))
    C, Tp, Rt, bb, gb = g.rec_tiling(RW_CHUNK, RW_BLOCK)
    r4, k4, v4, g4, lw4, a4 = [g.heads_to_batch(a, Tp) for a in pre]
    s0 = wkv_st.reshape(B, RW_H // 2, 2 * RW_N, RW_N)
    out, sT = _rwkv_rec(r4, k4, v4, lw4, a4, g4, g.rows_to_batch(x3, Tp), s0, p["k_k"], p["k_a"], p["r_k"],
                        p["gn_g"], p["gn_b"], p["wout"], lng, lnb, C=C, Rt=Rt, bb=bb, gb=gb)
    new_shift = g.from_rows(x3)[:, -1]
    return g.rows_from_batch(out), new_shift, sT.reshape(B, RW_H, RW_N, RW_N)


def _gdn_apply(g, x3, conv_st, S_st, p, lng, lnb):
    T = g.T
    C, Tp, Rt, bb, gb = g.rec_tiling(GDN_CHUNK, GDN_BLOCK)
    q4, k4, v4, z4, bg, tail = _gdn_pre(x3, g.halo(conv_st, CONV_W - 1), p["win"], p["cw"], p["alog"], p["dtb"],
                                        s=g.s, R=g.R, C=min(GDN_CHUNK, T))
    bgb = g.rows_to_batch(bg, T)
    if Tp != T:
        held = jnp.where(jnp.arange(LANES) < GDN_HV, 0.0, bgb[:, T - 1:T, :])
        bgb = jnp.concatenate([bgb, jnp.broadcast_to(held, (g.B, Tp - T, LANES))], axis=1)
    tb = lambda a: g.heads_to_batch(a, Tp)
    out, sT = _gdn_rec(tb(q4), tb(k4), tb(v4), tb(z4), bgb, g.rows_to_batch(x3, Tp), S_st, p["norm_g"], p["wout"],
                       lng, lnb, C=C, Rt=Rt, bb=bb, gb=gb)
    return g.rows_from_batch(out), g.unhalo(tail, CONV_W - 1), sT


def _trunk(g, x, st, params, ln_g, ln_b):
    lru_conv, lru_h, rw_shift, rw_S, gdn_conv, gdn_S = st
    new = ([], [], [], [], [], [])
    x3 = g.to_rows(x)
    ia = ib = ic = 0
    for layer in range(DEPTH):
        kind = layer % N_MIXERS
        lng, lnb = ln_g[layer][None, :], ln_b[layer][None, :]
        if kind == 0:
            x3, c, h = _lru_apply(g, x3, lru_conv[ia], lru_h[ia], params["lru"][ia], lng, lnb)
            new[0].append(c)
            new[1].append(h)
            ia += 1
        elif kind == 1:
            x3, sh, S = _rwkv_apply(g, x3, rw_shift[ib], rw_S[ib], params["rwkv"][ib], lng, lnb)
            new[2].append(sh)
            new[3].append(S)
            ib += 1
        else:
            x3, c, S = _gdn_apply(g, x3, gdn_conv[ic], gdn_S[ic], params["gdn"][ic], lng, lnb)
            new[4].append(c)
            new[5].append(S)
            ic += 1
    return g.from_rows(x3), tuple(s[0][None] if len(s) == 1 else jnp.stack(s) for s in new)


def _prep_params(lru_w_in, lru_conv_w, lru_conv_b, lru_wa, lru_ba, lru_wx, lru_bx, lru_lambda, lru_w_out, rw_mu,
                 rw_w_in, rw_w0, rw_w1, rw_w2, rw_a0, rw_a1, rw_a2, rw_k_k, rw_k_a, rw_r_k, rw_gn_g, rw_gn_b,
                 rw_w_out, gdn_w_in, gdn_conv_w, gdn_a_log, gdn_dt_bias, gdn_norm_g, gdn_w_out):
    row = lambda v: v[None, :]
    lru = []
    for n in range(lru_w_in.shape[0]):
        lru.append(dict(win=lru_w_in[n].astype(BF), cw=lru_conv_w[n][:, None, :], cb=row(lru_conv_b[n]),
                        wg=_lru_gate_weights(lru_wa[n], lru_wx[n]), bg=jnp.stack([lru_ba[n], lru_bx[n]])[:, None, :],
                        lam=row(lru_lambda[n]), wout=lru_w_out[n].astype(BF)))
    pairw = lambda v: v.reshape(RW_W // LANES, 1, LANES)
    rwkv = []
    for n in range(rw_w_in.shape[0]):
        rwkv.append(dict(mu=rw_mu[n][:, None, :],win=rw_w_in[n].astype(BF), w0=row(rw_w0[n]), w1=rw_w1[n].astype(BF),
                         w2=rw_w2[n].astype(BF), a0=row(rw_a0[n]), a1=rw_a1[n].astype(BF), a2=rw_a2[n].astype(BF),
                         k_k=pairw(rw_k_k[n]), k_a=pairw(rw_k_a[n]), r_k=pairw(rw_r_k[n]), gn_g=pairw(rw_gn_g[n]),
                         gn_b=pairw(rw_gn_b[n]), wout=rw_w_out[n].astype(BF)))
    gdn = []
    for n in range(gdn_w_in.shape[0]):
        w = gdn_w_in[n]
        o2 = GDN_CONV_CH + GDN_VAL_W
        wpad = jnp.pad(w[:, o2:], ((0, 0), (0, LANES - 2 * GDN_HV)))
        lanes = lambda v: jnp.pad(v, (GDN_HV, LANES - 2 * GDN_HV))[None, :]
        gdn.append(dict(win=jnp.concatenate([w[:, :o2], wpad], axis=1).astype(BF), cw=gdn_conv_w[n][:, None, :],
                        alog=lanes(gdn_a_log[n]), dtb=lanes(gdn_dt_bias[n]), norm_g=row(gdn_norm_g[n]),
                        wout=gdn_w_out[n].astype(BF)))
    return dict(lru=lru, rwkv=rwkv, gdn=gdn)


def kernel(x_prompt, x_sample, state_lru_conv, state_lru_h, state_rwkv_shift, state_rwkv_wkv, state_gdn_conv, state_gdn_S, ln_g, ln_b, lru_w_in, lru_conv_w, lru_conv_b, lru_wa, lru_ba, lru_wx, lru_bx, lru_lambda, lru_w_out, rw_mu, rw_w_in, rw_w0, rw_w1, rw_w2, rw_a0, rw_a1, rw_a2, rw_k_k, rw_k_a, rw_r_k, rw_gn_g, rw_gn_b, rw_w_out, gdn_w_in, gdn_conv_w, gdn_a_log, gdn_dt_bias, gdn_norm_g, gdn_w_out):
    params = _prep_params(lru_w_in, lru_conv_w, lru_conv_b, lru_wa, lru_ba, lru_wx, lru_bx, lru_lambda, lru_w_out,
                          rw_mu, rw_w_in, rw_w0, rw_w1, rw_w2, rw_a0, rw_a1, rw_a2, rw_k_k, rw_k_a, rw_r_k, rw_gn_g,
                          rw_gn_b, rw_w_out, gdn_w_in, gdn_conv_w, gdn_a_log, gdn_dt_bias, gdn_norm_g, gdn_w_out)
    bp, tp, _ = x_prompt.shape
    bs, ts, _ = x_sample.shape
    n_a, n_b, n_c = state_lru_conv.shape[0], state_rwkv_shift.shape[0], state_gdn_conv.shape[0]
    zero_state = (jnp.zeros((n_a, bp, CONV_W - 1, LRU_W), F32),
                  jnp.zeros((n_a, bp, LRU_W), F32),
                  jnp.zeros((n_b, bp, D_MODEL), F32),
                  jnp.zeros((n_b, bp, RW_H, RW_N, RW_N), F32),
                  jnp.zeros((n_c, bp, CONV_W - 1, GDN_CONV_CH), F32),
                  jnp.zeros((n_c, bp, GDN_HV, GDN_DK, GDN_DV), F32))
    y_prompt, sp = _trunk(_Group(bp, tp, False), x_prompt, zero_state, params, ln_g, ln_b)
    y_sample, ss = _trunk(_Group(bs, ts, True), x_sample,
                          (state_lru_conv, state_lru_h, state_rwkv_shift, state_rwkv_wkv, state_gdn_conv,
                           state_gdn_S), params, ln_g, ln_b)
    return (y_prompt, y_sample, sp[0], ss[0], sp[1], ss[1], sp[2], ss[2], sp[3], ss[3], sp[4], ss[4], sp[5], ss[5])
```
